```python
import math
import jax
import jax.numpy as jnp
from jax import lax
import numpy as np

D_MODEL = 2048
BATCH = 2
SEQ = 4096
DEPTH = 2
DEC_BATCH = 32
DEC_SEQ = 4
PAST_LEN = 8192
PAGE_SIZE = 128

N_EVEN = (DEPTH + 1) // 2
N_ODD = DEPTH // 2

N_HEADS = 8
HEAD_DIM = 128
N_KV = 2
GROUP = N_HEADS // N_KV
ROPE_DIM = HEAD_DIM // 4
ROPE_THETA = 500000.0
CMP_BLOCK = 32
CMP_STRIDE = 16
SEL_BLOCK = 64
SEL_TOPK = 16
N_LOCAL = 2
WINDOW = 512
Q_BLOCK = 128
FORCED_SCORE = 1e9
ATTN_SCALE = HEAD_DIM ** -0.5
D_RNN = D_MODEL // 2
RG_BLOCKS = 8
RG_BW = D_RNN // RG_BLOCKS
RG_CONV = 4
RG_C = 8.0
S5_CH = 16
S5_GROUPS = D_MODEL // S5_CH
S5_STATE = 64
D_FF = ((8 * D_MODEL // 3 + 255) // 256) * 256
FFN_CONV = 3
PLE_DIM = 256
NORM_EPS = 1e-6

Q_W = N_HEADS * HEAD_DIM
KV_W = N_KV * HEAD_DIM
_EVEN_SIZES = (Q_W, KV_W, KV_W, KV_W, KV_W, KV_W, KV_W, 3 * N_HEADS, D_RNN, D_RNN)
EVEN_SPLITS = tuple(int(s) for s in np.cumsum(_EVEN_SIZES)[:-1])
IN_W = sum(_EVEN_SIZES)
MIX_W = Q_W + D_RNN

kernel_name = 'nsa_rglru_s5_convffn_hybrid_step'


def rmsnorm(x, g):
    xf = x.astype(jnp.float32)
    y = xf * lax.rsqrt(jnp.mean(xf * xf, axis=-1, keepdims=True) + NORM_EPS)
    return (y * g.astype(jnp.float32)).astype(x.dtype)


def rope(x, pos):
    half = ROPE_DIM // 2
    inv = ROPE_THETA ** (-jnp.arange(half, dtype=jnp.float32) * 2.0 / ROPE_DIM)
    ang = pos.astype(jnp.float32)[:, None] * inv
    cos = jnp.cos(ang)[:, None, :]
    sin = jnp.sin(ang)[:, None, :]
    xr = x[..., :ROPE_DIM].astype(jnp.float32)
    x1, x2 = xr[..., :half], xr[..., half:]
    rot = jnp.concatenate([x1 * cos - x2 * sin, x2 * cos + x1 * sin], axis=-1)
    return jnp.concatenate([rot.astype(x.dtype), x[..., ROPE_DIM:]], axis=-1)


def masked_softmax(s, mask):
    s = jnp.where(mask, s, -1e30)
    m = jnp.max(s, axis=-1, keepdims=True)
    e = jnp.exp(s - m) * mask
    return e / jnp.maximum(jnp.sum(e, axis=-1, keepdims=True), 1e-30)


def causal_dwconv(x, buf, w, b):
    width = w.shape[0]
    T = x.shape[1]
    xp = jnp.concatenate([buf.astype(x.dtype), x], axis=1)
    y = b + sum(xp[:, k:k + T] * w[k] for k in range(width))
    return y, xp[:, T:]


def linear_scan(a, b, h0):
    a_c, b_c = lax.associative_scan(lambda l, r: (l[0] * r[0], r[0] * l[1] + r[1]), (a, b), axis=1)
    return b_c + a_c * h0[:, None]


def complex_combine(l, r):
    la_re, la_im, lb_re, lb_im = l
    ra_re, ra_im, rb_re, rb_im = r
    return (ra_re * la_re - ra_im * la_im,
            ra_re * la_im + ra_im * la_re,
            ra_re * lb_re - ra_im * lb_im + rb_re,
            ra_re * lb_im + ra_im * lb_re + rb_im)


def dense_attend(q, k, v, mask):
    bsz, T = q.shape[:2]
    qg = q.reshape(bsz, T, N_KV, GROUP, HEAD_DIM)
    s = jnp.einsum('btkgd,blkd->bkgtl', qg, k).astype(jnp.float32) * ATTN_SCALE
    p = masked_softmax(s, mask)
    o = jnp.einsum('bkgtl,blkd->btkgd', p.astype(v.dtype), v)
    return o.reshape(bsz, T, N_HEADS, HEAD_DIM), p


def compress(rows, pe, w1, w2):
    bsz, L = rows.shape[:2]
    n_chunk = L // CMP_STRIDE
    ch = rows[:, :n_chunk * CMP_STRIDE].reshape(bsz, n_chunk, CMP_STRIDE, N_KV, HEAD_DIM)
    lo = jnp.einsum('bnckd,cde->bnke', ch + pe[:CMP_STRIDE, None, :], w1[:CMP_STRIDE])
    hi = jnp.einsum('bnckd,cde->bnke', ch + pe[CMP_STRIDE:, None, :], w1[CMP_STRIDE:])
    hid = jax.nn.gelu(lo[:, :-1] + hi[:, 1:])
    return jnp.einsum('bnke,ed->bnkd', hid, w2)


def select_blocks(imp, pos, n_sel):
    n_cmp = imp.shape[-1]
    c0 = jnp.arange(n_cmp) * CMP_STRIDE
    s0 = jnp.arange(n_sel) * SEL_BLOCK
    ov = jnp.minimum(c0[:, None] + CMP_BLOCK, s0[None] + SEL_BLOCK) - jnp.maximum(c0[:, None], s0[None])
    overlap = jnp.maximum(ov, 0).astype(jnp.float32) / CMP_BLOCK
    score = jnp.einsum('bktn,nj->btkj', imp, overlap)
    qblk = (pos // SEL_BLOCK)[:, None]
    j = jnp.arange(n_sel)[None]
    forced = ((j == 0) | ((j <= qblk) & (j > qblk - N_LOCAL)))[None, :, None]
    future = (j > qblk)[None, :, None]
    score = jnp.where(forced, FORCED_SCORE, jnp.where(future, -1e30, score))
    _, idx = lax.top_k(score, min(SEL_TOPK, n_sel))
    return idx


def attend_gathered(q, kg, vg, idx, pos):
    bsz, T = q.shape[:2]
    n_k = idx.shape[-1] * SEL_BLOCK
    qg = q.reshape(bsz, T, N_KV, GROUP, HEAD_DIM)
    s = jnp.einsum('btkgd,btkjsd->btkgjs', qg, kg).astype(jnp.float32) * ATTN_SCALE
    kpos = idx[..., None] * SEL_BLOCK + jnp.arange(SEL_BLOCK)
    mask = (kpos <= pos[None, :, None, None, None]).reshape(bsz, T, N_KV, 1, n_k)
    p = masked_softmax(s.reshape(bsz, T, N_KV, GROUP, n_k), mask)
    o = jnp.einsum('btkgm,btkmd->btkgd', p.astype(vg.dtype), vg.reshape(bsz, T, N_KV, n_k, HEAD_DIM))
    return o.reshape(bsz, T, N_HEADS, HEAD_DIM)


def sel_attend_prompt(q, k, v, idx, pos):
    bsz, T = q.shape[:2]
    nqb = T // Q_BLOCK
    kb = k.reshape(bsz, T // SEL_BLOCK, SEL_BLOCK, N_KV, HEAD_DIM)
    vb = v.reshape(bsz, T // SEL_BLOCK, SEL_BLOCK, N_KV, HEAD_DIM)
    b_ix = jnp.arange(bsz)[:, None, None, None]
    kv_ix = jnp.arange(N_KV)[None, None, :, None]

    def one_block(args):
        qq, ii, pp = args
        return attend_gathered(qq, kb[b_ix, ii, :, kv_ix, :], vb[b_ix, ii, :, kv_ix, :], ii, pp)

    qs = q.reshape(bsz, nqb, Q_BLOCK, N_HEADS, HEAD_DIM).swapaxes(0, 1)
    iis = idx.reshape(bsz, nqb, Q_BLOCK, N_KV, idx.shape[-1]).swapaxes(0, 1)
    ps = pos.reshape(nqb, Q_BLOCK)
    o = lax.map(one_block, (qs, iis, ps))
    return o.swapaxes(0, 1).reshape(bsz, T, N_HEADS, HEAD_DIM)


def gather_pages(pool, page_table):
    g = pool[page_table]
    return g.reshape(page_table.shape[0], -1, *pool.shape[2:])


def gather_sel_sample(pool, new, page_table, idx):
    db, T = new.shape[:2]
    bpp = PAGE_SIZE // SEL_BLOCK
    n_past_blk = page_table.shape[1] * bpp
    pool_b = pool.reshape(pool.shape[0], bpp, SEL_BLOCK, N_KV, HEAD_DIM)
    b_ix = jnp.arange(db)[:, None, None, None]
    kv_ix = jnp.arange(N_KV)[None, None, :, None]
    past_j = jnp.minimum(idx, n_past_blk - 1)
    page = page_table[b_ix, past_j // bpp]
    g_past = pool_b[page, past_j % bpp, :, kv_ix, :]
    n_new_blk = -(-T // SEL_BLOCK)
    new_b = jnp.pad(new.astype(pool.dtype), ((0, 0), (0, n_new_blk * SEL_BLOCK - T), (0, 0), (0, 0)))
    new_b = new_b.reshape(db, n_new_blk, SEL_BLOCK, N_KV, HEAD_DIM)
    new_j = jnp.clip(idx - n_past_blk, 0, n_new_blk - 1)
    g_new = new_b[b_ix, new_j, :, kv_ix, :]
    return jnp.where((idx < n_past_blk)[..., None, None], g_past, g_new)


def window_attend_prompt(q, k, v):
    bsz, T = q.shape[:2]
    nqb = T // Q_BLOCK
    nb = WINDOW // Q_BLOCK
    padw = ((0, 0), (nb * Q_BLOCK, 0), (0, 0), (0, 0))
    idx = jnp.arange(nqb)[:, None] * Q_BLOCK + jnp.arange((nb + 1) * Q_BLOCK)[None]
    kb = jnp.pad(k, padw)[:, idx]
    vb = jnp.pad(v, padw)[:, idx]
    kpos = (idx - nb * Q_BLOCK)[:, None, :]
    qpos = (jnp.arange(nqb)[:, None] * Q_BLOCK + jnp.arange(Q_BLOCK)[None])[:, :, None]
    mask = (kpos >= 0) & (kpos <= qpos) & (kpos > qpos - WINDOW)
    qb = q.reshape(bsz, nqb, Q_BLOCK, N_KV, GROUP, HEAD_DIM)
    s = jnp.einsum('bnqkgd,bnlkd->bnkgql', qb, kb).astype(jnp.float32) * ATTN_SCALE
    p = masked_softmax(s, mask[None, :, None, None])
    o = jnp.einsum('bnkgql,bnlkd->bnqkgd', p.astype(vb.dtype), vb)
    return o.reshape(bsz, T, N_HEADS, HEAD_DIM)


def rg_lru(u, h0, w_a, b_a, w_x, b_x, lam):
    bsz, T, _ = u.shape
    ub = u.reshape(bsz, T, RG_BLOCKS, RG_BW)
    r = jax.nn.sigmoid((jnp.einsum('btnc,ncd->btnd', ub, w_a).reshape(bsz, T, D_RNN) + b_a).astype(jnp.float32))
    i = jax.nn.sigmoid((jnp.einsum('btnc,ncd->btnd', ub, w_x).reshape(bsz, T, D_RNN) + b_x).astype(jnp.float32))
    log_a = -RG_C * r * jax.nn.softplus(-lam.astype(jnp.float32))
    b = jnp.sqrt(-jnp.expm1(2.0 * log_a)) * i * u.astype(jnp.float32)
    h = linear_scan(jnp.exp(log_a), b, h0.astype(jnp.float32))
    return h, h[:, -1]


def s5_ssm(u, h0_re, h0_im, lam_re, lam_im, log_dt, b_re, b_im, c_re, c_im, d):
    bsz, T, _ = u.shape
    lam_re = lam_re.astype(jnp.float32)
    lam_im = lam_im.astype(jnp.float32)
    dt = jnp.exp(log_dt.astype(jnp.float32))[:, None]
    mag = jnp.exp(lam_re * dt)
    abar_re = mag * jnp.cos(lam_im * dt)
    abar_im = mag * jnp.sin(lam_im * dt)
    den = lam_re * lam_re + lam_im * lam_im
    n_re = abar_re - 1.0
    f_re = (n_re * lam_re + abar_im * lam_im) / den
    f_im = (abar_im * lam_re - n_re * lam_im) / den
    bb_re = f_re[..., None] * b_re - f_im[..., None] * b_im
    bb_im = f_re[..., None] * b_im + f_im[..., None] * b_re
    ug = u.astype(jnp.float32).reshape(bsz, T, S5_GROUPS, S5_CH)
    bu_re = jnp.einsum('btgc,gpc->btgp', ug, bb_re)
    bu_im = jnp.einsum('btgc,gpc->btgp', ug, bb_im)
    a_re = jnp.broadcast_to(abar_re, (1, T) + abar_re.shape)
    a_im = jnp.broadcast_to(abar_im, (1, T) + abar_im.shape)
    ac_re, ac_im, hc_re, hc_im = lax.associative_scan(complex_combine, (a_re, a_im, bu_re, bu_im), axis=1)
    h0r = h0_re.astype(jnp.float32)[:, None]
    h0i = h0_im.astype(jnp.float32)[:, None]
    h_re = hc_re + ac_re * h0r - ac_im * h0i
    h_im = hc_im + ac_re * h0i + ac_im * h0r
    y = jnp.einsum('btgp,gcp->btgc', h_re, c_re) - jnp.einsum('btgp,gcp->btgc', h_im, c_im)
    y = y.reshape(bsz, T, S5_GROUPS * S5_CH) + d * u.astype(jnp.float32)
    return y, h_re[:, -1], h_im[:, -1]


def even_mixer(h, past, win_len, w):
    (w_in, w_out, qn, knc, kns, knw, pe_k, w1_k, w2_k, pe_v, w1_v, w2_v,
     cw, cb, wa, ba, wx, bx, lam) = w
    bsz, T, _ = h.shape
    z = h @ w_in
    q, kc, vc, ks, vs, kw, vw, gt, xr, gr = jnp.split(z, EVEN_SPLITS, axis=-1)
    q = q.reshape(bsz, T, N_HEADS, HEAD_DIM)
    kc, vc, ks, vs, kw, vw = [t.reshape(bsz, T, N_KV, HEAD_DIM) for t in (kc, vc, ks, vs, kw, vw)]
    if past is None:
        past_len = 0
        conv_buf = jnp.zeros((bsz, RG_CONV - 1, D_RNN), h.dtype)
        h0 = jnp.zeros((bsz, D_RNN), jnp.float32)
    else:
        (page_table, pool_ck, pool_cv, pool_sk, pool_sv, buf_wk, buf_wv, h0, conv_buf) = past
        past_len = page_table.shape[1] * PAGE_SIZE
    pos = past_len + jnp.arange(T)
    q = rope(rmsnorm(q, qn), pos)
    ks = rope(rmsnorm(ks, kns), pos)
    kw = rope(rmsnorm(kw, knw), pos)

    if past is None:
        rows_k, rows_v = kc, vc
    else:
        rows_k = jnp.concatenate([gather_pages(pool_ck, page_table), kc.astype(pool_ck.dtype)], axis=1)
        rows_v = jnp.concatenate([gather_pages(pool_cv, page_table), vc.astype(pool_cv.dtype)], axis=1)
    k_cmp = compress(rows_k, pe_k, w1_k, w2_k)
    v_cmp = compress(rows_v, pe_v, w1_v, w2_v)
    cmp_end = jnp.arange(k_cmp.shape[1]) * CMP_STRIDE + CMP_BLOCK - 1
    k_cmp = rope(rmsnorm(k_cmp, knc), cmp_end)
    o_cmp, p_cmp = dense_attend(q, k_cmp, v_cmp, cmp_end[None, :] <= pos[:, None])

    n_sel = -(-(past_len + T) // SEL_BLOCK)
    idx = select_blocks(jnp.sum(p_cmp, axis=2), pos, n_sel)
    if past is None:
        o_sel = sel_attend_prompt(q, ks, vs, idx, pos)
        o_win = window_attend_prompt(q, kw, vw)
        pad = jnp.zeros((bsz, win_len, N_KV, HEAD_DIM), kw.dtype)
        new_wk = jnp.concatenate([pad, kw], axis=1)[:, -win_len:]
        new_wv = jnp.concatenate([pad.astype(vw.dtype), vw], axis=1)[:, -win_len:]
    else:
        kg = gather_sel_sample(pool_sk, ks, page_table, idx)
        vg = gather_sel_sample(pool_sv, vs, page_table, idx)
        o_sel = attend_gathered(q, kg, vg, idx, pos)
        wb = buf_wk.shape[1]
        kk = jnp.concatenate([buf_wk, kw.astype(buf_wk.dtype)], axis=1)
        vv = jnp.concatenate([buf_wv, vw.astype(buf_wv.dtype)], axis=1)
        kpos = past_len - wb + jnp.arange(wb + T)
        wmask = (kpos[None] <= pos[:, None]) & (kpos[None] > pos[:, None] - WINDOW)
        o_win, _ = dense_attend(q, kk, vv, wmask)
        new_wk = kk[:, -wb:]
        new_wv = vv[:, -wb:]
    g = jax.nn.sigmoid(gt.astype(jnp.float32)).reshape(bsz, T, 3, N_HEADS, 1)
    o_nsa = g[:, :, 0] * o_cmp + g[:, :, 1] * o_sel + g[:, :, 2] * o_win

    u, new_conv = causal_dwconv(xr, conv_buf, cw, cb)
    h_seq, h_last = rg_lru(u, h0, wa, ba, wx, bx, lam)
    y_rg = h_seq * jax.nn.gelu(gr.astype(jnp.float32))

    mix = jnp.concatenate([o_nsa.reshape(bsz, T, Q_W).astype(h.dtype), y_rg.astype(h.dtype)], axis=-1)
    out = mix @ w_out
    return out.astype(h.dtype), (kc, vc, ks, vs, new_wk, new_wv, h_last, new_conv)


def odd_mixer(h, past, w):
    lam_re, lam_im, log_dt, b_re, b_im, c_re, c_im, d, w_a, w_b = w
    bsz = h.shape[0]
    if past is None:
        h0_re = jnp.zeros((bsz, S5_GROUPS, S5_STATE), jnp.float32)
        h0_im = jnp.zeros((bsz, S5_GROUPS, S5_STATE), jnp.float32)
    else:
        h0_re, h0_im = past
    y, hr, hi = s5_ssm(h, h0_re, h0_im, lam_re, lam_im, log_dt, b_re, b_im, c_re, c_im, d)
    zz = jax.nn.gelu(y).astype(h.dtype)
    out = (zz @ w_a) * jax.nn.sigmoid(zz @ w_b)
    return out.astype(h.dtype), (hr, hi)


def conv_ffn(x, buf, w_up, w_gate, conv_w, conv_b, w_down):
    hu = x @ w_up
    hg = x @ w_gate
    if buf is None:
        buf = jnp.zeros((x.shape[0], FFN_CONV - 1, hu.shape[-1]), hu.dtype)
    hc, new_buf = causal_dwconv(hu, buf, conv_w, conv_b)
    return (jax.nn.gelu(hc) * hg) @ w_down, new_buf


def ple(x, p, g, w_proj, w_gate):
    return (p @ w_proj).astype(x.dtype) * jax.nn.sigmoid(rmsnorm(x, g) @ w_gate)


def setup_inputs(seed: int = 0) -> dict:
    key = jax.random.key(seed)
    keys = iter(jax.random.split(key, 96))
    f32 = jnp.float32

    def nrm(shape, scale=1.0):
        return jax.random.normal(next(keys), shape, f32) * scale

    def gain(shape):
        return 1.0 + nrm(shape, 0.02)

    n_pages = PAST_LEN // PAGE_SIZE
    n_phys = (DEC_BATCH * n_pages * 5) // 4
    w_buf = min(WINDOW, PAST_LEN)
    page_table = jax.random.permutation(next(keys), n_phys)[:DEC_BATCH * n_pages]
    page_table = page_table.reshape(DEC_BATCH, n_pages).astype(jnp.int32)
    pool = (N_EVEN, n_phys, PAGE_SIZE, N_KV, HEAD_DIM)
    win = (N_EVEN, DEC_BATCH, w_buf, N_KV, HEAD_DIM)
    a0 = jax.random.uniform(next(keys), (N_EVEN, D_RNN), f32, 0.9, 0.999) ** (1.0 / RG_C)
    rg_lam = jnp.log(a0) - jnp.log1p(-a0)
    s5_shape = (N_ODD, S5_GROUPS, S5_STATE)
    s5_lam_re = -0.5 + nrm(s5_shape, 0.01)
    s5_lam_im = math.pi * jnp.arange(S5_STATE, dtype=f32) + nrm(s5_shape, 0.01)
    s5_log_dt = jax.random.uniform(next(keys), (N_ODD, S5_GROUPS), f32, math.log(1e-3), math.log(1e-1))
    return {
        'x_prompt': nrm((BATCH, SEQ, D_MODEL)),
        'x_sample': nrm((DEC_BATCH, DEC_SEQ, D_MODEL)),
        'cache_cmp_k': nrm(pool),
        'cache_cmp_v': nrm(pool),
        'cache_sel_k': nrm(pool),
        'cache_sel_v': nrm(pool),
        'cache_win_k': nrm(win),
        'cache_win_v': nrm(win),
        'state_rglru_h': nrm((N_EVEN, DEC_BATCH, D_RNN), 0.5),
        'state_rglru_conv': nrm((N_EVEN, DEC_BATCH, RG_CONV - 1, D_RNN)),
        'state_s5_re': nrm((N_ODD, DEC_BATCH, S5_GROUPS, S5_STATE), 0.5),
        'state_s5_im': nrm((N_ODD, DEC_BATCH, S5_GROUPS, S5_STATE), 0.5),
        'state_ffn_conv': nrm((DEPTH, DEC_BATCH, FFN_CONV - 1, D_FF)),
        'page_table': page_table,
        'p_prompt': nrm((DEPTH, BATCH, SEQ, PLE_DIM)),
        'p_sample': nrm((DEPTH, DEC_BATCH, DEC_SEQ, PLE_DIM)),
        'norm_mix': gain((DEPTH, D_MODEL)),
        'norm_ffn': gain((DEPTH, D_MODEL)),
        'norm_ple': gain((DEPTH, D_MODEL)),
        'w_in_even': nrm((N_EVEN, D_MODEL, IN_W), D_MODEL ** -0.5),
        'w_out_even': nrm((N_EVEN, MIX_W, D_MODEL), MIX_W ** -0.5),
        'q_norm': gain((N_EVEN, HEAD_DIM)),
        'k_norm_cmp': gain((N_EVEN, HEAD_DIM)),
        'k_norm_sel': gain((N_EVEN, HEAD_DIM)),
        'k_norm_win': gain((N_EVEN, HEAD_DIM)),
        'cmp_pe_k': nrm((N_EVEN, CMP_BLOCK, HEAD_DIM), 0.1),
        'cmp_w1_k': nrm((N_EVEN, CMP_BLOCK, HEAD_DIM, HEAD_DIM), (CMP_BLOCK * HEAD_DIM) ** -0.5),
        'cmp_w2_k': nrm((N_EVEN, HEAD_DIM, HEAD_DIM), HEAD_DIM ** -0.5),
        'cmp_pe_v': nrm((N_EVEN, CMP_BLOCK, HEAD_DIM), 0.1),
        'cmp_w1_v': nrm((N_EVEN, CMP_BLOCK, HEAD_DIM, HEAD_DIM), (CMP_BLOCK * HEAD_DIM) ** -0.5),
        'cmp_w2_v': nrm((N_EVEN, HEAD_DIM, HEAD_DIM), HEAD_DIM ** -0.5),
        'rg_conv_w': nrm((N_EVEN, RG_CONV, D_RNN), RG_CONV ** -0.5),
        'rg_conv_b': nrm((N_EVEN, D_RNN), 0.01),
        'rg_w_a': nrm((N_EVEN, RG_BLOCKS, RG_BW, RG_BW), RG_BW ** -0.5),
        'rg_b_a': nrm((N_EVEN, D_RNN), 0.01),
        'rg_w_x': nrm((N_EVEN, RG_BLOCKS, RG_BW, RG_BW), RG_BW ** -0.5),
        'rg_b_x': nrm((N_EVEN, D_RNN), 0.01),
        'rg_lam': rg_lam,
        's5_lam_re': s5_lam_re,
        's5_lam_im': s5_lam_im,
        's5_log_dt': s5_log_dt,
        's5_b_re': nrm((N_ODD, S5_GROUPS, S5_STATE, S5_CH), (2 * S5_CH) ** -0.5),
        's5_b_im': nrm((N_ODD, S5_GROUPS, S5_STATE, S5_CH), (2 * S5_CH) ** -0.5),
        's5_c_re': nrm((N_ODD, S5_GROUPS, S5_CH, S5_STATE), S5_STATE ** -0.5),
        's5_c_im': nrm((N_ODD, S5_GROUPS, S5_CH, S5_STATE), S5_STATE ** -0.5),
        's5_d': nrm((N_ODD, D_MODEL), 0.5),
        's5_w_glu_a': nrm((N_ODD, D_MODEL, D_MODEL), D_MODEL ** -0.5),
        's5_w_glu_b': nrm((N_ODD, D_MODEL, D_MODEL), D_MODEL ** -0.5),
        'ffn_w_up': nrm((DEPTH, D_MODEL, D_FF), D_MODEL ** -0.5),
        'ffn_w_gate': nrm((DEPTH, D_MODEL, D_FF), D_MODEL ** -0.5),
        'ffn_conv_w': nrm((DEPTH, FFN_CONV, D_FF), FFN_CONV ** -0.5),
        'ffn_conv_b': nrm((DEPTH, D_FF), 0.01),
        'ffn_w_down': nrm((DEPTH, D_FF, D_MODEL), D_FF ** -0.5),
        'ple_w_proj': nrm((DEPTH, PLE_DIM, D_MODEL), PLE_DIM ** -0.5),
        'ple_w_gate': nrm((DEPTH, D_MODEL, D_MODEL), D_MODEL ** -0.5),
    }


def reference(x_prompt, x_sample, cache_cmp_k, cache_cmp_v, cache_sel_k, cache_sel_v, cache_win_k, cache_win_v,
              state_rglru_h, state_rglru_conv, state_s5_re, state_s5_im, state_ffn_conv, page_table,
              p_prompt, p_sample, norm_mix, norm_ffn, norm_ple, w_in_even, w_out_even,
              q_norm, k_norm_cmp, k_norm_sel, k_norm_win, cmp_pe_k, cmp_w1_k, cmp_w2_k, cmp_pe_v, cmp_w1_v, cmp_w2_v,
              rg_conv_w, rg_conv_b, rg_w_a, rg_b_a, rg_w_x, rg_b_x, rg_lam,
              s5_lam_re, s5_lam_im, s5_log_dt, s5_b_re, s5_b_im, s5_c_re, s5_c_im, s5_d, s5_w_glu_a, s5_w_glu_b,
              ffn_w_up, ffn_w_gate, ffn_conv_w, ffn_conv_b, ffn_w_down, ple_w_proj, ple_w_gate):
    win_len = cache_win_k.shape[2]

    def trunk(x, p, sample):
        ev, od, ff = [], [], []
        for li in range(DEPTH):
            h = rmsnorm(x, norm_mix[li])
            if li % 2 == 0:
                e = li // 2
                past = ((page_table, cache_cmp_k[e], cache_cmp_v[e], cache_sel_k[e], cache_sel_v[e],
                         cache_win_k[e], cache_win_v[e], state_rglru_h[e], state_rglru_conv[e]) if sample else None)
                w = (w_in_even[e], w_out_even[e], q_norm[e], k_norm_cmp[e], k_norm_sel[e], k_norm_win[e],
                     cmp_pe_k[e], cmp_w1_k[e], cmp_w2_k[e], cmp_pe_v[e], cmp_w1_v[e], cmp_w2_v[e],
                     rg_conv_w[e], rg_conv_b[e], rg_w_a[e], rg_b_a[e], rg_w_x[e], rg_b_x[e], rg_lam[e])
                m, st = even_mixer(h, past, win_len, w)
                ev.append(st)
            else:
                o = li // 2
                past = (state_s5_re[o], state_s5_im[o]) if sample else None
                w = (s5_lam_re[o], s5_lam_im[o], s5_log_dt[o], s5_b_re[o], s5_b_im[o], s5_c_re[o], s5_c_im[o],
                     s5_d[o], s5_w_glu_a[o], s5_w_glu_b[o])
                m, st = odd_mixer(h, past, w)
                od.append(st)
            x = x + m
            fbuf = state_ffn_conv[li] if sample else None
            f, nb = conv_ffn(rmsnorm(x, norm_ffn[li]), fbuf, ffn_w_up[li], ffn_w_gate[li],
                             ffn_conv_w[li], ffn_conv_b[li], ffn_w_down[li])
            ff.append(nb)
            x = x + f
            x = x + ple(x, p[li], norm_ple[li], ple_w_proj[li], ple_w_gate[li])
        return x, ev, od, ff

    y_prompt, ev_p, od_p, ff_p = trunk(x_prompt, p_prompt, False)
    y_sample, ev_s, od_s, ff_s = trunk(x_sample, p_sample, True)

    def stk(states, i):
        return jnp.stack([s[i] for s in states])

    new_cmp_k_prompt, new_cmp_k_sample = stk(ev_p, 0), stk(ev_s, 0)
    new_cmp_v_prompt, new_cmp_v_sample = stk(ev_p, 1), stk(ev_s, 1)
    new_sel_k_prompt, new_sel_k_sample = stk(ev_p, 2), stk(ev_s, 2)
    new_sel_v_prompt, new_sel_v_sample = stk(ev_p, 3), stk(ev_s, 3)
    new_win_k_prompt, new_win_k_sample = stk(ev_p, 4), stk(ev_s, 4)
    new_win_v_prompt, new_win_v_sample = stk(ev_p, 5), stk(ev_s, 5)
    new_rglru_h_prompt, new_rglru_h_sample = stk(ev_p, 6), stk(ev_s, 6)
    new_rglru_conv_prompt, new_rglru_conv_sample = stk(ev_p, 7), stk(ev_s, 7)
    new_s5_re_prompt, new_s5_re_sample = stk(od_p, 0), stk(od_s, 0)
    new_s5_im_prompt, new_s5_im_sample = stk(od_p, 1), stk(od_s, 1)
    new_ffn_conv_prompt, new_ffn_conv_sample = jnp.stack(ff_p), jnp.stack(ff_s)
    return (y_prompt, y_sample,
            new_cmp_k_prompt, new_cmp_k_sample, new_cmp_v_prompt, new_cmp_v_sample,
            new_sel_k_prompt, new_sel_k_sample, new_sel_v_prompt, new_sel_v_sample,
            new_win_k_prompt, new_win_k_sample, new_win_v_prompt, new_win_v_sample,
            new_rglru_h_prompt, new_rglru_h_sample, new_rglru_conv_prompt, new_rglru_conv_sample,
            new_s5_re_prompt, new_s5_re_sample, new_s5_im_prompt, new_s5_im_sample,
            new_ffn_conv_prompt, new_ffn_conv_sample)
```

```python
import functools

import numpy as np
import jax
import jax.numpy as jnp
from jax import lax
from jax.experimental import pallas as pl
from jax.experimental.pallas import tpu as pltpu

F32 = jnp.float32
BF16 = jnp.bfloat16

D_MODEL = 2048
DEC_SEQ = 4
PAGE_SIZE = 128
N_HEADS = 8
HEAD_DIM = 128
N_KV = 2
GROUP = N_HEADS // N_KV
ROPE_DIM = HEAD_DIM // 4
ROPE_HALF = ROPE_DIM // 2
ROPE_THETA = 500000.0
CMP_BLOCK = 32
CMP_STRIDE = 16
SEL_BLOCK = 64
SEL_TOPK = 16
N_LOCAL = 2
WINDOW = 512
Q_BLOCK = 128
FORCED_SCORE = 1e9
NEG_BIG = -1e30
ATTN_SCALE = HEAD_DIM ** -0.5
D_RNN = D_MODEL // 2
RG_BLOCKS = 8
RG_BW = D_RNN // RG_BLOCKS
RG_CONV = 4
RG_C = 8.0
S5_CH = 16
S5_GROUPS = D_MODEL // S5_CH
S5_STATE = 64
S5_GB = 8
S5_NGB = S5_GROUPS // S5_GB
S5_LANES = S5_GB * S5_STATE
D_FF = ((8 * D_MODEL // 3 + 255) // 256) * 256
FFN_CONV = 3
NORM_EPS = 1e-6
Q_W = N_HEADS * HEAD_DIM
KV_W = N_KV * HEAD_DIM
PAGES_PER_STEP = 8
Q_PAD = 8

COL_Q = 0
COL_XR = Q_W
COL_GR = Q_W + D_RNN
COL_KV = Q_W + 2 * D_RNN
COL_GT = COL_KV + 6 * KV_W
IN_PAD = 5120


def _cp(sem, vmem_mb=48):
    return pltpu.CompilerParams(dimension_semantics=sem, vmem_limit_bytes=vmem_mb * 1024 * 1024)


def _gelu(x):
    return 0.5 * x * (1.0 + jnp.tanh(0.7978845608028654 * (x + 0.044715 * (x * x * x))))


def _sigmoid(x):
    return 1.0 / (1.0 + jnp.exp(-x))


def _dotb(a, b):
    return jnp.dot(a.astype(BF16), b.astype(BF16), preferred_element_type=F32)


def _dotf(a, b):
    return jnp.dot(a, b, preferred_element_type=F32, precision=lax.Precision.HIGHEST)


def _dot_nt(a, b):
    return lax.dot_general(a, b, (((1,), (1,)), ((), ())), preferred_element_type=F32)


def _masked_softmax(s, mask):
    s = jnp.where(mask, s, NEG_BIG)
    m = jnp.max(s, axis=-1, keepdims=True)
    e = jnp.where(mask, jnp.exp(s - m), 0.0)
    return e / jnp.maximum(jnp.sum(e, axis=-1, keepdims=True), 1e-30)


def _norm_rope(x, g, cos_t, sin_t):
    ms = jnp.mean(x * x, axis=-1, keepdims=True)
    y = x * lax.rsqrt(ms + NORM_EPS) * g
    lane = lax.broadcasted_iota(jnp.int32, y.shape, 1)
    swapped = jnp.where(lane < ROPE_HALF, pltpu.roll(y, HEAD_DIM - ROPE_HALF, 1), pltpu.roll(y, ROPE_HALF, 1))
    return y * cos_t + swapped * sin_t


def _topk_mask(score, colf, k, ncols):
    sel = jnp.zeros_like(score)
    sc = score
    for _ in range(k):
        mx = jnp.max(sc, axis=1, keepdims=True)
        cand = jnp.where(sc == mx, colf, float(ncols))
        first = jnp.min(cand, axis=1, keepdims=True)
        pick = colf == first
        sel = jnp.where(pick, 1.0, sel)
        sc = jnp.where(pick, -jnp.inf, sc)
    return sel


def _rope_tables(pos):
    inv = ROPE_THETA ** (-jnp.arange(ROPE_HALF, dtype=F32) * 2.0 / ROPE_DIM)
    ang = pos.astype(F32)[:, None] * inv
    cos = jnp.cos(ang)
    sin = jnp.sin(ang)
    n = pos.shape[0]
    ones = jnp.ones((n, HEAD_DIM - ROPE_DIM), F32)
    zeros = jnp.zeros((n, HEAD_DIM - ROPE_DIM), F32)
    return (jnp.concatenate([cos, cos, ones], axis=1), jnp.concatenate([-sin, sin, zeros], axis=1))


def _overlap_matrix(n_rows, n_sel, n_cols):
    n = np.arange(n_rows)[:, None] - 1
    j = np.arange(n_cols)[None]
    c0 = n * CMP_STRIDE
    s0 = j * SEL_BLOCK
    ov = np.minimum(c0 + CMP_BLOCK, s0 + SEL_BLOCK) - np.maximum(c0, s0)
    ov = np.maximum(ov, 0).astype(np.float32) / CMP_BLOCK
    ov = np.where((n >= 0) & (j < n_sel), ov, 0.0)
    return jnp.asarray(ov, F32)


def _rmsnorm(x, g, out_dtype, tm):
    M, D = x.shape

    def body(x_ref, g_ref, o_ref):
        xv = x_ref[...]
        ms = jnp.mean(xv * xv, axis=-1, keepdims=True)
        o_ref[...] = (xv * lax.rsqrt(ms + NORM_EPS) * g_ref[...]).astype(o_ref.dtype)

    return pl.pallas_call(
        body, grid=(M // tm,),
        in_specs=[pl.BlockSpec((tm, D), lambda i: (i, 0)), pl.BlockSpec((1, D), lambda i: (0, 0))],
        out_specs=pl.BlockSpec((tm, D), lambda i: (i, 0)),
        out_shape=jax.ShapeDtypeStruct((M, D), out_dtype),
        compiler_params=_cp(("parallel",)), name="rmsnorm")(x, g.reshape(1, D))


def _mm(a_list, w_list, pair, extras, epilogue, out_dtypes, tm, tn, name):
    M = a_list[0].shape[0]
    N = w_list[0].shape[1]
    na, nw, ne = len(a_list), len(w_list), len(extras)

    def body(*refs):
        a_vals = [r[...] for r in refs[:na]]
        accs = [jnp.dot(a_vals[pair[i]], refs[na + i][...], preferred_element_type=F32) for i in range(nw)]
        ex = [r[...] for r in refs[na + nw:na + nw + ne]]
        res = epilogue(accs, ex)
        for o_ref, v in zip(refs[na + nw + ne:], res):
            o_ref[...] = v.astype(o_ref.dtype)

    in_specs = [pl.BlockSpec((tm, a.shape[1]), lambda i, j: (i, 0)) for a in a_list]
    in_specs += [pl.BlockSpec((w.shape[0], tn), lambda i, j: (0, j)) for w in w_list]
    for arr, kind in extras:
        if kind == "tile":
            in_specs.append(pl.BlockSpec((tm, tn), lambda i, j: (i, j)))
        else:
            in_specs.append(pl.BlockSpec((arr.shape[0], tn), lambda i, j: (0, j)))
    return pl.pallas_call(
        body, grid=(M // tm, N // tn), in_specs=in_specs,
        out_specs=[pl.BlockSpec((tm, tn), lambda i, j: (i, j)) for _ in out_dtypes],
        out_shape=[jax.ShapeDtypeStruct((M, N), dt) for dt in out_dtypes],
        compiler_params=_cp(("parallel", "parallel")), name=name,
    )(*a_list, *w_list, *[e[0] for e in extras])


def _qk_prep(z, cos_t, sin_t, tab_blocks, qn, kns, knw, tm):
    M = z.shape[0]

    def body(zq_ref, zkv_ref, c_ref, s_ref, qn_ref, kns_ref, knw_ref, q_ref, ks_ref, kw_ref, kvb_ref):
        c = c_ref[...]
        s = s_ref[...]
        for h in range(N_HEADS):
            sl = slice(h * HEAD_DIM, (h + 1) * HEAD_DIM)
            q_ref[:, sl] = _norm_rope(zq_ref[:, sl], qn_ref[...], c, s).astype(BF16)
        for kv in range(N_KV):
            sl = slice(kv * HEAD_DIM, (kv + 1) * HEAD_DIM)
            ks = _norm_rope(zkv_ref[:, 2 * KV_W + kv * HEAD_DIM:2 * KV_W + (kv + 1) * HEAD_DIM], kns_ref[...], c, s)
            kw = _norm_rope(zkv_ref[:, 4 * KV_W + kv * HEAD_DIM:4 * KV_W + (kv + 1) * HEAD_DIM], knw_ref[...], c, s)
            ks_ref[:, sl] = ks
            kw_ref[:, sl] = kw
            kvb_ref[:, sl] = ks.astype(BF16)
            kvb_ref[:, 2 * KV_W + kv * HEAD_DIM:2 * KV_W + (kv + 1) * HEAD_DIM] = kw.astype(BF16)
        kvb_ref[:, KV_W:2 * KV_W] = zkv_ref[:, 3 * KV_W:4 * KV_W].astype(BF16)
        kvb_ref[:, 3 * KV_W:4 * KV_W] = zkv_ref[:, 5 * KV_W:6 * KV_W].astype(BF16)

    vec = pl.BlockSpec((1, HEAD_DIM), lambda i: (0, 0))
    tab = pl.BlockSpec((tm, HEAD_DIM), lambda i: (i % tab_blocks, 0))
    return pl.pallas_call(
        body, grid=(M // tm,),
        in_specs=[pl.BlockSpec((tm, Q_W), lambda i: (i, COL_Q // Q_W)),
                  pl.BlockSpec((tm, 6 * KV_W), lambda i: (i, COL_KV // (6 * KV_W))),
                  tab, tab, vec, vec, vec],
        out_specs=[pl.BlockSpec((tm, Q_W), lambda i: (i, 0)),
                   pl.BlockSpec((tm, KV_W), lambda i: (i, 0)),
                   pl.BlockSpec((tm, KV_W), lambda i: (i, 0)),
                   pl.BlockSpec((tm, 4 * KV_W), lambda i: (i, 0))],
        out_shape=[jax.ShapeDtypeStruct((M, Q_W), BF16), jax.ShapeDtypeStruct((M, KV_W), F32),
                   jax.ShapeDtypeStruct((M, KV_W), F32), jax.ShapeDtypeStruct((M, 4 * KV_W), BF16)],
        compiler_params=_cp(("parallel",)), name="qk_prep",
    )(z, z, cos_t, sin_t, qn.reshape(1, -1), kns.reshape(1, -1), knw.reshape(1, -1))


def _compress(pages, ptab, n_seq, pages_per_seq, pe, w1, w2, norm_g, cos_t, sin_t, do_norm, name):
    G = PAGES_PER_STEP
    steps = pages_per_seq // G
    cpp = PAGE_SIZE // CMP_STRIDE
    ch = G * cpp
    n_chunk = pages_per_seq * cpp

    def body(pt_ref, *refs):
        page_refs = (refs[:G], refs[G:2 * G])
        pe_ref, w1_ref, w2_ref, g_ref, c_ref, s_ref, o_ref, carry_ref = refs[2 * G:]
        step = pl.program_id(1)

        @pl.when(step == 0)
        def _():
            carry_ref[...] = jnp.zeros_like(carry_ref)

        row = lax.broadcasted_iota(jnp.int32, (ch, HEAD_DIM), 0)
        for kv in range(N_KV):
            lo = jnp.zeros((ch, HEAD_DIM), F32)
            hi = jnp.zeros((ch, HEAD_DIM), F32)
            for c in range(CMP_STRIDE):
                xc = jnp.concatenate([pr[0, pl.ds(c, cpp, stride=CMP_STRIDE), :] for pr in page_refs[kv]], axis=0)
                lo = lo + _dotb(xc + pe_ref[c:c + 1, :], w1_ref[c])
                hi = hi + _dotb(xc + pe_ref[CMP_STRIDE + c:CMP_STRIDE + c + 1, :], w1_ref[CMP_STRIDE + c])
            lo_prev = jnp.where(row == 0, carry_ref[kv, 0:1, :], pltpu.roll(lo, 1, 0))
            carry_ref[kv, 0:1, :] = lo[ch - 1:ch, :]
            out = _dotb(_gelu(lo_prev + hi), w2_ref[...])
            if do_norm:
                out = _norm_rope(out, g_ref[...], c_ref[...], s_ref[...])
            o_ref[0, kv] = out.astype(BF16)

    def page_map(j, kv, s, g, pt):
        return (pt[s * pages_per_seq + g * G + j], 0, kv)

    full2 = lambda s, g, pt: (0, 0)
    in_specs = [pl.BlockSpec((1, PAGE_SIZE, HEAD_DIM), functools.partial(page_map, j, kv))
                for kv in range(N_KV) for j in range(G)]
    in_specs += [pl.BlockSpec((CMP_BLOCK, HEAD_DIM), full2),
                 pl.BlockSpec((CMP_BLOCK, HEAD_DIM, HEAD_DIM), lambda s, g, pt: (0, 0, 0)),
                 pl.BlockSpec((HEAD_DIM, HEAD_DIM), full2),
                 pl.BlockSpec((1, HEAD_DIM), full2),
                 pl.BlockSpec((ch, HEAD_DIM), lambda s, g, pt: (g, 0)),
                 pl.BlockSpec((ch, HEAD_DIM), lambda s, g, pt: (g, 0))]
    gs = pltpu.PrefetchScalarGridSpec(
        num_scalar_prefetch=1, grid=(n_seq, steps), in_specs=in_specs,
        out_specs=pl.BlockSpec((1, N_KV, ch, HEAD_DIM), lambda s, g, pt: (s, 0, g, 0)),
        scratch_shapes=[pltpu.VMEM((N_KV, 8, HEAD_DIM), F32)])
    return pl.pallas_call(
        body, grid_spec=gs, out_shape=jax.ShapeDtypeStruct((n_seq, N_KV, n_chunk, HEAD_DIM), BF16),
        compiler_params=_cp(("parallel", "arbitrary")), name=name,
    )(ptab, *([pages] * (N_KV * G)), pe, w1.astype(BF16), w2.astype(BF16), norm_g.reshape(1, -1), cos_t, sin_t)


def _nsa_prompt(q, kcmp, vcmp, kvb, gates, ovl, expand, B, T):
    nqb = T // Q_BLOCK
    n_cmp_rows = T // CMP_STRIDE
    n_sel = T // SEL_BLOCK
    rows = GROUP * Q_BLOCK
    win_tiles = WINDOW // Q_BLOCK

    def body(q_ref, kc_ref, vc_ref, ks_ref, vs_ref, kw_ref, vw_ref, gt_ref, ovl_ref, ex_ref, o_ref):
        i = pl.program_id(2)
        q4 = q_ref[0]
        qh = jnp.concatenate([q4[:, h * HEAD_DIM:(h + 1) * HEAD_DIM] for h in range(GROUP)], axis=0)
        tok = lax.broadcasted_iota(jnp.int32, (rows, 1), 0) & (Q_BLOCK - 1)
        pos = i * Q_BLOCK + tok

        s = _dot_nt(qh, kc_ref[0, 0]) * ATTN_SCALE
        col = lax.broadcasted_iota(jnp.int32, (rows, n_cmp_rows), 1)
        cmask = (col >= 1) & (col * CMP_STRIDE + (CMP_BLOCK - 1 - CMP_STRIDE) <= pos)
        p = _masked_softmax(s, cmask)
        o_cmp = _dotb(p, vc_ref[0, 0])
        imp = p[0:Q_BLOCK]
        for h in range(1, GROUP):
            imp = imp + p[h * Q_BLOCK:(h + 1) * Q_BLOCK]

        score = _dotf(imp, ovl_ref[...])
        j = lax.broadcasted_iota(jnp.int32, (Q_BLOCK, n_sel), 1)
        qblk = (i * Q_BLOCK + lax.broadcasted_iota(jnp.int32, (Q_BLOCK, n_sel), 0)) // SEL_BLOCK
        forced = (j == 0) | ((j <= qblk) & (j > qblk - N_LOCAL))
        score = jnp.where(forced, FORCED_SCORE, jnp.where(j > qblk, NEG_BIG, score))
        sel = _topk_mask(score, j.astype(F32), min(SEL_TOPK, n_sel), n_sel).astype(BF16)

        kcol = lax.broadcasted_iota(jnp.int32, (rows, Q_BLOCK), 1)

        def flash(k_ref, v_ref, lo, hi, mask_fn):
            def tile(jt, carry):
                m, l, acc = carry
                start = pl.multiple_of(jt * Q_BLOCK, Q_BLOCK)
                k = k_ref[0, pl.ds(start, Q_BLOCK), :]
                v = v_ref[0, pl.ds(start, Q_BLOCK), :]
                sc = _dot_nt(qh, k) * ATTN_SCALE
                mask = mask_fn(jt, jt * Q_BLOCK + kcol)
                sc = jnp.where(mask, sc, NEG_BIG)
                m_new = jnp.maximum(m, jnp.max(sc, axis=-1, keepdims=True))
                alpha = jnp.exp(m - m_new)
                e = jnp.where(mask, jnp.exp(sc - m_new), 0.0)
                l = alpha * l + jnp.sum(e, axis=-1, keepdims=True)
                acc = alpha * acc + _dotb(e, v)
                return m_new, l, acc

            init = (jnp.full((rows, 1), NEG_BIG, F32), jnp.zeros((rows, 1), F32), jnp.zeros((rows, HEAD_DIM), F32))
            _, l, acc = lax.fori_loop(lo, hi, tile, init)
            return acc / jnp.maximum(l, 1e-30)

        def sel_mask(jt, kpos):
            m1 = _dotb(sel, ex_ref[jt])
            m4 = jnp.concatenate([m1] * GROUP, axis=0)
            return (m4 > 0.5) & (kpos <= pos)

        def win_mask(jt, kpos):
            return (kpos <= pos) & (kpos > pos - WINDOW)

        o_sel = flash(ks_ref, vs_ref, 0, i + 1, sel_mask)
        o_win = flash(kw_ref, vw_ref, jnp.maximum(i - win_tiles, 0), i + 1, win_mask)

        g = _sigmoid(gt_ref[0, 0])
        for h in range(GROUP):
            r = slice(h * Q_BLOCK, (h + 1) * Q_BLOCK)
            o = (g[:, h:h + 1] * o_cmp[r] + g[:, GROUP + h:GROUP + h + 1] * o_sel[r]
                 + g[:, 2 * GROUP + h:2 * GROUP + h + 1] * o_win[r])
            o_ref[0, :, h * HEAD_DIM:(h + 1) * HEAD_DIM] = o.astype(BF16)

    def kv_spec(off):
        return pl.BlockSpec((1, T, HEAD_DIM), lambda b, kv, i: (b, 0, off + kv))

    cmp_spec = pl.BlockSpec((1, 1, n_cmp_rows, HEAD_DIM), lambda b, kv, i: (b, kv, 0, 0))
    return pl.pallas_call(
        body, grid=(B, N_KV, nqb),
        in_specs=[pl.BlockSpec((1, Q_BLOCK, GROUP * HEAD_DIM), lambda b, kv, i: (b, i, kv)),
                  cmp_spec, cmp_spec, kv_spec(0), kv_spec(2), kv_spec(4), kv_spec(6),
                  pl.BlockSpec((1, 1, Q_BLOCK, 3 * GROUP), lambda b, kv, i: (b, kv, i, 0)),
                  pl.BlockSpec((n_cmp_rows, n_sel), lambda b, kv, i: (0, 0)),
                  pl.BlockSpec((nqb, n_sel, Q_BLOCK), lambda b, kv, i: (0, 0, 0))],
        out_specs=pl.BlockSpec((1, Q_BLOCK, GROUP * HEAD_DIM), lambda b, kv, i: (b, i, kv)),
        out_shape=jax.ShapeDtypeStruct((B, T, Q_W), BF16),
        compiler_params=_cp(("parallel", "parallel", "arbitrary")), name="nsa_prompt",
    )(q, kcmp, vcmp, kvb, kvb, kvb, kvb, gates, ovl, expand)


def _nsa_sample(q, kcmp, vcmp, pool_k, pool_v, ptab, kvnew, win_k, win_v, gates, ovl, expand, DB, past_len):
    G = PAGES_PER_STEP
    n_pages = past_len // PAGE_SIZE
    steps = n_pages // G
    n_cmp_rows = kcmp.shape[2]
    sel_cols = ovl.shape[1]
    n_past_blk = past_len // SEL_BLOCK
    n_sel = -(-(past_len + DEC_SEQ) // SEL_BLOCK)
    w_buf = win_k.shape[1]
    rows = GROUP * Q_PAD
    keys = G * PAGE_SIZE
    new_rows = kvnew.shape[1]

    def body(pt_ref, q_ref, kc_ref, vc_ref, ovl_ref, *rest):
        pk = rest[:G]
        pv = rest[G:2 * G]
        (ex_ref, kvn_ref, wk_ref, wv_ref, gt_ref, o_ref,
         sel_scr, m_scr, l_scr, acc_scr, base_scr) = rest[2 * G:]
        step = pl.program_id(1)
        tok = lax.broadcasted_iota(jnp.int32, (rows, 1), 0) & (Q_PAD - 1)
        q8 = q_ref[0]

        def q_rows(kv):
            return jnp.concatenate(
                [q8[:, (kv * GROUP + h) * HEAD_DIM:(kv * GROUP + h + 1) * HEAD_DIM] for h in range(GROUP)], axis=0)

        def head_col(gt, c0):
            return jnp.concatenate(
                [jnp.broadcast_to(gt[:, c0 + h:c0 + h + 1], (Q_PAD, HEAD_DIM)) for h in range(GROUP)], axis=0)

        @pl.when(step == 0)
        def _():
            for kv in range(N_KV):
                qh = q_rows(kv)
                s = _dot_nt(qh, kc_ref[0, kv]) * ATTN_SCALE
                col = lax.broadcasted_iota(jnp.int32, (rows, n_cmp_rows), 1)
                cmask = (col >= 1) & (col * CMP_STRIDE + (CMP_BLOCK - 1 - CMP_STRIDE) <= past_len + tok)
                p = _masked_softmax(s, cmask)
                o_cmp = _dotb(p, vc_ref[0, kv])
                imp = p[0:Q_PAD]
                for h in range(1, GROUP):
                    imp = imp + p[h * Q_PAD:(h + 1) * Q_PAD]
                score = _dotf(imp, ovl_ref[...])
                j = lax.broadcasted_iota(jnp.int32, (Q_PAD, sel_cols), 1)
                qblk = (past_len + lax.broadcasted_iota(jnp.int32, (Q_PAD, sel_cols), 0)) // SEL_BLOCK
                forced = (j == 0) | ((j <= qblk) & (j > qblk - N_LOCAL))
                score = jnp.where(forced, FORCED_SCORE, jnp.where(j > qblk, NEG_BIG, score))
                score = jnp.where(j < n_sel, score, -jnp.inf)
                sel = _topk_mask(score, j.astype(F32), min(SEL_TOPK, n_sel), sel_cols)
                sel4 = jnp.concatenate([sel] * GROUP, axis=0)
                sel_scr[kv] = sel4.astype(BF16)

                kb = wk_ref[0, :, kv * HEAD_DIM:(kv + 1) * HEAD_DIM].astype(BF16)
                vb = wv_ref[0, :, kv * HEAD_DIM:(kv + 1) * HEAD_DIM].astype(BF16)
                kn = kvn_ref[0, :, (4 + kv) * HEAD_DIM:(5 + kv) * HEAD_DIM]
                vn = kvn_ref[0, :, (6 + kv) * HEAD_DIM:(7 + kv) * HEAD_DIM]
                s1 = jnp.where(lax.broadcasted_iota(jnp.int32, (rows, w_buf), 1) - w_buf > tok - WINDOW,
                               _dot_nt(qh, kb) * ATTN_SCALE, NEG_BIG)
                ncol = lax.broadcasted_iota(jnp.int32, (rows, new_rows), 1)
                nmask = (ncol <= tok) & (ncol < DEC_SEQ)
                s2 = jnp.where(nmask, _dot_nt(qh, kn) * ATTN_SCALE, NEG_BIG)
                m = jnp.maximum(jnp.max(s1, axis=-1, keepdims=True), jnp.max(s2, axis=-1, keepdims=True))
                e1 = jnp.where(s1 > 0.5 * NEG_BIG, jnp.exp(s1 - m), 0.0)
                e2 = jnp.where(nmask, jnp.exp(s2 - m), 0.0)
                den = jnp.sum(e1, axis=-1, keepdims=True) + jnp.sum(e2, axis=-1, keepdims=True)
                o_win = (_dotb(e1, vb) + _dotb(e2, vn)) / jnp.maximum(den, 1e-30)

                gt = _sigmoid(gt_ref[0, kv])
                base_scr[kv] = head_col(gt, 0) * o_cmp + head_col(gt, 2 * GROUP) * o_win

                ksn = kvn_ref[0, :, kv * HEAD_DIM:(kv + 1) * HEAD_DIM]
                vsn = kvn_ref[0, :, (2 + kv) * HEAD_DIM:(3 + kv) * HEAD_DIM]
                chosen = sel4[:, n_past_blk:n_past_blk + 1] > 0.5
                smask = nmask & chosen
                s3 = jnp.where(smask, _dot_nt(qh, ksn) * ATTN_SCALE, NEG_BIG)
                m0 = jnp.max(s3, axis=-1, keepdims=True)
                e3 = jnp.where(smask, jnp.exp(s3 - m0), 0.0)
                m_scr[kv] = jnp.broadcast_to(m0, (rows, HEAD_DIM))
                l_scr[kv] = jnp.broadcast_to(jnp.sum(e3, axis=-1, keepdims=True), (rows, HEAD_DIM))
                acc_scr[kv] = _dotb(e3, vsn)

        for kv in range(N_KV):
            qh = q_rows(kv)
            k = jnp.concatenate([r[0, :, kv * HEAD_DIM:(kv + 1) * HEAD_DIM] for r in pk], axis=0).astype(BF16)
            v = jnp.concatenate([r[0, :, kv * HEAD_DIM:(kv + 1) * HEAD_DIM] for r in pv], axis=0).astype(BF16)
            mask = _dotb(sel_scr[kv], ex_ref[0]) > 0.5
            sc = jnp.where(mask, _dot_nt(qh, k) * ATTN_SCALE, NEG_BIG)
            m_old = m_scr[kv][:, 0:1]
            l_old = l_scr[kv][:, 0:1]
            m_new = jnp.maximum(m_old, jnp.max(sc, axis=-1, keepdims=True))
            alpha = jnp.exp(m_old - m_new)
            e = jnp.where(mask, jnp.exp(sc - m_new), 0.0)
            l_new = alpha * l_old + jnp.sum(e, axis=-1, keepdims=True)
            acc = alpha * acc_scr[kv] + _dotb(e, v)
            m_scr[kv] = jnp.broadcast_to(m_new, (rows, HEAD_DIM))
            l_scr[kv] = jnp.broadcast_to(l_new, (rows, HEAD_DIM))
            acc_scr[kv] = acc

        @pl.when(step == steps - 1)
        def _():
            for kv in range(N_KV):
                gt = _sigmoid(gt_ref[0, kv])
                o_sel = acc_scr[kv] / jnp.maximum(l_scr[kv], 1e-30)
                o = base_scr[kv] + head_col(gt, GROUP) * o_sel
                for h in range(GROUP):
                    hh = kv * GROUP + h
                    o_ref[0, :, hh * HEAD_DIM:(hh + 1) * HEAD_DIM] = o[h * Q_PAD:(h + 1) * Q_PAD].astype(BF16)

    def page_map(jj, b, g, pt):
        return (pt[b * n_pages + g * G + jj], 0, 0)

    page_specs = [pl.BlockSpec((1, PAGE_SIZE, KV_W), functools.partial(page_map, jj)) for jj in range(G)]
    cmp_spec = pl.BlockSpec((1, N_KV, n_cmp_rows, HEAD_DIM), lambda b, g, pt: (b, 0, 0, 0))
    in_specs = [pl.BlockSpec((1, Q_PAD, Q_W), lambda b, g, pt: (b, 0, 0)), cmp_spec, cmp_spec,
                pl.BlockSpec((n_cmp_rows, sel_cols), lambda b, g, pt: (0, 0))]
    in_specs += page_specs + page_specs
    in_specs += [pl.BlockSpec((1, sel_cols, keys), lambda b, g, pt: (g, 0, 0)),
                 pl.BlockSpec((1, new_rows, 4 * KV_W), lambda b, g, pt: (b, 0, 0)),
                 pl.BlockSpec((1, w_buf, KV_W), lambda b, g, pt: (b, 0, 0)),
                 pl.BlockSpec((1, w_buf, KV_W), lambda b, g, pt: (b, 0, 0)),
                 pl.BlockSpec((1, N_KV, Q_PAD, 3 * GROUP), lambda b, g, pt: (b, 0, 0, 0))]
    gs = pltpu.PrefetchScalarGridSpec(
        num_scalar_prefetch=1, grid=(DB, steps), in_specs=in_specs,
        out_specs=pl.BlockSpec((1, Q_PAD, Q_W), lambda b, g, pt: (b, 0, 0)),
        scratch_shapes=[pltpu.VMEM((N_KV, rows, sel_cols), BF16), pltpu.VMEM((N_KV, rows, HEAD_DIM), F32),
                        pltpu.VMEM((N_KV, rows, HEAD_DIM), F32), pltpu.VMEM((N_KV, rows, HEAD_DIM), F32),
                        pltpu.VMEM((N_KV, rows, HEAD_DIM), F32)])
    return pl.pallas_call(
        body, grid_spec=gs, out_shape=jax.ShapeDtypeStruct((DB, Q_PAD, Q_W), BF16),
        compiler_params=_cp(("parallel", "arbitrary")), name="nsa_sample",
    )(ptab, q, kcmp, vcmp, ovl, *([pool_k] * G), *([pool_v] * G), expand, kvnew, win_k, win_v, gates)


def _rg_gates(u, wa_ref, ba, wx_ref, bx, sp):
    ra = []
    rx = []
    for n in range(RG_BLOCKS):
        ub = u[:, n * RG_BW:(n + 1) * RG_BW]
        ra.append(_dotf(ub, wa_ref[n]))
        rx.append(_dotf(ub, wx_ref[n]))
    r = _sigmoid(jnp.concatenate(ra, axis=1) + ba)
    i = _sigmoid(jnp.concatenate(rx, axis=1) + bx)
    log_a = -RG_C * r * sp
    a = jnp.exp(log_a)
    b = jnp.sqrt(1.0 - jnp.exp(2.0 * log_a)) * i * u
    return a, b


def _softplus_neg(lam):
    return jnp.maximum(-lam, 0.0) + jnp.log(1.0 + jnp.exp(-jnp.abs(lam)))


def _rglru_prompt(z, cw, cb, wa, ba, wx, bx, lam, B, T, tt):
    keep = 8

    def body(xr_ref, gr_ref, cw_ref, cb_ref, wa_ref, ba_ref, wx_ref, bx_ref, lam_ref,
             y_ref, hl_ref, a_scr, b_scr, h_scr, tail_scr):
        t = pl.program_id(1)

        @pl.when(t == 0)
        def _():
            h_scr[...] = jnp.zeros_like(h_scr)
            tail_scr[...] = jnp.zeros_like(tail_scr)

        x = xr_ref[0]
        row = lax.broadcasted_iota(jnp.int32, (tt, D_RNN), 0)
        cwv = cw_ref[...]
        u = cb_ref[...] + x * cwv[RG_CONV - 1:RG_CONV]
        for d in range(1, RG_CONV):
            xs = pltpu.roll(x, d, 0)
            for r in range(d):
                xs = jnp.where(row == r, tail_scr[keep - d + r:keep - d + r + 1, :], xs)
            u = u + xs * cwv[RG_CONV - 1 - d:RG_CONV - d]
        tail_scr[...] = x[tt - keep:tt]
        a, b = _rg_gates(u, wa_ref, ba_ref[...], wx_ref, bx_ref[...], _softplus_neg(lam_ref[...]))
        a_scr[...] = a
        b_scr[...] = b

        def step(s, h):
            h = a_scr[pl.ds(s, 1), :] * h + b_scr[pl.ds(s, 1), :]
            b_scr[pl.ds(s, 1), :] = h
            return h

        h = lax.fori_loop(0, tt, step, h_scr[0:1, :], unroll=8)
        h_scr[0:1, :] = h
        hl_ref[0] = h
        y_ref[0] = (b_scr[...] * _gelu(gr_ref[0])).astype(BF16)

    vec = pl.BlockSpec((1, D_RNN), lambda b, t: (0, 0))
    wspec = pl.BlockSpec((RG_BLOCKS, RG_BW, RG_BW), lambda b, t: (0, 0, 0))
    return pl.pallas_call(
        body, grid=(B, T // tt),
        in_specs=[pl.BlockSpec((1, tt, D_RNN), lambda b, t: (b, t, COL_XR // D_RNN)),
                  pl.BlockSpec((1, tt, D_RNN), lambda b, t: (b, t, COL_GR // D_RNN)),
                  pl.BlockSpec((RG_CONV, D_RNN), lambda b, t: (0, 0)), vec, wspec, vec, wspec, vec, vec],
        out_specs=[pl.BlockSpec((1, tt, D_RNN), lambda b, t: (b, t, 0)),
                   pl.BlockSpec((1, 1, D_RNN), lambda b, t: (b, 0, 0))],
        out_shape=[jax.ShapeDtypeStruct((B, T, D_RNN), BF16), jax.ShapeDtypeStruct((B, 1, D_RNN), F32)],
        scratch_shapes=[pltpu.VMEM((tt, D_RNN), F32), pltpu.VMEM((tt, D_RNN), F32),
                        pltpu.VMEM((8, D_RNN), F32), pltpu.VMEM((keep, D_RNN), F32)],
        compiler_params=_cp(("parallel", "arbitrary")), name="rglru_prompt",
    )(z, z, cw, cb.reshape(1, -1), wa, ba.reshape(1, -1), wx, bx.reshape(1, -1), lam.reshape(1, -1))


def _rglru_sample(xr, gr, buf, h0, cw, cb, wa, ba, wx, bx, lam):
    DB = h0.shape[0]

    def body(xr_ref, gr_ref, buf_ref, h0_ref, cw_ref, cb_ref, wa_ref, ba_ref, wx_ref, bx_ref, lam_ref, y_ref, hl_ref):
        xp = [buf_ref[k] for k in range(RG_CONV - 1)] + [xr_ref[t] for t in range(DEC_SEQ)]
        cwv = cw_ref[...]
        sp = _softplus_neg(lam_ref[...])
        h = h0_ref[...]
        for t in range(DEC_SEQ):
            u = cb_ref[...]
            for k in range(RG_CONV):
                u = u + xp[t + k] * cwv[k:k + 1]
            a, b = _rg_gates(u, wa_ref, ba_ref[...], wx_ref, bx_ref[...], sp)
            h = a * h + b
            y_ref[t] = (h * _gelu(gr_ref[t])).astype(BF16)
        hl_ref[...] = h

    return pl.pallas_call(
        body, out_shape=[jax.ShapeDtypeStruct((DEC_SEQ, DB, D_RNN), BF16), jax.ShapeDtypeStruct((DB, D_RNN), F32)],
        compiler_params=pltpu.CompilerParams(vmem_limit_bytes=32 * 1024 * 1024), name="rglru_sample",
    )(xr, gr, buf, h0, cw, cb.reshape(1, -1), wa, ba.reshape(1, -1), wx, bx.reshape(1, -1), lam.reshape(1, -1))


def _s5_discretize(lam_re, lam_im, log_dt, b_re_t, b_im_t):
    def body(lr_ref, li_ref, ldt_ref, br_ref, bi_ref, ar_ref, ai_ref, bbr_ref, bbi_ref):
        lr = lr_ref[...]
        li = li_ref[...]
        dt = jnp.exp(ldt_ref[...])
        mag = jnp.exp(lr * dt)
        ar = mag * jnp.cos(li * dt)
        ai = mag * jnp.sin(li * dt)
        den = lr * lr + li * li
        n_re = ar - 1.0
        f_re = (n_re * lr + ai * li) / den
        f_im = (ai * lr - n_re * li) / den
        ar_ref[...] = ar
        ai_ref[...] = ai
        for c in range(S5_CH):
            bbr_ref[c] = f_re * br_ref[c] - f_im * bi_ref[c]
            bbi_ref[c] = f_re * bi_ref[c] + f_im * br_ref[c]

    gp = jax.ShapeDtypeStruct(lam_re.shape, F32)
    cgp = jax.ShapeDtypeStruct(b_re_t.shape, F32)
    return pl.pallas_call(body, out_shape=[gp, gp, cgp, cgp], name="s5_discretize")(
        lam_re, lam_im, log_dt.reshape(-1, 1), b_re_t, b_im_t)


def _s5_block_weights(bbr, bbi, c_re, c_im):
    eye = jnp.eye(S5_GB, dtype=F32)

    def bblk(bb):
        x = bb.reshape(S5_CH, S5_NGB, S5_GB, S5_STATE)
        x = jnp.einsum("cngp,gh->ngchp", x, eye)
        return x.reshape(S5_NGB, S5_GB * S5_CH, S5_LANES)

    def cblk(c):
        x = c.reshape(S5_NGB, S5_GB, S5_CH, S5_STATE)
        x = jnp.einsum("ngcp,gh->ngphc", x, eye)
        return x.reshape(S5_NGB, S5_LANES, S5_GB * S5_CH)

    return jnp.concatenate([bblk(bbr), bblk(bbi)], axis=2), cblk(c_re), cblk(c_im)


def _s5_prompt(u, bw, cr, ci, ar, ai, d, B, T, tt):
    L = S5_LANES

    def body(u_ref, bw_ref, cr_ref, ci_ref, ar_ref, ai_ref, d_ref, zz_ref, sre_ref, sim_ref, bu_scr, car_scr):
        t = pl.program_id(2)

        @pl.when(t == 0)
        def _():
            car_scr[...] = jnp.zeros_like(car_scr)

        uv = u_ref[0]
        bu_scr[...] = _dotf(uv, bw_ref[0])
        a_re = ar_ref[0]
        a_im = ai_ref[0]

        def step(s, carry):
            hr, hi = carry
            bu = bu_scr[pl.ds(s, 1), :]
            nr = a_re * hr - a_im * hi + bu[:, :L]
            ni = a_re * hi + a_im * hr + bu[:, L:]
            bu_scr[pl.ds(s, 1), :] = jnp.concatenate([nr, ni], axis=1)
            return nr, ni

        hr, hi = lax.fori_loop(0, tt, step, (car_scr[0:1, :], car_scr[1:2, :]), unroll=8)
        car_scr[0:1, :] = hr
        car_scr[1:2, :] = hi
        sre_ref[0, 0] = hr
        sim_ref[0, 0] = hi
        y = _dotf(bu_scr[:, :L], cr_ref[0]) - _dotf(bu_scr[:, L:], ci_ref[0]) + d_ref[...] * uv
        zz_ref[0] = _gelu(y).astype(BF16)

    lanes = S5_GB * S5_CH
    st_spec = pl.BlockSpec((1, 1, 1, L), lambda b, g, t: (b, g, 0, 0))
    st_shape = jax.ShapeDtypeStruct((B, S5_NGB, 1, L), F32)
    return pl.pallas_call(
        body, grid=(B, S5_NGB, T // tt),
        in_specs=[pl.BlockSpec((1, tt, lanes), lambda b, g, t: (b, t, g)),
                  pl.BlockSpec((1, lanes, 2 * L), lambda b, g, t: (g, 0, 0)),
                  pl.BlockSpec((1, L, lanes), lambda b, g, t: (g, 0, 0)),
                  pl.BlockSpec((1, L, lanes), lambda b, g, t: (g, 0, 0)),
                  pl.BlockSpec((1, 1, L), lambda b, g, t: (g, 0, 0)),
                  pl.BlockSpec((1, 1, L), lambda b, g, t: (g, 0, 0)),
                  pl.BlockSpec((1, lanes), lambda b, g, t: (0, g))],
        out_specs=[pl.BlockSpec((1, tt, lanes), lambda b, g, t: (b, t, g)), st_spec, st_spec],
        out_shape=[jax.ShapeDtypeStruct((B, T, D_MODEL), BF16), st_shape, st_shape],
        scratch_shapes=[pltpu.VMEM((tt, 2 * L), F32), pltpu.VMEM((8, L), F32)],
        compiler_params=_cp(("parallel", "parallel", "arbitrary")), name="s5_prompt",
    )(u, bw, cr, ci, ar, ai, d.reshape(1, -1))


def _s5_sample(u, h0r, h0i, bw, cr, ci, ar, ai, d):
    DB = u.shape[1]
    L = S5_LANES
    lanes = S5_GB * S5_CH

    def body(u_ref, hr_ref, hi_ref, bw_ref, cr_ref, ci_ref, ar_ref, ai_ref, d_ref, zz_ref, sre_ref, sim_ref):
        hr = hr_ref[...]
        hi = hi_ref[...]
        a_re = ar_ref[0]
        a_im = ai_ref[0]
        for t in range(DEC_SEQ):
            uv = u_ref[t]
            bu = _dotf(uv, bw_ref[0])
            hr, hi = a_re * hr - a_im * hi + bu[:, :L], a_re * hi + a_im * hr + bu[:, L:]
            y = _dotf(hr, cr_ref[0]) - _dotf(hi, ci_ref[0]) + d_ref[...] * uv
            zz_ref[t] = _gelu(y).astype(BF16)
        sre_ref[...] = hr
        sim_ref[...] = hi

    st_spec = pl.BlockSpec((DB, L), lambda g: (0, g))
    st_shape = jax.ShapeDtypeStruct((DB, S5_GROUPS * S5_STATE), F32)
    return pl.pallas_call(
        body, grid=(S5_NGB,),
        in_specs=[pl.BlockSpec((DEC_SEQ, DB, lanes), lambda g: (0, 0, g)), st_spec, st_spec,
                  pl.BlockSpec((1, lanes, 2 * L), lambda g: (g, 0, 0)),
                  pl.BlockSpec((1, L, lanes), lambda g: (g, 0, 0)),
                  pl.BlockSpec((1, L, lanes), lambda g: (g, 0, 0)),
                  pl.BlockSpec((1, 1, L), lambda g: (g, 0, 0)),
                  pl.BlockSpec((1, 1, L), lambda g: (g, 0, 0)),
                  pl.BlockSpec((1, lanes), lambda g: (0, g))],
        out_specs=[pl.BlockSpec((DEC_SEQ, DB, lanes), lambda g: (0, 0, g)), st_spec, st_spec],
        out_shape=[jax.ShapeDtypeStruct((DEC_SEQ, DB, D_MODEL), BF16), st_shape, st_shape],
        compiler_params=_cp(("parallel",)), name="s5_sample",
    )(u, h0r, h0i, bw, cr, ci, ar, ai, d.reshape(1, -1))


def _ffn_upgate_prompt(hf, w_up, w_gate, cw, cb, T, tm, tn):
    M, K = hf.shape
    N = w_up.shape[1]
    tiles_per_seq = T // tm
    pr = 16

    def body(a_ref, ap_ref, wu_ref, wg_ref, cw_ref, cb_ref, act_ref, tail_ref):
        i = pl.program_id(0)
        a = a_ref[...]
        hu = jnp.dot(a, wu_ref[...], preferred_element_type=F32)
        hg = jnp.dot(a, wg_ref[...], preferred_element_type=F32)
        prev = jnp.dot(ap_ref[...], wu_ref[...], preferred_element_type=F32)
        prev = jnp.where(i % tiles_per_seq == 0, 0.0, prev)
        row = lax.broadcasted_iota(jnp.int32, hu.shape, 0)
        h1 = jnp.where(row == 0, prev[pr - 1:pr], pltpu.roll(hu, 1, 0))
        h2 = jnp.where(row == 0, prev[pr - 2:pr - 1], jnp.where(row == 1, prev[pr - 1:pr], pltpu.roll(hu, 2, 0)))
        cwv = cw_ref[...]
        hc = cb_ref[...] + h2 * cwv[0:1] + h1 * cwv[1:2] + hu * cwv[2:3]
        act_ref[...] = (_gelu(hc) * hg).astype(BF16)
        tail_ref[...] = hu[tm - 8:tm]

    return pl.pallas_call(
        body, grid=(M // tm, N // tn),
        in_specs=[pl.BlockSpec((tm, K), lambda i, j: (i, 0)),
                  pl.BlockSpec((pr, K), lambda i, j: (jnp.maximum(i * (tm // pr) - 1, 0), 0)),
                  pl.BlockSpec((K, tn), lambda i, j: (0, j)),
                  pl.BlockSpec((K, tn), lambda i, j: (0, j)),
                  pl.BlockSpec((FFN_CONV, tn), lambda i, j: (0, j)),
                  pl.BlockSpec((1, tn), lambda i, j: (0, j))],
        out_specs=[pl.BlockSpec((tm, tn), lambda i, j: (i, j)), pl.BlockSpec((8, tn), lambda i, j: (i, j))],
        out_shape=[jax.ShapeDtypeStruct((M, N), BF16), jax.ShapeDtypeStruct((M // tm * 8, N), F32)],
        compiler_params=_cp(("parallel", "parallel")), name="ffn_upgate_prompt",
    )(hf, hf, w_up, w_gate, cw, cb.reshape(1, -1))


def _tile(m, pref):
    return pref if m % pref == 0 else m


def _dense_tail(x, hmix_list, wmix_list, pair, mix_epilogue, p, wts, ffn_state, T):
    (norm_ffn, norm_ple, w_up, w_gate, conv_w, conv_b, w_down, w_proj, w_pgate) = wts
    M = x.shape[0]
    tm = _tile(M, 1024)
    tn = 512
    prompt = ffn_state is None

    (x1,) = _mm(hmix_list, wmix_list, pair, [(x, "tile")], mix_epilogue, [F32], tm, tn, "mixer_out")
    hf = _rmsnorm(x1, norm_ffn, BF16, _tile(M, 256))
    if prompt:
        act, tails = _ffn_upgate_prompt(hf, w_up, w_gate, conv_w, conv_b, T, tm, tn)
        tails = tails.reshape(M // tm, 8, D_FF)
        tiles_per_seq = T // tm
        new_buf = tails[tiles_per_seq - 1::tiles_per_seq, 8 - (FFN_CONV - 1):, :]
    else:
        nseq = M // T
        h1 = jnp.repeat(ffn_state[:, FFN_CONV - 2], T, axis=0)
        h2 = jnp.stack([ffn_state[:, 0], ffn_state[:, 1]] + [ffn_state[:, 1]] * (T - 2), axis=1).reshape(M, D_FF)

        def conv_epilogue(accs, ex):
            hu, hg = accs
            b1, b2, cwv, cbv = ex
            tok = lax.broadcasted_iota(jnp.int32, hu.shape, 0) % T
            s1 = jnp.where(tok >= 1, pltpu.roll(hu, 1, 0), b1)
            s2 = jnp.where(tok >= 2, pltpu.roll(hu, 2, 0), b2)
            hc = cbv + s2 * cwv[0:1] + s1 * cwv[1:2] + hu * cwv[2:3]
            return _gelu(hc) * hg, hu

        act, hu = _mm([hf], [w_up, w_gate], [0, 0],
                      [(h1, "tile"), (h2, "tile"), (conv_w, "rows"), (conv_b.reshape(1, -1), "rows")],
                      conv_epilogue, [BF16, F32], tm, tn, "ffn_upgate_sample")
        new_buf = hu.reshape(nseq, T, D_FF)[:, T - (FFN_CONV - 1):]

    (x2,) = _mm([act], [w_down], [0], [(x1, "tile")], lambda accs, ex: (ex[0] + accs[0],), [F32],
                _tile(M, 512), tn, "ffn_down")
    hp = _rmsnorm(x2, norm_ple, BF16, _tile(M, 256))
    (x3,) = _mm([p, hp], [w_proj, w_pgate], [0, 1], [(x2, "tile")],
                lambda accs, ex: (ex[0] + accs[0] * _sigmoid(accs[1]),), [F32], tm, tn, "ple")
    return x3, new_buf


def kernel(x_prompt, x_sample, cache_cmp_k, cache_cmp_v, cache_sel_k, cache_sel_v, cache_win_k, cache_win_v, state_rglru_h, state_rglru_conv, state_s5_re, state_s5_im, state_ffn_conv, page_table, p_prompt, p_sample, norm_mix, norm_ffn, norm_ple, w_in_even, w_out_even, q_norm, k_norm_cmp, k_norm_sel, k_norm_win, cmp_pe_k, cmp_w1_k, cmp_w2_k, cmp_pe_v, cmp_w1_v, cmp_w2_v, rg_conv_w, rg_conv_b, rg_w_a, rg_b_a, rg_w_x, rg_b_x, rg_lam, s5_lam_re, s5_lam_im, s5_log_dt, s5_b_re, s5_b_im, s5_c_re, s5_c_im, s5_d, s5_w_glu_a, s5_w_glu_b, ffn_w_up, ffn_w_gate, ffn_conv_w, ffn_conv_b, ffn_w_down, ple_w_proj, ple_w_gate):
    B, T, _ = x_prompt.shape
    DB = x_sample.shape[0]
    n_pages = page_table.shape[1]
    past_len = n_pages * PAGE_SIZE
    w_buf = cache_win_k.shape[2]
    MP = B * T
    MS = DB * DEC_SEQ
    depth = norm_mix.shape[0]

    def reorder_in(w):
        q, kv, gt, xr, gr = (w[:, :Q_W], w[:, Q_W:Q_W + 6 * KV_W], w[:, Q_W + 6 * KV_W:Q_W + 6 * KV_W + 3 * N_HEADS],
                             w[:, Q_W + 6 * KV_W + 3 * N_HEADS:Q_W + 6 * KV_W + 3 * N_HEADS + D_RNN],
                             w[:, Q_W + 6 * KV_W + 3 * N_HEADS + D_RNN:])
        pad = jnp.zeros((w.shape[0], IN_PAD - COL_GT - 3 * N_HEADS), w.dtype)
        return jnp.concatenate([q, xr, gr, kv, gt, pad], axis=1).astype(BF16)

    def layer_dense_weights(li):
        return (norm_ffn[li], norm_ple[li], ffn_w_up[li].astype(BF16), ffn_w_gate[li].astype(BF16),
                ffn_conv_w[li], ffn_conv_b[li], ffn_w_down[li].astype(BF16),
                ple_w_proj[li].astype(BF16), ple_w_gate[li].astype(BF16))

    xs = {"p": x_prompt.reshape(MP, D_MODEL), "s": x_sample.reshape(MS, D_MODEL)}
    pe_in = {"p": p_prompt.reshape(depth, MP, -1).astype(BF16), "s": p_sample.reshape(depth, MS, -1).astype(BF16)}
    seq_len = {"p": T, "s": DEC_SEQ}
    ev = {"p": [], "s": []}
    od = {"p": [], "s": []}
    ff = {"p": [], "s": []}
    ptab = page_table.reshape(-1).astype(jnp.int32)

    for li in range(depth):
        dense_w = layer_dense_weights(li)
        if li % 2 == 0:
            e = li // 2
            w_in = reorder_in(w_in_even[e])
            w_out = w_out_even[e].astype(BF16)
            w_out_a, w_out_r = w_out[:Q_W], w_out[Q_W:]
            for grp in ("p", "s"):
                x = xs[grp]
                M = x.shape[0]
                L = seq_len[grp]
                nseq = M // L
                hn = _rmsnorm(x, norm_mix[li], BF16, _tile(M, 256))
                (z,) = _mm([hn], [w_in], [0], [], lambda accs, ex: (accs[0],), [F32], _tile(M, 1024), 512, "in_proj")
                if grp == "p":
                    cos_t, sin_t = _rope_tables(jnp.arange(T))
                    tmq = _tile(T, 512)
                    tab_blocks = T // tmq
                else:
                    cos_t, sin_t = _rope_tables(jnp.tile(past_len + jnp.arange(DEC_SEQ), DB))
                    tmq = M
                    tab_blocks = 1
                q, ks, kw, kvb = _qk_prep(z, cos_t, sin_t, tab_blocks, q_norm[e], k_norm_sel[e], k_norm_win[e], tmq)
                kc = z[:, COL_KV:COL_KV + KV_W]
                vc = z[:, COL_KV + KV_W:COL_KV + 2 * KV_W]
                vs = z[:, COL_KV + 3 * KV_W:COL_KV + 4 * KV_W]
                vw = z[:, COL_KV + 5 * KV_W:COL_KV + 6 * KV_W]
                gt = z[:, COL_GT:COL_GT + 3 * N_HEADS].reshape(nseq, L, 3, N_KV, GROUP)
                gt = gt.transpose(0, 3, 1, 2, 4).reshape(nseq, N_KV, L, 3 * GROUP)
                xr3 = z[:, COL_XR:COL_XR + D_RNN].reshape(nseq, L, D_RNN)

                if grp == "p":
                    n_rows = T // CMP_STRIDE
                    ccos, csin = _rope_tables(jnp.arange(n_rows) * CMP_STRIDE + CMP_STRIDE - 1)
                    ident = jnp.arange(B * (T // PAGE_SIZE), dtype=jnp.int32)
                    kcmp = _compress(kc.reshape(-1, PAGE_SIZE, KV_W), ident, B, T // PAGE_SIZE, cmp_pe_k[e],
                                     cmp_w1_k[e], cmp_w2_k[e], k_norm_cmp[e], ccos, csin, True, "compress_k_prompt")
                    vcmp = _compress(vc.reshape(-1, PAGE_SIZE, KV_W), ident, B, T // PAGE_SIZE, cmp_pe_v[e],
                                     cmp_w1_v[e], cmp_w2_v[e], k_norm_cmp[e], ccos, csin, False, "compress_v_prompt")
                    n_sel = T // SEL_BLOCK
                    ovl = _overlap_matrix(n_rows, n_sel, n_sel)
                    ex = (np.arange(n_sel)[None, :, None]
                          == (np.arange(T // Q_BLOCK)[:, None, None] * Q_BLOCK + np.arange(Q_BLOCK)[None, None, :])
                          // SEL_BLOCK)
                    o_nsa = _nsa_prompt(q.reshape(B, T, Q_W), kcmp, vcmp, kvb.reshape(B, T, 4 * KV_W), gt, ovl,
                                        jnp.asarray(ex, BF16), B, T)
                    o_nsa = o_nsa.reshape(M, Q_W)
                    y_rg, h_last = _rglru_prompt(z.reshape(B, T, IN_PAD), rg_conv_w[e], rg_conv_b[e], rg_w_a[e],
                                                 rg_b_a[e], rg_w_x[e], rg_b_x[e], rg_lam[e], B, T, _tile(T, 512))
                    y_rg = y_rg.reshape(M, D_RNN)
                    h_last = h_last.reshape(B, D_RNN)
                    new_conv = xr3[:, T - (RG_CONV - 1):]
                    new_wk = kw.reshape(B, T, N_KV, HEAD_DIM)
                    new_wv = vw.reshape(B, T, N_KV, HEAD_DIM)
                    if T >= w_buf:
                        new_wk, new_wv = new_wk[:, T - w_buf:], new_wv[:, T - w_buf:]
                    else:
                        padw = ((0, 0), (w_buf - T, 0), (0, 0), (0, 0))
                        new_wk, new_wv = jnp.pad(new_wk, padw), jnp.pad(new_wv, padw)
                else:
                    n_rows = past_len // CMP_STRIDE
                    ccos, csin = _rope_tables(jnp.arange(n_rows) * CMP_STRIDE + CMP_STRIDE - 1)
                    kcmp = _compress(cache_cmp_k[e].reshape(-1, PAGE_SIZE, KV_W), ptab, DB, n_pages, cmp_pe_k[e],
                                     cmp_w1_k[e], cmp_w2_k[e], k_norm_cmp[e], ccos, csin, True, "compress_k_sample")
                    vcmp = _compress(cache_cmp_v[e].reshape(-1, PAGE_SIZE, KV_W), ptab, DB, n_pages, cmp_pe_v[e],
                                     cmp_w1_v[e], cmp_w2_v[e], k_norm_cmp[e], ccos, csin, False, "compress_v_sample")
                    n_sel = -(-(past_len + DEC_SEQ) // SEL_BLOCK)
                    sel_cols = -(-n_sel // 128) * 128
                    ovl = _overlap_matrix(n_rows, n_sel, sel_cols)
                    keys = PAGES_PER_STEP * PAGE_SIZE
                    steps = n_pages // PAGES_PER_STEP
                    ex = (np.arange(sel_cols)[None, :, None]
                          == (np.arange(steps)[:, None, None] * keys + np.arange(keys)[None, None, :]) // SEL_BLOCK)
                    q8 = jnp.pad(q.reshape(DB, DEC_SEQ, Q_W), ((0, 0), (0, Q_PAD - DEC_SEQ), (0, 0)))
                    kvnew = jnp.pad(kvb.reshape(DB, DEC_SEQ, 4 * KV_W), ((0, 0), (0, 128 - DEC_SEQ), (0, 0)))
                    gt8 = jnp.pad(gt, ((0, 0), (0, 0), (0, Q_PAD - DEC_SEQ), (0, 0)))
                    o8 = _nsa_sample(q8, kcmp, vcmp, cache_sel_k[e].reshape(-1, PAGE_SIZE, KV_W),
                                     cache_sel_v[e].reshape(-1, PAGE_SIZE, KV_W), ptab, kvnew,
                                     cache_win_k[e].reshape(DB, w_buf, KV_W), cache_win_v[e].reshape(DB, w_buf, KV_W),
                                     gt8, ovl, jnp.asarray(ex, BF16), DB, past_len)
                    o_nsa = o8[:, :DEC_SEQ].reshape(M, Q_W)
                    gr3 = z[:, COL_GR:COL_GR + D_RNN].reshape(DB, DEC_SEQ, D_RNN)
                    y_t, h_last = _rglru_sample(xr3.transpose(1, 0, 2), gr3.transpose(1, 0, 2),
                                                state_rglru_conv[e].transpose(1, 0, 2), state_rglru_h[e],
                                                rg_conv_w[e], rg_conv_b[e], rg_w_a[e], rg_b_a[e], rg_w_x[e],
                                                rg_b_x[e], rg_lam[e])
                    y_rg = y_t.transpose(1, 0, 2).reshape(M, D_RNN)
                    new_conv = jnp.concatenate([state_rglru_conv[e], xr3], axis=1)[:, -(RG_CONV - 1):]
                    kk = jnp.concatenate([cache_win_k[e], kw.reshape(DB, DEC_SEQ, N_KV, HEAD_DIM)], axis=1)
                    vv = jnp.concatenate([cache_win_v[e], vw.reshape(DB, DEC_SEQ, N_KV, HEAD_DIM)], axis=1)
                    new_wk, new_wv = kk[:, -w_buf:], vv[:, -w_buf:]

                rs = lambda a: a.reshape(nseq, L, N_KV, HEAD_DIM)
                ev[grp].append((rs(kc), rs(vc), rs(ks), rs(vs), new_wk, new_wv, h_last, new_conv))
                xs[grp], nb = _dense_tail(
                    x, [o_nsa, y_rg], [w_out_a, w_out_r], [0, 1], lambda accs, exs: (exs[0] + accs[0] + accs[1],),
                    pe_in[grp][li], dense_w, None if grp == "p" else state_ffn_conv[li], L)
                ff[grp].append(nb)
        else:
            o = li // 2
            ar, ai, bbr, bbi = _s5_discretize(s5_lam_re[o], s5_lam_im[o], s5_log_dt[o],
                                              s5_b_re[o].transpose(2, 0, 1), s5_b_im[o].transpose(2, 0, 1))
            bw, cr, ci = _s5_block_weights(bbr, bbi, s5_c_re[o], s5_c_im[o])
            ar = ar.reshape(S5_NGB, 1, S5_LANES)
            ai = ai.reshape(S5_NGB, 1, S5_LANES)
            w_a = s5_w_glu_a[o].astype(BF16)
            w_b = s5_w_glu_b[o].astype(BF16)
            for grp in ("p", "s"):
                x = xs[grp]
                M = x.shape[0]
                L = seq_len[grp]
                hn = _rmsnorm(x, norm_mix[li], F32, _tile(M, 256))
                if grp == "p":
                    zz, sre, sim = _s5_prompt(hn.reshape(B, T, D_MODEL), bw, cr, ci, ar, ai, s5_d[o], B, T,
                                              _tile(T, 512))
                    zz = zz.reshape(M, D_MODEL)
                    sre = sre.reshape(B, S5_GROUPS, S5_STATE)
                    sim = sim.reshape(B, S5_GROUPS, S5_STATE)
                else:
                    u = hn.reshape(DB, DEC_SEQ, D_MODEL).transpose(1, 0, 2)
                    zz, sre, sim = _s5_sample(u, state_s5_re[o].reshape(DB, -1), state_s5_im[o].reshape(DB, -1),
                                              bw, cr, ci, ar, ai, s5_d[o])
                    zz = zz.transpose(1, 0, 2).reshape(M, D_MODEL)
                    sre = sre.reshape(DB, S5_GROUPS, S5_STATE)
                    sim = sim.reshape(DB, S5_GROUPS, S5_STATE)
                od[grp].append((sre, sim))
                xs[grp], nb = _dense_tail(
                    x, [zz], [w_a, w_b], [0, 0], lambda accs, exs: (exs[0] + accs[0] * _sigmoid(accs[1]),),
                    pe_in[grp][li], dense_w, None if grp == "p" else state_ffn_conv[li], L)
                ff[grp].append(nb)

    def stk(states, i):
        return jnp.stack([s[i] for s in states])

    outs = [xs["p"].reshape(B, T, D_MODEL), xs["s"].reshape(DB, DEC_SEQ, D_MODEL)]
    for i in range(8):
        outs += [stk(ev["p"], i), stk(ev["s"], i)]
    for i in range(2):
        outs += [stk(od["p"], i), stk(od["s"], i)]
    outs += [jnp.stack(ff["p"]), jnp.stack(ff["s"])]
    return tuple(outs)
```

```python
import functools

import numpy as np
import jax
import jax.numpy as jnp
from jax import lax
from jax.experimental import pallas as pl
from jax.experimental.pallas import tpu as pltpu

F32 = jnp.float32
BF16 = jnp.bfloat16

D_MODEL = 2048
DEC_SEQ = 4
PAGE_SIZE = 128
N_HEADS = 8
HEAD_DIM = 128
N_KV = 2
GROUP = N_HEADS // N_KV
ROPE_DIM = HEAD_DIM // 4
ROPE_HALF = ROPE_DIM // 2
ROPE_THETA = 500000.0
CMP_BLOCK = 32
CMP_STRIDE = 16
SEL_BLOCK = 64
SEL_TOPK = 16
N_LOCAL = 2
WINDOW = 512
Q_BLOCK = 128
FORCED_SCORE = 1e9
NEG_BIG = -1e30
ATTN_SCALE = HEAD_DIM ** -0.5
D_RNN = D_MODEL // 2
RG_BLOCKS = 8
RG_BW = D_RNN // RG_BLOCKS
RG_CONV = 4
RG_C = 8.0
S5_CH = 16
S5_GROUPS = D_MODEL // S5_CH
S5_STATE = 64
S5_GB = 8
S5_NGB = S5_GROUPS // S5_GB
S5_LANES = S5_GB * S5_STATE
S5_NSEG = 8
D_FF = ((8 * D_MODEL // 3 + 255) // 256) * 256
FFN_CONV = 3
NORM_EPS = 1e-6
Q_W = N_HEADS * HEAD_DIM
KV_W = N_KV * HEAD_DIM
PAGES_PER_STEP = 8
Q_PAD = 8
SEL_KT = 512

COL_Q = 0
COL_XR = Q_W
COL_GR = Q_W + D_RNN
COL_KV = Q_W + 2 * D_RNN
COL_GT = COL_KV + 6 * KV_W
IN_PAD = 5120


def _cp(sem, vmem_mb=48):
    return pltpu.CompilerParams(dimension_semantics=sem, vmem_limit_bytes=vmem_mb * 1024 * 1024)


def _gelu(x):
    return 0.5 * x * (1.0 + jnp.tanh(0.7978845608028654 * (x + 0.044715 * (x * x * x))))


def _sigmoid(x):
    return 1.0 / (1.0 + jnp.exp(-x))


def _dotb(a, b):
    return jnp.dot(a.astype(BF16), b.astype(BF16), preferred_element_type=F32)


def _dotf(a, b):
    return jnp.dot(a, b, preferred_element_type=F32, precision=lax.Precision.HIGHEST)


def _dot_nt(a, b):
    return lax.dot_general(a, b, (((1,), (1,)), ((), ())), preferred_element_type=F32)


def _masked_softmax(s, mask):
    s = jnp.where(mask, s, NEG_BIG)
    m = jnp.max(s, axis=-1, keepdims=True)
    e = jnp.where(mask, jnp.exp(s - m), 0.0)
    return e / jnp.maximum(jnp.sum(e, axis=-1, keepdims=True), 1e-30)


def _norm_rope(x, g, cos_t, sin_t):
    ms = jnp.mean(x * x, axis=-1, keepdims=True)
    y = x * lax.rsqrt(ms + NORM_EPS) * g
    lane = lax.broadcasted_iota(jnp.int32, y.shape, 1)
    swapped = jnp.where(lane < ROPE_HALF, pltpu.roll(y, HEAD_DIM - ROPE_HALF, 1), pltpu.roll(y, ROPE_HALF, 1))
    return y * cos_t + swapped * sin_t


def _topk_mask(score, colf, k, ncols):
    sel = jnp.zeros_like(score)
    sc = score
    for _ in range(k):
        mx = jnp.max(sc, axis=1, keepdims=True)
        cand = jnp.where(sc == mx, colf, float(ncols))
        first = jnp.min(cand, axis=1, keepdims=True)
        pick = colf == first
        sel = jnp.where(pick, 1.0, sel)
        sc = jnp.where(pick, -jnp.inf, sc)
    return sel


def _rope_tables(pos):
    inv = ROPE_THETA ** (-jnp.arange(ROPE_HALF, dtype=F32) * 2.0 / ROPE_DIM)
    ang = pos.astype(F32)[:, None] * inv
    cos = jnp.cos(ang)
    sin = jnp.sin(ang)
    n = pos.shape[0]
    ones = jnp.ones((n, HEAD_DIM - ROPE_DIM), F32)
    zeros = jnp.zeros((n, HEAD_DIM - ROPE_DIM), F32)
    return (jnp.concatenate([cos, cos, ones], axis=1), jnp.concatenate([-sin, sin, zeros], axis=1))


def _overlap_matrix(n_rows, n_sel, n_cols):
    n = np.arange(n_rows)[:, None] - 1
    j = np.arange(n_cols)[None]
    c0 = n * CMP_STRIDE
    s0 = j * SEL_BLOCK
    ov = np.minimum(c0 + CMP_BLOCK, s0 + SEL_BLOCK) - np.maximum(c0, s0)
    ov = np.maximum(ov, 0).astype(np.float32) / CMP_BLOCK
    ov = np.where((n >= 0) & (j < n_sel), ov, 0.0)
    return jnp.asarray(ov, F32)


def _rmsnorm(x, g, out_dtype, tm):
    M, D = x.shape

    def body(x_ref, g_ref, o_ref):
        xv = x_ref[...]
        ms = jnp.mean(xv * xv, axis=-1, keepdims=True)
        o_ref[...] = (xv * lax.rsqrt(ms + NORM_EPS) * g_ref[...]).astype(o_ref.dtype)

    return pl.pallas_call(
        body, grid=(M // tm,),
        in_specs=[pl.BlockSpec((tm, D), lambda i: (i, 0)), pl.BlockSpec((1, D), lambda i: (0, 0))],
        out_specs=pl.BlockSpec((tm, D), lambda i: (i, 0)),
        out_shape=jax.ShapeDtypeStruct((M, D), out_dtype),
        compiler_params=_cp(("parallel",)), name="rmsnorm")(x, g.reshape(1, D))


def _rmsnorm_segmented(x, g, B, T, tm):
    D = x.shape[1]
    sl = T // S5_NSEG
    per_seq = T // tm
    per_seg = sl // tm

    def body(x_ref, g_ref, o_ref):
        xv = x_ref[...]
        ms = jnp.mean(xv * xv, axis=-1, keepdims=True)
        o_ref[0] = xv * lax.rsqrt(ms + NORM_EPS) * g_ref[...]

    return pl.pallas_call(
        body, grid=(B * per_seq,),
        in_specs=[pl.BlockSpec((tm, D), lambda i: (i, 0)), pl.BlockSpec((1, D), lambda i: (0, 0))],
        out_specs=pl.BlockSpec((1, tm, D), lambda i: (i // per_seq, (i % per_seq) % per_seg, (i % per_seq) // per_seg)),
        out_shape=jax.ShapeDtypeStruct((B, sl, S5_NSEG * D), F32),
        compiler_params=_cp(("parallel",)), name="rmsnorm_segmented")(x, g.reshape(1, D))


def _mm(a_list, w_list, pair, extras, epilogue, out_dtypes, tm, tn, name, segmented=False):
    N = w_list[0].shape[1]
    na, nw, ne = len(a_list), len(w_list), len(extras)
    if segmented:
        M = a_list[0].shape[0] * a_list[0].shape[1] * S5_NSEG
        assert na == 1 and a_list[0].shape[1] == tm
    else:
        M = a_list[0].shape[0]

    def body(*refs):
        a_vals = [(r[0] if segmented else r[...]).astype(BF16) for r in refs[:na]]
        accs = [jnp.dot(a_vals[pair[i]], refs[na + i][...], preferred_element_type=F32) for i in range(nw)]
        ex = [r[...] for r in refs[na + nw:na + nw + ne]]
        res = epilogue(accs, ex)
        for o_ref, v in zip(refs[na + nw + ne:], res):
            o_ref[...] = v.astype(o_ref.dtype)

    if segmented:
        in_specs = [pl.BlockSpec((1, tm, w_list[0].shape[0]), lambda i, j: (i // S5_NSEG, 0, i % S5_NSEG))]
    else:
        in_specs = [pl.BlockSpec((tm, a.shape[1]), lambda i, j: (i, 0)) for a in a_list]
    in_specs += [pl.BlockSpec((w.shape[0], tn), lambda i, j: (0, j)) for w in w_list]
    for arr, kind in extras:
        if kind == "tile":
            in_specs.append(pl.BlockSpec((tm, tn), lambda i, j: (i, j)))
        else:
            in_specs.append(pl.BlockSpec((arr.shape[0], tn), lambda i, j: (0, j)))
    return pl.pallas_call(
        body, grid=(M // tm, N // tn), in_specs=in_specs,
        out_specs=[pl.BlockSpec((tm, tn), lambda i, j: (i, j)) for _ in out_dtypes],
        out_shape=[jax.ShapeDtypeStruct((M, N), dt) for dt in out_dtypes],
        compiler_params=_cp(("parallel", "parallel")), name=name,
    )(*a_list, *w_list, *[e[0] for e in extras])


def _qk_prep(z, cos_t, sin_t, tab_blocks, qn, kns, knw, tm):
    M = z.shape[0]

    def body(zq_ref, zkv_ref, c_ref, s_ref, qn_ref, kns_ref, knw_ref, q_ref, ks_ref, kw_ref, kvb_ref):
        c = c_ref[...]
        s = s_ref[...]
        for h in range(N_HEADS):
            sl = slice(h * HEAD_DIM, (h + 1) * HEAD_DIM)
            q_ref[:, sl] = _norm_rope(zq_ref[:, sl], qn_ref[...], c, s).astype(BF16)
        for kv in range(N_KV):
            sl = slice(kv * HEAD_DIM, (kv + 1) * HEAD_DIM)
            ks = _norm_rope(zkv_ref[:, 2 * KV_W + kv * HEAD_DIM:2 * KV_W + (kv + 1) * HEAD_DIM], kns_ref[...], c, s)
            kw = _norm_rope(zkv_ref[:, 4 * KV_W + kv * HEAD_DIM:4 * KV_W + (kv + 1) * HEAD_DIM], knw_ref[...], c, s)
            ks_ref[:, sl] = ks
            kw_ref[:, sl] = kw
            kvb_ref[:, sl] = ks.astype(BF16)
            kvb_ref[:, 2 * KV_W + kv * HEAD_DIM:2 * KV_W + (kv + 1) * HEAD_DIM] = kw.astype(BF16)
        kvb_ref[:, KV_W:2 * KV_W] = zkv_ref[:, 3 * KV_W:4 * KV_W].astype(BF16)
        kvb_ref[:, 3 * KV_W:4 * KV_W] = zkv_ref[:, 5 * KV_W:6 * KV_W].astype(BF16)

    vec = pl.BlockSpec((1, HEAD_DIM), lambda i: (0, 0))
    tab = pl.BlockSpec((tm, HEAD_DIM), lambda i: (i % tab_blocks, 0))
    return pl.pallas_call(
        body, grid=(M // tm,),
        in_specs=[pl.BlockSpec((tm, Q_W), lambda i: (i, COL_Q // Q_W)),
                  pl.BlockSpec((tm, 6 * KV_W), lambda i: (i, COL_KV // (6 * KV_W))),
                  tab, tab, vec, vec, vec],
        out_specs=[pl.BlockSpec((tm, Q_W), lambda i: (i, 0)),
                   pl.BlockSpec((tm, KV_W), lambda i: (i, 0)),
                   pl.BlockSpec((tm, KV_W), lambda i: (i, 0)),
                   pl.BlockSpec((tm, 4 * KV_W), lambda i: (i, 0))],
        out_shape=[jax.ShapeDtypeStruct((M, Q_W), BF16), jax.ShapeDtypeStruct((M, KV_W), F32),
                   jax.ShapeDtypeStruct((M, KV_W), F32), jax.ShapeDtypeStruct((M, 4 * KV_W), BF16)],
        compiler_params=_cp(("parallel",)), name="qk_prep",
    )(z, z, cos_t, sin_t, qn.reshape(1, -1), kns.reshape(1, -1), knw.reshape(1, -1))


def _compress(pages, ptab, n_seq, pages_per_seq, pe, w1, w2, norm_g, cos_t, sin_t, do_norm, name):
    G = PAGES_PER_STEP
    steps = pages_per_seq // G
    cpp = PAGE_SIZE // CMP_STRIDE
    ch = G * cpp
    n_chunk = pages_per_seq * cpp

    def body(pt_ref, *refs):
        page_refs = (refs[:G], refs[G:2 * G])
        pe_ref, w1_ref, w2_ref, g_ref, c_ref, s_ref, o_ref, carry_ref = refs[2 * G:]
        step = pl.program_id(1)

        @pl.when(step == 0)
        def _():
            carry_ref[...] = jnp.zeros_like(carry_ref)

        row = lax.broadcasted_iota(jnp.int32, (ch, HEAD_DIM), 0)
        for kv in range(N_KV):
            lo = jnp.zeros((ch, HEAD_DIM), F32)
            hi = jnp.zeros((ch, HEAD_DIM), F32)
            for c in range(CMP_STRIDE):
                xc = jnp.concatenate([pr[0, pl.ds(c, cpp, stride=CMP_STRIDE), :] for pr in page_refs[kv]], axis=0)
                lo = lo + _dotb(xc + pe_ref[c:c + 1, :], w1_ref[c])
                hi = hi + _dotb(xc + pe_ref[CMP_STRIDE + c:CMP_STRIDE + c + 1, :], w1_ref[CMP_STRIDE + c])
            lo_prev = jnp.where(row == 0, carry_ref[kv, 0:1, :], pltpu.roll(lo, 1, 0))
            carry_ref[kv, 0:1, :] = lo[ch - 1:ch, :]
            out = _dotb(_gelu(lo_prev + hi), w2_ref[...])
            if do_norm:
                out = _norm_rope(out, g_ref[...], c_ref[...], s_ref[...])
            o_ref[0, kv] = out.astype(BF16)

    def page_map(j, kv, s, g, pt):
        return (pt[s * pages_per_seq + g * G + j], 0, kv)

    full2 = lambda s, g, pt: (0, 0)
    in_specs = [pl.BlockSpec((1, PAGE_SIZE, HEAD_DIM), functools.partial(page_map, j, kv))
                for kv in range(N_KV) for j in range(G)]
    in_specs += [pl.BlockSpec((CMP_BLOCK, HEAD_DIM), full2),
                 pl.BlockSpec((CMP_BLOCK, HEAD_DIM, HEAD_DIM), lambda s, g, pt: (0, 0, 0)),
                 pl.BlockSpec((HEAD_DIM, HEAD_DIM), full2),
                 pl.BlockSpec((1, HEAD_DIM), full2),
                 pl.BlockSpec((ch, HEAD_DIM), lambda s, g, pt: (g, 0)),
                 pl.BlockSpec((ch, HEAD_DIM), lambda s, g, pt: (g, 0))]
    gs = pltpu.PrefetchScalarGridSpec(
        num_scalar_prefetch=1, grid=(n_seq, steps), in_specs=in_specs,
        out_specs=pl.BlockSpec((1, N_KV, ch, HEAD_DIM), lambda s, g, pt: (s, 0, g, 0)),
        scratch_shapes=[pltpu.VMEM((N_KV, 8, HEAD_DIM), F32)])
    return pl.pallas_call(
        body, grid_spec=gs, out_shape=jax.ShapeDtypeStruct((n_seq, N_KV, n_chunk, HEAD_DIM), BF16),
        compiler_params=_cp(("parallel", "arbitrary")), name=name,
    )(ptab, *([pages] * (N_KV * G)), pe, w1.astype(BF16), w2.astype(BF16), norm_g.reshape(1, -1), cos_t, sin_t)


def _nsa_prompt(q_t, kcmp, vcmp_t, kvb, vs_t, vw_t, gates_t, ovl_t, expand_t, B, T):
    nqb = T // Q_BLOCK
    n_cmp_rows = T // CMP_STRIDE
    n_sel = T // SEL_BLOCK
    cols = GROUP * Q_BLOCK
    win_tiles = WINDOW // Q_BLOCK
    win_keys = (win_tiles + 1) * Q_BLOCK
    assert T % SEL_KT == 0 and T >= win_keys
    topk = min(SEL_TOPK, n_sel)

    def body(q_ref, kc_ref, vc_ref, ks_ref, vs_ref, kw_ref, vw_ref, gt_ref, ovl_ref, ex_ref, o_ref):
        i = pl.program_id(2)
        qt = q_ref[0, 0, 0]
        pos = i * Q_BLOCK + (lax.broadcasted_iota(jnp.int32, (1, cols), 1) & (Q_BLOCK - 1))

        s = jnp.dot(kc_ref[0, 0], qt, preferred_element_type=F32) * ATTN_SCALE
        crow = lax.broadcasted_iota(jnp.int32, (n_cmp_rows, cols), 0)
        cmask = (crow >= 1) & (crow * CMP_STRIDE + (CMP_BLOCK - 1 - CMP_STRIDE) <= pos)
        s = jnp.where(cmask, s, NEG_BIG)
        e = jnp.where(cmask, jnp.exp(s - jnp.max(s, axis=0, keepdims=True)), 0.0)
        p = e / jnp.maximum(jnp.sum(e, axis=0, keepdims=True), 1e-30)
        o_cmp = _dotb(vc_ref[0, 0], p)
        imp = p[:, 0:Q_BLOCK]
        for h in range(1, GROUP):
            imp = imp + p[:, h * Q_BLOCK:(h + 1) * Q_BLOCK]

        score = _dotf(ovl_ref[...], imp)
        j = lax.broadcasted_iota(jnp.int32, (n_sel, Q_BLOCK), 0)
        qblk = (i * Q_BLOCK + lax.broadcasted_iota(jnp.int32, (n_sel, Q_BLOCK), 1)) // SEL_BLOCK
        forced = (j == 0) | ((j <= qblk) & (j > qblk - N_LOCAL))
        score = jnp.where(forced, FORCED_SCORE, jnp.where(j > qblk, NEG_BIG, score))
        rank = jnp.zeros((n_sel, Q_BLOCK), F32)
        for r in range(n_sel):
            sr = score[r:r + 1, :]
            beats = (sr > score) | ((sr == score) & (j > r))
            rank = rank + jnp.where(beats, 1.0, 0.0)
        sel = jnp.where(rank < topk, 1.0, 0.0).astype(BF16)

        krow = lax.broadcasted_iota(jnp.int32, (SEL_KT, cols), 0)

        def sel_tile(u, carry):
            m, l, acc = carry
            k = ks_ref[0, pl.ds(pl.multiple_of(u * SEL_KT, SEL_KT), SEL_KT), :]
            sc = jnp.dot(k, qt, preferred_element_type=F32) * ATTN_SCALE
            m1 = _dotb(ex_ref[u], sel)
            mask = (jnp.concatenate([m1] * GROUP, axis=1) > 0.5) & (u * SEL_KT + krow <= pos)
            sc = jnp.where(mask, sc, NEG_BIG)
            m_new = jnp.maximum(m, jnp.max(sc, axis=0, keepdims=True))
            alpha = jnp.exp(m - m_new)
            ex = jnp.where(mask, jnp.exp(sc - m_new), 0.0)
            l = alpha * l + jnp.sum(ex, axis=0, keepdims=True)
            acc = alpha * acc + _dotb(vs_ref[0, 0, u], ex)
            return m_new, l, acc

        init = (jnp.full((1, cols), NEG_BIG, F32), jnp.zeros((1, cols), F32), jnp.zeros((HEAD_DIM, cols), F32))
        _, l_sel, acc_sel = lax.fori_loop(0, (i * Q_BLOCK + Q_BLOCK + SEL_KT - 1) // SEL_KT, sel_tile, init)
        o_sel = acc_sel / jnp.maximum(l_sel, 1e-30)

        t0 = jnp.maximum(i - win_tiles, 0)
        kwin = kw_ref[0, pl.ds(pl.multiple_of(t0 * Q_BLOCK, Q_BLOCK), win_keys), :]
        sw = jnp.dot(kwin, qt, preferred_element_type=F32) * ATTN_SCALE
        kpos = t0 * Q_BLOCK + lax.broadcasted_iota(jnp.int32, (win_keys, cols), 0)
        wmask = (kpos <= pos) & (kpos > pos - WINDOW)
        sw = jnp.where(wmask, sw, NEG_BIG)
        ew = jnp.where(wmask, jnp.exp(sw - jnp.max(sw, axis=0, keepdims=True)), 0.0)
        pw = ew / jnp.maximum(jnp.sum(ew, axis=0, keepdims=True), 1e-30)
        o_win = jnp.zeros((HEAD_DIM, cols), F32)
        for t in range(win_tiles + 1):
            o_win = o_win + _dotb(vw_ref[0, 0, t0 + t], pw[t * Q_BLOCK:(t + 1) * Q_BLOCK])

        g = _sigmoid(gt_ref[0, 0])
        for h in range(GROUP):
            c = slice(h * Q_BLOCK, (h + 1) * Q_BLOCK)
            o = (g[h:h + 1, :] * o_cmp[:, c] + g[GROUP + h:GROUP + h + 1, :] * o_sel[:, c]
                 + g[2 * GROUP + h:2 * GROUP + h + 1, :] * o_win[:, c])
            o_ref[0, 0, 0, :, c] = o.astype(BF16)

    def k_spec(off):
        return pl.BlockSpec((1, T, HEAD_DIM), lambda b, kv, i: (b, 0, off + kv))

    vs_spec = pl.BlockSpec((1, 1, T // SEL_KT, HEAD_DIM, SEL_KT), lambda b, kv, i: (b, kv, 0, 0, 0))
    vw_spec = pl.BlockSpec((1, 1, nqb, HEAD_DIM, Q_BLOCK), lambda b, kv, i: (b, kv, 0, 0, 0))
    qo_spec = pl.BlockSpec((1, 1, 1, HEAD_DIM, cols), lambda b, kv, i: (b, kv, i, 0, 0))
    return pl.pallas_call(
        body, grid=(B, N_KV, nqb),
        in_specs=[qo_spec,
                  pl.BlockSpec((1, 1, n_cmp_rows, HEAD_DIM), lambda b, kv, i: (b, kv, 0, 0)),
                  pl.BlockSpec((1, 1, HEAD_DIM, n_cmp_rows), lambda b, kv, i: (b, kv, 0, 0)),
                  k_spec(0), vs_spec, k_spec(4), vw_spec,
                  pl.BlockSpec((1, 1, 3 * GROUP, Q_BLOCK), lambda b, kv, i: (b, kv, 0, i)),
                  pl.BlockSpec((n_sel, n_cmp_rows), lambda b, kv, i: (0, 0)),
                  pl.BlockSpec((T // SEL_KT, SEL_KT, n_sel), lambda b, kv, i: (0, 0, 0))],
        out_specs=qo_spec,
        out_shape=jax.ShapeDtypeStruct((B, N_KV, nqb, HEAD_DIM, cols), BF16),
        compiler_params=_cp(("parallel", "parallel", "arbitrary")), name="nsa_prompt",
    )(q_t, kcmp, vcmp_t, kvb, vs_t, kvb, vw_t, gates_t, ovl_t, expand_t)


def _nsa_sample(q, kcmp, vcmp, pool_k, pool_v, ptab, kvnew, win_k, win_v, win_off, gates, ovl, expand, DB, past_len):
    G = PAGES_PER_STEP
    n_pages = past_len // PAGE_SIZE
    steps = n_pages // G
    n_cmp_rows = kcmp.shape[2]
    sel_cols = ovl.shape[1]
    n_past_blk = past_len // SEL_BLOCK
    n_sel = -(-(past_len + DEC_SEQ) // SEL_BLOCK)
    w_buf = win_k.shape[1]
    rows = GROUP * Q_PAD
    keys = G * PAGE_SIZE
    new_rows = kvnew.shape[1]

    def body(pt_ref, q_ref, kc_ref, vc_ref, ovl_ref, *rest):
        pk = rest[:G]
        pv = rest[G:2 * G]
        (ex_ref, kvn_ref, wk_ref, wv_ref, gt_ref, o_ref,
         sel_scr, m_scr, l_scr, acc_scr, base_scr) = rest[2 * G:]
        step = pl.program_id(1)
        tok = lax.broadcasted_iota(jnp.int32, (rows, 1), 0) & (Q_PAD - 1)
        q8 = q_ref[0]

        def q_rows(kv):
            return jnp.concatenate(
                [q8[:, (kv * GROUP + h) * HEAD_DIM:(kv * GROUP + h + 1) * HEAD_DIM] for h in range(GROUP)], axis=0)

        def head_col(gt, c0):
            return jnp.concatenate(
                [jnp.broadcast_to(gt[:, c0 + h:c0 + h + 1], (Q_PAD, HEAD_DIM)) for h in range(GROUP)], axis=0)

        @pl.when(step == 0)
        def _():
            for kv in range(N_KV):
                qh = q_rows(kv)
                s = _dot_nt(qh, kc_ref[0, kv]) * ATTN_SCALE
                col = lax.broadcasted_iota(jnp.int32, (rows, n_cmp_rows), 1)
                cmask = (col >= 1) & (col * CMP_STRIDE + (CMP_BLOCK - 1 - CMP_STRIDE) <= past_len + tok)
                p = _masked_softmax(s, cmask)
                o_cmp = _dotb(p, vc_ref[0, kv])
                imp = p[0:Q_PAD]
                for h in range(1, GROUP):
                    imp = imp + p[h * Q_PAD:(h + 1) * Q_PAD]
                score = _dotf(imp, ovl_ref[...])
                j = lax.broadcasted_iota(jnp.int32, (Q_PAD, sel_cols), 1)
                qblk = (past_len + lax.broadcasted_iota(jnp.int32, (Q_PAD, sel_cols), 0)) // SEL_BLOCK
                forced = (j == 0) | ((j <= qblk) & (j > qblk - N_LOCAL))
                score = jnp.where(forced, FORCED_SCORE, jnp.where(j > qblk, NEG_BIG, score))
                score = jnp.where(j < n_sel, score, -jnp.inf)
                sel = _topk_mask(score, j.astype(F32), min(SEL_TOPK, n_sel), sel_cols)
                sel4 = jnp.concatenate([sel] * GROUP, axis=0)
                sel_scr[kv] = sel4.astype(BF16)

                kb = wk_ref[0, :, kv * HEAD_DIM:(kv + 1) * HEAD_DIM].astype(BF16)
                vb = wv_ref[0, :, kv * HEAD_DIM:(kv + 1) * HEAD_DIM].astype(BF16)
                kn = kvn_ref[0, :, (4 + kv) * HEAD_DIM:(5 + kv) * HEAD_DIM]
                vn = kvn_ref[0, :, (6 + kv) * HEAD_DIM:(7 + kv) * HEAD_DIM]
                s1 = jnp.where(lax.broadcasted_iota(jnp.int32, (rows, w_buf), 1) - w_buf > tok - WINDOW,
                               _dot_nt(qh, kb) * ATTN_SCALE, NEG_BIG)
                ncol = lax.broadcasted_iota(jnp.int32, (rows, new_rows), 1)
                nmask = (ncol <= tok) & (ncol < DEC_SEQ)
                s2 = jnp.where(nmask, _dot_nt(qh, kn) * ATTN_SCALE, NEG_BIG)
                m = jnp.maximum(jnp.max(s1, axis=-1, keepdims=True), jnp.max(s2, axis=-1, keepdims=True))
                e1 = jnp.where(s1 > 0.5 * NEG_BIG, jnp.exp(s1 - m), 0.0)
                e2 = jnp.where(nmask, jnp.exp(s2 - m), 0.0)
                den = jnp.sum(e1, axis=-1, keepdims=True) + jnp.sum(e2, axis=-1, keepdims=True)
                o_win = (_dotb(e1, vb) + _dotb(e2, vn)) / jnp.maximum(den, 1e-30)

                gt = _sigmoid(gt_ref[0, kv])
                base_scr[kv] = head_col(gt, 0) * o_cmp + head_col(gt, 2 * GROUP) * o_win

                ksn = kvn_ref[0, :, kv * HEAD_DIM:(kv + 1) * HEAD_DIM]
                vsn = kvn_ref[0, :, (2 + kv) * HEAD_DIM:(3 + kv) * HEAD_DIM]
                chosen = sel4[:, n_past_blk:n_past_blk + 1] > 0.5
                smask = nmask & chosen
                s3 = jnp.where(smask, _dot_nt(qh, ksn) * ATTN_SCALE, NEG_BIG)
                m0 = jnp.max(s3, axis=-1, keepdims=True)
                e3 = jnp.where(smask, jnp.exp(s3 - m0), 0.0)
                m_scr[kv] = jnp.broadcast_to(m0, (rows, HEAD_DIM))
                l_scr[kv] = jnp.broadcast_to(jnp.sum(e3, axis=-1, keepdims=True), (rows, HEAD_DIM))
                acc_scr[kv] = _dotb(e3, vsn)

        for kv in range(N_KV):
            qh = q_rows(kv)
            k = jnp.concatenate([r[0, :, kv * HEAD_DIM:(kv + 1) * HEAD_DIM] for r in pk], axis=0).astype(BF16)
            v = jnp.concatenate([r[0, :, kv * HEAD_DIM:(kv + 1) * HEAD_DIM] for r in pv], axis=0).astype(BF16)
            mask = _dotb(sel_scr[kv], ex_ref[0]) > 0.5
            sc = jnp.where(mask, _dot_nt(qh, k) * ATTN_SCALE, NEG_BIG)
            m_old = m_scr[kv][:, 0:1]
            l_old = l_scr[kv][:, 0:1]
            m_new = jnp.maximum(m_old, jnp.max(sc, axis=-1, keepdims=True))
            alpha = jnp.exp(m_old - m_new)
            e = jnp.where(mask, jnp.exp(sc - m_new), 0.0)
            l_new = alpha * l_old + jnp.sum(e, axis=-1, keepdims=True)
            acc = alpha * acc_scr[kv] + _dotb(e, v)
            m_scr[kv] = jnp.broadcast_to(m_new, (rows, HEAD_DIM))
            l_scr[kv] = jnp.broadcast_to(l_new, (rows, HEAD_DIM))
            acc_scr[kv] = acc

        @pl.when(step == steps - 1)
        def _():
            for kv in range(N_KV):
                gt = _sigmoid(gt_ref[0, kv])
                o_sel = acc_scr[kv] / jnp.maximum(l_scr[kv], 1e-30)
                o = base_scr[kv] + head_col(gt, GROUP) * o_sel
                for h in range(GROUP):
                    hh = kv * GROUP + h
                    o_ref[0, :, hh * HEAD_DIM:(hh + 1) * HEAD_DIM] = o[h * Q_PAD:(h + 1) * Q_PAD].astype(BF16)

    def page_map(jj, b, g, pt):
        return (pt[b * n_pages + g * G + jj], 0, 0)

    page_specs = [pl.BlockSpec((1, PAGE_SIZE, KV_W), functools.partial(page_map, jj)) for jj in range(G)]
    cmp_spec = pl.BlockSpec((1, N_KV, n_cmp_rows, HEAD_DIM), lambda b, g, pt: (b, 0, 0, 0))
    in_specs = [pl.BlockSpec((1, Q_PAD, Q_W), lambda b, g, pt: (b, 0, 0)), cmp_spec, cmp_spec,
                pl.BlockSpec((n_cmp_rows, sel_cols), lambda b, g, pt: (0, 0))]
    in_specs += page_specs + page_specs
    in_specs += [pl.BlockSpec((1, sel_cols, keys), lambda b, g, pt: (g, 0, 0)),
                 pl.BlockSpec((1, new_rows, 4 * KV_W), lambda b, g, pt: (b, 0, 0)),
                 pl.BlockSpec((1, w_buf, KV_W), lambda b, g, pt: (win_off + b, 0, 0)),
                 pl.BlockSpec((1, w_buf, KV_W), lambda b, g, pt: (win_off + b, 0, 0)),
                 pl.BlockSpec((1, N_KV, Q_PAD, 3 * GROUP), lambda b, g, pt: (b, 0, 0, 0))]
    gs = pltpu.PrefetchScalarGridSpec(
        num_scalar_prefetch=1, grid=(DB, steps), in_specs=in_specs,
        out_specs=pl.BlockSpec((1, Q_PAD, Q_W), lambda b, g, pt: (b, 0, 0)),
        scratch_shapes=[pltpu.VMEM((N_KV, rows, sel_cols), BF16), pltpu.VMEM((N_KV, rows, HEAD_DIM), F32),
                        pltpu.VMEM((N_KV, rows, HEAD_DIM), F32), pltpu.VMEM((N_KV, rows, HEAD_DIM), F32),
                        pltpu.VMEM((N_KV, rows, HEAD_DIM), F32)])
    return pl.pallas_call(
        body, grid_spec=gs, out_shape=jax.ShapeDtypeStruct((DB, Q_PAD, Q_W), BF16),
        compiler_params=_cp(("parallel", "arbitrary")), name="nsa_sample",
    )(ptab, q, kcmp, vcmp, ovl, *([pool_k] * G), *([pool_v] * G), expand, kvnew, win_k, win_v, gates)


def _rg_gates(u, wa_ref, ba, wx_ref, bx, sp):
    ra = []
    rx = []
    for n in range(RG_BLOCKS):
        ub = u[:, n * RG_BW:(n + 1) * RG_BW]
        ra.append(_dotf(ub, wa_ref[n]))
        rx.append(_dotf(ub, wx_ref[n]))
    r = _sigmoid(jnp.concatenate(ra, axis=1) + ba)
    i = _sigmoid(jnp.concatenate(rx, axis=1) + bx)
    log_a = -RG_C * r * sp
    a = jnp.exp(log_a)
    b = jnp.sqrt(1.0 - jnp.exp(2.0 * log_a)) * i * u
    return a, b


def _softplus_neg(lam):
    return jnp.maximum(-lam, 0.0) + jnp.log(1.0 + jnp.exp(-jnp.abs(lam)))


def _rglru_prompt(z, cw, cb, wa, ba, wx, bx, lam, B, T, tt):
    keep = 8

    def body(xr_ref, gr_ref, cw_ref, cb_ref, wa_ref, ba_ref, wx_ref, bx_ref, lam_ref,
             y_ref, hl_ref, a_scr, b_scr, h_scr, tail_scr):
        t = pl.program_id(1)

        @pl.when(t == 0)
        def _():
            h_scr[...] = jnp.zeros_like(h_scr)
            tail_scr[...] = jnp.zeros_like(tail_scr)

        x = xr_ref[0]
        row = lax.broadcasted_iota(jnp.int32, (tt, D_RNN), 0)
        cwv = cw_ref[...]
        u = cb_ref[...] + x * cwv[RG_CONV - 1:RG_CONV]
        for d in range(1, RG_CONV):
            xs = pltpu.roll(x, d, 0)
            for r in range(d):
                xs = jnp.where(row == r, tail_scr[keep - d + r:keep - d + r + 1, :], xs)
            u = u + xs * cwv[RG_CONV - 1 - d:RG_CONV - d]
        tail_scr[...] = x[tt - keep:tt]
        a, b = _rg_gates(u, wa_ref, ba_ref[...], wx_ref, bx_ref[...], _softplus_neg(lam_ref[...]))
        a_scr[...] = a
        b_scr[...] = b

        def step(s, h):
            h = a_scr[pl.ds(s, 1), :] * h + b_scr[pl.ds(s, 1), :]
            b_scr[pl.ds(s, 1), :] = h
            return h

        h = lax.fori_loop(0, tt, step, h_scr[0:1, :], unroll=8)
        h_scr[0:1, :] = h
        hl_ref[0] = h
        y_ref[0] = (b_scr[...] * _gelu(gr_ref[0])).astype(BF16)

    vec = pl.BlockSpec((1, D_RNN), lambda b, t: (0, 0))
    wspec = pl.BlockSpec((RG_BLOCKS, RG_BW, RG_BW), lambda b, t: (0, 0, 0))
    return pl.pallas_call(
        body, grid=(B, T // tt),
        in_specs=[pl.BlockSpec((1, tt, D_RNN), lambda b, t: (b, t, COL_XR // D_RNN)),
                  pl.BlockSpec((1, tt, D_RNN), lambda b, t: (b, t, COL_GR // D_RNN)),
                  pl.BlockSpec((RG_CONV, D_RNN), lambda b, t: (0, 0)), vec, wspec, vec, wspec, vec, vec],
        out_specs=[pl.BlockSpec((1, tt, D_RNN), lambda b, t: (b, t, 0)),
                   pl.BlockSpec((1, 1, D_RNN), lambda b, t: (b, 0, 0))],
        out_shape=[jax.ShapeDtypeStruct((B, T, D_RNN), BF16), jax.ShapeDtypeStruct((B, 1, D_RNN), F32)],
        scratch_shapes=[pltpu.VMEM((tt, D_RNN), F32), pltpu.VMEM((tt, D_RNN), F32),
                        pltpu.VMEM((8, D_RNN), F32), pltpu.VMEM((keep, D_RNN), F32)],
        compiler_params=_cp(("parallel", "arbitrary")), name="rglru_prompt",
    )(z, z, cw, cb.reshape(1, -1), wa, ba.reshape(1, -1), wx, bx.reshape(1, -1), lam.reshape(1, -1))


def _rglru_sample(xr, gr, buf, h0, cw, cb, wa, ba, wx, bx, lam):
    DB = h0.shape[0]

    def body(xr_ref, gr_ref, buf_ref, h0_ref, cw_ref, cb_ref, wa_ref, ba_ref, wx_ref, bx_ref, lam_ref, y_ref, hl_ref):
        xp = [buf_ref[k] for k in range(RG_CONV - 1)] + [xr_ref[t] for t in range(DEC_SEQ)]
        cwv = cw_ref[...]
        sp = _softplus_neg(lam_ref[...])
        h = h0_ref[...]
        for t in range(DEC_SEQ):
            u = cb_ref[...]
            for k in range(RG_CONV):
                u = u + xp[t + k] * cwv[k:k + 1]
            a, b = _rg_gates(u, wa_ref, ba_ref[...], wx_ref, bx_ref[...], sp)
            h = a * h + b
            y_ref[t] = (h * _gelu(gr_ref[t])).astype(BF16)
        hl_ref[...] = h

    return pl.pallas_call(
        body, out_shape=[jax.ShapeDtypeStruct((DEC_SEQ, DB, D_RNN), BF16), jax.ShapeDtypeStruct((DB, D_RNN), F32)],
        compiler_params=pltpu.CompilerParams(vmem_limit_bytes=32 * 1024 * 1024), name="rglru_sample",
    )(xr, gr, buf, h0, cw, cb.reshape(1, -1), wa, ba.reshape(1, -1), wx, bx.reshape(1, -1), lam.reshape(1, -1))


def _s5_discretize(lam_re, lam_im, log_dt, b_re_t, b_im_t):
    def body(lr_ref, li_ref, ldt_ref, br_ref, bi_ref, ar_ref, ai_ref, bbr_ref, bbi_ref):
        lr = lr_ref[...]
        li = li_ref[...]
        dt = jnp.exp(ldt_ref[...])
        mag = jnp.exp(lr * dt)
        ar = mag * jnp.cos(li * dt)
        ai = mag * jnp.sin(li * dt)
        den = lr * lr + li * li
        n_re = ar - 1.0
        f_re = (n_re * lr + ai * li) / den
        f_im = (ai * lr - n_re * li) / den
        ar_ref[...] = ar
        ai_ref[...] = ai
        for c in range(S5_CH):
            bbr_ref[c] = f_re * br_ref[c] - f_im * bi_ref[c]
            bbi_ref[c] = f_re * bi_ref[c] + f_im * br_ref[c]

    gp = jax.ShapeDtypeStruct(lam_re.shape, F32)
    cgp = jax.ShapeDtypeStruct(b_re_t.shape, F32)
    return pl.pallas_call(body, out_shape=[gp, gp, cgp, cgp], name="s5_discretize")(
        lam_re, lam_im, log_dt.reshape(-1, 1), b_re_t, b_im_t)


def _s5_block_weights(bbr, bbi, c_re, c_im):
    eye = jnp.eye(S5_GB, dtype=F32)

    def bblk(bb):
        x = bb.reshape(S5_CH, S5_NGB, S5_GB, S5_STATE)
        x = jnp.einsum("cngp,gh->ngchp", x, eye)
        return x.reshape(S5_NGB, S5_GB * S5_CH, S5_LANES)

    def cblk(c):
        x = c.reshape(S5_NGB, S5_GB, S5_CH, S5_STATE)
        x = jnp.einsum("ngcp,gh->ngphc", x, eye)
        return x.reshape(S5_NGB, S5_LANES, S5_GB * S5_CH)

    return jnp.concatenate([bblk(bbr), bblk(bbi)], axis=2), cblk(c_re), cblk(c_im)


def _s5_prompt(u, bw, cr, ci, ar, ai, d, B, T):
    L = S5_LANES
    lanes = S5_GB * S5_CH
    nseg = S5_NSEG
    sl = T // nseg
    assert sl & (sl - 1) == 0, "segment length must be a power of two"
    chs = min(64, sl)
    ch = chs * nseg
    nchunks = sl // chs

    def body(u_ref, bw_ref, cr_ref, ci_ref, ar_ref, ai_ref, d_ref, zz_ref, sre_ref, sim_ref, h_scr):
        a_re = ar_ref[0]
        a_im = ai_ref[0]
        ar8 = jnp.broadcast_to(a_re, (nseg, L))
        ai8 = jnp.broadcast_to(a_im, (nseg, L))

        def project(c, carry):
            s0 = pl.multiple_of(c * chs, chs)
            uc = u_ref[0, pl.ds(s0, chs)].reshape(ch, lanes)
            h_scr[pl.ds(pl.multiple_of(c * ch, ch), ch), :] = _dotb(uc, bw_ref[0])
            return carry

        lax.fori_loop(0, nchunks, project, 0)

        def advance(s, carry, store):
            hr, hi = carry
            rows = pl.ds(pl.multiple_of(s * nseg, nseg), nseg)
            bu = h_scr[rows, :]
            nr = ar8 * hr - ai8 * hi + bu[:, :L]
            ni = ar8 * hi + ai8 * hr + bu[:, L:]
            if store:
                h_scr[rows, :] = jnp.concatenate([nr, ni], axis=1)
            return nr, ni

        zero = jnp.zeros((nseg, L), F32)
        er, ei = lax.fori_loop(0, sl, functools.partial(advance, store=False), (zero, zero), unroll=8)

        pr, pi = a_re, a_im
        for _ in range(sl.bit_length() - 1):
            pr, pi = pr * pr - pi * pi, 2.0 * pr * pi
        sr = [jnp.zeros((1, L), F32)]
        si = [jnp.zeros((1, L), F32)]
        for k in range(nseg):
            sr.append(er[k:k + 1] + pr * sr[k] - pi * si[k])
            si.append(ei[k:k + 1] + pr * si[k] + pi * sr[k])
        sre_ref[0, 0] = sr[nseg]
        sim_ref[0, 0] = si[nseg]
        start = (jnp.concatenate(sr[:nseg], axis=0), jnp.concatenate(si[:nseg], axis=0))
        lax.fori_loop(0, sl, functools.partial(advance, store=True), start, unroll=8)

        def emit(c, carry):
            s0 = pl.multiple_of(c * chs, chs)
            hc = h_scr[pl.ds(pl.multiple_of(c * ch, ch), ch), :]
            uc = u_ref[0, pl.ds(s0, chs)].reshape(ch, lanes)
            y = _dotb(hc[:, :L], cr_ref[0]) - _dotb(hc[:, L:], ci_ref[0]) + d_ref[...] * uc
            zz_ref[0, pl.ds(s0, chs)] = _gelu(y).reshape(chs, nseg, lanes)
            return carry

        lax.fori_loop(0, nchunks, emit, 0)

    st_spec = pl.BlockSpec((1, 1, 1, L), lambda b, g: (b, g, 0, 0))
    st_shape = jax.ShapeDtypeStruct((B, S5_NGB, 1, L), F32)
    io_spec = pl.BlockSpec((1, sl, nseg, lanes), lambda b, g: (b, 0, 0, g))
    return pl.pallas_call(
        body, grid=(B, S5_NGB),
        in_specs=[io_spec,
                  pl.BlockSpec((1, lanes, 2 * L), lambda b, g: (g, 0, 0)),
                  pl.BlockSpec((1, L, lanes), lambda b, g: (g, 0, 0)),
                  pl.BlockSpec((1, L, lanes), lambda b, g: (g, 0, 0)),
                  pl.BlockSpec((1, 1, L), lambda b, g: (g, 0, 0)),
                  pl.BlockSpec((1, 1, L), lambda b, g: (g, 0, 0)),
                  pl.BlockSpec((1, lanes), lambda b, g: (0, g))],
        out_specs=[io_spec, st_spec, st_spec],
        out_shape=[jax.ShapeDtypeStruct((B, sl, nseg, D_MODEL), F32), st_shape, st_shape],
        scratch_shapes=[pltpu.VMEM((T, 2 * L), F32)],
        compiler_params=_cp(("parallel", "parallel")), name="s5_prompt",
    )(u, bw, cr, ci, ar, ai, d.reshape(1, -1))


def _s5_sample(u, h0r, h0i, bw, cr, ci, ar, ai, d):
    DB = u.shape[1]
    L = S5_LANES
    lanes = S5_GB * S5_CH

    def body(u_ref, hr_ref, hi_ref, bw_ref, cr_ref, ci_ref, ar_ref, ai_ref, d_ref, zz_ref, sre_ref, sim_ref):
        hr = hr_ref[...]
        hi = hi_ref[...]
        a_re = ar_ref[0]
        a_im = ai_ref[0]
        for t in range(DEC_SEQ):
            uv = u_ref[t]
            bu = _dotf(uv, bw_ref[0])
            hr, hi = a_re * hr - a_im * hi + bu[:, :L], a_re * hi + a_im * hr + bu[:, L:]
            y = _dotf(hr, cr_ref[0]) - _dotf(hi, ci_ref[0]) + d_ref[...] * uv
            zz_ref[t] = _gelu(y).astype(BF16)
        sre_ref[...] = hr
        sim_ref[...] = hi

    st_spec = pl.BlockSpec((DB, L), lambda g: (0, g))
    st_shape = jax.ShapeDtypeStruct((DB, S5_GROUPS * S5_STATE), F32)
    return pl.pallas_call(
        body, grid=(S5_NGB,),
        in_specs=[pl.BlockSpec((DEC_SEQ, DB, lanes), lambda g: (0, 0, g)), st_spec, st_spec,
                  pl.BlockSpec((1, lanes, 2 * L), lambda g: (g, 0, 0)),
                  pl.BlockSpec((1, L, lanes), lambda g: (g, 0, 0)),
                  pl.BlockSpec((1, L, lanes), lambda g: (g, 0, 0)),
                  pl.BlockSpec((1, 1, L), lambda g: (g, 0, 0)),
                  pl.BlockSpec((1, 1, L), lambda g: (g, 0, 0)),
                  pl.BlockSpec((1, lanes), lambda g: (0, g))],
        out_specs=[pl.BlockSpec((DEC_SEQ, DB, lanes), lambda g: (0, 0, g)), st_spec, st_spec],
        out_shape=[jax.ShapeDtypeStruct((DEC_SEQ, DB, D_MODEL), BF16), st_shape, st_shape],
        compiler_params=_cp(("parallel",)), name="s5_sample",
    )(u, h0r, h0i, bw, cr, ci, ar, ai, d.reshape(1, -1))


def _ffn_upgate_prompt(hf, w_up, w_gate, cw, cb, T, tm, tn):
    M, K = hf.shape
    N = w_up.shape[1]
    tiles_per_seq = T // tm
    pr = 16

    def body(a_ref, ap_ref, wu_ref, wg_ref, cw_ref, cb_ref, act_ref, tail_ref):
        i = pl.program_id(0)
        a = a_ref[...]
        hu = jnp.dot(a, wu_ref[...], preferred_element_type=F32)
        hg = jnp.dot(a, wg_ref[...], preferred_element_type=F32)
        prev = jnp.dot(ap_ref[...], wu_ref[...], preferred_element_type=F32)
        prev = jnp.where(i % tiles_per_seq == 0, 0.0, prev)
        row = lax.broadcasted_iota(jnp.int32, hu.shape, 0)
        h1 = jnp.where(row == 0, prev[pr - 1:pr], pltpu.roll(hu, 1, 0))
        h2 = jnp.where(row == 0, prev[pr - 2:pr - 1], jnp.where(row == 1, prev[pr - 1:pr], pltpu.roll(hu, 2, 0)))
        cwv = cw_ref[...]
        hc = cb_ref[...] + h2 * cwv[0:1] + h1 * cwv[1:2] + hu * cwv[2:3]
        act_ref[...] = (_gelu(hc) * hg).astype(BF16)
        tail_ref[...] = hu[tm - 8:tm]

    return pl.pallas_call(
        body, grid=(M // tm, N // tn),
        in_specs=[pl.BlockSpec((tm, K), lambda i, j: (i, 0)),
                  pl.BlockSpec((pr, K), lambda i, j: (jnp.maximum(i * (tm // pr) - 1, 0), 0)),
                  pl.BlockSpec((K, tn), lambda i, j: (0, j)),
                  pl.BlockSpec((K, tn), lambda i, j: (0, j)),
                  pl.BlockSpec((FFN_CONV, tn), lambda i, j: (0, j)),
                  pl.BlockSpec((1, tn), lambda i, j: (0, j))],
        out_specs=[pl.BlockSpec((tm, tn), lambda i, j: (i, j)), pl.BlockSpec((8, tn), lambda i, j: (i, j))],
        out_shape=[jax.ShapeDtypeStruct((M, N), BF16), jax.ShapeDtypeStruct((M // tm * 8, N), F32)],
        compiler_params=_cp(("parallel", "parallel")), name="ffn_upgate_prompt",
    )(hf, hf, w_up, w_gate, cw, cb.reshape(1, -1))


def _tile(m, pref):
    return pref if m % pref == 0 else m


def _dense_tail(x1, p, wts, ffn_state, T):
    (norm_ffn, norm_ple, w_up, w_gate, conv_w, conv_b, w_down, w_proj, w_pgate) = wts
    M = x1.shape[0]
    tm = _tile(M, 1024)
    tn = 512
    prompt = ffn_state is None

    hf = _rmsnorm(x1, norm_ffn, BF16, _tile(M, 256))
    if prompt:
        act, tails = _ffn_upgate_prompt(hf, w_up, w_gate, conv_w, conv_b, T, tm, tn)
        tails = tails.reshape(M // tm, 8, D_FF)
        tiles_per_seq = T // tm
        new_buf = tails[tiles_per_seq - 1::tiles_per_seq, 8 - (FFN_CONV - 1):, :]
    else:
        nseq = M // T
        h1 = jnp.repeat(ffn_state[:, FFN_CONV - 2], T, axis=0)
        h2 = jnp.stack([ffn_state[:, 0], ffn_state[:, 1]] + [ffn_state[:, 1]] * (T - 2), axis=1).reshape(M, D_FF)

        def conv_epilogue(accs, ex):
            hu, hg = accs
            b1, b2, cwv, cbv = ex
            tok = lax.broadcasted_iota(jnp.int32, hu.shape, 0) % T
            s1 = jnp.where(tok >= 1, pltpu.roll(hu, 1, 0), b1)
            s2 = jnp.where(tok >= 2, pltpu.roll(hu, 2, 0), b2)
            hc = cbv + s2 * cwv[0:1] + s1 * cwv[1:2] + hu * cwv[2:3]
            return _gelu(hc) * hg, hu

        act, hu = _mm([hf], [w_up, w_gate], [0, 0],
                      [(h1, "tile"), (h2, "tile"), (conv_w, "rows"), (conv_b.reshape(1, -1), "rows")],
                      conv_epilogue, [BF16, F32], tm, tn, "ffn_upgate_sample")
        new_buf = hu.reshape(nseq, T, D_FF)[:, T - (FFN_CONV - 1):]

    (x2,) = _mm([act], [w_down], [0], [(x1, "tile")], lambda accs, ex: (ex[0] + accs[0],), [F32],
                _tile(M, 512), tn, "ffn_down")
    hp = _rmsnorm(x2, norm_ple, BF16, _tile(M, 256))
    (x3,) = _mm([p, hp], [w_proj, w_pgate], [0, 1], [(x2, "tile")],
                lambda accs, ex: (ex[0] + accs[0] * _sigmoid(accs[1]),), [F32], tm, tn, "ple")
    return x3, new_buf


def kernel(x_prompt, x_sample, cache_cmp_k, cache_cmp_v, cache_sel_k, cache_sel_v, cache_win_k, cache_win_v, state_rglru_h, state_rglru_conv, state_s5_re, state_s5_im, state_ffn_conv, page_table, p_prompt, p_sample, norm_mix, norm_ffn, norm_ple, w_in_even, w_out_even, q_norm, k_norm_cmp, k_norm_sel, k_norm_win, cmp_pe_k, cmp_w1_k, cmp_w2_k, cmp_pe_v, cmp_w1_v, cmp_w2_v, rg_conv_w, rg_conv_b, rg_w_a, rg_b_a, rg_w_x, rg_b_x, rg_lam, s5_lam_re, s5_lam_im, s5_log_dt, s5_b_re, s5_b_im, s5_c_re, s5_c_im, s5_d, s5_w_glu_a, s5_w_glu_b, ffn_w_up, ffn_w_gate, ffn_conv_w, ffn_conv_b, ffn_w_down, ple_w_proj, ple_w_gate):
    B, T, _ = x_prompt.shape
    DB = x_sample.shape[0]
    n_pages = page_table.shape[1]
    past_len = n_pages * PAGE_SIZE
    w_buf = cache_win_k.shape[2]
    MP = B * T
    MS = DB * DEC_SEQ
    depth = norm_mix.shape[0]

    def reorder_in(w):
        q, kv, gt, xr, gr = (w[:, :Q_W], w[:, Q_W:Q_W + 6 * KV_W], w[:, Q_W + 6 * KV_W:Q_W + 6 * KV_W + 3 * N_HEADS],
                             w[:, Q_W + 6 * KV_W + 3 * N_HEADS:Q_W + 6 * KV_W + 3 * N_HEADS + D_RNN],
                             w[:, Q_W + 6 * KV_W + 3 * N_HEADS + D_RNN:])
        pad = jnp.zeros((w.shape[0], IN_PAD - COL_GT - 3 * N_HEADS), w.dtype)
        return jnp.concatenate([q, xr, gr, kv, gt, pad], axis=1).astype(BF16)

    def layer_dense_weights(li):
        return (norm_ffn[li], norm_ple[li], ffn_w_up[li].astype(BF16), ffn_w_gate[li].astype(BF16),
                ffn_conv_w[li], ffn_conv_b[li], ffn_w_down[li].astype(BF16),
                ple_w_proj[li].astype(BF16), ple_w_gate[li].astype(BF16))

    xs = {"p": x_prompt.reshape(MP, D_MODEL), "s": x_sample.reshape(MS, D_MODEL)}
    pe_in = {"p": p_prompt.reshape(depth, MP, -1).astype(BF16), "s": p_sample.reshape(depth, MS, -1).astype(BF16)}
    seq_len = {"p": T, "s": DEC_SEQ}
    ev = {"p": [], "s": []}
    od = {"p": [], "s": []}
    ff = {"p": [], "s": []}
    ptab = page_table.reshape(-1).astype(jnp.int32)

    for li in range(depth):
        dense_w = layer_dense_weights(li)
        if li % 2 == 0:
            e = li // 2
            w_in = reorder_in(w_in_even[e])
            w_out = w_out_even[e].astype(BF16)
            w_out_a, w_out_r = w_out[:Q_W], w_out[Q_W:]
            for grp in ("p", "s"):
                x = xs[grp]
                M = x.shape[0]
                L = seq_len[grp]
                nseq = M // L
                hn = _rmsnorm(x, norm_mix[li], BF16, _tile(M, 256))
                (z,) = _mm([hn], [w_in], [0], [], lambda accs, ex: (accs[0],), [F32], _tile(M, 1024), 512, "in_proj")
                if grp == "p":
                    cos_t, sin_t = _rope_tables(jnp.arange(T))
                    tmq = _tile(T, 512)
                    tab_blocks = T // tmq
                else:
                    cos_t, sin_t = _rope_tables(jnp.tile(past_len + jnp.arange(DEC_SEQ), DB))
                    tmq = M
                    tab_blocks = 1
                q, ks, kw, kvb = _qk_prep(z, cos_t, sin_t, tab_blocks, q_norm[e], k_norm_sel[e], k_norm_win[e], tmq)
                kc = z[:, COL_KV:COL_KV + KV_W]
                vc = z[:, COL_KV + KV_W:COL_KV + 2 * KV_W]
                vs = z[:, COL_KV + 3 * KV_W:COL_KV + 4 * KV_W]
                vw = z[:, COL_KV + 5 * KV_W:COL_KV + 6 * KV_W]
                gt = z[:, COL_GT:COL_GT + 3 * N_HEADS].reshape(nseq, L, 3, N_KV, GROUP)
                gt = gt.transpose(0, 3, 1, 2, 4).reshape(nseq, N_KV, L, 3 * GROUP)
                xr3 = z[:, COL_XR:COL_XR + D_RNN].reshape(nseq, L, D_RNN)

                if grp == "p":
                    n_rows = T // CMP_STRIDE
                    ccos, csin = _rope_tables(jnp.arange(n_rows) * CMP_STRIDE + CMP_STRIDE - 1)
                    ident = jnp.arange(B * (T // PAGE_SIZE), dtype=jnp.int32)
                    kcmp = _compress(kc.reshape(-1, PAGE_SIZE, KV_W), ident, B, T // PAGE_SIZE, cmp_pe_k[e],
                                     cmp_w1_k[e], cmp_w2_k[e], k_norm_cmp[e], ccos, csin, True, "compress_k_prompt")
                    vcmp = _compress(vc.reshape(-1, PAGE_SIZE, KV_W), ident, B, T // PAGE_SIZE, cmp_pe_v[e],
                                     cmp_w1_v[e], cmp_w2_v[e], k_norm_cmp[e], ccos, csin, False, "compress_v_prompt")
                    n_sel = T // SEL_BLOCK
                    nqb = T // Q_BLOCK
                    ovl_t = _overlap_matrix(n_rows, n_sel, n_sel).T
                    ex_t = ((np.arange(T // SEL_KT)[:, None, None] * SEL_KT + np.arange(SEL_KT)[None, :, None])
                            // SEL_BLOCK == np.arange(n_sel)[None, None, :])
                    q_t = q.reshape(B, nqb, Q_BLOCK, N_KV, GROUP, HEAD_DIM).transpose(0, 3, 1, 5, 4, 2)
                    q_t = q_t.reshape(B, N_KV, nqb, HEAD_DIM, GROUP * Q_BLOCK)
                    vs_t = kvb.reshape(B, T // SEL_KT, SEL_KT, 4, N_KV, HEAD_DIM)[:, :, :, 1].transpose(0, 3, 1, 4, 2)
                    vw_t = kvb.reshape(B, nqb, Q_BLOCK, 4, N_KV, HEAD_DIM)[:, :, :, 3].transpose(0, 3, 1, 4, 2)
                    gates_t = z[:, COL_GT:COL_GT + 3 * N_HEADS].reshape(B, T, 3, N_KV, GROUP)
                    gates_t = gates_t.transpose(0, 3, 2, 4, 1).reshape(B, N_KV, 3 * GROUP, T)
                    o_t = _nsa_prompt(q_t, kcmp, vcmp.swapaxes(2, 3), kvb.reshape(B, T, 4 * KV_W), vs_t, vw_t,
                                      gates_t, ovl_t, jnp.asarray(ex_t, BF16), B, T)
                    o_nsa = o_t.reshape(B, N_KV, nqb, HEAD_DIM, GROUP, Q_BLOCK).transpose(0, 2, 5, 1, 4, 3)
                    o_nsa = o_nsa.reshape(M, Q_W)
                    y_rg, h_last = _rglru_prompt(z.reshape(B, T, IN_PAD), rg_conv_w[e], rg_conv_b[e], rg_w_a[e],
                                                 rg_b_a[e], rg_w_x[e], rg_b_x[e], rg_lam[e], B, T, _tile(T, 512))
                    y_rg = y_rg.reshape(M, D_RNN)
                    h_last = h_last.reshape(B, D_RNN)
                    new_conv = xr3[:, T - (RG_CONV - 1):]
                    new_wk = kw.reshape(B, T, N_KV, HEAD_DIM)
                    new_wv = vw.reshape(B, T, N_KV, HEAD_DIM)
                    if T >= w_buf:
                        new_wk, new_wv = new_wk[:, T - w_buf:], new_wv[:, T - w_buf:]
                    else:
                        padw = ((0, 0), (w_buf - T, 0), (0, 0), (0, 0))
                        new_wk, new_wv = jnp.pad(new_wk, padw), jnp.pad(new_wv, padw)
                else:
                    n_rows = past_len // CMP_STRIDE
                    ccos, csin = _rope_tables(jnp.arange(n_rows) * CMP_STRIDE + CMP_STRIDE - 1)
                    ptab_e = ptab + e * cache_cmp_k.shape[1]
                    kcmp = _compress(cache_cmp_k.reshape(-1, PAGE_SIZE, KV_W), ptab_e, DB, n_pages, cmp_pe_k[e],
                                     cmp_w1_k[e], cmp_w2_k[e], k_norm_cmp[e], ccos, csin, True, "compress_k_sample")
                    vcmp = _compress(cache_cmp_v.reshape(-1, PAGE_SIZE, KV_W), ptab_e, DB, n_pages, cmp_pe_v[e],
                                     cmp_w1_v[e], cmp_w2_v[e], k_norm_cmp[e], ccos, csin, False, "compress_v_sample")
                    n_sel = -(-(past_len + DEC_SEQ) // SEL_BLOCK)
                    sel_cols = -(-n_sel // 128) * 128
                    ovl = _overlap_matrix(n_rows, n_sel, sel_cols)
                    keys = PAGES_PER_STEP * PAGE_SIZE
                    steps = n_pages // PAGES_PER_STEP
                    ex = (np.arange(sel_cols)[None, :, None]
                          == (np.arange(steps)[:, None, None] * keys + np.arange(keys)[None, None, :]) // SEL_BLOCK)
                    q8 = jnp.pad(q.reshape(DB, DEC_SEQ, Q_W), ((0, 0), (0, Q_PAD - DEC_SEQ), (0, 0)))
                    kvnew = jnp.pad(kvb.reshape(DB, DEC_SEQ, 4 * KV_W), ((0, 0), (0, 128 - DEC_SEQ), (0, 0)))
                    gt8 = jnp.pad(gt, ((0, 0), (0, 0), (0, Q_PAD - DEC_SEQ), (0, 0)))
                    o8 = _nsa_sample(q8, kcmp, vcmp, cache_sel_k.reshape(-1, PAGE_SIZE, KV_W),
                                     cache_sel_v.reshape(-1, PAGE_SIZE, KV_W), ptab_e, kvnew,
                                     cache_win_k.reshape(-1, w_buf, KV_W), cache_win_v.reshape(-1, w_buf, KV_W),
                                     e * DB, gt8, ovl, jnp.asarray(ex, BF16), DB, past_len)
                    o_nsa = o8[:, :DEC_SEQ].reshape(M, Q_W)
                    gr3 = z[:, COL_GR:COL_GR + D_RNN].reshape(DB, DEC_SEQ, D_RNN)
                    y_t, h_last = _rglru_sample(xr3.transpose(1, 0, 2), gr3.transpose(1, 0, 2),
                                                state_rglru_conv[e].transpose(1, 0, 2), state_rglru_h[e],
                                                rg_conv_w[e], rg_conv_b[e], rg_w_a[e], rg_b_a[e], rg_w_x[e],
                                                rg_b_x[e], rg_lam[e])
                    y_rg = y_t.transpose(1, 0, 2).reshape(M, D_RNN)
                    new_conv = jnp.concatenate([state_rglru_conv[e], xr3], axis=1)[:, -(RG_CONV - 1):]
                    kk = jnp.concatenate([cache_win_k[e], kw.reshape(DB, DEC_SEQ, N_KV, HEAD_DIM)], axis=1)
                    vv = jnp.concatenate([cache_win_v[e], vw.reshape(DB, DEC_SEQ, N_KV, HEAD_DIM)], axis=1)
                    new_wk, new_wv = kk[:, -w_buf:], vv[:, -w_buf:]

                rs = lambda a: a.reshape(nseq, L, N_KV, HEAD_DIM)
                ev[grp].append((rs(kc), rs(vc), rs(ks), rs(vs), new_wk, new_wv, h_last, new_conv))
                (x1,) = _mm([o_nsa, y_rg], [w_out_a, w_out_r], [0, 1], [(x, "tile")],
                            lambda accs, exs: (exs[0] + accs[0] + accs[1],), [F32], _tile(M, 1024), 512, "mixer_out")
                xs[grp], nb = _dense_tail(x1, pe_in[grp][li], dense_w, None if grp == "p" else state_ffn_conv[li], L)
                ff[grp].append(nb)
        else:
            o = li // 2
            ar, ai, bbr, bbi = _s5_discretize(s5_lam_re[o], s5_lam_im[o], s5_log_dt[o],
                                              s5_b_re[o].transpose(2, 0, 1), s5_b_im[o].transpose(2, 0, 1))
            bw, cr, ci = _s5_block_weights(bbr, bbi, s5_c_re[o], s5_c_im[o])
            ar = ar.reshape(S5_NGB, 1, S5_LANES)
            ai = ai.reshape(S5_NGB, 1, S5_LANES)
            w_a = s5_w_glu_a[o].astype(BF16)
            w_b = s5_w_glu_b[o].astype(BF16)
            for grp in ("p", "s"):
                x = xs[grp]
                M = x.shape[0]
                L = seq_len[grp]
                glu = lambda accs, exs: (exs[0] + accs[0] * _sigmoid(accs[1]),)
                if grp == "p":
                    sl = T // S5_NSEG
                    hn = _rmsnorm_segmented(x, norm_mix[li], B, T, min(256, sl))
                    zz, sre, sim = _s5_prompt(hn.reshape(B, sl, S5_NSEG, D_MODEL), bw.astype(BF16), cr.astype(BF16),
                                              ci.astype(BF16), ar, ai, s5_d[o], B, T)
                    sre = sre.reshape(B, S5_GROUPS, S5_STATE)
                    sim = sim.reshape(B, S5_GROUPS, S5_STATE)
                    (x1,) = _mm([zz.reshape(B, sl, S5_NSEG * D_MODEL)], [w_a, w_b], [0, 0], [(x, "tile")], glu, [F32],
                                sl, 512, "mixer_out", segmented=True)
                else:
                    hn = _rmsnorm(x, norm_mix[li], F32, _tile(M, 256))
                    u = hn.reshape(DB, DEC_SEQ, D_MODEL).transpose(1, 0, 2)
                    zz, sre, sim = _s5_sample(u, state_s5_re[o].reshape(DB, -1), state_s5_im[o].reshape(DB, -1),
                                              bw, cr, ci, ar, ai, s5_d[o])
                    zz = zz.transpose(1, 0, 2).reshape(M, D_MODEL)
                    sre = sre.reshape(DB, S5_GROUPS, S5_STATE)
                    sim = sim.reshape(DB, S5_GROUPS, S5_STATE)
                    (x1,) = _mm([zz], [w_a, w_b], [0, 0], [(x, "tile")], glu, [F32], M, 512, "mixer_out")
                od[grp].append((sre, sim))
                xs[grp], nb = _dense_tail(x1, pe_in[grp][li], dense_w, None if grp == "p" else state_ffn_conv[li], L)
                ff[grp].append(nb)

    def stk(states, i):
        return jnp.stack([s[i] for s in states])

    outs = [xs["p"].reshape(B, T, D_MODEL), xs["s"].reshape(DB, DEC_SEQ, D_MODEL)]
    for i in range(8):
        outs += [stk(ev["p"], i), stk(ev["s"], i)]
    for i in range(2):
        outs += [stk(od["p"], i), stk(od["s"], i)]
    outs += [jnp.stack(ff["p"]), jnp.stack(ff["s"])]
    return tuple(outs)
```

```python
import functools

import numpy as np
import jax
import jax.numpy as jnp
from jax import lax
from jax.experimental import pallas as pl
from jax.experimental.pallas import tpu as pltpu

F32 = jnp.float32
BF16 = jnp.bfloat16

D_MODEL = 2048
DEC_SEQ = 4
PAGE_SIZE = 128
N_HEADS = 8
HEAD_DIM = 128
N_KV = 2
GROUP = N_HEADS // N_KV
ROPE_DIM = HEAD_DIM // 4
ROPE_HALF = ROPE_DIM // 2
ROPE_THETA = 500000.0
CMP_BLOCK = 32
CMP_STRIDE = 16
SEL_BLOCK = 64
SEL_TOPK = 16
N_LOCAL = 2
WINDOW = 512
Q_BLOCK = 128
FORCED_SCORE = 1e9
NEG_BIG = -1e30
ATTN_SCALE = HEAD_DIM ** -0.5
D_RNN = D_MODEL // 2
RG_BLOCKS = 8
RG_BW = D_RNN // RG_BLOCKS
RG_CONV = 4
RG_C = 8.0
S5_CH = 16
S5_GROUPS = D_MODEL // S5_CH
S5_STATE = 64
S5_GB = 8
S5_NGB = S5_GROUPS // S5_GB
S5_LANES = S5_GB * S5_STATE
S5_NSEG = 8
D_FF = ((8 * D_MODEL // 3 + 255) // 256) * 256
FFN_CONV = 3
NORM_EPS = 1e-6
Q_W = N_HEADS * HEAD_DIM
KV_W = N_KV * HEAD_DIM
PAGES_PER_STEP = 8
Q_PAD = 8
SEL_KT = 512

COL_Q = 0
COL_XR = Q_W
COL_GR = Q_W + D_RNN
COL_KV = Q_W + 2 * D_RNN
COL_GT = COL_KV + 6 * KV_W
IN_PAD = 5120


def _cp(sem, vmem_mb=48):
    return pltpu.CompilerParams(dimension_semantics=sem, vmem_limit_bytes=vmem_mb * 1024 * 1024)


def _gelu(x):
    return 0.5 * x * (1.0 + jnp.tanh(0.7978845608028654 * (x + 0.044715 * (x * x * x))))


def _sigmoid(x):
    return 1.0 / (1.0 + jnp.exp(-x))


def _dotb(a, b):
    return jnp.dot(a.astype(BF16), b.astype(BF16), preferred_element_type=F32)


def _dotf(a, b):
    return jnp.dot(a, b, preferred_element_type=F32, precision=lax.Precision.HIGHEST)


def _dot_nt(a, b):
    return lax.dot_general(a, b, (((1,), (1,)), ((), ())), preferred_element_type=F32)


def _masked_softmax(s, mask):
    s = jnp.where(mask, s, NEG_BIG)
    m = jnp.max(s, axis=-1, keepdims=True)
    e = jnp.where(mask, jnp.exp(s - m), 0.0)
    return e / jnp.maximum(jnp.sum(e, axis=-1, keepdims=True), 1e-30)


def _norm_rope(x, g, cos_t, sin_t):
    ms = jnp.mean(x * x, axis=-1, keepdims=True)
    y = x * lax.rsqrt(ms + NORM_EPS) * g
    lane = lax.broadcasted_iota(jnp.int32, y.shape, 1)
    swapped = jnp.where(lane < ROPE_HALF, pltpu.roll(y, HEAD_DIM - ROPE_HALF, 1), pltpu.roll(y, ROPE_HALF, 1))
    return y * cos_t + swapped * sin_t


def _rope_tables(pos):
    inv = ROPE_THETA ** (-jnp.arange(ROPE_HALF, dtype=F32) * 2.0 / ROPE_DIM)
    ang = pos.astype(F32)[:, None] * inv
    cos = jnp.cos(ang)
    sin = jnp.sin(ang)
    n = pos.shape[0]
    ones = jnp.ones((n, HEAD_DIM - ROPE_DIM), F32)
    zeros = jnp.zeros((n, HEAD_DIM - ROPE_DIM), F32)
    return (jnp.concatenate([cos, cos, ones], axis=1), jnp.concatenate([-sin, sin, zeros], axis=1))


def _overlap_matrix(n_rows, n_sel, n_cols):
    n = np.arange(n_rows)[:, None] - 1
    j = np.arange(n_cols)[None]
    c0 = n * CMP_STRIDE
    s0 = j * SEL_BLOCK
    ov = np.minimum(c0 + CMP_BLOCK, s0 + SEL_BLOCK) - np.maximum(c0, s0)
    ov = np.maximum(ov, 0).astype(np.float32) / CMP_BLOCK
    ov = np.where((n >= 0) & (j < n_sel), ov, 0.0)
    return jnp.asarray(ov, F32)


def _rmsnorm(x, g, out_dtype, tm):
    M, D = x.shape

    def body(x_ref, g_ref, o_ref):
        xv = x_ref[...]
        ms = jnp.mean(xv * xv, axis=-1, keepdims=True)
        o_ref[...] = (xv * lax.rsqrt(ms + NORM_EPS) * g_ref[...]).astype(o_ref.dtype)

    return pl.pallas_call(
        body, grid=(M // tm,),
        in_specs=[pl.BlockSpec((tm, D), lambda i: (i, 0)), pl.BlockSpec((1, D), lambda i: (0, 0))],
        out_specs=pl.BlockSpec((tm, D), lambda i: (i, 0)),
        out_shape=jax.ShapeDtypeStruct((M, D), out_dtype),
        compiler_params=_cp(("parallel",)), name="rmsnorm")(x, g.reshape(1, D))


def _rmsnorm_segmented(x, g, B, T, tm):
    D = x.shape[1]
    sl = T // S5_NSEG
    per_seq = T // tm
    per_seg = sl // tm

    def body(x_ref, g_ref, o_ref):
        xv = x_ref[...]
        ms = jnp.mean(xv * xv, axis=-1, keepdims=True)
        o_ref[0] = xv * lax.rsqrt(ms + NORM_EPS) * g_ref[...]

    return pl.pallas_call(
        body, grid=(B * per_seq,),
        in_specs=[pl.BlockSpec((tm, D), lambda i: (i, 0)), pl.BlockSpec((1, D), lambda i: (0, 0))],
        out_specs=pl.BlockSpec((1, tm, D), lambda i: (i // per_seq, (i % per_seq) % per_seg, (i % per_seq) // per_seg)),
        out_shape=jax.ShapeDtypeStruct((B, sl, S5_NSEG * D), F32),
        compiler_params=_cp(("parallel",)), name="rmsnorm_segmented")(x, g.reshape(1, D))


def _mm(a_list, w_list, pair, extras, epilogue, out_dtypes, tm, tn, name, segmented=False):
    N = w_list[0].shape[1]
    na, nw, ne = len(a_list), len(w_list), len(extras)
    if segmented:
        M = a_list[0].shape[0] * a_list[0].shape[1] * S5_NSEG
        assert na == 1 and a_list[0].shape[1] == tm
    else:
        M = a_list[0].shape[0]

    def body(*refs):
        a_vals = [(r[0] if segmented else r[...]).astype(BF16) for r in refs[:na]]
        accs = [jnp.dot(a_vals[pair[i]], refs[na + i][...], preferred_element_type=F32) for i in range(nw)]
        ex = [r[...] for r in refs[na + nw:na + nw + ne]]
        res = epilogue(accs, ex)
        for o_ref, v in zip(refs[na + nw + ne:], res):
            o_ref[...] = v.astype(o_ref.dtype)

    if segmented:
        in_specs = [pl.BlockSpec((1, tm, w_list[0].shape[0]), lambda i, j: (i // S5_NSEG, 0, i % S5_NSEG))]
    else:
        in_specs = [pl.BlockSpec((tm, a.shape[1]), lambda i, j: (i, 0)) for a in a_list]
    in_specs += [pl.BlockSpec((w.shape[0], tn), lambda i, j: (0, j)) for w in w_list]
    for arr, kind in extras:
        if kind == "tile":
            in_specs.append(pl.BlockSpec((tm, tn), lambda i, j: (i, j)))
        else:
            in_specs.append(pl.BlockSpec((arr.shape[0], tn), lambda i, j: (0, j)))
    return pl.pallas_call(
        body, grid=(M // tm, N // tn), in_specs=in_specs,
        out_specs=[pl.BlockSpec((tm, tn), lambda i, j: (i, j)) for _ in out_dtypes],
        out_shape=[jax.ShapeDtypeStruct((M, N), dt) for dt in out_dtypes],
        compiler_params=_cp(("parallel", "parallel")), name=name,
    )(*a_list, *w_list, *[e[0] for e in extras])


def _qk_prep(z, cos_t, sin_t, tab_blocks, qn, kns, knw, tm):
    M = z.shape[0]

    def body(zq_ref, zkv_ref, c_ref, s_ref, qn_ref, kns_ref, knw_ref, q_ref, ks_ref, kw_ref, kvb_ref):
        c = c_ref[...]
        s = s_ref[...]
        for h in range(N_HEADS):
            sl = slice(h * HEAD_DIM, (h + 1) * HEAD_DIM)
            q_ref[:, sl] = _norm_rope(zq_ref[:, sl], qn_ref[...], c, s).astype(BF16)
        for kv in range(N_KV):
            sl = slice(kv * HEAD_DIM, (kv + 1) * HEAD_DIM)
            ks = _norm_rope(zkv_ref[:, 2 * KV_W + kv * HEAD_DIM:2 * KV_W + (kv + 1) * HEAD_DIM], kns_ref[...], c, s)
            kw = _norm_rope(zkv_ref[:, 4 * KV_W + kv * HEAD_DIM:4 * KV_W + (kv + 1) * HEAD_DIM], knw_ref[...], c, s)
            ks_ref[:, sl] = ks
            kw_ref[:, sl] = kw
            kvb_ref[:, sl] = ks.astype(BF16)
            kvb_ref[:, 2 * KV_W + kv * HEAD_DIM:2 * KV_W + (kv + 1) * HEAD_DIM] = kw.astype(BF16)
        kvb_ref[:, KV_W:2 * KV_W] = zkv_ref[:, 3 * KV_W:4 * KV_W].astype(BF16)
        kvb_ref[:, 3 * KV_W:4 * KV_W] = zkv_ref[:, 5 * KV_W:6 * KV_W].astype(BF16)

    vec = pl.BlockSpec((1, HEAD_DIM), lambda i: (0, 0))
    tab = pl.BlockSpec((tm, HEAD_DIM), lambda i: (i % tab_blocks, 0))
    return pl.pallas_call(
        body, grid=(M // tm,),
        in_specs=[pl.BlockSpec((tm, Q_W), lambda i: (i, COL_Q // Q_W)),
                  pl.BlockSpec((tm, 6 * KV_W), lambda i: (i, COL_KV // (6 * KV_W))),
                  tab, tab, vec, vec, vec],
        out_specs=[pl.BlockSpec((tm, Q_W), lambda i: (i, 0)),
                   pl.BlockSpec((tm, KV_W), lambda i: (i, 0)),
                   pl.BlockSpec((tm, KV_W), lambda i: (i, 0)),
                   pl.BlockSpec((tm, 4 * KV_W), lambda i: (i, 0))],
        out_shape=[jax.ShapeDtypeStruct((M, Q_W), BF16), jax.ShapeDtypeStruct((M, KV_W), F32),
                   jax.ShapeDtypeStruct((M, KV_W), F32), jax.ShapeDtypeStruct((M, 4 * KV_W), BF16)],
        compiler_params=_cp(("parallel",)), name="qk_prep",
    )(z, z, cos_t, sin_t, qn.reshape(1, -1), kns.reshape(1, -1), knw.reshape(1, -1))


def _compress(pages, ptab, n_seq, pages_per_seq, pe, w1, w2, norm_g, cos_t, sin_t, do_norm, name):
    G = 2 * PAGES_PER_STEP if pages_per_seq % (2 * PAGES_PER_STEP) == 0 else PAGES_PER_STEP
    steps = pages_per_seq // G
    cpp = PAGE_SIZE // CMP_STRIDE
    ch = G * cpp
    n_chunk = pages_per_seq * cpp

    def body(pt_ref, *refs):
        page_refs = refs[:G]
        pe_ref, w1_ref, w2_ref, g_ref, c_ref, s_ref, o_ref, carry_ref = refs[G:]
        step = pl.program_id(1)

        @pl.when(step == 0)
        def _():
            carry_ref[...] = jnp.zeros_like(carry_ref)

        row = lax.broadcasted_iota(jnp.int32, (ch, HEAD_DIM), 0)
        for kv in range(N_KV):
            lo = jnp.zeros((ch, HEAD_DIM), F32)
            hi = jnp.zeros((ch, HEAD_DIM), F32)
            for c in range(CMP_STRIDE):
                xc = jnp.concatenate(
                    [pr[0, pl.ds(N_KV * c + kv, cpp, stride=N_KV * CMP_STRIDE), :] for pr in page_refs], axis=0)
                lo = lo + _dotb(xc + pe_ref[c:c + 1, :], w1_ref[c])
                hi = hi + _dotb(xc + pe_ref[CMP_STRIDE + c:CMP_STRIDE + c + 1, :], w1_ref[CMP_STRIDE + c])
            lo_prev = jnp.where(row == 0, carry_ref[kv, 0:1, :], pltpu.roll(lo, 1, 0))
            carry_ref[kv, 0:1, :] = lo[ch - 1:ch, :]
            out = _dotb(_gelu(lo_prev + hi), w2_ref[...])
            if do_norm:
                out = _norm_rope(out, g_ref[...], c_ref[...], s_ref[...])
            o_ref[0, kv] = out.astype(BF16)

    def page_map(j, s, g, pt):
        return (pt[s * pages_per_seq + g * G + j], 0, 0)

    full2 = lambda s, g, pt: (0, 0)
    in_specs = [pl.BlockSpec((1, N_KV * PAGE_SIZE, HEAD_DIM), functools.partial(page_map, j)) for j in range(G)]
    in_specs += [pl.BlockSpec((CMP_BLOCK, HEAD_DIM), full2),
                 pl.BlockSpec((CMP_BLOCK, HEAD_DIM, HEAD_DIM), lambda s, g, pt: (0, 0, 0)),
                 pl.BlockSpec((HEAD_DIM, HEAD_DIM), full2),
                 pl.BlockSpec((1, HEAD_DIM), full2),
                 pl.BlockSpec((ch, HEAD_DIM), lambda s, g, pt: (g, 0)),
                 pl.BlockSpec((ch, HEAD_DIM), lambda s, g, pt: (g, 0))]
    gs = pltpu.PrefetchScalarGridSpec(
        num_scalar_prefetch=1, grid=(n_seq, steps), in_specs=in_specs,
        out_specs=pl.BlockSpec((1, N_KV, ch, HEAD_DIM), lambda s, g, pt: (s, 0, g, 0)),
        scratch_shapes=[pltpu.VMEM((N_KV, 8, HEAD_DIM), F32)])
    return pl.pallas_call(
        body, grid_spec=gs, out_shape=jax.ShapeDtypeStruct((n_seq, N_KV, n_chunk, HEAD_DIM), BF16),
        compiler_params=_cp(("parallel", "arbitrary")), name=name,
    )(ptab, *([pages] * G), pe, w1.astype(BF16), w2.astype(BF16), norm_g.reshape(1, -1), cos_t, sin_t)


def _nsa_prompt(q_t, kcmp, vcmp_t, kvb, vs_t, vw_t, gates_t, ovl_t, expand_t, B, T):
    nqb = T // Q_BLOCK
    n_cmp_rows = T // CMP_STRIDE
    n_sel = T // SEL_BLOCK
    cols = GROUP * Q_BLOCK
    win_tiles = WINDOW // Q_BLOCK
    win_keys = (win_tiles + 1) * Q_BLOCK
    assert T % SEL_KT == 0 and T >= win_keys
    topk = min(SEL_TOPK, n_sel)

    def body(q_ref, kc_ref, vc_ref, ks_ref, vs_ref, kw_ref, vw_ref, gt_ref, ovl_ref, ex_ref, o_ref):
        i = pl.program_id(2)
        qt = q_ref[0, 0, 0]
        pos = i * Q_BLOCK + (lax.broadcasted_iota(jnp.int32, (1, cols), 1) & (Q_BLOCK - 1))

        s = jnp.dot(kc_ref[0, 0], qt, preferred_element_type=F32) * ATTN_SCALE
        crow = lax.broadcasted_iota(jnp.int32, (n_cmp_rows, cols), 0)
        cmask = (crow >= 1) & (crow * CMP_STRIDE + (CMP_BLOCK - 1 - CMP_STRIDE) <= pos)
        s = jnp.where(cmask, s, NEG_BIG)
        e = jnp.where(cmask, jnp.exp(s - jnp.max(s, axis=0, keepdims=True)), 0.0)
        p = e / jnp.maximum(jnp.sum(e, axis=0, keepdims=True), 1e-30)
        o_cmp = _dotb(vc_ref[0, 0], p)
        imp = p[:, 0:Q_BLOCK]
        for h in range(1, GROUP):
            imp = imp + p[:, h * Q_BLOCK:(h + 1) * Q_BLOCK]

        score = _dotf(ovl_ref[...], imp)
        j = lax.broadcasted_iota(jnp.int32, (n_sel, Q_BLOCK), 0)
        qblk = (i * Q_BLOCK + lax.broadcasted_iota(jnp.int32, (n_sel, Q_BLOCK), 1)) // SEL_BLOCK
        forced = (j == 0) | ((j <= qblk) & (j > qblk - N_LOCAL))
        score = jnp.where(forced, FORCED_SCORE, jnp.where(j > qblk, NEG_BIG, score))
        rank = jnp.zeros((n_sel, Q_BLOCK), F32)
        for r in range(n_sel):
            sr = score[r:r + 1, :]
            beats = (sr > score) | ((sr == score) & (j > r))
            rank = rank + jnp.where(beats, 1.0, 0.0)
        sel = jnp.where(rank < topk, 1.0, 0.0).astype(BF16)

        krow = lax.broadcasted_iota(jnp.int32, (SEL_KT, cols), 0)

        def sel_tile(u, carry):
            m, l, acc = carry
            k = ks_ref[0, pl.ds(pl.multiple_of(u * SEL_KT, SEL_KT), SEL_KT), :]
            sc = jnp.dot(k, qt, preferred_element_type=F32) * ATTN_SCALE
            m1 = _dotb(ex_ref[u], sel)
            mask = (jnp.concatenate([m1] * GROUP, axis=1) > 0.5) & (u * SEL_KT + krow <= pos)
            sc = jnp.where(mask, sc, NEG_BIG)
            m_new = jnp.maximum(m, jnp.max(sc, axis=0, keepdims=True))
            alpha = jnp.exp(m - m_new)
            ex = jnp.where(mask, jnp.exp(sc - m_new), 0.0)
            l = alpha * l + jnp.sum(ex, axis=0, keepdims=True)
            acc = alpha * acc + _dotb(vs_ref[0, 0, u], ex)
            return m_new, l, acc

        init = (jnp.full((1, cols), NEG_BIG, F32), jnp.zeros((1, cols), F32), jnp.zeros((HEAD_DIM, cols), F32))
        _, l_sel, acc_sel = lax.fori_loop(0, (i * Q_BLOCK + Q_BLOCK + SEL_KT - 1) // SEL_KT, sel_tile, init)
        o_sel = acc_sel / jnp.maximum(l_sel, 1e-30)

        t0 = jnp.maximum(i - win_tiles, 0)
        kwin = kw_ref[0, pl.ds(pl.multiple_of(t0 * Q_BLOCK, Q_BLOCK), win_keys), :]
        sw = jnp.dot(kwin, qt, preferred_element_type=F32) * ATTN_SCALE
        kpos = t0 * Q_BLOCK + lax.broadcasted_iota(jnp.int32, (win_keys, cols), 0)
        wmask = (kpos <= pos) & (kpos > pos - WINDOW)
        sw = jnp.where(wmask, sw, NEG_BIG)
        ew = jnp.where(wmask, jnp.exp(sw - jnp.max(sw, axis=0, keepdims=True)), 0.0)
        pw = ew / jnp.maximum(jnp.sum(ew, axis=0, keepdims=True), 1e-30)
        o_win = jnp.zeros((HEAD_DIM, cols), F32)
        for t in range(win_tiles + 1):
            o_win = o_win + _dotb(vw_ref[0, 0, t0 + t], pw[t * Q_BLOCK:(t + 1) * Q_BLOCK])

        g = _sigmoid(gt_ref[0, 0])
        for h in range(GROUP):
            c = slice(h * Q_BLOCK, (h + 1) * Q_BLOCK)
            o = (g[h:h + 1, :] * o_cmp[:, c] + g[GROUP + h:GROUP + h + 1, :] * o_sel[:, c]
                 + g[2 * GROUP + h:2 * GROUP + h + 1, :] * o_win[:, c])
            o_ref[0, 0, 0, :, c] = o.astype(BF16)

    def k_spec(off):
        return pl.BlockSpec((1, T, HEAD_DIM), lambda b, kv, i: (b, 0, off + kv))

    vs_spec = pl.BlockSpec((1, 1, T // SEL_KT, HEAD_DIM, SEL_KT), lambda b, kv, i: (b, kv, 0, 0, 0))
    vw_spec = pl.BlockSpec((1, 1, nqb, HEAD_DIM, Q_BLOCK), lambda b, kv, i: (b, kv, 0, 0, 0))
    qo_spec = pl.BlockSpec((1, 1, 1, HEAD_DIM, cols), lambda b, kv, i: (b, kv, i, 0, 0))
    return pl.pallas_call(
        body, grid=(B, N_KV, nqb),
        in_specs=[qo_spec,
                  pl.BlockSpec((1, 1, n_cmp_rows, HEAD_DIM), lambda b, kv, i: (b, kv, 0, 0)),
                  pl.BlockSpec((1, 1, HEAD_DIM, n_cmp_rows), lambda b, kv, i: (b, kv, 0, 0)),
                  k_spec(0), vs_spec, k_spec(4), vw_spec,
                  pl.BlockSpec((1, 1, 3 * GROUP, Q_BLOCK), lambda b, kv, i: (b, kv, 0, i)),
                  pl.BlockSpec((n_sel, n_cmp_rows), lambda b, kv, i: (0, 0)),
                  pl.BlockSpec((T // SEL_KT, SEL_KT, n_sel), lambda b, kv, i: (0, 0, 0))],
        out_specs=qo_spec,
        out_shape=jax.ShapeDtypeStruct((B, N_KV, nqb, HEAD_DIM, cols), BF16),
        compiler_params=_cp(("parallel", "parallel", "arbitrary")), name="nsa_prompt",
    )(q_t, kcmp, vcmp_t, kvb, vs_t, kvb, vw_t, gates_t, ovl_t, expand_t)


def _nsa_sample(q, kcmp, vcmp, pool_k, pool_v, ptab, kvnew, win_k, win_v, win_off, gates, ovl, expand, DB, past_len):
    G = PAGES_PER_STEP
    n_pages = past_len // PAGE_SIZE
    steps = n_pages // G
    n_cmp_rows = kcmp.shape[2]
    sel_cols = ovl.shape[1]
    n_past_blk = past_len // SEL_BLOCK
    n_sel = -(-(past_len + DEC_SEQ) // SEL_BLOCK)
    w_buf = win_k.shape[1] // N_KV
    rows = GROUP * Q_PAD
    keys = G * PAGE_SIZE
    new_rows = kvnew.shape[1]

    def body(pt_ref, q_ref, kc_ref, vc_ref, ovl_ref, *rest):
        pk = rest[:G]
        pv = rest[G:2 * G]
        (ex_ref, kvn_ref, wk_ref, wv_ref, gt_ref, o_ref,
         sel_scr, m_scr, l_scr, acc_scr, base_scr) = rest[2 * G:]
        step = pl.program_id(1)
        tok = lax.broadcasted_iota(jnp.int32, (rows, 1), 0) & (Q_PAD - 1)
        q8 = q_ref[0]

        def q_rows(kv):
            return jnp.concatenate(
                [q8[:, (kv * GROUP + h) * HEAD_DIM:(kv * GROUP + h + 1) * HEAD_DIM] for h in range(GROUP)], axis=0)

        def head_col(gt, c0):
            return jnp.concatenate(
                [jnp.broadcast_to(gt[:, c0 + h:c0 + h + 1], (Q_PAD, HEAD_DIM)) for h in range(GROUP)], axis=0)

        @pl.when(step == 0)
        def _():
            o_cmps = []
            imps = []
            for kv in range(N_KV):
                s = _dot_nt(q_rows(kv), kc_ref[0, kv]) * ATTN_SCALE
                col = lax.broadcasted_iota(jnp.int32, (rows, n_cmp_rows), 1)
                cmask = (col >= 1) & (col * CMP_STRIDE + (CMP_BLOCK - 1 - CMP_STRIDE) <= past_len + tok)
                p = _masked_softmax(s, cmask)
                o_cmps.append(_dotb(p, vc_ref[0, kv]))
                imp = p[0:Q_PAD]
                for h in range(1, GROUP):
                    imp = imp + p[h * Q_PAD:(h + 1) * Q_PAD]
                imps.append(imp)

            score = _dotf(jnp.concatenate(imps, axis=0), ovl_ref[...])
            j = lax.broadcasted_iota(jnp.int32, (N_KV * Q_PAD, sel_cols), 1)
            trow = lax.broadcasted_iota(jnp.int32, (N_KV * Q_PAD, sel_cols), 0) & (Q_PAD - 1)
            qblk = (past_len + trow) // SEL_BLOCK
            forced = (j == 0) | ((j <= qblk) & (j > qblk - N_LOCAL))
            score = jnp.where(forced, FORCED_SCORE, jnp.where(j > qblk, NEG_BIG, score))
            score = jnp.where(j < n_sel, score, -jnp.inf)
            rank = jnp.zeros((N_KV * Q_PAD, sel_cols), F32)
            for r in range(n_sel):
                sr = score[:, r:r + 1]
                rank = rank + jnp.where((sr > score) | ((sr == score) & (j > r)), 1.0, 0.0)
            sel_all = jnp.where((rank < min(SEL_TOPK, n_sel)) & (j < n_sel), 1.0, 0.0)

            for kv in range(N_KV):
                qh = q_rows(kv)
                o_cmp = o_cmps[kv]
                sel4 = jnp.concatenate([sel_all[kv * Q_PAD:(kv + 1) * Q_PAD]] * GROUP, axis=0)
                sel_scr[kv] = sel4.astype(BF16)

                kb = wk_ref[0, pl.ds(kv, w_buf, stride=N_KV), :].astype(BF16)
                vb = wv_ref[0, pl.ds(kv, w_buf, stride=N_KV), :].astype(BF16)
                kn = kvn_ref[0, :, (4 + kv) * HEAD_DIM:(5 + kv) * HEAD_DIM]
                vn = kvn_ref[0, :, (6 + kv) * HEAD_DIM:(7 + kv) * HEAD_DIM]
                s1 = jnp.where(lax.broadcasted_iota(jnp.int32, (rows, w_buf), 1) - w_buf > tok - WINDOW,
                               _dot_nt(qh, kb) * ATTN_SCALE, NEG_BIG)
                ncol = lax.broadcasted_iota(jnp.int32, (rows, new_rows), 1)
                nmask = (ncol <= tok) & (ncol < DEC_SEQ)
                s2 = jnp.where(nmask, _dot_nt(qh, kn) * ATTN_SCALE, NEG_BIG)
                m = jnp.maximum(jnp.max(s1, axis=-1, keepdims=True), jnp.max(s2, axis=-1, keepdims=True))
                e1 = jnp.where(s1 > 0.5 * NEG_BIG, jnp.exp(s1 - m), 0.0)
                e2 = jnp.where(nmask, jnp.exp(s2 - m), 0.0)
                den = jnp.sum(e1, axis=-1, keepdims=True) + jnp.sum(e2, axis=-1, keepdims=True)
                o_win = (_dotb(e1, vb) + _dotb(e2, vn)) / jnp.maximum(den, 1e-30)

                gt = _sigmoid(gt_ref[0, kv])
                base_scr[kv] = head_col(gt, 0) * o_cmp + head_col(gt, 2 * GROUP) * o_win

                ksn = kvn_ref[0, :, kv * HEAD_DIM:(kv + 1) * HEAD_DIM]
                vsn = kvn_ref[0, :, (2 + kv) * HEAD_DIM:(3 + kv) * HEAD_DIM]
                chosen = sel4[:, n_past_blk:n_past_blk + 1] > 0.5
                smask = nmask & chosen
                s3 = jnp.where(smask, _dot_nt(qh, ksn) * ATTN_SCALE, NEG_BIG)
                m0 = jnp.max(s3, axis=-1, keepdims=True)
                e3 = jnp.where(smask, jnp.exp(s3 - m0), 0.0)
                m_scr[kv] = jnp.broadcast_to(m0, (rows, HEAD_DIM))
                l_scr[kv] = jnp.broadcast_to(jnp.sum(e3, axis=-1, keepdims=True), (rows, HEAD_DIM))
                acc_scr[kv] = _dotb(e3, vsn)

        for kv in range(N_KV):
            qh = q_rows(kv)
            k = jnp.concatenate([r[0, pl.ds(kv, PAGE_SIZE, stride=N_KV), :] for r in pk], axis=0).astype(BF16)
            v = jnp.concatenate([r[0, pl.ds(kv, PAGE_SIZE, stride=N_KV), :] for r in pv], axis=0).astype(BF16)
            mask = _dotb(sel_scr[kv], ex_ref[0]) > 0.5
            sc = jnp.where(mask, _dot_nt(qh, k) * ATTN_SCALE, NEG_BIG)
            m_old = m_scr[kv][:, 0:1]
            l_old = l_scr[kv][:, 0:1]
            m_new = jnp.maximum(m_old, jnp.max(sc, axis=-1, keepdims=True))
            alpha = jnp.exp(m_old - m_new)
            e = jnp.where(mask, jnp.exp(sc - m_new), 0.0)
            l_new = alpha * l_old + jnp.sum(e, axis=-1, keepdims=True)
            acc = alpha * acc_scr[kv] + _dotb(e, v)
            m_scr[kv] = jnp.broadcast_to(m_new, (rows, HEAD_DIM))
            l_scr[kv] = jnp.broadcast_to(l_new, (rows, HEAD_DIM))
            acc_scr[kv] = acc

        @pl.when(step == steps - 1)
        def _():
            for kv in range(N_KV):
                gt = _sigmoid(gt_ref[0, kv])
                o_sel = acc_scr[kv] / jnp.maximum(l_scr[kv], 1e-30)
                o = base_scr[kv] + head_col(gt, GROUP) * o_sel
                for h in range(GROUP):
                    hh = kv * GROUP + h
                    o_ref[0, :, hh * HEAD_DIM:(hh + 1) * HEAD_DIM] = o[h * Q_PAD:(h + 1) * Q_PAD].astype(BF16)

    def page_map(jj, b, g, pt):
        return (pt[b * n_pages + g * G + jj], 0, 0)

    page_specs = [pl.BlockSpec((1, N_KV * PAGE_SIZE, HEAD_DIM), functools.partial(page_map, jj)) for jj in range(G)]
    cmp_spec = pl.BlockSpec((1, N_KV, n_cmp_rows, HEAD_DIM), lambda b, g, pt: (b, 0, 0, 0))
    in_specs = [pl.BlockSpec((1, Q_PAD, Q_W), lambda b, g, pt: (b, 0, 0)), cmp_spec, cmp_spec,
                pl.BlockSpec((n_cmp_rows, sel_cols), lambda b, g, pt: (0, 0))]
    in_specs += page_specs + page_specs
    in_specs += [pl.BlockSpec((1, sel_cols, keys), lambda b, g, pt: (g, 0, 0)),
                 pl.BlockSpec((1, new_rows, 4 * KV_W), lambda b, g, pt: (b, 0, 0)),
                 pl.BlockSpec((1, N_KV * w_buf, HEAD_DIM), lambda b, g, pt: (win_off + b, 0, 0)),
                 pl.BlockSpec((1, N_KV * w_buf, HEAD_DIM), lambda b, g, pt: (win_off + b, 0, 0)),
                 pl.BlockSpec((1, N_KV, Q_PAD, 3 * GROUP), lambda b, g, pt: (b, 0, 0, 0))]
    gs = pltpu.PrefetchScalarGridSpec(
        num_scalar_prefetch=1, grid=(DB, steps), in_specs=in_specs,
        out_specs=pl.BlockSpec((1, Q_PAD, Q_W), lambda b, g, pt: (b, 0, 0)),
        scratch_shapes=[pltpu.VMEM((N_KV, rows, sel_cols), BF16), pltpu.VMEM((N_KV, rows, HEAD_DIM), F32),
                        pltpu.VMEM((N_KV, rows, HEAD_DIM), F32), pltpu.VMEM((N_KV, rows, HEAD_DIM), F32),
                        pltpu.VMEM((N_KV, rows, HEAD_DIM), F32)])
    return pl.pallas_call(
        body, grid_spec=gs, out_shape=jax.ShapeDtypeStruct((DB, Q_PAD, Q_W), BF16),
        compiler_params=_cp(("parallel", "arbitrary")), name="nsa_sample",
    )(ptab, q, kcmp, vcmp, ovl, *([pool_k] * G), *([pool_v] * G), expand, kvnew, win_k, win_v, gates)


def _rg_gates(u, wa_ref, ba, wx_ref, bx, sp):
    ra = []
    rx = []
    for n in range(RG_BLOCKS):
        ub = u[:, n * RG_BW:(n + 1) * RG_BW]
        ra.append(_dotf(ub, wa_ref[n]))
        rx.append(_dotf(ub, wx_ref[n]))
    r = _sigmoid(jnp.concatenate(ra, axis=1) + ba)
    i = _sigmoid(jnp.concatenate(rx, axis=1) + bx)
    log_a = -RG_C * r * sp
    a = jnp.exp(log_a)
    b = jnp.sqrt(1.0 - jnp.exp(2.0 * log_a)) * i * u
    return a, b


def _softplus_neg(lam):
    return jnp.maximum(-lam, 0.0) + jnp.log(1.0 + jnp.exp(-jnp.abs(lam)))


def _rglru_prompt(z, cw, cb, wa, ba, wx, bx, lam, B, T, tt):
    keep = 8

    def body(xr_ref, gr_ref, cw_ref, cb_ref, wa_ref, ba_ref, wx_ref, bx_ref, lam_ref,
             y_ref, hl_ref, a_scr, b_scr, h_scr, tail_scr):
        t = pl.program_id(1)

        @pl.when(t == 0)
        def _():
            h_scr[...] = jnp.zeros_like(h_scr)
            tail_scr[...] = jnp.zeros_like(tail_scr)

        x = xr_ref[0]
        row = lax.broadcasted_iota(jnp.int32, (tt, D_RNN), 0)
        cwv = cw_ref[...]
        u = cb_ref[...] + x * cwv[RG_CONV - 1:RG_CONV]
        for d in range(1, RG_CONV):
            xs = pltpu.roll(x, d, 0)
            for r in range(d):
                xs = jnp.where(row == r, tail_scr[keep - d + r:keep - d + r + 1, :], xs)
            u = u + xs * cwv[RG_CONV - 1 - d:RG_CONV - d]
        tail_scr[...] = x[tt - keep:tt]
        a, b = _rg_gates(u, wa_ref, ba_ref[...], wx_ref, bx_ref[...], _softplus_neg(lam_ref[...]))
        a_scr[...] = a
        b_scr[...] = b

        def step(s, h):
            h = a_scr[pl.ds(s, 1), :] * h + b_scr[pl.ds(s, 1), :]
            b_scr[pl.ds(s, 1), :] = h
            return h

        h = lax.fori_loop(0, tt, step, h_scr[0:1, :], unroll=8)
        h_scr[0:1, :] = h
        hl_ref[0] = h
        y_ref[0] = (b_scr[...] * _gelu(gr_ref[0])).astype(BF16)

    vec = pl.BlockSpec((1, D_RNN), lambda b, t: (0, 0))
    wspec = pl.BlockSpec((RG_BLOCKS, RG_BW, RG_BW), lambda b, t: (0, 0, 0))
    return pl.pallas_call(
        body, grid=(B, T // tt),
        in_specs=[pl.BlockSpec((1, tt, D_RNN), lambda b, t: (b, t, COL_XR // D_RNN)),
                  pl.BlockSpec((1, tt, D_RNN), lambda b, t: (b, t, COL_GR // D_RNN)),
                  pl.BlockSpec((RG_CONV, D_RNN), lambda b, t: (0, 0)), vec, wspec, vec, wspec, vec, vec],
        out_specs=[pl.BlockSpec((1, tt, D_RNN), lambda b, t: (b, t, 0)),
                   pl.BlockSpec((1, 1, D_RNN), lambda b, t: (b, 0, 0))],
        out_shape=[jax.ShapeDtypeStruct((B, T, D_RNN), BF16), jax.ShapeDtypeStruct((B, 1, D_RNN), F32)],
        scratch_shapes=[pltpu.VMEM((tt, D_RNN), F32), pltpu.VMEM((tt, D_RNN), F32),
                        pltpu.VMEM((8, D_RNN), F32), pltpu.VMEM((keep, D_RNN), F32)],
        compiler_params=_cp(("parallel", "arbitrary")), name="rglru_prompt",
    )(z, z, cw, cb.reshape(1, -1), wa, ba.reshape(1, -1), wx, bx.reshape(1, -1), lam.reshape(1, -1))


def _rglru_sample(xr, gr, buf, h0, cw, cb, wa, ba, wx, bx, lam):
    DB = h0.shape[0]

    def body(xr_ref, gr_ref, buf_ref, h0_ref, cw_ref, cb_ref, wa_ref, ba_ref, wx_ref, bx_ref, lam_ref, y_ref, hl_ref):
        xp = [buf_ref[k] for k in range(RG_CONV - 1)] + [xr_ref[t] for t in range(DEC_SEQ)]
        cwv = cw_ref[...]
        sp = _softplus_neg(lam_ref[...])
        h = h0_ref[...]
        for t in range(DEC_SEQ):
            u = cb_ref[...]
            for k in range(RG_CONV):
                u = u + xp[t + k] * cwv[k:k + 1]
            a, b = _rg_gates(u, wa_ref, ba_ref[...], wx_ref, bx_ref[...], sp)
            h = a * h + b
            y_ref[t] = (h * _gelu(gr_ref[t])).astype(BF16)
        hl_ref[...] = h

    return pl.pallas_call(
        body, out_shape=[jax.ShapeDtypeStruct((DEC_SEQ, DB, D_RNN), BF16), jax.ShapeDtypeStruct((DB, D_RNN), F32)],
        compiler_params=pltpu.CompilerParams(vmem_limit_bytes=32 * 1024 * 1024), name="rglru_sample",
    )(xr, gr, buf, h0, cw, cb.reshape(1, -1), wa, ba.reshape(1, -1), wx, bx.reshape(1, -1), lam.reshape(1, -1))


def _s5_discretize(lam_re, lam_im, log_dt, b_re_t, b_im_t):
    def body(lr_ref, li_ref, ldt_ref, br_ref, bi_ref, ar_ref, ai_ref, bbr_ref, bbi_ref):
        lr = lr_ref[...]
        li = li_ref[...]
        dt = jnp.exp(ldt_ref[...])
        mag = jnp.exp(lr * dt)
        ar = mag * jnp.cos(li * dt)
        ai = mag * jnp.sin(li * dt)
        den = lr * lr + li * li
        n_re = ar - 1.0
        f_re = (n_re * lr + ai * li) / den
        f_im = (ai * lr - n_re * li) / den
        ar_ref[...] = ar
        ai_ref[...] = ai
        for c in range(S5_CH):
            bbr_ref[c] = f_re * br_ref[c] - f_im * bi_ref[c]
            bbi_ref[c] = f_re * bi_ref[c] + f_im * br_ref[c]

    gp = jax.ShapeDtypeStruct(lam_re.shape, F32)
    cgp = jax.ShapeDtypeStruct(b_re_t.shape, F32)
    return pl.pallas_call(body, out_shape=[gp, gp, cgp, cgp], name="s5_discretize")(
        lam_re, lam_im, log_dt.reshape(-1, 1), b_re_t, b_im_t)


def _s5_block_weights(bbr, bbi, c_re, c_im):
    eye = jnp.eye(S5_GB, dtype=F32)

    def bblk(bb):
        x = bb.reshape(S5_CH, S5_NGB, S5_GB, S5_STATE)
        x = jnp.einsum("cngp,gh->ngchp", x, eye)
        return x.reshape(S5_NGB, S5_GB * S5_CH, S5_LANES)

    def cblk(c):
        x = c.reshape(S5_NGB, S5_GB, S5_CH, S5_STATE)
        x = jnp.einsum("ngcp,gh->ngphc", x, eye)
        return x.reshape(S5_NGB, S5_LANES, S5_GB * S5_CH)

    return jnp.concatenate([bblk(bbr), bblk(bbi)], axis=2), cblk(c_re), cblk(c_im)


def _s5_prompt(u, bw, cr, ci, ar, ai, d, B, T):
    L = S5_LANES
    lanes = S5_GB * S5_CH
    nseg = S5_NSEG
    sl = T // nseg
    assert sl & (sl - 1) == 0, "segment length must be a power of two"
    chs = min(64, sl)
    ch = chs * nseg
    nchunks = sl // chs

    def body(u_ref, bw_ref, cr_ref, ci_ref, ar_ref, ai_ref, d_ref, zz_ref, sre_ref, sim_ref, h_scr):
        a_re = ar_ref[0]
        a_im = ai_ref[0]
        ar8 = jnp.broadcast_to(a_re, (nseg, L))
        ai8 = jnp.broadcast_to(a_im, (nseg, L))

        def project(c, carry):
            s0 = pl.multiple_of(c * chs, chs)
            uc = u_ref[0, pl.ds(s0, chs)].reshape(ch, lanes)
            h_scr[pl.ds(pl.multiple_of(c * ch, ch), ch), :] = _dotb(uc, bw_ref[0])
            return carry

        lax.fori_loop(0, nchunks, project, 0)

        def advance(s, carry, store):
            hr, hi = carry
            rows = pl.ds(pl.multiple_of(s * nseg, nseg), nseg)
            bu = h_scr[rows, :]
            nr = ar8 * hr - ai8 * hi + bu[:, :L]
            ni = ar8 * hi + ai8 * hr + bu[:, L:]
            if store:
                h_scr[rows, :] = jnp.concatenate([nr, ni], axis=1)
            return nr, ni

        zero = jnp.zeros((nseg, L), F32)
        er, ei = lax.fori_loop(0, sl, functools.partial(advance, store=False), (zero, zero), unroll=8)

        pr, pi = a_re, a_im
        for _ in range(sl.bit_length() - 1):
            pr, pi = pr * pr - pi * pi, 2.0 * pr * pi
        sr = [jnp.zeros((1, L), F32)]
        si = [jnp.zeros((1, L), F32)]
        for k in range(nseg):
            sr.append(er[k:k + 1] + pr * sr[k] - pi * si[k])
            si.append(ei[k:k + 1] + pr * si[k] + pi * sr[k])
        sre_ref[0, 0] = sr[nseg]
        sim_ref[0, 0] = si[nseg]
        start = (jnp.concatenate(sr[:nseg], axis=0), jnp.concatenate(si[:nseg], axis=0))
        lax.fori_loop(0, sl, functools.partial(advance, store=True), start, unroll=8)

        def emit(c, carry):
            s0 = pl.multiple_of(c * chs, chs)
            hc = h_scr[pl.ds(pl.multiple_of(c * ch, ch), ch), :]
            uc = u_ref[0, pl.ds(s0, chs)].reshape(ch, lanes)
            y = _dotb(hc[:, :L], cr_ref[0]) - _dotb(hc[:, L:], ci_ref[0]) + d_ref[...] * uc
            zz_ref[0, pl.ds(s0, chs)] = _gelu(y).reshape(chs, nseg, lanes)
            return carry

        lax.fori_loop(0, nchunks, emit, 0)

    st_spec = pl.BlockSpec((1, 1, 1, L), lambda b, g: (b, g, 0, 0))
    st_shape = jax.ShapeDtypeStruct((B, S5_NGB, 1, L), F32)
    io_spec = pl.BlockSpec((1, sl, nseg, lanes), lambda b, g: (b, 0, 0, g))
    return pl.pallas_call(
        body, grid=(B, S5_NGB),
        in_specs=[io_spec,
                  pl.BlockSpec((1, lanes, 2 * L), lambda b, g: (g, 0, 0)),
                  pl.BlockSpec((1, L, lanes), lambda b, g: (g, 0, 0)),
                  pl.BlockSpec((1, L, lanes), lambda b, g: (g, 0, 0)),
                  pl.BlockSpec((1, 1, L), lambda b, g: (g, 0, 0)),
                  pl.BlockSpec((1, 1, L), lambda b, g: (g, 0, 0)),
                  pl.BlockSpec((1, lanes), lambda b, g: (0, g))],
        out_specs=[io_spec, st_spec, st_spec],
        out_shape=[jax.ShapeDtypeStruct((B, sl, nseg, D_MODEL), F32), st_shape, st_shape],
        scratch_shapes=[pltpu.VMEM((T, 2 * L), F32)],
        compiler_params=_cp(("parallel", "parallel")), name="s5_prompt",
    )(u, bw, cr, ci, ar, ai, d.reshape(1, -1))


def _s5_sample(u, h0r, h0i, bw, cr, ci, ar, ai, d):
    DB = u.shape[1]
    L = S5_LANES
    lanes = S5_GB * S5_CH

    def body(u_ref, hr_ref, hi_ref, bw_ref, cr_ref, ci_ref, ar_ref, ai_ref, d_ref, zz_ref, sre_ref, sim_ref):
        hr = hr_ref[...]
        hi = hi_ref[...]
        a_re = ar_ref[0]
        a_im = ai_ref[0]
        for t in range(DEC_SEQ):
            uv = u_ref[t]
            bu = _dotf(uv, bw_ref[0])
            hr, hi = a_re * hr - a_im * hi + bu[:, :L], a_re * hi + a_im * hr + bu[:, L:]
            y = _dotf(hr, cr_ref[0]) - _dotf(hi, ci_ref[0]) + d_ref[...] * uv
            zz_ref[t] = _gelu(y).astype(BF16)
        sre_ref[...] = hr
        sim_ref[...] = hi

    st_spec = pl.BlockSpec((DB, L), lambda g: (0, g))
    st_shape = jax.ShapeDtypeStruct((DB, S5_GROUPS * S5_STATE), F32)
    return pl.pallas_call(
        body, grid=(S5_NGB,),
        in_specs=[pl.BlockSpec((DEC_SEQ, DB, lanes), lambda g: (0, 0, g)), st_spec, st_spec,
                  pl.BlockSpec((1, lanes, 2 * L), lambda g: (g, 0, 0)),
                  pl.BlockSpec((1, L, lanes), lambda g: (g, 0, 0)),
                  pl.BlockSpec((1, L, lanes), lambda g: (g, 0, 0)),
                  pl.BlockSpec((1, 1, L), lambda g: (g, 0, 0)),
                  pl.BlockSpec((1, 1, L), lambda g: (g, 0, 0)),
                  pl.BlockSpec((1, lanes), lambda g: (0, g))],
        out_specs=[pl.BlockSpec((DEC_SEQ, DB, lanes), lambda g: (0, 0, g)), st_spec, st_spec],
        out_shape=[jax.ShapeDtypeStruct((DEC_SEQ, DB, D_MODEL), BF16), st_shape, st_shape],
        compiler_params=_cp(("parallel",)), name="s5_sample",
    )(u, h0r, h0i, bw, cr, ci, ar, ai, d.reshape(1, -1))


def _ffn_upgate_prompt(hf, w_up, w_gate, cw, cb, T, tm, tn):
    M, K = hf.shape
    N = w_up.shape[1]
    tiles_per_seq = T // tm
    pr = 16

    def body(a_ref, ap_ref, wu_ref, wg_ref, cw_ref, cb_ref, act_ref, tail_ref):
        i = pl.program_id(0)
        a = a_ref[...]
        hu = jnp.dot(a, wu_ref[...], preferred_element_type=F32)
        hg = jnp.dot(a, wg_ref[...], preferred_element_type=F32)
        prev = jnp.dot(ap_ref[...], wu_ref[...], preferred_element_type=F32)
        prev = jnp.where(i % tiles_per_seq == 0, 0.0, prev)
        row = lax.broadcasted_iota(jnp.int32, hu.shape, 0)
        h1 = jnp.where(row == 0, prev[pr - 1:pr], pltpu.roll(hu, 1, 0))
        h2 = jnp.where(row == 0, prev[pr - 2:pr - 1], jnp.where(row == 1, prev[pr - 1:pr], pltpu.roll(hu, 2, 0)))
        cwv = cw_ref[...]
        hc = cb_ref[...] + h2 * cwv[0:1] + h1 * cwv[1:2] + hu * cwv[2:3]
        act_ref[...] = (_gelu(hc) * hg).astype(BF16)
        tail_ref[...] = hu[tm - 8:tm]

    return pl.pallas_call(
        body, grid=(M // tm, N // tn),
        in_specs=[pl.BlockSpec((tm, K), lambda i, j: (i, 0)),
                  pl.BlockSpec((pr, K), lambda i, j: (jnp.maximum(i * (tm // pr) - 1, 0), 0)),
                  pl.BlockSpec((K, tn), lambda i, j: (0, j)),
                  pl.BlockSpec((K, tn), lambda i, j: (0, j)),
                  pl.BlockSpec((FFN_CONV, tn), lambda i, j: (0, j)),
                  pl.BlockSpec((1, tn), lambda i, j: (0, j))],
        out_specs=[pl.BlockSpec((tm, tn), lambda i, j: (i, j)), pl.BlockSpec((8, tn), lambda i, j: (i, j))],
        out_shape=[jax.ShapeDtypeStruct((M, N), BF16), jax.ShapeDtypeStruct((M // tm * 8, N), F32)],
        compiler_params=_cp(("parallel", "parallel")), name="ffn_upgate_prompt",
    )(hf, hf, w_up, w_gate, cw, cb.reshape(1, -1))


def _tile(m, pref):
    return pref if m % pref == 0 else m


def _dense_tail(x1, p, wts, ffn_state, T):
    (norm_ffn, norm_ple, w_up, w_gate, conv_w, conv_b, w_down, w_proj, w_pgate) = wts
    M = x1.shape[0]
    tm = _tile(M, 1024)
    tn = 512
    prompt = ffn_state is None

    hf = _rmsnorm(x1, norm_ffn, BF16, _tile(M, 256))
    if prompt:
        act, tails = _ffn_upgate_prompt(hf, w_up, w_gate, conv_w, conv_b, T, tm, tn)
        tails = tails.reshape(M // tm, 8, D_FF)
        tiles_per_seq = T // tm
        new_buf = tails[tiles_per_seq - 1::tiles_per_seq, 8 - (FFN_CONV - 1):, :]
    else:
        nseq = M // T
        h1 = jnp.repeat(ffn_state[:, FFN_CONV - 2], T, axis=0)
        h2 = jnp.stack([ffn_state[:, 0], ffn_state[:, 1]] + [ffn_state[:, 1]] * (T - 2), axis=1).reshape(M, D_FF)

        def conv_epilogue(accs, ex):
            hu, hg = accs
            b1, b2, cwv, cbv = ex
            tok = lax.broadcasted_iota(jnp.int32, hu.shape, 0) % T
            s1 = jnp.where(tok >= 1, pltpu.roll(hu, 1, 0), b1)
            s2 = jnp.where(tok >= 2, pltpu.roll(hu, 2, 0), b2)
            hc = cbv + s2 * cwv[0:1] + s1 * cwv[1:2] + hu * cwv[2:3]
            return _gelu(hc) * hg, hu

        act, hu = _mm([hf], [w_up, w_gate], [0, 0],
                      [(h1, "tile"), (h2, "tile"), (conv_w, "rows"), (conv_b.reshape(1, -1), "rows")],
                      conv_epilogue, [BF16, F32], tm, tn, "ffn_upgate_sample")
        new_buf = hu.reshape(nseq, T, D_FF)[:, T - (FFN_CONV - 1):]

    (x2,) = _mm([act], [w_down], [0], [(x1, "tile")], lambda accs, ex: (ex[0] + accs[0],), [F32],
                _tile(M, 512), tn, "ffn_down")
    hp = _rmsnorm(x2, norm_ple, BF16, _tile(M, 256))
    (x3,) = _mm([p, hp], [w_proj, w_pgate], [0, 1], [(x2, "tile")],
                lambda accs, ex: (ex[0] + accs[0] * _sigmoid(accs[1]),), [F32], tm, tn, "ple")
    return x3, new_buf


def kernel(x_prompt, x_sample, cache_cmp_k, cache_cmp_v, cache_sel_k, cache_sel_v, cache_win_k, cache_win_v, state_rglru_h, state_rglru_conv, state_s5_re, state_s5_im, state_ffn_conv, page_table, p_prompt, p_sample, norm_mix, norm_ffn, norm_ple, w_in_even, w_out_even, q_norm, k_norm_cmp, k_norm_sel, k_norm_win, cmp_pe_k, cmp_w1_k, cmp_w2_k, cmp_pe_v, cmp_w1_v, cmp_w2_v, rg_conv_w, rg_conv_b, rg_w_a, rg_b_a, rg_w_x, rg_b_x, rg_lam, s5_lam_re, s5_lam_im, s5_log_dt, s5_b_re, s5_b_im, s5_c_re, s5_c_im, s5_d, s5_w_glu_a, s5_w_glu_b, ffn_w_up, ffn_w_gate, ffn_conv_w, ffn_conv_b, ffn_w_down, ple_w_proj, ple_w_gate):
    B, T, _ = x_prompt.shape
    DB = x_sample.shape[0]
    n_pages = page_table.shape[1]
    past_len = n_pages * PAGE_SIZE
    w_buf = cache_win_k.shape[2]
    MP = B * T
    MS = DB * DEC_SEQ
    depth = norm_mix.shape[0]

    def reorder_in(w):
        q, kv, gt, xr, gr = (w[:, :Q_W], w[:, Q_W:Q_W + 6 * KV_W], w[:, Q_W + 6 * KV_W:Q_W + 6 * KV_W + 3 * N_HEADS],
                             w[:, Q_W + 6 * KV_W + 3 * N_HEADS:Q_W + 6 * KV_W + 3 * N_HEADS + D_RNN],
                             w[:, Q_W + 6 * KV_W + 3 * N_HEADS + D_RNN:])
        pad = jnp.zeros((w.shape[0], IN_PAD - COL_GT - 3 * N_HEADS), w.dtype)
        return jnp.concatenate([q, xr, gr, kv, gt, pad], axis=1).astype(BF16)

    def layer_dense_weights(li):
        return (norm_ffn[li], norm_ple[li], ffn_w_up[li].astype(BF16), ffn_w_gate[li].astype(BF16),
                ffn_conv_w[li], ffn_conv_b[li], ffn_w_down[li].astype(BF16),
                ple_w_proj[li].astype(BF16), ple_w_gate[li].astype(BF16))

    xs = {"p": x_prompt.reshape(MP, D_MODEL), "s": x_sample.reshape(MS, D_MODEL)}
    pe_in = {"p": p_prompt.reshape(depth, MP, -1).astype(BF16), "s": p_sample.reshape(depth, MS, -1).astype(BF16)}
    seq_len = {"p": T, "s": DEC_SEQ}
    ev = {"p": [], "s": []}
    od = {"p": [], "s": []}
    ff = {"p": [], "s": []}
    ptab = page_table.reshape(-1).astype(jnp.int32)

    for li in range(depth):
        dense_w = layer_dense_weights(li)
        if li % 2 == 0:
            e = li // 2
            w_in = reorder_in(w_in_even[e])
            w_out = w_out_even[e].astype(BF16)
            w_out_a, w_out_r = w_out[:Q_W], w_out[Q_W:]
            for grp in ("p", "s"):
                x = xs[grp]
                M = x.shape[0]
                L = seq_len[grp]
                nseq = M // L
                hn = _rmsnorm(x, norm_mix[li], BF16, _tile(M, 256))
                (z,) = _mm([hn], [w_in], [0], [], lambda accs, ex: (accs[0],), [F32], _tile(M, 1024), 512, "in_proj")
                if grp == "p":
                    cos_t, sin_t = _rope_tables(jnp.arange(T))
                    tmq = _tile(T, 512)
                    tab_blocks = T // tmq
                else:
                    cos_t, sin_t = _rope_tables(jnp.tile(past_len + jnp.arange(DEC_SEQ), DB))
                    tmq = M
                    tab_blocks = 1
                q, ks, kw, kvb = _qk_prep(z, cos_t, sin_t, tab_blocks, q_norm[e], k_norm_sel[e], k_norm_win[e], tmq)
                kc = z[:, COL_KV:COL_KV + KV_W]
                vc = z[:, COL_KV + KV_W:COL_KV + 2 * KV_W]
                vs = z[:, COL_KV + 3 * KV_W:COL_KV + 4 * KV_W]
                vw = z[:, COL_KV + 5 * KV_W:COL_KV + 6 * KV_W]
                gt = z[:, COL_GT:COL_GT + 3 * N_HEADS].reshape(nseq, L, 3, N_KV, GROUP)
                gt = gt.transpose(0, 3, 1, 2, 4).reshape(nseq, N_KV, L, 3 * GROUP)
                xr3 = z[:, COL_XR:COL_XR + D_RNN].reshape(nseq, L, D_RNN)

                if grp == "p":
                    n_rows = T // CMP_STRIDE
                    ccos, csin = _rope_tables(jnp.arange(n_rows) * CMP_STRIDE + CMP_STRIDE - 1)
                    ident = jnp.arange(B * (T // PAGE_SIZE), dtype=jnp.int32)
                    kcmp = _compress(kc.reshape(-1, N_KV * PAGE_SIZE, HEAD_DIM), ident, B, T // PAGE_SIZE, cmp_pe_k[e],
                                     cmp_w1_k[e], cmp_w2_k[e], k_norm_cmp[e], ccos, csin, True, "compress_k_prompt")
                    vcmp = _compress(vc.reshape(-1, N_KV * PAGE_SIZE, HEAD_DIM), ident, B, T // PAGE_SIZE, cmp_pe_v[e],
                                     cmp_w1_v[e], cmp_w2_v[e], k_norm_cmp[e], ccos, csin, False, "compress_v_prompt")
                    n_sel = T // SEL_BLOCK
                    nqb = T // Q_BLOCK
                    ovl_t = _overlap_matrix(n_rows, n_sel, n_sel).T
                    ex_t = ((np.arange(T // SEL_KT)[:, None, None] * SEL_KT + np.arange(SEL_KT)[None, :, None])
                            // SEL_BLOCK == np.arange(n_sel)[None, None, :])
                    q_t = q.reshape(B, nqb, Q_BLOCK, N_KV, GROUP, HEAD_DIM).transpose(0, 3, 1, 5, 4, 2)
                    q_t = q_t.reshape(B, N_KV, nqb, HEAD_DIM, GROUP * Q_BLOCK)
                    vs_t = kvb.reshape(B, T // SEL_KT, SEL_KT, 4, N_KV, HEAD_DIM)[:, :, :, 1].transpose(0, 3, 1, 4, 2)
                    vw_t = kvb.reshape(B, nqb, Q_BLOCK, 4, N_KV, HEAD_DIM)[:, :, :, 3].transpose(0, 3, 1, 4, 2)
                    gates_t = z[:, COL_GT:COL_GT + 3 * N_HEADS].reshape(B, T, 3, N_KV, GROUP)
                    gates_t = gates_t.transpose(0, 3, 2, 4, 1).reshape(B, N_KV, 3 * GROUP, T)
                    o_t = _nsa_prompt(q_t, kcmp, vcmp.swapaxes(2, 3), kvb.reshape(B, T, 4 * KV_W), vs_t, vw_t,
                                      gates_t, ovl_t, jnp.asarray(ex_t, BF16), B, T)
                    o_nsa = o_t.reshape(B, N_KV, nqb, HEAD_DIM, GROUP, Q_BLOCK).transpose(0, 2, 5, 1, 4, 3)
                    o_nsa = o_nsa.reshape(M, Q_W)
                    y_rg, h_last = _rglru_prompt(z.reshape(B, T, IN_PAD), rg_conv_w[e], rg_conv_b[e], rg_w_a[e],
                                                 rg_b_a[e], rg_w_x[e], rg_b_x[e], rg_lam[e], B, T, _tile(T, 512))
                    y_rg = y_rg.reshape(M, D_RNN)
                    h_last = h_last.reshape(B, D_RNN)
                    new_conv = xr3[:, T - (RG_CONV - 1):]
                    new_wk = kw.reshape(B, T, N_KV, HEAD_DIM)
                    new_wv = vw.reshape(B, T, N_KV, HEAD_DIM)
                    if T >= w_buf:
                        new_wk, new_wv = new_wk[:, T - w_buf:], new_wv[:, T - w_buf:]
                    else:
                        padw = ((0, 0), (w_buf - T, 0), (0, 0), (0, 0))
                        new_wk, new_wv = jnp.pad(new_wk, padw), jnp.pad(new_wv, padw)
                else:
                    n_rows = past_len // CMP_STRIDE
                    ccos, csin = _rope_tables(jnp.arange(n_rows) * CMP_STRIDE + CMP_STRIDE - 1)
                    ptab_e = ptab + e * cache_cmp_k.shape[1]
                    kcmp = _compress(cache_cmp_k.reshape(-1, N_KV * PAGE_SIZE, HEAD_DIM), ptab_e, DB, n_pages, cmp_pe_k[e],
                                     cmp_w1_k[e], cmp_w2_k[e], k_norm_cmp[e], ccos, csin, True, "compress_k_sample")
                    vcmp = _compress(cache_cmp_v.reshape(-1, N_KV * PAGE_SIZE, HEAD_DIM), ptab_e, DB, n_pages, cmp_pe_v[e],
                                     cmp_w1_v[e], cmp_w2_v[e], k_norm_cmp[e], ccos, csin, False, "compress_v_sample")
                    n_sel = -(-(past_len + DEC_SEQ) // SEL_BLOCK)
                    sel_cols = -(-n_sel // 128) * 128
                    ovl = _overlap_matrix(n_rows, n_sel, sel_cols)
                    keys = PAGES_PER_STEP * PAGE_SIZE
                    steps = n_pages // PAGES_PER_STEP
                    ex = (np.arange(sel_cols)[None, :, None]
                          == (np.arange(steps)[:, None, None] * keys + np.arange(keys)[None, None, :]) // SEL_BLOCK)
                    q8 = jnp.pad(q.reshape(DB, DEC_SEQ, Q_W), ((0, 0), (0, Q_PAD - DEC_SEQ), (0, 0)))
                    kvnew = jnp.pad(kvb.reshape(DB, DEC_SEQ, 4 * KV_W), ((0, 0), (0, 128 - DEC_SEQ), (0, 0)))
                    gt8 = jnp.pad(gt, ((0, 0), (0, 0), (0, Q_PAD - DEC_SEQ), (0, 0)))
                    o8 = _nsa_sample(q8, kcmp, vcmp, cache_sel_k.reshape(-1, N_KV * PAGE_SIZE, HEAD_DIM),
                                     cache_sel_v.reshape(-1, N_KV * PAGE_SIZE, HEAD_DIM), ptab_e, kvnew,
                                     cache_win_k.reshape(-1, N_KV * w_buf, HEAD_DIM),
                                     cache_win_v.reshape(-1, N_KV * w_buf, HEAD_DIM),
                                     e * DB, gt8, ovl, jnp.asarray(ex, BF16), DB, past_len)
                    o_nsa = o8[:, :DEC_SEQ].reshape(M, Q_W)
                    gr3 = z[:, COL_GR:COL_GR + D_RNN].reshape(DB, DEC_SEQ, D_RNN)
                    y_t, h_last = _rglru_sample(xr3.transpose(1, 0, 2), gr3.transpose(1, 0, 2),
                                                state_rglru_conv[e].transpose(1, 0, 2), state_rglru_h[e],
                                                rg_conv_w[e], rg_conv_b[e], rg_w_a[e], rg_b_a[e], rg_w_x[e],
                                                rg_b_x[e], rg_lam[e])
                    y_rg = y_t.transpose(1, 0, 2).reshape(M, D_RNN)
                    new_conv = jnp.concatenate([state_rglru_conv[e], xr3], axis=1)[:, -(RG_CONV - 1):]
                    kk = jnp.concatenate([cache_win_k[e], kw.reshape(DB, DEC_SEQ, N_KV, HEAD_DIM)], axis=1)
                    vv = jnp.concatenate([cache_win_v[e], vw.reshape(DB, DEC_SEQ, N_KV, HEAD_DIM)], axis=1)
                    new_wk, new_wv = kk[:, -w_buf:], vv[:, -w_buf:]

                rs = lambda a: a.reshape(nseq, L, N_KV, HEAD_DIM)
                ev[grp].append((rs(kc), rs(vc), rs(ks), rs(vs), new_wk, new_wv, h_last, new_conv))
                (x1,) = _mm([o_nsa, y_rg], [w_out_a, w_out_r], [0, 1], [(x, "tile")],
                            lambda accs, exs: (exs[0] + accs[0] + accs[1],), [F32], _tile(M, 1024), 512, "mixer_out")
                xs[grp], nb = _dense_tail(x1, pe_in[grp][li], dense_w, None if grp == "p" else state_ffn_conv[li], L)
                ff[grp].append(nb)
        else:
            o = li // 2
            ar, ai, bbr, bbi = _s5_discretize(s5_lam_re[o], s5_lam_im[o], s5_log_dt[o],
                                              s5_b_re[o].transpose(2, 0, 1), s5_b_im[o].transpose(2, 0, 1))
            bw, cr, ci = _s5_block_weights(bbr, bbi, s5_c_re[o], s5_c_im[o])
            ar = ar.reshape(S5_NGB, 1, S5_LANES)
            ai = ai.reshape(S5_NGB, 1, S5_LANES)
            w_a = s5_w_glu_a[o].astype(BF16)
            w_b = s5_w_glu_b[o].astype(BF16)
            for grp in ("p", "s"):
                x = xs[grp]
                M = x.shape[0]
                L = seq_len[grp]
                glu = lambda accs, exs: (exs[0] + accs[0] * _sigmoid(accs[1]),)
                if grp == "p":
                    sl = T // S5_NSEG
                    hn = _rmsnorm_segmented(x, norm_mix[li], B, T, min(256, sl))
                    zz, sre, sim = _s5_prompt(hn.reshape(B, sl, S5_NSEG, D_MODEL), bw.astype(BF16), cr.astype(BF16),
                                              ci.astype(BF16), ar, ai, s5_d[o], B, T)
                    sre = sre.reshape(B, S5_GROUPS, S5_STATE)
                    sim = sim.reshape(B, S5_GROUPS, S5_STATE)
                    (x1,) = _mm([zz.reshape(B, sl, S5_NSEG * D_MODEL)], [w_a, w_b], [0, 0], [(x, "tile")], glu, [F32],
                                sl, 512, "mixer_out", segmented=True)
                else:
                    hn = _rmsnorm(x, norm_mix[li], F32, _tile(M, 256))
                    u = hn.reshape(DB, DEC_SEQ, D_MODEL).transpose(1, 0, 2)
                    zz, sre, sim = _s5_sample(u, state_s5_re[o].reshape(DB, -1), state_s5_im[o].reshape(DB, -1),
                                              bw, cr, ci, ar, ai, s5_d[o])
                    zz = zz.transpose(1, 0, 2).reshape(M, D_MODEL)
                    sre = sre.reshape(DB, S5_GROUPS, S5_STATE)
                    sim = sim.reshape(DB, S5_GROUPS, S5_STATE)
                    (x1,) = _mm([zz], [w_a, w_b], [0, 0], [(x, "tile")], glu, [F32], M, 512, "mixer_out")
                od[grp].append((sre, sim))
                xs[grp], nb = _dense_tail(x1, pe_in[grp][li], dense_w, None if grp == "p" else state_ffn_conv[li], L)
                ff[grp].append(nb)

    def stk(states, i):
        return jnp.stack([s[i] for s in states])

    outs = [xs["p"].reshape(B, T, D_MODEL), xs["s"].reshape(DB, DEC_SEQ, D_MODEL)]
    for i in range(8):
        outs += [stk(ev["p"], i), stk(ev["s"], i)]
    for i in range(2):
        outs += [stk(od["p"], i), stk(od["s"], i)]
    outs += [jnp.stack(ff["p"]), jnp.stack(ff["s"])]
    return tuple(outs)
```

```python
import functools

import numpy as np
import jax
import jax.numpy as jnp
from jax import lax
from jax.experimental import pallas as pl
from jax.experimental.pallas import tpu as pltpu

F32 = jnp.float32
BF16 = jnp.bfloat16

D_MODEL = 2048
DEC_SEQ = 4
PAGE_SIZE = 128
N_HEADS = 8
HEAD_DIM = 128
N_KV = 2
GROUP = N_HEADS // N_KV
ROPE_DIM = HEAD_DIM // 4
ROPE_HALF = ROPE_DIM // 2
ROPE_THETA = 500000.0
CMP_BLOCK = 32
CMP_STRIDE = 16
SEL_BLOCK = 64
SEL_TOPK = 16
N_LOCAL = 2
WINDOW = 512
Q_BLOCK = 128
FORCED_SCORE = 1e9
NEG_BIG = -1e30
ATTN_SCALE = HEAD_DIM ** -0.5
D_RNN = D_MODEL // 2
RG_BLOCKS = 8
RG_BW = D_RNN // RG_BLOCKS
RG_CONV = 4
RG_C = 8.0
S5_CH = 16
S5_GROUPS = D_MODEL // S5_CH
S5_STATE = 64
S5_GB = 8
S5_NGB = S5_GROUPS // S5_GB
S5_LANES = S5_GB * S5_STATE
S5_NSEG = 8
D_FF = ((8 * D_MODEL // 3 + 255) // 256) * 256
FFN_CONV = 3
NORM_EPS = 1e-6
Q_W = N_HEADS * HEAD_DIM
KV_W = N_KV * HEAD_DIM
PAGES_PER_STEP = 8
Q_PAD = 8
SEL_KT = 512

COL_Q = 0
COL_XR = Q_W
COL_GR = Q_W + D_RNN
COL_KV = Q_W + 2 * D_RNN
COL_GT = COL_KV + 6 * KV_W
IN_PAD = 5120


def _cp(sem, vmem_mb=48):
    return pltpu.CompilerParams(dimension_semantics=sem, vmem_limit_bytes=vmem_mb * 1024 * 1024)


def _gelu(x):
    return 0.5 * x * (1.0 + jnp.tanh(0.7978845608028654 * (x + 0.044715 * (x * x * x))))


def _sigmoid(x):
    return 1.0 / (1.0 + jnp.exp(-x))


def _dotb(a, b):
    return jnp.dot(a.astype(BF16), b.astype(BF16), preferred_element_type=F32)


def _dotf(a, b):
    return jnp.dot(a, b, preferred_element_type=F32, precision=lax.Precision.HIGHEST)


def _dot_nt(a, b):
    return lax.dot_general(a, b, (((1,), (1,)), ((), ())), preferred_element_type=F32)


def _masked_softmax(s, mask):
    s = jnp.where(mask, s, NEG_BIG)
    m = jnp.max(s, axis=-1, keepdims=True)
    e = jnp.where(mask, jnp.exp(s - m), 0.0)
    return e / jnp.maximum(jnp.sum(e, axis=-1, keepdims=True), 1e-30)


def _norm_rope(x, g, cos_t, sin_t):
    ms = jnp.mean(x * x, axis=-1, keepdims=True)
    y = x * lax.rsqrt(ms + NORM_EPS) * g
    lane = lax.broadcasted_iota(jnp.int32, y.shape, 1)
    swapped = jnp.where(lane < ROPE_HALF, pltpu.roll(y, HEAD_DIM - ROPE_HALF, 1), pltpu.roll(y, ROPE_HALF, 1))
    return y * cos_t + swapped * sin_t


def _rope_tables(pos):
    inv = ROPE_THETA ** (-jnp.arange(ROPE_HALF, dtype=F32) * 2.0 / ROPE_DIM)
    ang = pos.astype(F32)[:, None] * inv
    cos = jnp.cos(ang)
    sin = jnp.sin(ang)
    n = pos.shape[0]
    ones = jnp.ones((n, HEAD_DIM - ROPE_DIM), F32)
    zeros = jnp.zeros((n, HEAD_DIM - ROPE_DIM), F32)
    return (jnp.concatenate([cos, cos, ones], axis=1), jnp.concatenate([-sin, sin, zeros], axis=1))


def _overlap_matrix(n_rows, n_sel, n_cols):
    n = np.arange(n_rows)[:, None] - 1
    j = np.arange(n_cols)[None]
    c0 = n * CMP_STRIDE
    s0 = j * SEL_BLOCK
    ov = np.minimum(c0 + CMP_BLOCK, s0 + SEL_BLOCK) - np.maximum(c0, s0)
    ov = np.maximum(ov, 0).astype(np.float32) / CMP_BLOCK
    ov = np.where((n >= 0) & (j < n_sel), ov, 0.0)
    return jnp.asarray(ov, F32)


def _rmsnorm(x, g, out_dtype, tm):
    M, D = x.shape

    def body(x_ref, g_ref, o_ref):
        xv = x_ref[...]
        ms = jnp.mean(xv * xv, axis=-1, keepdims=True)
        o_ref[...] = (xv * lax.rsqrt(ms + NORM_EPS) * g_ref[...]).astype(o_ref.dtype)

    return pl.pallas_call(
        body, grid=(M // tm,),
        in_specs=[pl.BlockSpec((tm, D), lambda i: (i, 0)), pl.BlockSpec((1, D), lambda i: (0, 0))],
        out_specs=pl.BlockSpec((tm, D), lambda i: (i, 0)),
        out_shape=jax.ShapeDtypeStruct((M, D), out_dtype),
        compiler_params=_cp(("parallel",)), name="rmsnorm")(x, g.reshape(1, D))


def _rmsnorm_segmented(x, g, B, T, tm):
    D = x.shape[1]
    sl = T // S5_NSEG
    per_seq = T // tm
    per_seg = sl // tm

    def body(x_ref, g_ref, o_ref):
        xv = x_ref[...]
        ms = jnp.mean(xv * xv, axis=-1, keepdims=True)
        o_ref[0] = xv * lax.rsqrt(ms + NORM_EPS) * g_ref[...]

    return pl.pallas_call(
        body, grid=(B * per_seq,),
        in_specs=[pl.BlockSpec((tm, D), lambda i: (i, 0)), pl.BlockSpec((1, D), lambda i: (0, 0))],
        out_specs=pl.BlockSpec((1, tm, D), lambda i: (i // per_seq, (i % per_seq) % per_seg, (i % per_seq) // per_seg)),
        out_shape=jax.ShapeDtypeStruct((B, sl, S5_NSEG * D), F32),
        compiler_params=_cp(("parallel",)), name="rmsnorm_segmented")(x, g.reshape(1, D))


def _rms_rows(xv, g):
    ms = jnp.mean(xv * xv, axis=-1, keepdims=True)
    return xv * lax.rsqrt(ms + NORM_EPS) * g


def _mm(a_list, w_list, pair, extras, epilogue, out_dtypes, tm, tn, name, segmented=False, norm=None):
    N = w_list[0].shape[1]
    na, nw, ne, no = len(a_list), len(w_list), len(extras), len(out_dtypes)
    if segmented:
        M = a_list[0].shape[0] * a_list[0].shape[1] * S5_NSEG
        assert na == 1 and a_list[0].shape[1] == tm
    else:
        M = a_list[0].shape[0]

    def body(*refs):
        if norm is not None:
            g_ref = refs[na + nw + ne]
            an_ref = refs[na + nw + ne + 1 + no]

            @pl.when(pl.program_id(1) == 0)
            def _():
                an_ref[...] = _rms_rows(refs[norm[0]][...], g_ref[...]).astype(BF16)

        a_vals = []
        for k, r in enumerate(refs[:na]):
            if norm is not None and k == norm[0]:
                a_vals.append(an_ref[...])
            else:
                a_vals.append((r[0] if segmented else r[...]).astype(BF16))
        accs = [jnp.dot(a_vals[pair[i]], refs[na + i][...], preferred_element_type=F32) for i in range(nw)]
        ex = [r[...] for r in refs[na + nw:na + nw + ne]]
        res = epilogue(accs, ex)
        first_out = na + nw + ne + (1 if norm is not None else 0)
        for o_ref, v in zip(refs[first_out:first_out + no], res):
            o_ref[...] = v.astype(o_ref.dtype)

    if segmented:
        in_specs = [pl.BlockSpec((1, tm, w_list[0].shape[0]), lambda i, j: (i // S5_NSEG, 0, i % S5_NSEG))]
    else:
        in_specs = [pl.BlockSpec((tm, a.shape[1]), lambda i, j: (i, 0)) for a in a_list]
    in_specs += [pl.BlockSpec((w.shape[0], tn), lambda i, j: (0, j)) for w in w_list]
    for arr, kind in extras:
        if kind == "tile":
            in_specs.append(pl.BlockSpec((tm, tn), lambda i, j: (i, j)))
        else:
            in_specs.append(pl.BlockSpec((arr.shape[0], tn), lambda i, j: (0, j)))
    operands = [*a_list, *w_list, *[e[0] for e in extras]]
    scratch = []
    if norm is not None:
        kn = a_list[norm[0]].shape[1]
        in_specs.append(pl.BlockSpec((1, kn), lambda i, j: (0, 0)))
        operands.append(norm[1].reshape(1, kn))
        scratch.append(pltpu.VMEM((tm, kn), BF16))
    return pl.pallas_call(
        body, grid=(M // tm, N // tn), in_specs=in_specs,
        out_specs=[pl.BlockSpec((tm, tn), lambda i, j: (i, j)) for _ in out_dtypes],
        out_shape=[jax.ShapeDtypeStruct((M, N), dt) for dt in out_dtypes],
        scratch_shapes=scratch,
        compiler_params=_cp(("parallel", "arbitrary" if norm is not None else "parallel")), name=name,
    )(*operands)


def _qk_prep(z, cos_t, sin_t, tab_blocks, qn, kns, knw, tm):
    M = z.shape[0]

    def body(zq_ref, zkv_ref, c_ref, s_ref, qn_ref, kns_ref, knw_ref, q_ref, ks_ref, kw_ref, kvb_ref):
        c = c_ref[...]
        s = s_ref[...]
        for h in range(N_HEADS):
            sl = slice(h * HEAD_DIM, (h + 1) * HEAD_DIM)
            q_ref[:, sl] = _norm_rope(zq_ref[:, sl], qn_ref[...], c, s).astype(BF16)
        for kv in range(N_KV):
            sl = slice(kv * HEAD_DIM, (kv + 1) * HEAD_DIM)
            ks = _norm_rope(zkv_ref[:, 2 * KV_W + kv * HEAD_DIM:2 * KV_W + (kv + 1) * HEAD_DIM], kns_ref[...], c, s)
            kw = _norm_rope(zkv_ref[:, 4 * KV_W + kv * HEAD_DIM:4 * KV_W + (kv + 1) * HEAD_DIM], knw_ref[...], c, s)
            ks_ref[:, sl] = ks
            kw_ref[:, sl] = kw
            kvb_ref[:, sl] = ks.astype(BF16)
            kvb_ref[:, 2 * KV_W + kv * HEAD_DIM:2 * KV_W + (kv + 1) * HEAD_DIM] = kw.astype(BF16)
        kvb_ref[:, KV_W:2 * KV_W] = zkv_ref[:, 3 * KV_W:4 * KV_W].astype(BF16)
        kvb_ref[:, 3 * KV_W:4 * KV_W] = zkv_ref[:, 5 * KV_W:6 * KV_W].astype(BF16)

    vec = pl.BlockSpec((1, HEAD_DIM), lambda i: (0, 0))
    tab = pl.BlockSpec((tm, HEAD_DIM), lambda i: (i % tab_blocks, 0))
    return pl.pallas_call(
        body, grid=(M // tm,),
        in_specs=[pl.BlockSpec((tm, Q_W), lambda i: (i, COL_Q // Q_W)),
                  pl.BlockSpec((tm, 6 * KV_W), lambda i: (i, COL_KV // (6 * KV_W))),
                  tab, tab, vec, vec, vec],
        out_specs=[pl.BlockSpec((tm, Q_W), lambda i: (i, 0)),
                   pl.BlockSpec((tm, KV_W), lambda i: (i, 0)),
                   pl.BlockSpec((tm, KV_W), lambda i: (i, 0)),
                   pl.BlockSpec((tm, 4 * KV_W), lambda i: (i, 0))],
        out_shape=[jax.ShapeDtypeStruct((M, Q_W), BF16), jax.ShapeDtypeStruct((M, KV_W), F32),
                   jax.ShapeDtypeStruct((M, KV_W), F32), jax.ShapeDtypeStruct((M, 4 * KV_W), BF16)],
        compiler_params=_cp(("parallel",)), name="qk_prep",
    )(z, z, cos_t, sin_t, qn.reshape(1, -1), kns.reshape(1, -1), knw.reshape(1, -1))


def _compress_bias(pe, w1):
    def body(pe_ref, w1_ref, o_ref):
        for half in range(2):
            acc = jnp.zeros((8, HEAD_DIM), F32)
            for c in range(half * CMP_STRIDE, (half + 1) * CMP_STRIDE):
                acc = acc + _dotf(jnp.broadcast_to(pe_ref[c:c + 1, :], (8, HEAD_DIM)), w1_ref[c])
            o_ref[:, half * HEAD_DIM:(half + 1) * HEAD_DIM] = acc[0:1, :]

    return pl.pallas_call(body, out_shape=jax.ShapeDtypeStruct((1, 2 * HEAD_DIM), F32), name="compress_bias")(pe, w1)


def _compress(pages, ptab, n_seq, pages_per_seq, pe, w1, w2, norm_g, cos_t, sin_t, do_norm, name):
    bias = _compress_bias(pe, w1)
    w1cat = jnp.concatenate([w1[:CMP_STRIDE], w1[CMP_STRIDE:]], axis=2).astype(BF16)
    G = 2 * PAGES_PER_STEP if pages_per_seq % (2 * PAGES_PER_STEP) == 0 else PAGES_PER_STEP
    steps = pages_per_seq // G
    cpp = PAGE_SIZE // CMP_STRIDE
    ch = G * cpp
    n_chunk = pages_per_seq * cpp

    def body(pt_ref, *refs):
        page_refs = refs[:G]
        bias_ref, w1_ref, w2_ref, g_ref, c_ref, s_ref, o_ref, carry_ref = refs[G:]
        step = pl.program_id(1)

        @pl.when(step == 0)
        def _():
            carry_ref[...] = jnp.zeros_like(carry_ref)

        row = lax.broadcasted_iota(jnp.int32, (ch, HEAD_DIM), 0)
        for kv in range(N_KV):
            acc = jnp.zeros((ch, 2 * HEAD_DIM), F32)
            for c in range(CMP_STRIDE):
                xc = jnp.concatenate(
                    [pr[0, pl.ds(N_KV * c + kv, cpp, stride=N_KV * CMP_STRIDE), :] for pr in page_refs], axis=0)
                acc = acc + _dotb(xc, w1_ref[c])
            acc = acc + bias_ref[...]
            lo = acc[:, :HEAD_DIM]
            hi = acc[:, HEAD_DIM:]
            lo_prev = jnp.where(row == 0, carry_ref[kv, 0:1, :], pltpu.roll(lo, 1, 0))
            carry_ref[kv, 0:1, :] = lo[ch - 1:ch, :]
            out = _dotb(_gelu(lo_prev + hi), w2_ref[...])
            if do_norm:
                out = _norm_rope(out, g_ref[...], c_ref[...], s_ref[...])
            o_ref[0, kv] = out.astype(BF16)

    def page_map(j, s, g, pt):
        return (pt[s * pages_per_seq + g * G + j], 0, 0)

    full2 = lambda s, g, pt: (0, 0)
    in_specs = [pl.BlockSpec((1, N_KV * PAGE_SIZE, HEAD_DIM), functools.partial(page_map, j)) for j in range(G)]
    in_specs += [pl.BlockSpec((1, 2 * HEAD_DIM), full2),
                 pl.BlockSpec((CMP_STRIDE, HEAD_DIM, 2 * HEAD_DIM), lambda s, g, pt: (0, 0, 0)),
                 pl.BlockSpec((HEAD_DIM, HEAD_DIM), full2),
                 pl.BlockSpec((1, HEAD_DIM), full2),
                 pl.BlockSpec((ch, HEAD_DIM), lambda s, g, pt: (g, 0)),
                 pl.BlockSpec((ch, HEAD_DIM), lambda s, g, pt: (g, 0))]
    gs = pltpu.PrefetchScalarGridSpec(
        num_scalar_prefetch=1, grid=(n_seq, steps), in_specs=in_specs,
        out_specs=pl.BlockSpec((1, N_KV, ch, HEAD_DIM), lambda s, g, pt: (s, 0, g, 0)),
        scratch_shapes=[pltpu.VMEM((N_KV, 8, HEAD_DIM), F32)])
    return pl.pallas_call(
        body, grid_spec=gs, out_shape=jax.ShapeDtypeStruct((n_seq, N_KV, n_chunk, HEAD_DIM), BF16),
        compiler_params=_cp(("parallel", "arbitrary")), name=name,
    )(ptab, *([pages] * G), bias, w1cat, w2.astype(BF16), norm_g.reshape(1, -1), cos_t, sin_t)


def _nsa_prompt(q_t, kcmp, vcmp_t, kvb, vs_t, vw_t, gates_t, ovl_t, expand_t, B, T):
    nqb = T // Q_BLOCK
    n_cmp_rows = T // CMP_STRIDE
    n_sel = T // SEL_BLOCK
    cols = GROUP * Q_BLOCK
    win_tiles = WINDOW // Q_BLOCK
    win_keys = (win_tiles + 1) * Q_BLOCK
    assert T % SEL_KT == 0 and T >= win_keys
    topk = min(SEL_TOPK, n_sel)

    def body(q_ref, kc_ref, vc_ref, ks_ref, vs_ref, kw_ref, vw_ref, gt_ref, ovl_ref, ex_ref, o_ref):
        i = pl.program_id(2)
        qt = q_ref[0, 0, 0]
        pos = i * Q_BLOCK + (lax.broadcasted_iota(jnp.int32, (1, cols), 1) & (Q_BLOCK - 1))

        s = jnp.dot(kc_ref[0, 0], qt, preferred_element_type=F32) * ATTN_SCALE
        crow = lax.broadcasted_iota(jnp.int32, (n_cmp_rows, cols), 0)
        cmask = (crow >= 1) & (crow * CMP_STRIDE + (CMP_BLOCK - 1 - CMP_STRIDE) <= pos)
        s = jnp.where(cmask, s, NEG_BIG)
        e = jnp.where(cmask, jnp.exp(s - jnp.max(s, axis=0, keepdims=True)), 0.0)
        p = e / jnp.maximum(jnp.sum(e, axis=0, keepdims=True), 1e-30)
        o_cmp = _dotb(vc_ref[0, 0], p)
        imp = p[:, 0:Q_BLOCK]
        for h in range(1, GROUP):
            imp = imp + p[:, h * Q_BLOCK:(h + 1) * Q_BLOCK]

        score = _dotf(ovl_ref[...], imp)
        j = lax.broadcasted_iota(jnp.int32, (n_sel, Q_BLOCK), 0)
        qblk = (i * Q_BLOCK + lax.broadcasted_iota(jnp.int32, (n_sel, Q_BLOCK), 1)) // SEL_BLOCK
        forced = (j == 0) | ((j <= qblk) & (j > qblk - N_LOCAL))
        score = jnp.where(forced, FORCED_SCORE, jnp.where(j > qblk, NEG_BIG, score))
        rank = jnp.zeros((n_sel, Q_BLOCK), F32)
        for r in range(n_sel):
            sr = score[r:r + 1, :]
            beats = (sr > score) | ((sr == score) & (j > r))
            rank = rank + jnp.where(beats, 1.0, 0.0)
        sel = jnp.where(rank < topk, 1.0, 0.0).astype(BF16)

        krow = lax.broadcasted_iota(jnp.int32, (SEL_KT, cols), 0)

        def sel_tile(u, carry):
            m, l, acc = carry
            k = ks_ref[0, pl.ds(pl.multiple_of(u * SEL_KT, SEL_KT), SEL_KT), :]
            sc = jnp.dot(k, qt, preferred_element_type=F32) * ATTN_SCALE
            m1 = _dotb(ex_ref[u], sel)
            mask = (jnp.concatenate([m1] * GROUP, axis=1) > 0.5) & (u * SEL_KT + krow <= pos)
            sc = jnp.where(mask, sc, NEG_BIG)
            m_new = jnp.maximum(m, jnp.max(sc, axis=0, keepdims=True))
            alpha = jnp.exp(m - m_new)
            ex = jnp.where(mask, jnp.exp(sc - m_new), 0.0)
            l = alpha * l + jnp.sum(ex, axis=0, keepdims=True)
            acc = alpha * acc + _dotb(vs_ref[0, 0, u], ex)
            return m_new, l, acc

        init = (jnp.full((1, cols), NEG_BIG, F32), jnp.zeros((1, cols), F32), jnp.zeros((HEAD_DIM, cols), F32))
        _, l_sel, acc_sel = lax.fori_loop(0, (i * Q_BLOCK + Q_BLOCK + SEL_KT - 1) // SEL_KT, sel_tile, init)
        o_sel = acc_sel / jnp.maximum(l_sel, 1e-30)

        t0 = jnp.maximum(i - win_tiles, 0)
        kwin = kw_ref[0, pl.ds(pl.multiple_of(t0 * Q_BLOCK, Q_BLOCK), win_keys), :]
        sw = jnp.dot(kwin, qt, preferred_element_type=F32) * ATTN_SCALE
        kpos = t0 * Q_BLOCK + lax.broadcasted_iota(jnp.int32, (win_keys, cols), 0)
        wmask = (kpos <= pos) & (kpos > pos - WINDOW)
        sw = jnp.where(wmask, sw, NEG_BIG)
        ew = jnp.where(wmask, jnp.exp(sw - jnp.max(sw, axis=0, keepdims=True)), 0.0)
        pw = ew / jnp.maximum(jnp.sum(ew, axis=0, keepdims=True), 1e-30)
        o_win = jnp.zeros((HEAD_DIM, cols), F32)
        for t in range(win_tiles + 1):
            o_win = o_win + _dotb(vw_ref[0, 0, t0 + t], pw[t * Q_BLOCK:(t + 1) * Q_BLOCK])

        g = _sigmoid(gt_ref[0, 0])
        for h in range(GROUP):
            c = slice(h * Q_BLOCK, (h + 1) * Q_BLOCK)
            o = (g[h:h + 1, :] * o_cmp[:, c] + g[GROUP + h:GROUP + h + 1, :] * o_sel[:, c]
                 + g[2 * GROUP + h:2 * GROUP + h + 1, :] * o_win[:, c])
            o_ref[0, 0, 0, :, c] = o.astype(BF16)

    def k_spec(off):
        return pl.BlockSpec((1, T, HEAD_DIM), lambda b, kv, i: (b, 0, off + kv))

    vs_spec = pl.BlockSpec((1, 1, T // SEL_KT, HEAD_DIM, SEL_KT), lambda b, kv, i: (b, kv, 0, 0, 0))
    vw_spec = pl.BlockSpec((1, 1, nqb, HEAD_DIM, Q_BLOCK), lambda b, kv, i: (b, kv, 0, 0, 0))
    qo_spec = pl.BlockSpec((1, 1, 1, HEAD_DIM, cols), lambda b, kv, i: (b, kv, i, 0, 0))
    return pl.pallas_call(
        body, grid=(B, N_KV, nqb),
        in_specs=[qo_spec,
                  pl.BlockSpec((1, 1, n_cmp_rows, HEAD_DIM), lambda b, kv, i: (b, kv, 0, 0)),
                  pl.BlockSpec((1, 1, HEAD_DIM, n_cmp_rows), lambda b, kv, i: (b, kv, 0, 0)),
                  k_spec(0), vs_spec, k_spec(4), vw_spec,
                  pl.BlockSpec((1, 1, 3 * GROUP, Q_BLOCK), lambda b, kv, i: (b, kv, 0, i)),
                  pl.BlockSpec((n_sel, n_cmp_rows), lambda b, kv, i: (0, 0)),
                  pl.BlockSpec((T // SEL_KT, SEL_KT, n_sel), lambda b, kv, i: (0, 0, 0))],
        out_specs=qo_spec,
        out_shape=jax.ShapeDtypeStruct((B, N_KV, nqb, HEAD_DIM, cols), BF16),
        compiler_params=_cp(("parallel", "parallel", "arbitrary")), name="nsa_prompt",
    )(q_t, kcmp, vcmp_t, kvb, vs_t, kvb, vw_t, gates_t, ovl_t, expand_t)


def _nsa_sample(q, kcmp, vcmp, pool_k, pool_v, ptab, kvnew, win_k, win_v, win_off, gates, ovl, expand, DB, past_len):
    G = PAGES_PER_STEP
    n_pages = past_len // PAGE_SIZE
    steps = n_pages // G
    n_cmp_rows = kcmp.shape[2]
    sel_cols = ovl.shape[1]
    n_past_blk = past_len // SEL_BLOCK
    n_sel = -(-(past_len + DEC_SEQ) // SEL_BLOCK)
    w_buf = win_k.shape[1] // N_KV
    rows = GROUP * Q_PAD
    keys = G * PAGE_SIZE
    new_rows = kvnew.shape[1]

    def body(pt_ref, q_ref, kc_ref, vc_ref, ovl_ref, *rest):
        pk = rest[:G]
        pv = rest[G:2 * G]
        (ex_ref, kvn_ref, wk_ref, wv_ref, gt_ref, o_ref,
         sel_scr, m_scr, l_scr, acc_scr, base_scr) = rest[2 * G:]
        step = pl.program_id(1)
        tok = lax.broadcasted_iota(jnp.int32, (rows, 1), 0) & (Q_PAD - 1)
        q8 = q_ref[0]

        def q_rows(kv):
            return jnp.concatenate(
                [q8[:, (kv * GROUP + h) * HEAD_DIM:(kv * GROUP + h + 1) * HEAD_DIM] for h in range(GROUP)], axis=0)

        def head_col(gt, c0):
            return jnp.concatenate(
                [jnp.broadcast_to(gt[:, c0 + h:c0 + h + 1], (Q_PAD, HEAD_DIM)) for h in range(GROUP)], axis=0)

        @pl.when(step == 0)
        def _():
            o_cmps = []
            imps = []
            for kv in range(N_KV):
                s = _dot_nt(q_rows(kv), kc_ref[0, kv]) * ATTN_SCALE
                col = lax.broadcasted_iota(jnp.int32, (rows, n_cmp_rows), 1)
                cmask = (col >= 1) & (col * CMP_STRIDE + (CMP_BLOCK - 1 - CMP_STRIDE) <= past_len + tok)
                p = _masked_softmax(s, cmask)
                o_cmps.append(_dotb(p, vc_ref[0, kv]))
                imp = p[0:Q_PAD]
                for h in range(1, GROUP):
                    imp = imp + p[h * Q_PAD:(h + 1) * Q_PAD]
                imps.append(imp)

            score = _dotf(jnp.concatenate(imps, axis=0), ovl_ref[...])
            j = lax.broadcasted_iota(jnp.int32, (N_KV * Q_PAD, sel_cols), 1)
            trow = lax.broadcasted_iota(jnp.int32, (N_KV * Q_PAD, sel_cols), 0) & (Q_PAD - 1)
            qblk = (past_len + trow) // SEL_BLOCK
            forced = (j == 0) | ((j <= qblk) & (j > qblk - N_LOCAL))
            score = jnp.where(forced, FORCED_SCORE, jnp.where(j > qblk, NEG_BIG, score))
            score = jnp.where(j < n_sel, score, -jnp.inf)
            rank = jnp.zeros((N_KV * Q_PAD, sel_cols), F32)
            for r in range(n_sel):
                sr = score[:, r:r + 1]
                rank = rank + jnp.where((sr > score) | ((sr == score) & (j > r)), 1.0, 0.0)
            sel_all = jnp.where((rank < min(SEL_TOPK, n_sel)) & (j < n_sel), 1.0, 0.0)

            for kv in range(N_KV):
                qh = q_rows(kv)
                o_cmp = o_cmps[kv]
                sel4 = jnp.concatenate([sel_all[kv * Q_PAD:(kv + 1) * Q_PAD]] * GROUP, axis=0)
                sel_scr[kv] = sel4.astype(BF16)

                kb = wk_ref[0, pl.ds(kv, w_buf, stride=N_KV), :].astype(BF16)
                vb = wv_ref[0, pl.ds(kv, w_buf, stride=N_KV), :].astype(BF16)
                kn = kvn_ref[0, :, (4 + kv) * HEAD_DIM:(5 + kv) * HEAD_DIM]
                vn = kvn_ref[0, :, (6 + kv) * HEAD_DIM:(7 + kv) * HEAD_DIM]
                s1 = jnp.where(lax.broadcasted_iota(jnp.int32, (rows, w_buf), 1) - w_buf > tok - WINDOW,
                               _dot_nt(qh, kb) * ATTN_SCALE, NEG_BIG)
                ncol = lax.broadcasted_iota(jnp.int32, (rows, new_rows), 1)
                nmask = (ncol <= tok) & (ncol < DEC_SEQ)
                s2 = jnp.where(nmask, _dot_nt(qh, kn) * ATTN_SCALE, NEG_BIG)
                m = jnp.maximum(jnp.max(s1, axis=-1, keepdims=True), jnp.max(s2, axis=-1, keepdims=True))
                e1 = jnp.where(s1 > 0.5 * NEG_BIG, jnp.exp(s1 - m), 0.0)
                e2 = jnp.where(nmask, jnp.exp(s2 - m), 0.0)
                den = jnp.sum(e1, axis=-1, keepdims=True) + jnp.sum(e2, axis=-1, keepdims=True)
                o_win = (_dotb(e1, vb) + _dotb(e2, vn)) / jnp.maximum(den, 1e-30)

                gt = _sigmoid(gt_ref[0, kv])
                base_scr[kv] = head_col(gt, 0) * o_cmp + head_col(gt, 2 * GROUP) * o_win

                ksn = kvn_ref[0, :, kv * HEAD_DIM:(kv + 1) * HEAD_DIM]
                vsn = kvn_ref[0, :, (2 + kv) * HEAD_DIM:(3 + kv) * HEAD_DIM]
                chosen = sel4[:, n_past_blk:n_past_blk + 1] > 0.5
                smask = nmask & chosen
                s3 = jnp.where(smask, _dot_nt(qh, ksn) * ATTN_SCALE, NEG_BIG)
                m0 = jnp.max(s3, axis=-1, keepdims=True)
                e3 = jnp.where(smask, jnp.exp(s3 - m0), 0.0)
                m_scr[kv] = jnp.broadcast_to(m0, (rows, HEAD_DIM))
                l_scr[kv] = jnp.broadcast_to(jnp.sum(e3, axis=-1, keepdims=True), (rows, HEAD_DIM))
                acc_scr[kv] = _dotb(e3, vsn)

        for kv in range(N_KV):
            qh = q_rows(kv)
            k = jnp.concatenate([r[0, pl.ds(kv, PAGE_SIZE, stride=N_KV), :] for r in pk], axis=0).astype(BF16)
            v = jnp.concatenate([r[0, pl.ds(kv, PAGE_SIZE, stride=N_KV), :] for r in pv], axis=0).astype(BF16)
            mask = _dotb(sel_scr[kv], ex_ref[0]) > 0.5
            sc = jnp.where(mask, _dot_nt(qh, k) * ATTN_SCALE, NEG_BIG)
            m_old = m_scr[kv][:, 0:1]
            l_old = l_scr[kv][:, 0:1]
            m_new = jnp.maximum(m_old, jnp.max(sc, axis=-1, keepdims=True))
            alpha = jnp.exp(m_old - m_new)
            e = jnp.where(mask, jnp.exp(sc - m_new), 0.0)
            l_new = alpha * l_old + jnp.sum(e, axis=-1, keepdims=True)
            acc = alpha * acc_scr[kv] + _dotb(e, v)
            m_scr[kv] = jnp.broadcast_to(m_new, (rows, HEAD_DIM))
            l_scr[kv] = jnp.broadcast_to(l_new, (rows, HEAD_DIM))
            acc_scr[kv] = acc

        @pl.when(step == steps - 1)
        def _():
            for kv in range(N_KV):
                gt = _sigmoid(gt_ref[0, kv])
                o_sel = acc_scr[kv] / jnp.maximum(l_scr[kv], 1e-30)
                o = base_scr[kv] + head_col(gt, GROUP) * o_sel
                for h in range(GROUP):
                    hh = kv * GROUP + h
                    o_ref[0, :, hh * HEAD_DIM:(hh + 1) * HEAD_DIM] = o[h * Q_PAD:(h + 1) * Q_PAD].astype(BF16)

    def page_map(jj, b, g, pt):
        return (pt[b * n_pages + g * G + jj], 0, 0)

    page_specs = [pl.BlockSpec((1, N_KV * PAGE_SIZE, HEAD_DIM), functools.partial(page_map, jj)) for jj in range(G)]
    cmp_spec = pl.BlockSpec((1, N_KV, n_cmp_rows, HEAD_DIM), lambda b, g, pt: (b, 0, 0, 0))
    in_specs = [pl.BlockSpec((1, Q_PAD, Q_W), lambda b, g, pt: (b, 0, 0)), cmp_spec, cmp_spec,
                pl.BlockSpec((n_cmp_rows, sel_cols), lambda b, g, pt: (0, 0))]
    in_specs += page_specs + page_specs
    in_specs += [pl.BlockSpec((1, sel_cols, keys), lambda b, g, pt: (g, 0, 0)),
                 pl.BlockSpec((1, new_rows, 4 * KV_W), lambda b, g, pt: (b, 0, 0)),
                 pl.BlockSpec((1, N_KV * w_buf, HEAD_DIM), lambda b, g, pt: (win_off + b, 0, 0)),
                 pl.BlockSpec((1, N_KV * w_buf, HEAD_DIM), lambda b, g, pt: (win_off + b, 0, 0)),
                 pl.BlockSpec((1, N_KV, Q_PAD, 3 * GROUP), lambda b, g, pt: (b, 0, 0, 0))]
    gs = pltpu.PrefetchScalarGridSpec(
        num_scalar_prefetch=1, grid=(DB, steps), in_specs=in_specs,
        out_specs=pl.BlockSpec((1, Q_PAD, Q_W), lambda b, g, pt: (b, 0, 0)),
        scratch_shapes=[pltpu.VMEM((N_KV, rows, sel_cols), BF16), pltpu.VMEM((N_KV, rows, HEAD_DIM), F32),
                        pltpu.VMEM((N_KV, rows, HEAD_DIM), F32), pltpu.VMEM((N_KV, rows, HEAD_DIM), F32),
                        pltpu.VMEM((N_KV, rows, HEAD_DIM), F32)])
    return pl.pallas_call(
        body, grid_spec=gs, out_shape=jax.ShapeDtypeStruct((DB, Q_PAD, Q_W), BF16),
        compiler_params=_cp(("parallel", "arbitrary")), name="nsa_sample",
    )(ptab, q, kcmp, vcmp, ovl, *([pool_k] * G), *([pool_v] * G), expand, kvnew, win_k, win_v, gates)


def _rg_gates(u, wa_ref, ba, wx_ref, bx, sp):
    ra = []
    rx = []
    for n in range(RG_BLOCKS):
        ub = u[:, n * RG_BW:(n + 1) * RG_BW]
        ra.append(_dotf(ub, wa_ref[n]))
        rx.append(_dotf(ub, wx_ref[n]))
    r = _sigmoid(jnp.concatenate(ra, axis=1) + ba)
    i = _sigmoid(jnp.concatenate(rx, axis=1) + bx)
    log_a = -RG_C * r * sp
    a = jnp.exp(log_a)
    b = jnp.sqrt(1.0 - jnp.exp(2.0 * log_a)) * i * u
    return a, b


def _softplus_neg(lam):
    return jnp.maximum(-lam, 0.0) + jnp.log(1.0 + jnp.exp(-jnp.abs(lam)))


def _rglru_prompt(z, cw, cb, wa, ba, wx, bx, lam, B, T, tt):
    keep = 8

    def body(xr_ref, gr_ref, cw_ref, cb_ref, wa_ref, ba_ref, wx_ref, bx_ref, lam_ref,
             y_ref, hl_ref, a_scr, b_scr, h_scr, tail_scr):
        t = pl.program_id(1)

        @pl.when(t == 0)
        def _():
            h_scr[...] = jnp.zeros_like(h_scr)
            tail_scr[...] = jnp.zeros_like(tail_scr)

        x = xr_ref[0]
        row = lax.broadcasted_iota(jnp.int32, (tt, D_RNN), 0)
        cwv = cw_ref[...]
        u = cb_ref[...] + x * cwv[RG_CONV - 1:RG_CONV]
        for d in range(1, RG_CONV):
            xs = pltpu.roll(x, d, 0)
            for r in range(d):
                xs = jnp.where(row == r, tail_scr[keep - d + r:keep - d + r + 1, :], xs)
            u = u + xs * cwv[RG_CONV - 1 - d:RG_CONV - d]
        tail_scr[...] = x[tt - keep:tt]
        a, b = _rg_gates(u, wa_ref, ba_ref[...], wx_ref, bx_ref[...], _softplus_neg(lam_ref[...]))
        a_scr[...] = a
        b_scr[...] = b

        def step(s, h):
            h = a_scr[pl.ds(s, 1), :] * h + b_scr[pl.ds(s, 1), :]
            b_scr[pl.ds(s, 1), :] = h
            return h

        h = lax.fori_loop(0, tt, step, h_scr[0:1, :], unroll=8)
        h_scr[0:1, :] = h
        hl_ref[0] = h
        y_ref[0] = (b_scr[...] * _gelu(gr_ref[0])).astype(BF16)

    vec = pl.BlockSpec((1, D_RNN), lambda b, t: (0, 0))
    wspec = pl.BlockSpec((RG_BLOCKS, RG_BW, RG_BW), lambda b, t: (0, 0, 0))
    return pl.pallas_call(
        body, grid=(B, T // tt),
        in_specs=[pl.BlockSpec((1, tt, D_RNN), lambda b, t: (b, t, COL_XR // D_RNN)),
                  pl.BlockSpec((1, tt, D_RNN), lambda b, t: (b, t, COL_GR // D_RNN)),
                  pl.BlockSpec((RG_CONV, D_RNN), lambda b, t: (0, 0)), vec, wspec, vec, wspec, vec, vec],
        out_specs=[pl.BlockSpec((1, tt, D_RNN), lambda b, t: (b, t, 0)),
                   pl.BlockSpec((1, 1, D_RNN), lambda b, t: (b, 0, 0))],
        out_shape=[jax.ShapeDtypeStruct((B, T, D_RNN), BF16), jax.ShapeDtypeStruct((B, 1, D_RNN), F32)],
        scratch_shapes=[pltpu.VMEM((tt, D_RNN), F32), pltpu.VMEM((tt, D_RNN), F32),
                        pltpu.VMEM((8, D_RNN), F32), pltpu.VMEM((keep, D_RNN), F32)],
        compiler_params=_cp(("parallel", "arbitrary")), name="rglru_prompt",
    )(z, z, cw, cb.reshape(1, -1), wa, ba.reshape(1, -1), wx, bx.reshape(1, -1), lam.reshape(1, -1))


def _rglru_sample(xr, gr, buf, h0, cw, cb, wa, ba, wx, bx, lam):
    DB = h0.shape[0]

    def body(xr_ref, gr_ref, buf_ref, h0_ref, cw_ref, cb_ref, wa_ref, ba_ref, wx_ref, bx_ref, lam_ref, y_ref, hl_ref):
        xp = [buf_ref[k] for k in range(RG_CONV - 1)] + [xr_ref[t] for t in range(DEC_SEQ)]
        cwv = cw_ref[...]
        sp = _softplus_neg(lam_ref[...])
        h = h0_ref[...]
        for t in range(DEC_SEQ):
            u = cb_ref[...]
            for k in range(RG_CONV):
                u = u + xp[t + k] * cwv[k:k + 1]
            a, b = _rg_gates(u, wa_ref, ba_ref[...], wx_ref, bx_ref[...], sp)
            h = a * h + b
            y_ref[t] = (h * _gelu(gr_ref[t])).astype(BF16)
        hl_ref[...] = h

    return pl.pallas_call(
        body, out_shape=[jax.ShapeDtypeStruct((DEC_SEQ, DB, D_RNN), BF16), jax.ShapeDtypeStruct((DB, D_RNN), F32)],
        compiler_params=pltpu.CompilerParams(vmem_limit_bytes=32 * 1024 * 1024), name="rglru_sample",
    )(xr, gr, buf, h0, cw, cb.reshape(1, -1), wa, ba.reshape(1, -1), wx, bx.reshape(1, -1), lam.reshape(1, -1))


def _s5_discretize(lam_re, lam_im, log_dt, b_re_t, b_im_t):
    def body(lr_ref, li_ref, ldt_ref, br_ref, bi_ref, ar_ref, ai_ref, bbr_ref, bbi_ref):
        lr = lr_ref[...]
        li = li_ref[...]
        dt = jnp.exp(ldt_ref[...])
        mag = jnp.exp(lr * dt)
        ar = mag * jnp.cos(li * dt)
        ai = mag * jnp.sin(li * dt)
        den = lr * lr + li * li
        n_re = ar - 1.0
        f_re = (n_re * lr + ai * li) / den
        f_im = (ai * lr - n_re * li) / den
        ar_ref[...] = ar
        ai_ref[...] = ai
        for c in range(S5_CH):
            bbr_ref[c] = f_re * br_ref[c] - f_im * bi_ref[c]
            bbi_ref[c] = f_re * bi_ref[c] + f_im * br_ref[c]

    gp = jax.ShapeDtypeStruct(lam_re.shape, F32)
    cgp = jax.ShapeDtypeStruct(b_re_t.shape, F32)
    return pl.pallas_call(body, out_shape=[gp, gp, cgp, cgp], name="s5_discretize")(
        lam_re, lam_im, log_dt.reshape(-1, 1), b_re_t, b_im_t)


def _s5_block_weights(bbr, bbi, c_re, c_im):
    eye = jnp.eye(S5_GB, dtype=F32)

    def bblk(bb):
        x = bb.reshape(S5_CH, S5_NGB, S5_GB, S5_STATE)
        x = jnp.einsum("cngp,gh->ngchp", x, eye)
        return x.reshape(S5_NGB, S5_GB * S5_CH, S5_LANES)

    def cblk(c):
        x = c.reshape(S5_NGB, S5_GB, S5_CH, S5_STATE)
        x = jnp.einsum("ngcp,gh->ngphc", x, eye)
        return x.reshape(S5_NGB, S5_LANES, S5_GB * S5_CH)

    return jnp.concatenate([bblk(bbr), bblk(bbi)], axis=2), cblk(c_re), cblk(c_im)


def _s5_prompt(u, bw, cr, ci, ar, ai, d, B, T):
    L = S5_LANES
    lanes = S5_GB * S5_CH
    nseg = S5_NSEG
    sl = T // nseg
    assert sl & (sl - 1) == 0, "segment length must be a power of two"
    chs = min(64, sl)
    ch = chs * nseg
    nchunks = sl // chs

    def body(u_ref, bw_ref, cr_ref, ci_ref, ar_ref, ai_ref, d_ref, zz_ref, sre_ref, sim_ref, h_scr):
        a_re = ar_ref[0]
        a_im = ai_ref[0]
        ar8 = jnp.broadcast_to(a_re, (nseg, L))
        ai8 = jnp.broadcast_to(a_im, (nseg, L))

        def project(c, carry):
            s0 = pl.multiple_of(c * chs, chs)
            uc = u_ref[0, pl.ds(s0, chs)].reshape(ch, lanes)
            h_scr[pl.ds(pl.multiple_of(c * ch, ch), ch), :] = _dotb(uc, bw_ref[0])
            return carry

        lax.fori_loop(0, nchunks, project, 0)

        def advance(s, carry, store):
            hr, hi = carry
            rows = pl.ds(pl.multiple_of(s * nseg, nseg), nseg)
            bu = h_scr[rows, :]
            nr = ar8 * hr - ai8 * hi + bu[:, :L]
            ni = ar8 * hi + ai8 * hr + bu[:, L:]
            if store:
                h_scr[rows, :] = jnp.concatenate([nr, ni], axis=1)
            return nr, ni

        zero = jnp.zeros((nseg, L), F32)
        er, ei = lax.fori_loop(0, sl, functools.partial(advance, store=False), (zero, zero), unroll=8)

        pr, pi = a_re, a_im
        for _ in range(sl.bit_length() - 1):
            pr, pi = pr * pr - pi * pi, 2.0 * pr * pi
        sr = [jnp.zeros((1, L), F32)]
        si = [jnp.zeros((1, L), F32)]
        for k in range(nseg):
            sr.append(er[k:k + 1] + pr * sr[k] - pi * si[k])
            si.append(ei[k:k + 1] + pr * si[k] + pi * sr[k])
        sre_ref[0, 0] = sr[nseg]
        sim_ref[0, 0] = si[nseg]
        start = (jnp.concatenate(sr[:nseg], axis=0), jnp.concatenate(si[:nseg], axis=0))
        lax.fori_loop(0, sl, functools.partial(advance, store=True), start, unroll=8)

        def emit(c, carry):
            s0 = pl.multiple_of(c * chs, chs)
            hc = h_scr[pl.ds(pl.multiple_of(c * ch, ch), ch), :]
            uc = u_ref[0, pl.ds(s0, chs)].reshape(ch, lanes)
            y = _dotb(hc[:, :L], cr_ref[0]) - _dotb(hc[:, L:], ci_ref[0]) + d_ref[...] * uc
            zz_ref[0, pl.ds(s0, chs)] = _gelu(y).reshape(chs, nseg, lanes)
            return carry

        lax.fori_loop(0, nchunks, emit, 0)

    st_spec = pl.BlockSpec((1, 1, 1, L), lambda b, g: (b, g, 0, 0))
    st_shape = jax.ShapeDtypeStruct((B, S5_NGB, 1, L), F32)
    io_spec = pl.BlockSpec((1, sl, nseg, lanes), lambda b, g: (b, 0, 0, g))
    return pl.pallas_call(
        body, grid=(B, S5_NGB),
        in_specs=[io_spec,
                  pl.BlockSpec((1, lanes, 2 * L), lambda b, g: (g, 0, 0)),
                  pl.BlockSpec((1, L, lanes), lambda b, g: (g, 0, 0)),
                  pl.BlockSpec((1, L, lanes), lambda b, g: (g, 0, 0)),
                  pl.BlockSpec((1, 1, L), lambda b, g: (g, 0, 0)),
                  pl.BlockSpec((1, 1, L), lambda b, g: (g, 0, 0)),
                  pl.BlockSpec((1, lanes), lambda b, g: (0, g))],
        out_specs=[io_spec, st_spec, st_spec],
        out_shape=[jax.ShapeDtypeStruct((B, sl, nseg, D_MODEL), F32), st_shape, st_shape],
        scratch_shapes=[pltpu.VMEM((T, 2 * L), F32)],
        compiler_params=_cp(("parallel", "parallel")), name="s5_prompt",
    )(u, bw, cr, ci, ar, ai, d.reshape(1, -1))


def _s5_sample(u, h0r, h0i, bw, cr, ci, ar, ai, d):
    DB = u.shape[1]
    L = S5_LANES
    lanes = S5_GB * S5_CH

    def body(u_ref, hr_ref, hi_ref, bw_ref, cr_ref, ci_ref, ar_ref, ai_ref, d_ref, zz_ref, sre_ref, sim_ref):
        hr = hr_ref[...]
        hi = hi_ref[...]
        a_re = ar_ref[0]
        a_im = ai_ref[0]
        for t in range(DEC_SEQ):
            uv = u_ref[t]
            bu = _dotf(uv, bw_ref[0])
            hr, hi = a_re * hr - a_im * hi + bu[:, :L], a_re * hi + a_im * hr + bu[:, L:]
            y = _dotf(hr, cr_ref[0]) - _dotf(hi, ci_ref[0]) + d_ref[...] * uv
            zz_ref[t] = _gelu(y).astype(BF16)
        sre_ref[...] = hr
        sim_ref[...] = hi

    st_spec = pl.BlockSpec((DB, L), lambda g: (0, g))
    st_shape = jax.ShapeDtypeStruct((DB, S5_GROUPS * S5_STATE), F32)
    return pl.pallas_call(
        body, grid=(S5_NGB,),
        in_specs=[pl.BlockSpec((DEC_SEQ, DB, lanes), lambda g: (0, 0, g)), st_spec, st_spec,
                  pl.BlockSpec((1, lanes, 2 * L), lambda g: (g, 0, 0)),
                  pl.BlockSpec((1, L, lanes), lambda g: (g, 0, 0)),
                  pl.BlockSpec((1, L, lanes), lambda g: (g, 0, 0)),
                  pl.BlockSpec((1, 1, L), lambda g: (g, 0, 0)),
                  pl.BlockSpec((1, 1, L), lambda g: (g, 0, 0)),
                  pl.BlockSpec((1, lanes), lambda g: (0, g))],
        out_specs=[pl.BlockSpec((DEC_SEQ, DB, lanes), lambda g: (0, 0, g)), st_spec, st_spec],
        out_shape=[jax.ShapeDtypeStruct((DEC_SEQ, DB, D_MODEL), BF16), st_shape, st_shape],
        compiler_params=_cp(("parallel",)), name="s5_sample",
    )(u, h0r, h0i, bw, cr, ci, ar, ai, d.reshape(1, -1))


def _ffn_upgate_prompt(x, g, w_up, w_gate, cw, cb, T, tm, tn):
    M, K = x.shape
    N = w_up.shape[1]
    tiles_per_seq = T // tm
    pr = 16

    def body(x_ref, xp_ref, g_ref, wu_ref, wg_ref, cw_ref, cb_ref, act_ref, tail_ref, a_scr):
        i = pl.program_id(0)

        @pl.when(pl.program_id(1) == 0)
        def _():
            a_scr[...] = _rms_rows(x_ref[...], g_ref[...]).astype(BF16)

        a = a_scr[...]
        hu = jnp.dot(a, wu_ref[...], preferred_element_type=F32)
        hg = jnp.dot(a, wg_ref[...], preferred_element_type=F32)
        ap = _rms_rows(xp_ref[...], g_ref[...]).astype(BF16)
        prev = jnp.dot(ap, wu_ref[...], preferred_element_type=F32)
        prev = jnp.where(i % tiles_per_seq == 0, 0.0, prev)
        row = lax.broadcasted_iota(jnp.int32, hu.shape, 0)
        h1 = jnp.where(row == 0, prev[pr - 1:pr], pltpu.roll(hu, 1, 0))
        h2 = jnp.where(row == 0, prev[pr - 2:pr - 1], jnp.where(row == 1, prev[pr - 1:pr], pltpu.roll(hu, 2, 0)))
        cwv = cw_ref[...]
        hc = cb_ref[...] + h2 * cwv[0:1] + h1 * cwv[1:2] + hu * cwv[2:3]
        act_ref[...] = (_gelu(hc) * hg).astype(BF16)
        tail_ref[...] = hu[tm - 8:tm]

    return pl.pallas_call(
        body, grid=(M // tm, N // tn),
        in_specs=[pl.BlockSpec((tm, K), lambda i, j: (i, 0)),
                  pl.BlockSpec((pr, K), lambda i, j: (jnp.maximum(i * (tm // pr) - 1, 0), 0)),
                  pl.BlockSpec((1, K), lambda i, j: (0, 0)),
                  pl.BlockSpec((K, tn), lambda i, j: (0, j)),
                  pl.BlockSpec((K, tn), lambda i, j: (0, j)),
                  pl.BlockSpec((FFN_CONV, tn), lambda i, j: (0, j)),
                  pl.BlockSpec((1, tn), lambda i, j: (0, j))],
        out_specs=[pl.BlockSpec((tm, tn), lambda i, j: (i, j)), pl.BlockSpec((8, tn), lambda i, j: (i, j))],
        out_shape=[jax.ShapeDtypeStruct((M, N), BF16), jax.ShapeDtypeStruct((M // tm * 8, N), F32)],
        scratch_shapes=[pltpu.VMEM((tm, K), BF16)],
        compiler_params=_cp(("parallel", "arbitrary")), name="ffn_upgate_prompt",
    )(x, x, g.reshape(1, K), w_up, w_gate, cw, cb.reshape(1, -1))


def _tile(m, pref):
    return pref if m % pref == 0 else m


def _dense_tail(x1, p, wts, ffn_state, T):
    (norm_ffn, norm_ple, w_up, w_gate, conv_w, conv_b, w_down, w_proj, w_pgate) = wts
    M = x1.shape[0]
    tm = _tile(M, 1024)
    tn = 512
    prompt = ffn_state is None

    if prompt:
        act, tails = _ffn_upgate_prompt(x1, norm_ffn, w_up, w_gate, conv_w, conv_b, T, tm, tn)
        tails = tails.reshape(M // tm, 8, D_FF)
        tiles_per_seq = T // tm
        new_buf = tails[tiles_per_seq - 1::tiles_per_seq, 8 - (FFN_CONV - 1):, :]
    else:
        nseq = M // T
        h1 = jnp.repeat(ffn_state[:, FFN_CONV - 2], T, axis=0)
        h2 = jnp.stack([ffn_state[:, 0], ffn_state[:, 1]] + [ffn_state[:, 1]] * (T - 2), axis=1).reshape(M, D_FF)

        def conv_epilogue(accs, ex):
            hu, hg = accs
            b1, b2, cwv, cbv = ex
            tok = lax.broadcasted_iota(jnp.int32, hu.shape, 0) % T
            s1 = jnp.where(tok >= 1, pltpu.roll(hu, 1, 0), b1)
            s2 = jnp.where(tok >= 2, pltpu.roll(hu, 2, 0), b2)
            hc = cbv + s2 * cwv[0:1] + s1 * cwv[1:2] + hu * cwv[2:3]
            return _gelu(hc) * hg, hu

        act, hu = _mm([x1], [w_up, w_gate], [0, 0],
                      [(h1, "tile"), (h2, "tile"), (conv_w, "rows"), (conv_b.reshape(1, -1), "rows")],
                      conv_epilogue, [BF16, F32], tm, tn, "ffn_upgate_sample", norm=(0, norm_ffn))
        new_buf = hu.reshape(nseq, T, D_FF)[:, T - (FFN_CONV - 1):]

    (x2,) = _mm([act], [w_down], [0], [(x1, "tile")], lambda accs, ex: (ex[0] + accs[0],), [F32],
                _tile(M, 512), tn, "ffn_down")
    (x3,) = _mm([p, x2], [w_proj, w_pgate], [0, 1], [(x2, "tile")],
                lambda accs, ex: (ex[0] + accs[0] * _sigmoid(accs[1]),), [F32], tm, tn, "ple", norm=(1, norm_ple))
    return x3, new_buf


def kernel(x_prompt, x_sample, cache_cmp_k, cache_cmp_v, cache_sel_k, cache_sel_v, cache_win_k, cache_win_v, state_rglru_h, state_rglru_conv, state_s5_re, state_s5_im, state_ffn_conv, page_table, p_prompt, p_sample, norm_mix, norm_ffn, norm_ple, w_in_even, w_out_even, q_norm, k_norm_cmp, k_norm_sel, k_norm_win, cmp_pe_k, cmp_w1_k, cmp_w2_k, cmp_pe_v, cmp_w1_v, cmp_w2_v, rg_conv_w, rg_conv_b, rg_w_a, rg_b_a, rg_w_x, rg_b_x, rg_lam, s5_lam_re, s5_lam_im, s5_log_dt, s5_b_re, s5_b_im, s5_c_re, s5_c_im, s5_d, s5_w_glu_a, s5_w_glu_b, ffn_w_up, ffn_w_gate, ffn_conv_w, ffn_conv_b, ffn_w_down, ple_w_proj, ple_w_gate):
    B, T, _ = x_prompt.shape
    DB = x_sample.shape[0]
    n_pages = page_table.shape[1]
    past_len = n_pages * PAGE_SIZE
    w_buf = cache_win_k.shape[2]
    MP = B * T
    MS = DB * DEC_SEQ
    depth = norm_mix.shape[0]

    def reorder_in(w):
        q, kv, gt, xr, gr = (w[:, :Q_W], w[:, Q_W:Q_W + 6 * KV_W], w[:, Q_W + 6 * KV_W:Q_W + 6 * KV_W + 3 * N_HEADS],
                             w[:, Q_W + 6 * KV_W + 3 * N_HEADS:Q_W + 6 * KV_W + 3 * N_HEADS + D_RNN],
                             w[:, Q_W + 6 * KV_W + 3 * N_HEADS + D_RNN:])
        pad = jnp.zeros((w.shape[0], IN_PAD - COL_GT - 3 * N_HEADS), w.dtype)
        return jnp.concatenate([q, xr, gr, kv, gt, pad], axis=1).astype(BF16)

    def layer_dense_weights(li):
        return (norm_ffn[li], norm_ple[li], ffn_w_up[li].astype(BF16), ffn_w_gate[li].astype(BF16),
                ffn_conv_w[li], ffn_conv_b[li], ffn_w_down[li].astype(BF16),
                ple_w_proj[li].astype(BF16), ple_w_gate[li].astype(BF16))

    xs = {"p": x_prompt.reshape(MP, D_MODEL), "s": x_sample.reshape(MS, D_MODEL)}
    pe_in = {"p": p_prompt.reshape(depth, MP, -1).astype(BF16), "s": p_sample.reshape(depth, MS, -1).astype(BF16)}
    seq_len = {"p": T, "s": DEC_SEQ}
    ev = {"p": [], "s": []}
    od = {"p": [], "s": []}
    ff = {"p": [], "s": []}
    ptab = page_table.reshape(-1).astype(jnp.int32)

    for li in range(depth):
        dense_w = layer_dense_weights(li)
        if li % 2 == 0:
            e = li // 2
            w_in = reorder_in(w_in_even[e])
            w_out = w_out_even[e].astype(BF16)
            w_out_a, w_out_r = w_out[:Q_W], w_out[Q_W:]
            for grp in ("p", "s"):
                x = xs[grp]
                M = x.shape[0]
                L = seq_len[grp]
                nseq = M // L
                (z,) = _mm([x], [w_in], [0], [], lambda accs, ex: (accs[0],), [F32], _tile(M, 1024), 512, "in_proj",
                           norm=(0, norm_mix[li]))
                if grp == "p":
                    cos_t, sin_t = _rope_tables(jnp.arange(T))
                    tmq = _tile(T, 512)
                    tab_blocks = T // tmq
                else:
                    cos_t, sin_t = _rope_tables(jnp.tile(past_len + jnp.arange(DEC_SEQ), DB))
                    tmq = M
                    tab_blocks = 1
                q, ks, kw, kvb = _qk_prep(z, cos_t, sin_t, tab_blocks, q_norm[e], k_norm_sel[e], k_norm_win[e], tmq)
                kc = z[:, COL_KV:COL_KV + KV_W]
                vc = z[:, COL_KV + KV_W:COL_KV + 2 * KV_W]
                vs = z[:, COL_KV + 3 * KV_W:COL_KV + 4 * KV_W]
                vw = z[:, COL_KV + 5 * KV_W:COL_KV + 6 * KV_W]
                gt = z[:, COL_GT:COL_GT + 3 * N_HEADS].reshape(nseq, L, 3, N_KV, GROUP)
                gt = gt.transpose(0, 3, 1, 2, 4).reshape(nseq, N_KV, L, 3 * GROUP)
                xr3 = z[:, COL_XR:COL_XR + D_RNN].reshape(nseq, L, D_RNN)

                if grp == "p":
                    n_rows = T // CMP_STRIDE
                    ccos, csin = _rope_tables(jnp.arange(n_rows) * CMP_STRIDE + CMP_STRIDE - 1)
                    ident = jnp.arange(B * (T // PAGE_SIZE), dtype=jnp.int32)
                    kcmp = _compress(kc.reshape(-1, N_KV * PAGE_SIZE, HEAD_DIM), ident, B, T // PAGE_SIZE, cmp_pe_k[e],
                                     cmp_w1_k[e], cmp_w2_k[e], k_norm_cmp[e], ccos, csin, True, "compress_k_prompt")
                    vcmp = _compress(vc.reshape(-1, N_KV * PAGE_SIZE, HEAD_DIM), ident, B, T // PAGE_SIZE, cmp_pe_v[e],
                                     cmp_w1_v[e], cmp_w2_v[e], k_norm_cmp[e], ccos, csin, False, "compress_v_prompt")
                    n_sel = T // SEL_BLOCK
                    nqb = T // Q_BLOCK
                    ovl_t = _overlap_matrix(n_rows, n_sel, n_sel).T
                    ex_t = ((np.arange(T // SEL_KT)[:, None, None] * SEL_KT + np.arange(SEL_KT)[None, :, None])
                            // SEL_BLOCK == np.arange(n_sel)[None, None, :])
                    q_t = q.reshape(B, nqb, Q_BLOCK, N_KV, GROUP, HEAD_DIM).transpose(0, 3, 1, 5, 4, 2)
                    q_t = q_t.reshape(B, N_KV, nqb, HEAD_DIM, GROUP * Q_BLOCK)
                    vs_t = kvb.reshape(B, T // SEL_KT, SEL_KT, 4, N_KV, HEAD_DIM)[:, :, :, 1].transpose(0, 3, 1, 4, 2)
                    vw_t = kvb.reshape(B, nqb, Q_BLOCK, 4, N_KV, HEAD_DIM)[:, :, :, 3].transpose(0, 3, 1, 4, 2)
                    gates_t = z[:, COL_GT:COL_GT + 3 * N_HEADS].reshape(B, T, 3, N_KV, GROUP)
                    gates_t = gates_t.transpose(0, 3, 2, 4, 1).reshape(B, N_KV, 3 * GROUP, T)
                    o_t = _nsa_prompt(q_t, kcmp, vcmp.swapaxes(2, 3), kvb.reshape(B, T, 4 * KV_W), vs_t, vw_t,
                                      gates_t, ovl_t, jnp.asarray(ex_t, BF16), B, T)
                    o_nsa = o_t.reshape(B, N_KV, nqb, HEAD_DIM, GROUP, Q_BLOCK).transpose(0, 2, 5, 1, 4, 3)
                    o_nsa = o_nsa.reshape(M, Q_W)
                    y_rg, h_last = _rglru_prompt(z.reshape(B, T, IN_PAD), rg_conv_w[e], rg_conv_b[e], rg_w_a[e],
                                                 rg_b_a[e], rg_w_x[e], rg_b_x[e], rg_lam[e], B, T, _tile(T, 512))
                    y_rg = y_rg.reshape(M, D_RNN)
                    h_last = h_last.reshape(B, D_RNN)
                    new_conv = xr3[:, T - (RG_CONV - 1):]
                    new_wk = kw.reshape(B, T, N_KV, HEAD_DIM)
                    new_wv = vw.reshape(B, T, N_KV, HEAD_DIM)
                    if T >= w_buf:
                        new_wk, new_wv = new_wk[:, T - w_buf:], new_wv[:, T - w_buf:]
                    else:
                        padw = ((0, 0), (w_buf - T, 0), (0, 0), (0, 0))
                        new_wk, new_wv = jnp.pad(new_wk, padw), jnp.pad(new_wv, padw)
                else:
                    n_rows = past_len // CMP_STRIDE
                    ccos, csin = _rope_tables(jnp.arange(n_rows) * CMP_STRIDE + CMP_STRIDE - 1)
                    ptab_e = ptab + e * cache_cmp_k.shape[1]
                    kcmp = _compress(cache_cmp_k.reshape(-1, N_KV * PAGE_SIZE, HEAD_DIM), ptab_e, DB, n_pages, cmp_pe_k[e],
                                     cmp_w1_k[e], cmp_w2_k[e], k_norm_cmp[e], ccos, csin, True, "compress_k_sample")
                    vcmp = _compress(cache_cmp_v.reshape(-1, N_KV * PAGE_SIZE, HEAD_DIM), ptab_e, DB, n_pages, cmp_pe_v[e],
                                     cmp_w1_v[e], cmp_w2_v[e], k_norm_cmp[e], ccos, csin, False, "compress_v_sample")
                    n_sel = -(-(past_len + DEC_SEQ) // SEL_BLOCK)
                    sel_cols = -(-n_sel // 128) * 128
                    ovl = _overlap_matrix(n_rows, n_sel, sel_cols)
                    keys = PAGES_PER_STEP * PAGE_SIZE
                    steps = n_pages // PAGES_PER_STEP
                    ex = (np.arange(sel_cols)[None, :, None]
                          == (np.arange(steps)[:, None, None] * keys + np.arange(keys)[None, None, :]) // SEL_BLOCK)
                    q8 = jnp.pad(q.reshape(DB, DEC_SEQ, Q_W), ((0, 0), (0, Q_PAD - DEC_SEQ), (0, 0)))
                    kvnew = jnp.pad(kvb.reshape(DB, DEC_SEQ, 4 * KV_W), ((0, 0), (0, 128 - DEC_SEQ), (0, 0)))
                    gt8 = jnp.pad(gt, ((0, 0), (0, 0), (0, Q_PAD - DEC_SEQ), (0, 0)))
                    o8 = _nsa_sample(q8, kcmp, vcmp, cache_sel_k.reshape(-1, N_KV * PAGE_SIZE, HEAD_DIM),
                                     cache_sel_v.reshape(-1, N_KV * PAGE_SIZE, HEAD_DIM), ptab_e, kvnew,
                                     cache_win_k.reshape(-1, N_KV * w_buf, HEAD_DIM),
                                     cache_win_v.reshape(-1, N_KV * w_buf, HEAD_DIM),
                                     e * DB, gt8, ovl, jnp.asarray(ex, BF16), DB, past_len)
                    o_nsa = o8[:, :DEC_SEQ].reshape(M, Q_W)
                    gr3 = z[:, COL_GR:COL_GR + D_RNN].reshape(DB, DEC_SEQ, D_RNN)
                    y_t, h_last = _rglru_sample(xr3.transpose(1, 0, 2), gr3.transpose(1, 0, 2),
                                                state_rglru_conv[e].transpose(1, 0, 2), state_rglru_h[e],
                                                rg_conv_w[e], rg_conv_b[e], rg_w_a[e], rg_b_a[e], rg_w_x[e],
                                                rg_b_x[e], rg_lam[e])
                    y_rg = y_t.transpose(1, 0, 2).reshape(M, D_RNN)
                    new_conv = jnp.concatenate([state_rglru_conv[e], xr3], axis=1)[:, -(RG_CONV - 1):]
                    kk = jnp.concatenate([cache_win_k[e], kw.reshape(DB, DEC_SEQ, N_KV, HEAD_DIM)], axis=1)
                    vv = jnp.concatenate([cache_win_v[e], vw.reshape(DB, DEC_SEQ, N_KV, HEAD_DIM)], axis=1)
                    new_wk, new_wv = kk[:, -w_buf:], vv[:, -w_buf:]

                rs = lambda a: a.reshape(nseq, L, N_KV, HEAD_DIM)
                ev[grp].append((rs(kc), rs(vc), rs(ks), rs(vs), new_wk, new_wv, h_last, new_conv))
                (x1,) = _mm([o_nsa, y_rg], [w_out_a, w_out_r], [0, 1], [(x, "tile")],
                            lambda accs, exs: (exs[0] + accs[0] + accs[1],), [F32], _tile(M, 1024), 512, "mixer_out")
                xs[grp], nb = _dense_tail(x1, pe_in[grp][li], dense_w, None if grp == "p" else state_ffn_conv[li], L)
                ff[grp].append(nb)
        else:
            o = li // 2
            ar, ai, bbr, bbi = _s5_discretize(s5_lam_re[o], s5_lam_im[o], s5_log_dt[o],
                                              s5_b_re[o].transpose(2, 0, 1), s5_b_im[o].transpose(2, 0, 1))
            bw, cr, ci = _s5_block_weights(bbr, bbi, s5_c_re[o], s5_c_im[o])
            ar = ar.reshape(S5_NGB, 1, S5_LANES)
            ai = ai.reshape(S5_NGB, 1, S5_LANES)
            w_a = s5_w_glu_a[o].astype(BF16)
            w_b = s5_w_glu_b[o].astype(BF16)
            for grp in ("p", "s"):
                x = xs[grp]
                M = x.shape[0]
                L = seq_len[grp]
                glu = lambda accs, exs: (exs[0] + accs[0] * _sigmoid(accs[1]),)
                if grp == "p":
                    sl = T // S5_NSEG
                    hn = _rmsnorm_segmented(x, norm_mix[li], B, T, min(256, sl))
                    zz, sre, sim = _s5_prompt(hn.reshape(B, sl, S5_NSEG, D_MODEL), bw.astype(BF16), cr.astype(BF16),
                                              ci.astype(BF16), ar, ai, s5_d[o], B, T)
                    sre = sre.reshape(B, S5_GROUPS, S5_STATE)
                    sim = sim.reshape(B, S5_GROUPS, S5_STATE)
                    (x1,) = _mm([zz.reshape(B, sl, S5_NSEG * D_MODEL)], [w_a, w_b], [0, 0], [(x, "tile")], glu, [F32],
                                sl, 512, "mixer_out", segmented=True)
                else:
                    hn = _rmsnorm(x, norm_mix[li], F32, _tile(M, 256))
                    u = hn.reshape(DB, DEC_SEQ, D_MODEL).transpose(1, 0, 2)
                    zz, sre, sim = _s5_sample(u, state_s5_re[o].reshape(DB, -1), state_s5_im[o].reshape(DB, -1),
                                              bw, cr, ci, ar, ai, s5_d[o])
                    zz = zz.transpose(1, 0, 2).reshape(M, D_MODEL)
                    sre = sre.reshape(DB, S5_GROUPS, S5_STATE)
                    sim = sim.reshape(DB, S5_GROUPS, S5_STATE)
                    (x1,) = _mm([zz], [w_a, w_b], [0, 0], [(x, "tile")], glu, [F32], M, 512, "mixer_out")
                od[grp].append((sre, sim))
                xs[grp], nb = _dense_tail(x1, pe_in[grp][li], dense_w, None if grp == "p" else state_ffn_conv[li], L)
                ff[grp].append(nb)

    def stk(states, i):
        return jnp.stack([s[i] for s in states])

    outs = [xs["p"].reshape(B, T, D_MODEL), xs["s"].reshape(DB, DEC_SEQ, D_MODEL)]
    for i in range(8):
        outs += [stk(ev["p"], i), stk(ev["s"], i)]
    for i in range(2):
        outs += [stk(od["p"], i), stk(od["s"], i)]
    outs += [jnp.stack(ff["p"]), jnp.stack(ff["s"])]
    return tuple(outs)
```

```python
import functools

import numpy as np
import jax
import jax.numpy as jnp
from jax import lax
from jax.experimental import pallas as pl
from jax.experimental.pallas import tpu as pltpu

F32 = jnp.float32
BF16 = jnp.bfloat16

D_MODEL = 2048
DEC_SEQ = 4
PAGE_SIZE = 128
N_HEADS = 8
HEAD_DIM = 128
N_KV = 2
GROUP = N_HEADS // N_KV
ROPE_DIM = HEAD_DIM // 4
ROPE_HALF = ROPE_DIM // 2
ROPE_THETA = 500000.0
CMP_BLOCK = 32
CMP_STRIDE = 16
SEL_BLOCK = 64
SEL_TOPK = 16
N_LOCAL = 2
WINDOW = 512
Q_BLOCK = 128
FORCED_SCORE = 1e9
NEG_BIG = -1e30
ATTN_SCALE = HEAD_DIM ** -0.5
D_RNN = D_MODEL // 2
RG_BLOCKS = 8
RG_BW = D_RNN // RG_BLOCKS
RG_CONV = 4
RG_C = 8.0
S5_CH = 16
S5_GROUPS = D_MODEL // S5_CH
S5_STATE = 64
S5_GB = 8
S5_NGB = S5_GROUPS // S5_GB
S5_LANES = S5_GB * S5_STATE
S5_NSEG = 8
D_FF = ((8 * D_MODEL // 3 + 255) // 256) * 256
FFN_CONV = 3
NORM_EPS = 1e-6
Q_W = N_HEADS * HEAD_DIM
KV_W = N_KV * HEAD_DIM
PAGES_PER_STEP = 8
Q_PAD = 8
SEL_KT = 512

COL_Q = 0
COL_XR = Q_W
COL_GR = Q_W + D_RNN
COL_KV = Q_W + 2 * D_RNN
COL_GT = COL_KV + 6 * KV_W
IN_PAD = 5120


def _cp(sem, vmem_mb=48):
    return pltpu.CompilerParams(dimension_semantics=sem, vmem_limit_bytes=vmem_mb * 1024 * 1024)


def _gelu(x):
    return 0.5 * x * (1.0 + jnp.tanh(0.7978845608028654 * (x + 0.044715 * (x * x * x))))


def _sigmoid(x):
    return 1.0 / (1.0 + jnp.exp(-x))


def _dotb(a, b):
    return jnp.dot(a.astype(BF16), b.astype(BF16), preferred_element_type=F32)


def _dotf(a, b):
    return jnp.dot(a, b, preferred_element_type=F32, precision=lax.Precision.HIGHEST)


def _dot_nt(a, b):
    return lax.dot_general(a, b, (((1,), (1,)), ((), ())), preferred_element_type=F32)


def _masked_softmax(s, mask):
    s = jnp.where(mask, s, NEG_BIG)
    m = jnp.max(s, axis=-1, keepdims=True)
    e = jnp.where(mask, jnp.exp(s - m), 0.0)
    return e / jnp.maximum(jnp.sum(e, axis=-1, keepdims=True), 1e-30)


def _norm_rope(x, g, cos_t, sin_t):
    ms = jnp.mean(x * x, axis=-1, keepdims=True)
    y = x * lax.rsqrt(ms + NORM_EPS) * g
    lane = lax.broadcasted_iota(jnp.int32, y.shape, 1)
    swapped = jnp.where(lane < ROPE_HALF, pltpu.roll(y, HEAD_DIM - ROPE_HALF, 1), pltpu.roll(y, ROPE_HALF, 1))
    return y * cos_t + swapped * sin_t


def _rope_tables(pos):
    inv = ROPE_THETA ** (-jnp.arange(ROPE_HALF, dtype=F32) * 2.0 / ROPE_DIM)
    ang = pos.astype(F32)[:, None] * inv
    cos = jnp.cos(ang)
    sin = jnp.sin(ang)
    n = pos.shape[0]
    ones = jnp.ones((n, HEAD_DIM - ROPE_DIM), F32)
    zeros = jnp.zeros((n, HEAD_DIM - ROPE_DIM), F32)
    return (jnp.concatenate([cos, cos, ones], axis=1), jnp.concatenate([-sin, sin, zeros], axis=1))


def _overlap_matrix(n_rows, n_sel, n_cols):
    n = np.arange(n_rows)[:, None] - 1
    j = np.arange(n_cols)[None]
    c0 = n * CMP_STRIDE
    s0 = j * SEL_BLOCK
    ov = np.minimum(c0 + CMP_BLOCK, s0 + SEL_BLOCK) - np.maximum(c0, s0)
    ov = np.maximum(ov, 0).astype(np.float32) / CMP_BLOCK
    ov = np.where((n >= 0) & (j < n_sel), ov, 0.0)
    return jnp.asarray(ov, F32)


def _rmsnorm(x, g, out_dtype, tm):
    M, D = x.shape

    def body(x_ref, g_ref, o_ref):
        xv = x_ref[...]
        ms = jnp.mean(xv * xv, axis=-1, keepdims=True)
        o_ref[...] = (xv * lax.rsqrt(ms + NORM_EPS) * g_ref[...]).astype(o_ref.dtype)

    return pl.pallas_call(
        body, grid=(M // tm,),
        in_specs=[pl.BlockSpec((tm, D), lambda i: (i, 0)), pl.BlockSpec((1, D), lambda i: (0, 0))],
        out_specs=pl.BlockSpec((tm, D), lambda i: (i, 0)),
        out_shape=jax.ShapeDtypeStruct((M, D), out_dtype),
        compiler_params=_cp(("parallel",)), name="rmsnorm")(x, g.reshape(1, D))


def _rms_rows(xv, g):
    ms = jnp.mean(xv * xv, axis=-1, keepdims=True)
    return xv * lax.rsqrt(ms + NORM_EPS) * g


def _mm(a_list, w_list, pair, extras, epilogue, out_dtypes, tm, tn, name, norm=None):
    N = w_list[0].shape[1]
    na, nw, ne, no = len(a_list), len(w_list), len(extras), len(out_dtypes)
    M = a_list[0].shape[0]

    def body(*refs):
        if norm is not None:
            g_ref = refs[na + nw + ne]
            an_ref = refs[na + nw + ne + 1 + no]

            @pl.when(pl.program_id(1) == 0)
            def _():
                an_ref[...] = _rms_rows(refs[norm[0]][...], g_ref[...]).astype(BF16)

        a_vals = []
        for k, r in enumerate(refs[:na]):
            if norm is not None and k == norm[0]:
                a_vals.append(an_ref[...])
            else:
                a_vals.append(r[...].astype(BF16))
        accs = [jnp.dot(a_vals[pair[i]], refs[na + i][...], preferred_element_type=F32) for i in range(nw)]
        ex = [r[...] for r in refs[na + nw:na + nw + ne]]
        res = epilogue(accs, ex)
        first_out = na + nw + ne + (1 if norm is not None else 0)
        for o_ref, v in zip(refs[first_out:first_out + no], res):
            o_ref[...] = v.astype(o_ref.dtype)

    in_specs = [pl.BlockSpec((tm, a.shape[1]), lambda i, j: (i, 0)) for a in a_list]
    in_specs += [pl.BlockSpec((w.shape[0], tn), lambda i, j: (0, j)) for w in w_list]
    for arr, kind in extras:
        if kind == "tile":
            in_specs.append(pl.BlockSpec((tm, tn), lambda i, j: (i, j)))
        else:
            in_specs.append(pl.BlockSpec((arr.shape[0], tn), lambda i, j: (0, j)))
    operands = [*a_list, *w_list, *[e[0] for e in extras]]
    scratch = []
    if norm is not None:
        kn = a_list[norm[0]].shape[1]
        in_specs.append(pl.BlockSpec((1, kn), lambda i, j: (0, 0)))
        operands.append(norm[1].reshape(1, kn))
        scratch.append(pltpu.VMEM((tm, kn), BF16))
    return pl.pallas_call(
        body, grid=(M // tm, N // tn), in_specs=in_specs,
        out_specs=[pl.BlockSpec((tm, tn), lambda i, j: (i, j)) for _ in out_dtypes],
        out_shape=[jax.ShapeDtypeStruct((M, N), dt) for dt in out_dtypes],
        scratch_shapes=scratch,
        compiler_params=_cp(("parallel", "arbitrary" if norm is not None else "parallel")), name=name,
    )(*operands)


def _qk_prep(z, cos_t, sin_t, tab_blocks, qn, kns, knw, tm):
    M = z.shape[0]

    def body(zq_ref, zkv_ref, c_ref, s_ref, qn_ref, kns_ref, knw_ref, q_ref, kvb_ref, *row_refs):
        c = c_ref[...]
        s = s_ref[...]
        for h in range(N_HEADS):
            sl = slice(h * HEAD_DIM, (h + 1) * HEAD_DIM)
            q_ref[:, sl] = _norm_rope(zq_ref[:, sl], qn_ref[...], c, s).astype(BF16)
        for kv in range(N_KV):
            vals = [zkv_ref[:, (k * N_KV + kv) * HEAD_DIM:(k * N_KV + kv + 1) * HEAD_DIM] for k in range(6)]
            vals[2] = _norm_rope(vals[2], kns_ref[...], c, s)
            vals[4] = _norm_rope(vals[4], knw_ref[...], c, s)
            for k in range(6):
                row_refs[k][pl.ds(kv, tm, stride=N_KV), :] = vals[k]
            for k in range(2, 6):
                kvb_ref[:, ((k - 2) * N_KV + kv) * HEAD_DIM:((k - 2) * N_KV + kv + 1) * HEAD_DIM] = vals[k].astype(BF16)

    vec = pl.BlockSpec((1, HEAD_DIM), lambda i: (0, 0))
    tab = pl.BlockSpec((tm, HEAD_DIM), lambda i: (i % tab_blocks, 0))
    rows_spec = pl.BlockSpec((N_KV * tm, HEAD_DIM), lambda i: (i, 0))
    rows_shape = jax.ShapeDtypeStruct((N_KV * M, HEAD_DIM), F32)
    return pl.pallas_call(
        body, grid=(M // tm,),
        in_specs=[pl.BlockSpec((tm, Q_W), lambda i: (i, COL_Q // Q_W)),
                  pl.BlockSpec((tm, 6 * KV_W), lambda i: (i, COL_KV // (6 * KV_W))),
                  tab, tab, vec, vec, vec],
        out_specs=[pl.BlockSpec((tm, Q_W), lambda i: (i, 0)), pl.BlockSpec((tm, 4 * KV_W), lambda i: (i, 0))]
        + [rows_spec] * 6,
        out_shape=[jax.ShapeDtypeStruct((M, Q_W), BF16), jax.ShapeDtypeStruct((M, 4 * KV_W), BF16)] + [rows_shape] * 6,
        compiler_params=_cp(("parallel",)), name="qk_prep",
    )(z, z, cos_t, sin_t, qn.reshape(1, -1), kns.reshape(1, -1), knw.reshape(1, -1))


def _compress_bias(pe, w1):
    def body(pe_ref, w1_ref, o_ref):
        for half in range(2):
            acc = jnp.zeros((8, HEAD_DIM), F32)
            for c in range(half * CMP_STRIDE, (half + 1) * CMP_STRIDE):
                acc = acc + _dotf(jnp.broadcast_to(pe_ref[c:c + 1, :], (8, HEAD_DIM)), w1_ref[c])
            o_ref[:, half * HEAD_DIM:(half + 1) * HEAD_DIM] = acc[0:1, :]

    return pl.pallas_call(body, out_shape=jax.ShapeDtypeStruct((1, 2 * HEAD_DIM), F32), name="compress_bias")(pe, w1)


def _compress(pages, ptab, n_seq, pages_per_seq, pe, w1, w2, norm_g, cos_t, sin_t, do_norm, name):
    bias = _compress_bias(pe, w1)
    w1cat = jnp.concatenate([w1[:CMP_STRIDE], w1[CMP_STRIDE:]], axis=2).astype(BF16)
    G = 2 * PAGES_PER_STEP if pages_per_seq % (2 * PAGES_PER_STEP) == 0 else PAGES_PER_STEP
    steps = pages_per_seq // G
    cpp = PAGE_SIZE // CMP_STRIDE
    ch = G * cpp
    n_chunk = pages_per_seq * cpp

    def body(pt_ref, *refs):
        page_refs = refs[:G]
        bias_ref, w1_ref, w2_ref, g_ref, c_ref, s_ref, o_ref, carry_ref = refs[G:]
        step = pl.program_id(1)

        @pl.when(step == 0)
        def _():
            carry_ref[...] = jnp.zeros_like(carry_ref)

        row = lax.broadcasted_iota(jnp.int32, (ch, HEAD_DIM), 0)
        for kv in range(N_KV):
            acc = jnp.zeros((ch, 2 * HEAD_DIM), F32)
            for c in range(CMP_STRIDE):
                xc = jnp.concatenate(
                    [pr[0, pl.ds(N_KV * c + kv, cpp, stride=N_KV * CMP_STRIDE), :] for pr in page_refs], axis=0)
                acc = acc + _dotb(xc, w1_ref[c])
            acc = acc + bias_ref[...]
            lo = acc[:, :HEAD_DIM]
            hi = acc[:, HEAD_DIM:]
            lo_prev = jnp.where(row == 0, carry_ref[kv, 0:1, :], pltpu.roll(lo, 1, 0))
            carry_ref[kv, 0:1, :] = lo[ch - 1:ch, :]
            out = _dotb(_gelu(lo_prev + hi), w2_ref[...])
            if do_norm:
                out = _norm_rope(out, g_ref[...], c_ref[...], s_ref[...])
            o_ref[0, kv] = out.astype(BF16)

    def page_map(j, s, g, pt):
        return (pt[s * pages_per_seq + g * G + j], 0, 0)

    full2 = lambda s, g, pt: (0, 0)
    in_specs = [pl.BlockSpec((1, N_KV * PAGE_SIZE, HEAD_DIM), functools.partial(page_map, j)) for j in range(G)]
    in_specs += [pl.BlockSpec((1, 2 * HEAD_DIM), full2),
                 pl.BlockSpec((CMP_STRIDE, HEAD_DIM, 2 * HEAD_DIM), lambda s, g, pt: (0, 0, 0)),
                 pl.BlockSpec((HEAD_DIM, HEAD_DIM), full2),
                 pl.BlockSpec((1, HEAD_DIM), full2),
                 pl.BlockSpec((ch, HEAD_DIM), lambda s, g, pt: (g, 0)),
                 pl.BlockSpec((ch, HEAD_DIM), lambda s, g, pt: (g, 0))]
    gs = pltpu.PrefetchScalarGridSpec(
        num_scalar_prefetch=1, grid=(n_seq, steps), in_specs=in_specs,
        out_specs=pl.BlockSpec((1, N_KV, ch, HEAD_DIM), lambda s, g, pt: (s, 0, g, 0)),
        scratch_shapes=[pltpu.VMEM((N_KV, 8, HEAD_DIM), F32)])
    return pl.pallas_call(
        body, grid_spec=gs, out_shape=jax.ShapeDtypeStruct((n_seq, N_KV, n_chunk, HEAD_DIM), BF16),
        compiler_params=_cp(("parallel", "arbitrary")), name=name,
    )(ptab, *([pages] * G), bias, w1cat, w2.astype(BF16), norm_g.reshape(1, -1), cos_t, sin_t)


def _nsa_prompt(q_t, kcmp, vcmp_t, kvb, vs_t, vw_t, gates_t, ovl_t, expand_t, B, T):
    nqb = T // Q_BLOCK
    n_cmp_rows = T // CMP_STRIDE
    n_sel = T // SEL_BLOCK
    cols = GROUP * Q_BLOCK
    win_tiles = WINDOW // Q_BLOCK
    win_keys = (win_tiles + 1) * Q_BLOCK
    assert T % SEL_KT == 0 and T >= win_keys
    topk = min(SEL_TOPK, n_sel)

    def body(q_ref, kc_ref, vc_ref, ks_ref, vs_ref, kw_ref, vw_ref, gt_ref, ovl_ref, ex_ref, o_ref):
        i = pl.program_id(2)
        qt = q_ref[0, 0, 0]
        pos = i * Q_BLOCK + (lax.broadcasted_iota(jnp.int32, (1, cols), 1) & (Q_BLOCK - 1))

        s = jnp.dot(kc_ref[0, 0], qt, preferred_element_type=F32) * ATTN_SCALE
        crow = lax.broadcasted_iota(jnp.int32, (n_cmp_rows, cols), 0)
        cmask = (crow >= 1) & (crow * CMP_STRIDE + (CMP_BLOCK - 1 - CMP_STRIDE) <= pos)
        s = jnp.where(cmask, s, NEG_BIG)
        e = jnp.where(cmask, jnp.exp(s - jnp.max(s, axis=0, keepdims=True)), 0.0)
        p = e / jnp.maximum(jnp.sum(e, axis=0, keepdims=True), 1e-30)
        o_cmp = _dotb(vc_ref[0, 0], p)
        imp = p[:, 0:Q_BLOCK]
        for h in range(1, GROUP):
            imp = imp + p[:, h * Q_BLOCK:(h + 1) * Q_BLOCK]

        score = _dotf(ovl_ref[...], imp)
        j = lax.broadcasted_iota(jnp.int32, (n_sel, Q_BLOCK), 0)
        qblk = (i * Q_BLOCK + lax.broadcasted_iota(jnp.int32, (n_sel, Q_BLOCK), 1)) // SEL_BLOCK
        forced = (j == 0) | ((j <= qblk) & (j > qblk - N_LOCAL))
        score = jnp.where(forced, FORCED_SCORE, jnp.where(j > qblk, NEG_BIG, score))
        rank = jnp.zeros((n_sel, Q_BLOCK), F32)
        for r in range(n_sel):
            sr = score[r:r + 1, :]
            beats = (sr > score) | ((sr == score) & (j > r))
            rank = rank + jnp.where(beats, 1.0, 0.0)
        sel_bias = jnp.where(rank < topk, 0.0, NEG_BIG).astype(BF16)

        krow = lax.broadcasted_iota(jnp.int32, (SEL_KT, cols), 0)

        def sel_tile(u, carry, causal):
            m, l, acc = carry
            k = ks_ref[0, pl.ds(pl.multiple_of(u * SEL_KT, SEL_KT), SEL_KT), :]
            b1 = _dotb(ex_ref[u], sel_bias)
            sc = (jnp.dot(k, qt, preferred_element_type=F32) * ATTN_SCALE
                  + jnp.concatenate([b1] * GROUP, axis=1))
            if causal:
                sc = jnp.where(u * SEL_KT + krow <= pos, sc, NEG_BIG)
            m_new = jnp.maximum(m, jnp.max(sc, axis=0, keepdims=True))
            alpha = jnp.exp(m - m_new)
            ex = jnp.exp(sc - m_new)
            l = alpha * l + jnp.sum(ex, axis=0, keepdims=True)
            acc = alpha * acc + _dotb(vs_ref[0, 0, u], ex)
            return m_new, l, acc

        init = (jnp.full((1, cols), NEG_BIG, F32), jnp.zeros((1, cols), F32), jnp.zeros((HEAD_DIM, cols), F32))
        last = (i * Q_BLOCK + Q_BLOCK + SEL_KT - 1) // SEL_KT - 1
        carry = lax.fori_loop(0, last, functools.partial(sel_tile, causal=False), init)
        _, l_sel, acc_sel = sel_tile(last, carry, causal=True)
        o_sel = acc_sel / jnp.maximum(l_sel, 1e-30)

        t0 = jnp.maximum(i - win_tiles, 0)
        kwin = kw_ref[0, pl.ds(pl.multiple_of(t0 * Q_BLOCK, Q_BLOCK), win_keys), :]
        sw = jnp.dot(kwin, qt, preferred_element_type=F32) * ATTN_SCALE
        kpos = t0 * Q_BLOCK + lax.broadcasted_iota(jnp.int32, (win_keys, cols), 0)
        wmask = (kpos <= pos) & (kpos > pos - WINDOW)
        sw = jnp.where(wmask, sw, NEG_BIG)
        ew = jnp.exp(sw - jnp.max(sw, axis=0, keepdims=True))
        pw = ew / jnp.maximum(jnp.sum(ew, axis=0, keepdims=True), 1e-30)
        o_win = jnp.zeros((HEAD_DIM, cols), F32)
        for t in range(win_tiles + 1):
            o_win = o_win + _dotb(vw_ref[0, 0, t0 + t], pw[t * Q_BLOCK:(t + 1) * Q_BLOCK])

        g = _sigmoid(gt_ref[0, 0])
        for h in range(GROUP):
            c = slice(h * Q_BLOCK, (h + 1) * Q_BLOCK)
            o = (g[h:h + 1, :] * o_cmp[:, c] + g[GROUP + h:GROUP + h + 1, :] * o_sel[:, c]
                 + g[2 * GROUP + h:2 * GROUP + h + 1, :] * o_win[:, c])
            o_ref[0, 0, 0, :, c] = o.astype(BF16)

    def k_spec(off):
        return pl.BlockSpec((1, T, HEAD_DIM), lambda b, kv, i: (b, 0, off + kv))

    vs_spec = pl.BlockSpec((1, 1, T // SEL_KT, HEAD_DIM, SEL_KT), lambda b, kv, i: (b, kv, 0, 0, 0))
    vw_spec = pl.BlockSpec((1, 1, nqb, HEAD_DIM, Q_BLOCK), lambda b, kv, i: (b, kv, 0, 0, 0))
    qo_spec = pl.BlockSpec((1, 1, 1, HEAD_DIM, cols), lambda b, kv, i: (b, kv, i, 0, 0))
    return pl.pallas_call(
        body, grid=(B, N_KV, nqb),
        in_specs=[qo_spec,
                  pl.BlockSpec((1, 1, n_cmp_rows, HEAD_DIM), lambda b, kv, i: (b, kv, 0, 0)),
                  pl.BlockSpec((1, 1, HEAD_DIM, n_cmp_rows), lambda b, kv, i: (b, kv, 0, 0)),
                  k_spec(0), vs_spec, k_spec(4), vw_spec,
                  pl.BlockSpec((1, 1, 3 * GROUP, Q_BLOCK), lambda b, kv, i: (b, kv, 0, i)),
                  pl.BlockSpec((n_sel, n_cmp_rows), lambda b, kv, i: (0, 0)),
                  pl.BlockSpec((T // SEL_KT, SEL_KT, n_sel), lambda b, kv, i: (0, 0, 0))],
        out_specs=qo_spec,
        out_shape=jax.ShapeDtypeStruct((B, N_KV, nqb, HEAD_DIM, cols), BF16),
        compiler_params=_cp(("parallel", "parallel", "arbitrary")), name="nsa_prompt",
    )(q_t, kcmp, vcmp_t, kvb, vs_t, kvb, vw_t, gates_t, ovl_t, expand_t)


def _nsa_sample(q, kcmp, vcmp, pool_k, pool_v, ptab, kvnew, win_k, win_v, win_off, gates, ovl, expand, DB, past_len):
    G = PAGES_PER_STEP
    n_pages = past_len // PAGE_SIZE
    steps = n_pages // G
    n_cmp_rows = kcmp.shape[2]
    sel_cols = ovl.shape[1]
    n_past_blk = past_len // SEL_BLOCK
    n_sel = -(-(past_len + DEC_SEQ) // SEL_BLOCK)
    w_buf = win_k.shape[1] // N_KV
    rows = GROUP * Q_PAD
    keys = G * PAGE_SIZE
    new_rows = kvnew.shape[1]

    def body(pt_ref, q_ref, kc_ref, vc_ref, ovl_ref, *rest):
        pk = rest[:G]
        pv = rest[G:2 * G]
        (ex_ref, kvn_ref, wk_ref, wv_ref, gt_ref, o_ref,
         sel_scr, m_scr, l_scr, acc_scr, base_scr) = rest[2 * G:]
        step = pl.program_id(1)
        tok = lax.broadcasted_iota(jnp.int32, (rows, 1), 0) & (Q_PAD - 1)
        q8 = q_ref[0]

        def q_rows(kv):
            return jnp.concatenate(
                [q8[:, (kv * GROUP + h) * HEAD_DIM:(kv * GROUP + h + 1) * HEAD_DIM] for h in range(GROUP)], axis=0)

        def head_col(gt, c0):
            return jnp.concatenate(
                [jnp.broadcast_to(gt[:, c0 + h:c0 + h + 1], (Q_PAD, HEAD_DIM)) for h in range(GROUP)], axis=0)

        @pl.when(step == 0)
        def _():
            o_cmps = []
            imps = []
            for kv in range(N_KV):
                s = _dot_nt(q_rows(kv), kc_ref[0, kv]) * ATTN_SCALE
                col = lax.broadcasted_iota(jnp.int32, (rows, n_cmp_rows), 1)
                cmask = (col >= 1) & (col * CMP_STRIDE + (CMP_BLOCK - 1 - CMP_STRIDE) <= past_len + tok)
                p = _masked_softmax(s, cmask)
                o_cmps.append(_dotb(p, vc_ref[0, kv]))
                imp = p[0:Q_PAD]
                for h in range(1, GROUP):
                    imp = imp + p[h * Q_PAD:(h + 1) * Q_PAD]
                imps.append(imp)

            score = _dotf(jnp.concatenate(imps, axis=0), ovl_ref[...])
            j = lax.broadcasted_iota(jnp.int32, (N_KV * Q_PAD, sel_cols), 1)
            trow = lax.broadcasted_iota(jnp.int32, (N_KV * Q_PAD, sel_cols), 0) & (Q_PAD - 1)
            qblk = (past_len + trow) // SEL_BLOCK
            forced = (j == 0) | ((j <= qblk) & (j > qblk - N_LOCAL))
            score = jnp.where(forced, FORCED_SCORE, jnp.where(j > qblk, NEG_BIG, score))
            score = jnp.where(j < n_sel, score, -jnp.inf)
            rank = jnp.zeros((N_KV * Q_PAD, sel_cols), F32)
            for r in range(n_sel):
                sr = score[:, r:r + 1]
                rank = rank + jnp.where((sr > score) | ((sr == score) & (j > r)), 1.0, 0.0)
            sel_all = jnp.where((rank < min(SEL_TOPK, n_sel)) & (j < n_sel), 1.0, 0.0)

            for kv in range(N_KV):
                qh = q_rows(kv)
                o_cmp = o_cmps[kv]
                sel4 = jnp.concatenate([sel_all[kv * Q_PAD:(kv + 1) * Q_PAD]] * GROUP, axis=0)
                sel_scr[kv] = sel4.astype(BF16)

                kb = wk_ref[0, pl.ds(kv, w_buf, stride=N_KV), :].astype(BF16)
                vb = wv_ref[0, pl.ds(kv, w_buf, stride=N_KV), :].astype(BF16)
                kn = kvn_ref[0, :, (4 + kv) * HEAD_DIM:(5 + kv) * HEAD_DIM]
                vn = kvn_ref[0, :, (6 + kv) * HEAD_DIM:(7 + kv) * HEAD_DIM]
                s1 = jnp.where(lax.broadcasted_iota(jnp.int32, (rows, w_buf), 1) - w_buf > tok - WINDOW,
                               _dot_nt(qh, kb) * ATTN_SCALE, NEG_BIG)
                ncol = lax.broadcasted_iota(jnp.int32, (rows, new_rows), 1)
                nmask = (ncol <= tok) & (ncol < DEC_SEQ)
                s2 = jnp.where(nmask, _dot_nt(qh, kn) * ATTN_SCALE, NEG_BIG)
                m = jnp.maximum(jnp.max(s1, axis=-1, keepdims=True), jnp.max(s2, axis=-1, keepdims=True))
                e1 = jnp.where(s1 > 0.5 * NEG_BIG, jnp.exp(s1 - m), 0.0)
                e2 = jnp.where(nmask, jnp.exp(s2 - m), 0.0)
                den = jnp.sum(e1, axis=-1, keepdims=True) + jnp.sum(e2, axis=-1, keepdims=True)
                o_win = (_dotb(e1, vb) + _dotb(e2, vn)) / jnp.maximum(den, 1e-30)

                gt = _sigmoid(gt_ref[0, kv])
                base_scr[kv] = head_col(gt, 0) * o_cmp + head_col(gt, 2 * GROUP) * o_win

                ksn = kvn_ref[0, :, kv * HEAD_DIM:(kv + 1) * HEAD_DIM]
                vsn = kvn_ref[0, :, (2 + kv) * HEAD_DIM:(3 + kv) * HEAD_DIM]
                chosen = sel4[:, n_past_blk:n_past_blk + 1] > 0.5
                smask = nmask & chosen
                s3 = jnp.where(smask, _dot_nt(qh, ksn) * ATTN_SCALE, NEG_BIG)
                m0 = jnp.max(s3, axis=-1, keepdims=True)
                e3 = jnp.where(smask, jnp.exp(s3 - m0), 0.0)
                m_scr[kv] = jnp.broadcast_to(m0, (rows, HEAD_DIM))
                l_scr[kv] = jnp.broadcast_to(jnp.sum(e3, axis=-1, keepdims=True), (rows, HEAD_DIM))
                acc_scr[kv] = _dotb(e3, vsn)

        for kv in range(N_KV):
            qh = q_rows(kv)
            k = jnp.concatenate([r[0, pl.ds(kv, PAGE_SIZE, stride=N_KV), :] for r in pk], axis=0).astype(BF16)
            v = jnp.concatenate([r[0, pl.ds(kv, PAGE_SIZE, stride=N_KV), :] for r in pv], axis=0).astype(BF16)
            mask = _dotb(sel_scr[kv], ex_ref[0]) > 0.5
            sc = jnp.where(mask, _dot_nt(qh, k) * ATTN_SCALE, NEG_BIG)
            m_old = m_scr[kv][:, 0:1]
            l_old = l_scr[kv][:, 0:1]
            m_new = jnp.maximum(m_old, jnp.max(sc, axis=-1, keepdims=True))
            alpha = jnp.exp(m_old - m_new)
            e = jnp.where(mask, jnp.exp(sc - m_new), 0.0)
            l_new = alpha * l_old + jnp.sum(e, axis=-1, keepdims=True)
            acc = alpha * acc_scr[kv] + _dotb(e, v)
            m_scr[kv] = jnp.broadcast_to(m_new, (rows, HEAD_DIM))
            l_scr[kv] = jnp.broadcast_to(l_new, (rows, HEAD_DIM))
            acc_scr[kv] = acc

        @pl.when(step == steps - 1)
        def _():
            for kv in range(N_KV):
                gt = _sigmoid(gt_ref[0, kv])
                o_sel = acc_scr[kv] / jnp.maximum(l_scr[kv], 1e-30)
                o = base_scr[kv] + head_col(gt, GROUP) * o_sel
                for h in range(GROUP):
                    hh = kv * GROUP + h
                    o_ref[0, :, hh * HEAD_DIM:(hh + 1) * HEAD_DIM] = o[h * Q_PAD:(h + 1) * Q_PAD].astype(BF16)

    def page_map(jj, b, g, pt):
        return (pt[b * n_pages + g * G + jj], 0, 0)

    page_specs = [pl.BlockSpec((1, N_KV * PAGE_SIZE, HEAD_DIM), functools.partial(page_map, jj)) for jj in range(G)]
    cmp_spec = pl.BlockSpec((1, N_KV, n_cmp_rows, HEAD_DIM), lambda b, g, pt: (b, 0, 0, 0))
    in_specs = [pl.BlockSpec((1, Q_PAD, Q_W), lambda b, g, pt: (b, 0, 0)), cmp_spec, cmp_spec,
                pl.BlockSpec((n_cmp_rows, sel_cols), lambda b, g, pt: (0, 0))]
    in_specs += page_specs + page_specs
    in_specs += [pl.BlockSpec((1, sel_cols, keys), lambda b, g, pt: (g, 0, 0)),
                 pl.BlockSpec((1, new_rows, 4 * KV_W), lambda b, g, pt: (b, 0, 0)),
                 pl.BlockSpec((1, N_KV * w_buf, HEAD_DIM), lambda b, g, pt: (win_off + b, 0, 0)),
                 pl.BlockSpec((1, N_KV * w_buf, HEAD_DIM), lambda b, g, pt: (win_off + b, 0, 0)),
                 pl.BlockSpec((1, N_KV, Q_PAD, 3 * GROUP), lambda b, g, pt: (b, 0, 0, 0))]
    gs = pltpu.PrefetchScalarGridSpec(
        num_scalar_prefetch=1, grid=(DB, steps), in_specs=in_specs,
        out_specs=pl.BlockSpec((1, Q_PAD, Q_W), lambda b, g, pt: (b, 0, 0)),
        scratch_shapes=[pltpu.VMEM((N_KV, rows, sel_cols), BF16), pltpu.VMEM((N_KV, rows, HEAD_DIM), F32),
                        pltpu.VMEM((N_KV, rows, HEAD_DIM), F32), pltpu.VMEM((N_KV, rows, HEAD_DIM), F32),
                        pltpu.VMEM((N_KV, rows, HEAD_DIM), F32)])
    return pl.pallas_call(
        body, grid_spec=gs, out_shape=jax.ShapeDtypeStruct((DB, Q_PAD, Q_W), BF16),
        compiler_params=_cp(("parallel", "arbitrary")), name="nsa_sample",
    )(ptab, q, kcmp, vcmp, ovl, *([pool_k] * G), *([pool_v] * G), expand, kvnew, win_k, win_v, gates)


def _rg_gates(u, wa_ref, ba, wx_ref, bx, sp):
    ra = []
    rx = []
    for n in range(RG_BLOCKS):
        ub = u[:, n * RG_BW:(n + 1) * RG_BW]
        ra.append(_dotf(ub, wa_ref[n]))
        rx.append(_dotf(ub, wx_ref[n]))
    r = _sigmoid(jnp.concatenate(ra, axis=1) + ba)
    i = _sigmoid(jnp.concatenate(rx, axis=1) + bx)
    log_a = -RG_C * r * sp
    a = jnp.exp(log_a)
    b = jnp.sqrt(1.0 - jnp.exp(2.0 * log_a)) * i * u
    return a, b


def _softplus_neg(lam):
    return jnp.maximum(-lam, 0.0) + jnp.log(1.0 + jnp.exp(-jnp.abs(lam)))


def _rglru_prompt(z, cw, cb, wa, ba, wx, bx, lam, B, T, tt):
    keep = 8

    def body(xr_ref, gr_ref, cw_ref, cb_ref, wa_ref, ba_ref, wx_ref, bx_ref, lam_ref,
             y_ref, hl_ref, a_scr, b_scr, h_scr, tail_scr):
        t = pl.program_id(1)

        @pl.when(t == 0)
        def _():
            h_scr[...] = jnp.zeros_like(h_scr)
            tail_scr[...] = jnp.zeros_like(tail_scr)

        x = xr_ref[0]
        row = lax.broadcasted_iota(jnp.int32, (tt, D_RNN), 0)
        cwv = cw_ref[...]
        u = cb_ref[...] + x * cwv[RG_CONV - 1:RG_CONV]
        for d in range(1, RG_CONV):
            xs = pltpu.roll(x, d, 0)
            for r in range(d):
                xs = jnp.where(row == r, tail_scr[keep - d + r:keep - d + r + 1, :], xs)
            u = u + xs * cwv[RG_CONV - 1 - d:RG_CONV - d]
        tail_scr[...] = x[tt - keep:tt]
        a, b = _rg_gates(u, wa_ref, ba_ref[...], wx_ref, bx_ref[...], _softplus_neg(lam_ref[...]))
        a_scr[...] = a
        b_scr[...] = b

        def step(s, h):
            h = a_scr[pl.ds(s, 1), :] * h + b_scr[pl.ds(s, 1), :]
            b_scr[pl.ds(s, 1), :] = h
            return h

        h = lax.fori_loop(0, tt, step, h_scr[0:1, :], unroll=8)
        h_scr[0:1, :] = h
        hl_ref[0] = h
        y_ref[0] = (b_scr[...] * _gelu(gr_ref[0])).astype(BF16)

    vec = pl.BlockSpec((1, D_RNN), lambda b, t: (0, 0))
    wspec = pl.BlockSpec((RG_BLOCKS, RG_BW, RG_BW), lambda b, t: (0, 0, 0))
    return pl.pallas_call(
        body, grid=(B, T // tt),
        in_specs=[pl.BlockSpec((1, tt, D_RNN), lambda b, t: (b, t, COL_XR // D_RNN)),
                  pl.BlockSpec((1, tt, D_RNN), lambda b, t: (b, t, COL_GR // D_RNN)),
                  pl.BlockSpec((RG_CONV, D_RNN), lambda b, t: (0, 0)), vec, wspec, vec, wspec, vec, vec],
        out_specs=[pl.BlockSpec((1, tt, D_RNN), lambda b, t: (b, t, 0)),
                   pl.BlockSpec((1, 1, D_RNN), lambda b, t: (b, 0, 0))],
        out_shape=[jax.ShapeDtypeStruct((B, T, D_RNN), BF16), jax.ShapeDtypeStruct((B, 1, D_RNN), F32)],
        scratch_shapes=[pltpu.VMEM((tt, D_RNN), F32), pltpu.VMEM((tt, D_RNN), F32),
                        pltpu.VMEM((8, D_RNN), F32), pltpu.VMEM((keep, D_RNN), F32)],
        compiler_params=_cp(("parallel", "arbitrary")), name="rglru_prompt",
    )(z, z, cw, cb.reshape(1, -1), wa, ba.reshape(1, -1), wx, bx.reshape(1, -1), lam.reshape(1, -1))


def _rglru_sample(xr, gr, buf, h0, cw, cb, wa, ba, wx, bx, lam):
    DB = h0.shape[0]

    def body(xr_ref, gr_ref, buf_ref, h0_ref, cw_ref, cb_ref, wa_ref, ba_ref, wx_ref, bx_ref, lam_ref, y_ref, hl_ref):
        xp = [buf_ref[k] for k in range(RG_CONV - 1)] + [xr_ref[t] for t in range(DEC_SEQ)]
        cwv = cw_ref[...]
        sp = _softplus_neg(lam_ref[...])
        h = h0_ref[...]
        for t in range(DEC_SEQ):
            u = cb_ref[...]
            for k in range(RG_CONV):
                u = u + xp[t + k] * cwv[k:k + 1]
            a, b = _rg_gates(u, wa_ref, ba_ref[...], wx_ref, bx_ref[...], sp)
            h = a * h + b
            y_ref[t] = (h * _gelu(gr_ref[t])).astype(BF16)
        hl_ref[...] = h

    return pl.pallas_call(
        body, out_shape=[jax.ShapeDtypeStruct((DEC_SEQ, DB, D_RNN), BF16), jax.ShapeDtypeStruct((DB, D_RNN), F32)],
        compiler_params=pltpu.CompilerParams(vmem_limit_bytes=32 * 1024 * 1024), name="rglru_sample",
    )(xr, gr, buf, h0, cw, cb.reshape(1, -1), wa, ba.reshape(1, -1), wx, bx.reshape(1, -1), lam.reshape(1, -1))


def _s5_discretize(lam_re, lam_im, log_dt, b_re_t, b_im_t):
    def body(lr_ref, li_ref, ldt_ref, br_ref, bi_ref, ar_ref, ai_ref, bbr_ref, bbi_ref):
        lr = lr_ref[...]
        li = li_ref[...]
        dt = jnp.exp(ldt_ref[...])
        mag = jnp.exp(lr * dt)
        ar = mag * jnp.cos(li * dt)
        ai = mag * jnp.sin(li * dt)
        den = lr * lr + li * li
        n_re = ar - 1.0
        f_re = (n_re * lr + ai * li) / den
        f_im = (ai * lr - n_re * li) / den
        ar_ref[...] = ar
        ai_ref[...] = ai
        for c in range(S5_CH):
            bbr_ref[c] = f_re * br_ref[c] - f_im * bi_ref[c]
            bbi_ref[c] = f_re * bi_ref[c] + f_im * br_ref[c]

    gp = jax.ShapeDtypeStruct(lam_re.shape, F32)
    cgp = jax.ShapeDtypeStruct(b_re_t.shape, F32)
    return pl.pallas_call(body, out_shape=[gp, gp, cgp, cgp], name="s5_discretize")(
        lam_re, lam_im, log_dt.reshape(-1, 1), b_re_t, b_im_t)


def _s5_block_weights(bbr, bbi, c_re, c_im):
    eye = jnp.eye(S5_GB, dtype=F32)

    def bblk(bb):
        x = bb.reshape(S5_CH, S5_NGB, S5_GB, S5_STATE)
        x = jnp.einsum("cngp,gh->ngchp", x, eye)
        return x.reshape(S5_NGB, S5_GB * S5_CH, S5_LANES)

    def cblk(c):
        x = c.reshape(S5_NGB, S5_GB, S5_CH, S5_STATE)
        x = jnp.einsum("ngcp,gh->ngphc", x, eye)
        return x.reshape(S5_NGB, S5_LANES, S5_GB * S5_CH)

    return jnp.concatenate([bblk(bbr), bblk(bbi)], axis=2), cblk(c_re), cblk(c_im)


def _s5_prompt(u, bw, cr, ci, ar, ai, d, B, T):
    L = S5_LANES
    lanes = S5_GB * S5_CH
    nseg = S5_NSEG
    sl = T // nseg
    assert sl & (sl - 1) == 0, "segment length must be a power of two"
    chs = min(64, sl)
    ch = chs * nseg
    nchunks = sl // chs

    def body(u_ref, bw_ref, cr_ref, ci_ref, ar_ref, ai_ref, d_ref, zz_ref, sre_ref, sim_ref, h_scr, t_scr):
        a_re = ar_ref[0]
        a_im = ai_ref[0]
        ar8 = jnp.broadcast_to(a_re, (nseg, L))
        ai8 = jnp.broadcast_to(a_im, (nseg, L))

        def load_u(s0):
            for seg in range(nseg):
                t_scr[pl.ds(seg, chs, stride=nseg), :] = u_ref[0, seg, pl.ds(s0, chs), :]
            return t_scr[...]

        def project(c, carry):
            s0 = pl.multiple_of(c * chs, chs)
            h_scr[pl.ds(pl.multiple_of(c * ch, ch), ch), :] = _dotb(load_u(s0), bw_ref[0])
            return carry

        lax.fori_loop(0, nchunks, project, 0)

        def advance(s, carry, store):
            hr, hi = carry
            rows = pl.ds(pl.multiple_of(s * nseg, nseg), nseg)
            bu = h_scr[rows, :]
            nr = ar8 * hr - ai8 * hi + bu[:, :L]
            ni = ar8 * hi + ai8 * hr + bu[:, L:]
            if store:
                h_scr[rows, :] = jnp.concatenate([nr, ni], axis=1)
            return nr, ni

        zero = jnp.zeros((nseg, L), F32)
        er, ei = lax.fori_loop(0, sl, functools.partial(advance, store=False), (zero, zero), unroll=8)

        pr, pi = a_re, a_im
        for _ in range(sl.bit_length() - 1):
            pr, pi = pr * pr - pi * pi, 2.0 * pr * pi
        sr = [jnp.zeros((1, L), F32)]
        si = [jnp.zeros((1, L), F32)]
        for k in range(nseg):
            sr.append(er[k:k + 1] + pr * sr[k] - pi * si[k])
            si.append(ei[k:k + 1] + pr * si[k] + pi * sr[k])
        sre_ref[0, 0] = sr[nseg]
        sim_ref[0, 0] = si[nseg]
        start = (jnp.concatenate(sr[:nseg], axis=0), jnp.concatenate(si[:nseg], axis=0))
        lax.fori_loop(0, sl, functools.partial(advance, store=True), start, unroll=8)

        def emit(c, carry):
            s0 = pl.multiple_of(c * chs, chs)
            hc = h_scr[pl.ds(pl.multiple_of(c * ch, ch), ch), :]
            uc = load_u(s0)
            y = _dotb(hc[:, :L], cr_ref[0]) - _dotb(hc[:, L:], ci_ref[0]) + d_ref[...] * uc
            t_scr[...] = _gelu(y)
            for seg in range(nseg):
                zz_ref[0, seg, pl.ds(s0, chs), :] = t_scr[pl.ds(seg, chs, stride=nseg), :].astype(BF16)
            return carry

        lax.fori_loop(0, nchunks, emit, 0)

    st_spec = pl.BlockSpec((1, 1, 1, L), lambda b, g: (b, g, 0, 0))
    st_shape = jax.ShapeDtypeStruct((B, S5_NGB, 1, L), F32)
    io_spec = pl.BlockSpec((1, nseg, sl, lanes), lambda b, g: (b, 0, 0, g))
    return pl.pallas_call(
        body, grid=(B, S5_NGB),
        in_specs=[io_spec,
                  pl.BlockSpec((1, lanes, 2 * L), lambda b, g: (g, 0, 0)),
                  pl.BlockSpec((1, L, lanes), lambda b, g: (g, 0, 0)),
                  pl.BlockSpec((1, L, lanes), lambda b, g: (g, 0, 0)),
                  pl.BlockSpec((1, 1, L), lambda b, g: (g, 0, 0)),
                  pl.BlockSpec((1, 1, L), lambda b, g: (g, 0, 0)),
                  pl.BlockSpec((1, lanes), lambda b, g: (0, g))],
        out_specs=[io_spec, st_spec, st_spec],
        out_shape=[jax.ShapeDtypeStruct((B, nseg, sl, D_MODEL), BF16), st_shape, st_shape],
        scratch_shapes=[pltpu.VMEM((T, 2 * L), F32), pltpu.VMEM((ch, lanes), F32)],
        compiler_params=_cp(("parallel", "parallel")), name="s5_prompt",
    )(u, bw, cr, ci, ar, ai, d.reshape(1, -1))


def _s5_sample(u, h0r, h0i, bw, cr, ci, ar, ai, d):
    DB = u.shape[1]
    L = S5_LANES
    lanes = S5_GB * S5_CH

    def body(u_ref, hr_ref, hi_ref, bw_ref, cr_ref, ci_ref, ar_ref, ai_ref, d_ref, zz_ref, sre_ref, sim_ref):
        hr = hr_ref[...]
        hi = hi_ref[...]
        a_re = ar_ref[0]
        a_im = ai_ref[0]
        for t in range(DEC_SEQ):
            uv = u_ref[t]
            bu = _dotf(uv, bw_ref[0])
            hr, hi = a_re * hr - a_im * hi + bu[:, :L], a_re * hi + a_im * hr + bu[:, L:]
            y = _dotf(hr, cr_ref[0]) - _dotf(hi, ci_ref[0]) + d_ref[...] * uv
            zz_ref[t] = _gelu(y).astype(BF16)
        sre_ref[...] = hr
        sim_ref[...] = hi

    st_spec = pl.BlockSpec((DB, L), lambda g: (0, g))
    st_shape = jax.ShapeDtypeStruct((DB, S5_GROUPS * S5_STATE), F32)
    return pl.pallas_call(
        body, grid=(S5_NGB,),
        in_specs=[pl.BlockSpec((DEC_SEQ, DB, lanes), lambda g: (0, 0, g)), st_spec, st_spec,
                  pl.BlockSpec((1, lanes, 2 * L), lambda g: (g, 0, 0)),
                  pl.BlockSpec((1, L, lanes), lambda g: (g, 0, 0)),
                  pl.BlockSpec((1, L, lanes), lambda g: (g, 0, 0)),
                  pl.BlockSpec((1, 1, L), lambda g: (g, 0, 0)),
                  pl.BlockSpec((1, 1, L), lambda g: (g, 0, 0)),
                  pl.BlockSpec((1, lanes), lambda g: (0, g))],
        out_specs=[pl.BlockSpec((DEC_SEQ, DB, lanes), lambda g: (0, 0, g)), st_spec, st_spec],
        out_shape=[jax.ShapeDtypeStruct((DEC_SEQ, DB, D_MODEL), BF16), st_shape, st_shape],
        compiler_params=_cp(("parallel",)), name="s5_sample",
    )(u, h0r, h0i, bw, cr, ci, ar, ai, d.reshape(1, -1))


def _ffn_upgate_prompt(x, g, w_up, w_gate, cw, cb, T, tm, tn):
    M, K = x.shape
    N = w_up.shape[1]
    tiles_per_seq = T // tm
    pr = 16

    def body(x_ref, xp_ref, g_ref, wu_ref, wg_ref, cw_ref, cb_ref, act_ref, tail_ref, a_scr):
        i = pl.program_id(0)

        @pl.when(pl.program_id(1) == 0)
        def _():
            a_scr[...] = _rms_rows(x_ref[...], g_ref[...]).astype(BF16)

        a = a_scr[...]
        hu = jnp.dot(a, wu_ref[...], preferred_element_type=F32)
        hg = jnp.dot(a, wg_ref[...], preferred_element_type=F32)
        ap = _rms_rows(xp_ref[...], g_ref[...]).astype(BF16)
        prev = jnp.dot(ap, wu_ref[...], preferred_element_type=F32)
        prev = jnp.where(i % tiles_per_seq == 0, 0.0, prev)
        row = lax.broadcasted_iota(jnp.int32, hu.shape, 0)
        h1 = jnp.where(row == 0, prev[pr - 1:pr], pltpu.roll(hu, 1, 0))
        h2 = jnp.where(row == 0, prev[pr - 2:pr - 1], jnp.where(row == 1, prev[pr - 1:pr], pltpu.roll(hu, 2, 0)))
        cwv = cw_ref[...]
        hc = cb_ref[...] + h2 * cwv[0:1] + h1 * cwv[1:2] + hu * cwv[2:3]
        act_ref[...] = (_gelu(hc) * hg).astype(BF16)
        tail_ref[...] = hu[tm - 8:tm]

    return pl.pallas_call(
        body, grid=(M // tm, N // tn),
        in_specs=[pl.BlockSpec((tm, K), lambda i, j: (i, 0)),
                  pl.BlockSpec((pr, K), lambda i, j: (jnp.maximum(i * (tm // pr) - 1, 0), 0)),
                  pl.BlockSpec((1, K), lambda i, j: (0, 0)),
                  pl.BlockSpec((K, tn), lambda i, j: (0, j)),
                  pl.BlockSpec((K, tn), lambda i, j: (0, j)),
                  pl.BlockSpec((FFN_CONV, tn), lambda i, j: (0, j)),
                  pl.BlockSpec((1, tn), lambda i, j: (0, j))],
        out_specs=[pl.BlockSpec((tm, tn), lambda i, j: (i, j)), pl.BlockSpec((8, tn), lambda i, j: (i, j))],
        out_shape=[jax.ShapeDtypeStruct((M, N), BF16), jax.ShapeDtypeStruct((M // tm * 8, N), F32)],
        scratch_shapes=[pltpu.VMEM((tm, K), BF16)],
        compiler_params=_cp(("parallel", "arbitrary")), name="ffn_upgate_prompt",
    )(x, x, g.reshape(1, K), w_up, w_gate, cw, cb.reshape(1, -1))


def _tile(m, pref):
    return pref if m % pref == 0 else m


def _dense_tail(x1, p, wts, ffn_state, T):
    (norm_ffn, norm_ple, w_up, w_gate, conv_w, conv_b, w_down, w_proj, w_pgate) = wts
    M = x1.shape[0]
    tm = _tile(M, 1024)
    tn = 512
    prompt = ffn_state is None

    if prompt:
        act, tails = _ffn_upgate_prompt(x1, norm_ffn, w_up, w_gate, conv_w, conv_b, T, tm, tn)
        tails = tails.reshape(M // tm, 8, D_FF)
        tiles_per_seq = T // tm
        new_buf = tails[tiles_per_seq - 1::tiles_per_seq, 8 - (FFN_CONV - 1):, :]
    else:
        nseq = M // T
        h1 = jnp.repeat(ffn_state[:, FFN_CONV - 2], T, axis=0)
        h2 = jnp.stack([ffn_state[:, 0], ffn_state[:, 1]] + [ffn_state[:, 1]] * (T - 2), axis=1).reshape(M, D_FF)

        def conv_epilogue(accs, ex):
            hu, hg = accs
            b1, b2, cwv, cbv = ex
            tok = lax.broadcasted_iota(jnp.int32, hu.shape, 0) % T
            s1 = jnp.where(tok >= 1, pltpu.roll(hu, 1, 0), b1)
            s2 = jnp.where(tok >= 2, pltpu.roll(hu, 2, 0), b2)
            hc = cbv + s2 * cwv[0:1] + s1 * cwv[1:2] + hu * cwv[2:3]
            return _gelu(hc) * hg, hu

        act, hu = _mm([x1], [w_up, w_gate], [0, 0],
                      [(h1, "tile"), (h2, "tile"), (conv_w, "rows"), (conv_b.reshape(1, -1), "rows")],
                      conv_epilogue, [BF16, F32], tm, tn, "ffn_upgate_sample", norm=(0, norm_ffn))
        new_buf = hu.reshape(nseq, T, D_FF)[:, T - (FFN_CONV - 1):]

    (x2,) = _mm([act], [w_down], [0], [(x1, "tile")], lambda accs, ex: (ex[0] + accs[0],), [F32],
                _tile(M, 512), tn, "ffn_down")
    (x3,) = _mm([p, x2], [w_proj, w_pgate], [0, 1], [(x2, "tile")],
                lambda accs, ex: (ex[0] + accs[0] * _sigmoid(accs[1]),), [F32], tm, tn, "ple", norm=(1, norm_ple))
    return x3, new_buf


def kernel(x_prompt, x_sample, cache_cmp_k, cache_cmp_v, cache_sel_k, cache_sel_v, cache_win_k, cache_win_v, state_rglru_h, state_rglru_conv, state_s5_re, state_s5_im, state_ffn_conv, page_table, p_prompt, p_sample, norm_mix, norm_ffn, norm_ple, w_in_even, w_out_even, q_norm, k_norm_cmp, k_norm_sel, k_norm_win, cmp_pe_k, cmp_w1_k, cmp_w2_k, cmp_pe_v, cmp_w1_v, cmp_w2_v, rg_conv_w, rg_conv_b, rg_w_a, rg_b_a, rg_w_x, rg_b_x, rg_lam, s5_lam_re, s5_lam_im, s5_log_dt, s5_b_re, s5_b_im, s5_c_re, s5_c_im, s5_d, s5_w_glu_a, s5_w_glu_b, ffn_w_up, ffn_w_gate, ffn_conv_w, ffn_conv_b, ffn_w_down, ple_w_proj, ple_w_gate):
    B, T, _ = x_prompt.shape
    DB = x_sample.shape[0]
    n_pages = page_table.shape[1]
    past_len = n_pages * PAGE_SIZE
    w_buf = cache_win_k.shape[2]
    MP = B * T
    MS = DB * DEC_SEQ
    depth = norm_mix.shape[0]

    def reorder_in(w):
        q, kv, gt, xr, gr = (w[:, :Q_W], w[:, Q_W:Q_W + 6 * KV_W], w[:, Q_W + 6 * KV_W:Q_W + 6 * KV_W + 3 * N_HEADS],
                             w[:, Q_W + 6 * KV_W + 3 * N_HEADS:Q_W + 6 * KV_W + 3 * N_HEADS + D_RNN],
                             w[:, Q_W + 6 * KV_W + 3 * N_HEADS + D_RNN:])
        pad = jnp.zeros((w.shape[0], IN_PAD - COL_GT - 3 * N_HEADS), w.dtype)
        return jnp.concatenate([q, xr, gr, kv, gt, pad], axis=1).astype(BF16)

    def layer_dense_weights(li):
        return (norm_ffn[li], norm_ple[li], ffn_w_up[li].astype(BF16), ffn_w_gate[li].astype(BF16),
                ffn_conv_w[li], ffn_conv_b[li], ffn_w_down[li].astype(BF16),
                ple_w_proj[li].astype(BF16), ple_w_gate[li].astype(BF16))

    xs = {"p": x_prompt.reshape(MP, D_MODEL), "s": x_sample.reshape(MS, D_MODEL)}
    pe_in = {"p": p_prompt.reshape(depth, MP, -1).astype(BF16), "s": p_sample.reshape(depth, MS, -1).astype(BF16)}
    seq_len = {"p": T, "s": DEC_SEQ}
    ev = {"p": [], "s": []}
    od = {"p": [], "s": []}
    ff = {"p": [], "s": []}
    ptab = page_table.reshape(-1).astype(jnp.int32)

    for li in range(depth):
        dense_w = layer_dense_weights(li)
        if li % 2 == 0:
            e = li // 2
            w_in = reorder_in(w_in_even[e])
            w_out = w_out_even[e].astype(BF16)
            w_out_a, w_out_r = w_out[:Q_W], w_out[Q_W:]
            for grp in ("p", "s"):
                x = xs[grp]
                M = x.shape[0]
                L = seq_len[grp]
                nseq = M // L
                (z,) = _mm([x], [w_in], [0], [], lambda accs, ex: (accs[0],), [F32], _tile(M, 1024), 512, "in_proj",
                           norm=(0, norm_mix[li]))
                if grp == "p":
                    cos_t, sin_t = _rope_tables(jnp.arange(T))
                    tmq = _tile(T, 512)
                    tab_blocks = T // tmq
                else:
                    cos_t, sin_t = _rope_tables(jnp.tile(past_len + jnp.arange(DEC_SEQ), DB))
                    tmq = M
                    tab_blocks = 1
                q, kvb, kc, vc, ks, vs, kw, vw = _qk_prep(z, cos_t, sin_t, tab_blocks, q_norm[e], k_norm_sel[e],
                                                          k_norm_win[e], tmq)
                z3 = z.reshape(nseq, L, IN_PAD)

                if grp == "p":
                    n_rows = T // CMP_STRIDE
                    ccos, csin = _rope_tables(jnp.arange(n_rows) * CMP_STRIDE + CMP_STRIDE - 1)
                    ident = jnp.arange(B * (T // PAGE_SIZE), dtype=jnp.int32)
                    kcmp = _compress(kc.reshape(-1, N_KV * PAGE_SIZE, HEAD_DIM), ident, B, T // PAGE_SIZE, cmp_pe_k[e],
                                     cmp_w1_k[e], cmp_w2_k[e], k_norm_cmp[e], ccos, csin, True, "compress_k_prompt")
                    vcmp = _compress(vc.reshape(-1, N_KV * PAGE_SIZE, HEAD_DIM), ident, B, T // PAGE_SIZE, cmp_pe_v[e],
                                     cmp_w1_v[e], cmp_w2_v[e], k_norm_cmp[e], ccos, csin, False, "compress_v_prompt")
                    n_sel = T // SEL_BLOCK
                    nqb = T // Q_BLOCK
                    ovl_t = _overlap_matrix(n_rows, n_sel, n_sel).T
                    ex_t = ((np.arange(T // SEL_KT)[:, None, None] * SEL_KT + np.arange(SEL_KT)[None, :, None])
                            // SEL_BLOCK == np.arange(n_sel)[None, None, :])
                    q_t = q.reshape(B, nqb, Q_BLOCK, N_KV, GROUP, HEAD_DIM).transpose(0, 3, 1, 5, 4, 2)
                    q_t = q_t.reshape(B, N_KV, nqb, HEAD_DIM, GROUP * Q_BLOCK)
                    vs_t = kvb.reshape(B, T // SEL_KT, SEL_KT, 4, N_KV, HEAD_DIM)[:, :, :, 1].transpose(0, 3, 1, 4, 2)
                    vw_t = kvb.reshape(B, nqb, Q_BLOCK, 4, N_KV, HEAD_DIM)[:, :, :, 3].transpose(0, 3, 1, 4, 2)
                    gates_t = z[:, COL_GT:COL_GT + 3 * N_HEADS].reshape(B, T, 3, N_KV, GROUP)
                    gates_t = gates_t.transpose(0, 3, 2, 4, 1).reshape(B, N_KV, 3 * GROUP, T)
                    o_t = _nsa_prompt(q_t, kcmp, vcmp.swapaxes(2, 3), kvb.reshape(B, T, 4 * KV_W), vs_t, vw_t,
                                      gates_t, ovl_t, jnp.asarray(ex_t, BF16), B, T)
                    o_nsa = o_t.reshape(B, N_KV, nqb, HEAD_DIM, GROUP, Q_BLOCK).transpose(0, 2, 5, 1, 4, 3)
                    o_nsa = o_nsa.reshape(M, Q_W)
                    y_rg, h_last = _rglru_prompt(z.reshape(B, T, IN_PAD), rg_conv_w[e], rg_conv_b[e], rg_w_a[e],
                                                 rg_b_a[e], rg_w_x[e], rg_b_x[e], rg_lam[e], B, T, _tile(T, 512))
                    y_rg = y_rg.reshape(M, D_RNN)
                    h_last = h_last.reshape(B, D_RNN)
                    new_conv = z3[:, T - (RG_CONV - 1):, COL_XR:COL_XR + D_RNN]
                    new_wk = kw.reshape(B, T, N_KV, HEAD_DIM)
                    new_wv = vw.reshape(B, T, N_KV, HEAD_DIM)
                    if T >= w_buf:
                        new_wk, new_wv = new_wk[:, T - w_buf:], new_wv[:, T - w_buf:]
                    else:
                        padw = ((0, 0), (w_buf - T, 0), (0, 0), (0, 0))
                        new_wk, new_wv = jnp.pad(new_wk, padw), jnp.pad(new_wv, padw)
                else:
                    n_rows = past_len // CMP_STRIDE
                    ccos, csin = _rope_tables(jnp.arange(n_rows) * CMP_STRIDE + CMP_STRIDE - 1)
                    ptab_e = ptab + e * cache_cmp_k.shape[1]
                    kcmp = _compress(cache_cmp_k.reshape(-1, N_KV * PAGE_SIZE, HEAD_DIM), ptab_e, DB, n_pages, cmp_pe_k[e],
                                     cmp_w1_k[e], cmp_w2_k[e], k_norm_cmp[e], ccos, csin, True, "compress_k_sample")
                    vcmp = _compress(cache_cmp_v.reshape(-1, N_KV * PAGE_SIZE, HEAD_DIM), ptab_e, DB, n_pages, cmp_pe_v[e],
                                     cmp_w1_v[e], cmp_w2_v[e], k_norm_cmp[e], ccos, csin, False, "compress_v_sample")
                    n_sel = -(-(past_len + DEC_SEQ) // SEL_BLOCK)
                    sel_cols = -(-n_sel // 128) * 128
                    ovl = _overlap_matrix(n_rows, n_sel, sel_cols)
                    keys = PAGES_PER_STEP * PAGE_SIZE
                    steps = n_pages // PAGES_PER_STEP
                    ex = (np.arange(sel_cols)[None, :, None]
                          == (np.arange(steps)[:, None, None] * keys + np.arange(keys)[None, None, :]) // SEL_BLOCK)
                    q8 = jnp.pad(q.reshape(DB, DEC_SEQ, Q_W), ((0, 0), (0, Q_PAD - DEC_SEQ), (0, 0)))
                    kvnew = jnp.pad(kvb.reshape(DB, DEC_SEQ, 4 * KV_W), ((0, 0), (0, 128 - DEC_SEQ), (0, 0)))
                    gt = z3[:, :, COL_GT:COL_GT + 3 * N_HEADS].reshape(DB, DEC_SEQ, 3, N_KV, GROUP)
                    gt = gt.transpose(0, 3, 1, 2, 4).reshape(DB, N_KV, DEC_SEQ, 3 * GROUP)
                    gt8 = jnp.pad(gt, ((0, 0), (0, 0), (0, Q_PAD - DEC_SEQ), (0, 0)))
                    o8 = _nsa_sample(q8, kcmp, vcmp, cache_sel_k.reshape(-1, N_KV * PAGE_SIZE, HEAD_DIM),
                                     cache_sel_v.reshape(-1, N_KV * PAGE_SIZE, HEAD_DIM), ptab_e, kvnew,
                                     cache_win_k.reshape(-1, N_KV * w_buf, HEAD_DIM),
                                     cache_win_v.reshape(-1, N_KV * w_buf, HEAD_DIM),
                                     e * DB, gt8, ovl, jnp.asarray(ex, BF16), DB, past_len)
                    o_nsa = o8[:, :DEC_SEQ].reshape(M, Q_W)
                    xr3 = z3[:, :, COL_XR:COL_XR + D_RNN]
                    gr3 = z3[:, :, COL_GR:COL_GR + D_RNN]
                    y_t, h_last = _rglru_sample(xr3.transpose(1, 0, 2), gr3.transpose(1, 0, 2),
                                                state_rglru_conv[e].transpose(1, 0, 2), state_rglru_h[e],
                                                rg_conv_w[e], rg_conv_b[e], rg_w_a[e], rg_b_a[e], rg_w_x[e],
                                                rg_b_x[e], rg_lam[e])
                    y_rg = y_t.transpose(1, 0, 2).reshape(M, D_RNN)
                    new_conv = jnp.concatenate([state_rglru_conv[e], xr3], axis=1)[:, -(RG_CONV - 1):]
                    kk = jnp.concatenate([cache_win_k[e], kw.reshape(DB, DEC_SEQ, N_KV, HEAD_DIM)], axis=1)
                    vv = jnp.concatenate([cache_win_v[e], vw.reshape(DB, DEC_SEQ, N_KV, HEAD_DIM)], axis=1)
                    new_wk, new_wv = kk[:, -w_buf:], vv[:, -w_buf:]

                rs = lambda a: a.reshape(nseq, L, N_KV, HEAD_DIM)
                ev[grp].append((rs(kc), rs(vc), rs(ks), rs(vs), new_wk, new_wv, h_last, new_conv))
                (x1,) = _mm([o_nsa, y_rg], [w_out_a, w_out_r], [0, 1], [(x, "tile")],
                            lambda accs, exs: (exs[0] + accs[0] + accs[1],), [F32], _tile(M, 1024), 512, "mixer_out")
                xs[grp], nb = _dense_tail(x1, pe_in[grp][li], dense_w, None if grp == "p" else state_ffn_conv[li], L)
                ff[grp].append(nb)
        else:
            o = li // 2
            ar, ai, bbr, bbi = _s5_discretize(s5_lam_re[o], s5_lam_im[o], s5_log_dt[o],
                                              s5_b_re[o].transpose(2, 0, 1), s5_b_im[o].transpose(2, 0, 1))
            bw, cr, ci = _s5_block_weights(bbr, bbi, s5_c_re[o], s5_c_im[o])
            ar = ar.reshape(S5_NGB, 1, S5_LANES)
            ai = ai.reshape(S5_NGB, 1, S5_LANES)
            w_a = s5_w_glu_a[o].astype(BF16)
            w_b = s5_w_glu_b[o].astype(BF16)
            for grp in ("p", "s"):
                x = xs[grp]
                M = x.shape[0]
                L = seq_len[grp]
                glu = lambda accs, exs: (exs[0] + accs[0] * _sigmoid(accs[1]),)
                if grp == "p":
                    sl = T // S5_NSEG
                    hn = _rmsnorm(x, norm_mix[li], F32, _tile(M, 256))
                    zz, sre, sim = _s5_prompt(hn.reshape(B, S5_NSEG, sl, D_MODEL), bw.astype(BF16), cr.astype(BF16),
                                              ci.astype(BF16), ar, ai, s5_d[o], B, T)
                    sre = sre.reshape(B, S5_GROUPS, S5_STATE)
                    sim = sim.reshape(B, S5_GROUPS, S5_STATE)
                    (x1,) = _mm([zz.reshape(M, D_MODEL)], [w_a, w_b], [0, 0], [(x, "tile")], glu, [F32],
                                _tile(M, 1024), 512, "mixer_out")
                else:
                    hn = _rmsnorm(x, norm_mix[li], F32, _tile(M, 256))
                    u = hn.reshape(DB, DEC_SEQ, D_MODEL).transpose(1, 0, 2)
                    zz, sre, sim = _s5_sample(u, state_s5_re[o].reshape(DB, -1), state_s5_im[o].reshape(DB, -1),
                                              bw, cr, ci, ar, ai, s5_d[o])
                    zz = zz.transpose(1, 0, 2).reshape(M, D_MODEL)
                    sre = sre.reshape(DB, S5_GROUPS, S5_STATE)
                    sim = sim.reshape(DB, S5_GROUPS, S5_STATE)
                    (x1,) = _mm([zz], [w_a, w_b], [0, 0], [(x, "tile")], glu, [F32], M, 512, "mixer_out")
                od[grp].append((sre, sim))
                xs[grp], nb = _dense_tail(x1, pe_in[grp][li], dense_w, None if grp == "p" else state_ffn_conv[li], L)
                ff[grp].append(nb)

    def stk(states, i):
        return jnp.stack([s[i] for s in states])

    outs = [xs["p"].reshape(B, T, D_MODEL), xs["s"].reshape(DB, DEC_SEQ, D_MODEL)]
    for i in range(8):
        outs += [stk(ev["p"], i), stk(ev["s"], i)]
    for i in range(2):
        outs += [stk(od["p"], i), stk(od["s"], i)]
    outs += [jnp.stack(ff["p"]), jnp.stack(ff["s"])]
    return tuple(outs)
```

```python
import functools

import numpy as np
import jax
import jax.numpy as jnp
from jax import lax
from jax.experimental import pallas as pl
from jax.experimental.pallas import tpu as pltpu

F32 = jnp.float32
BF16 = jnp.bfloat16

D_MODEL = 2048
DEC_SEQ = 4
PAGE_SIZE = 128
N_HEADS = 8
HEAD_DIM = 128
N_KV = 2
GROUP = N_HEADS // N_KV
ROPE_DIM = HEAD_DIM // 4
ROPE_HALF = ROPE_DIM // 2
ROPE_THETA = 500000.0
CMP_BLOCK = 32
CMP_STRIDE = 16
SEL_BLOCK = 64
SEL_TOPK = 16
N_LOCAL = 2
WINDOW = 512
Q_BLOCK = 128
FORCED_SCORE = 1e9
NEG_BIG = -1e30
ATTN_SCALE = HEAD_DIM ** -0.5
D_RNN = D_MODEL // 2
RG_BLOCKS = 8
RG_BW = D_RNN // RG_BLOCKS
RG_CONV = 4
RG_C = 8.0
S5_CH = 16
S5_GROUPS = D_MODEL // S5_CH
S5_STATE = 64
S5_GB = 8
S5_NGB = S5_GROUPS // S5_GB
S5_LANES = S5_GB * S5_STATE
S5_NSEG = 8
D_FF = ((8 * D_MODEL // 3 + 255) // 256) * 256
FFN_CONV = 3
NORM_EPS = 1e-6
Q_W = N_HEADS * HEAD_DIM
KV_W = N_KV * HEAD_DIM
PAGES_PER_STEP = 8
Q_PAD = 8
CMP_SPLIT = 4
SEL_KT = 512

COL_Q = 0
COL_XR = Q_W
COL_GR = Q_W + D_RNN
COL_KV = Q_W + 2 * D_RNN
COL_GT = COL_KV + 6 * KV_W
IN_PAD = 5120


def _cp(sem, vmem_mb=48):
    return pltpu.CompilerParams(dimension_semantics=sem, vmem_limit_bytes=vmem_mb * 1024 * 1024)


def _gelu(x):
    return 0.5 * x * (1.0 + jnp.tanh(0.7978845608028654 * (x + 0.044715 * (x * x * x))))


def _sigmoid(x):
    return 1.0 / (1.0 + jnp.exp(-x))


def _dotb(a, b):
    return jnp.dot(a.astype(BF16), b.astype(BF16), preferred_element_type=F32)


def _dotf(a, b):
    return jnp.dot(a, b, preferred_element_type=F32, precision=lax.Precision.HIGHEST)


def _dot_nt(a, b):
    return lax.dot_general(a, b, (((1,), (1,)), ((), ())), preferred_element_type=F32)


def _masked_softmax(s, mask):
    s = jnp.where(mask, s, NEG_BIG)
    m = jnp.max(s, axis=-1, keepdims=True)
    e = jnp.where(mask, jnp.exp(s - m), 0.0)
    return e / jnp.maximum(jnp.sum(e, axis=-1, keepdims=True), 1e-30)


def _norm_rope(x, g, cos_t, sin_t):
    ms = jnp.mean(x * x, axis=-1, keepdims=True)
    y = x * lax.rsqrt(ms + NORM_EPS) * g
    lane = lax.broadcasted_iota(jnp.int32, y.shape, 1)
    swapped = jnp.where(lane < ROPE_HALF, pltpu.roll(y, HEAD_DIM - ROPE_HALF, 1), pltpu.roll(y, ROPE_HALF, 1))
    return y * cos_t + swapped * sin_t


def _rope_tables(pos):
    inv = ROPE_THETA ** (-jnp.arange(ROPE_HALF, dtype=F32) * 2.0 / ROPE_DIM)
    ang = pos.astype(F32)[:, None] * inv
    cos = jnp.cos(ang)
    sin = jnp.sin(ang)
    n = pos.shape[0]
    ones = jnp.ones((n, HEAD_DIM - ROPE_DIM), F32)
    zeros = jnp.zeros((n, HEAD_DIM - ROPE_DIM), F32)
    return (jnp.concatenate([cos, cos, ones], axis=1), jnp.concatenate([-sin, sin, zeros], axis=1))


def _overlap_matrix(n_rows, n_sel, n_cols):
    n = np.arange(n_rows)[:, None] - 1
    j = np.arange(n_cols)[None]
    c0 = n * CMP_STRIDE
    s0 = j * SEL_BLOCK
    ov = np.minimum(c0 + CMP_BLOCK, s0 + SEL_BLOCK) - np.maximum(c0, s0)
    ov = np.maximum(ov, 0).astype(np.float32) / CMP_BLOCK
    ov = np.where((n >= 0) & (j < n_sel), ov, 0.0)
    return jnp.asarray(ov, F32)


def _rmsnorm(x, g, out_dtype, tm):
    M, D = x.shape

    def body(x_ref, g_ref, o_ref):
        xv = x_ref[...]
        ms = jnp.mean(xv * xv, axis=-1, keepdims=True)
        o_ref[...] = (xv * lax.rsqrt(ms + NORM_EPS) * g_ref[...]).astype(o_ref.dtype)

    return pl.pallas_call(
        body, grid=(M // tm,),
        in_specs=[pl.BlockSpec((tm, D), lambda i: (i, 0)), pl.BlockSpec((1, D), lambda i: (0, 0))],
        out_specs=pl.BlockSpec((tm, D), lambda i: (i, 0)),
        out_shape=jax.ShapeDtypeStruct((M, D), out_dtype),
        compiler_params=_cp(("parallel",)), name="rmsnorm")(x, g.reshape(1, D))


def _rms_rows(xv, g):
    ms = jnp.mean(xv * xv, axis=-1, keepdims=True)
    return xv * lax.rsqrt(ms + NORM_EPS) * g


def _weight_spec(w, tn):
    if isinstance(w, tuple):
        arr, layer = w
        return arr, pl.BlockSpec((None, arr.shape[1], tn), lambda i, j: (layer, 0, j))
    return w, pl.BlockSpec((w.shape[0], tn), lambda i, j: (0, j))


def _mm(a_list, w_list, pair, extras, epilogue, out_dtypes, tm, tn, name, norm=None):
    w_arrays, w_specs = zip(*[_weight_spec(w, tn) for w in w_list])
    N = w_arrays[0].shape[-1]
    na, nw, ne, no = len(a_list), len(w_list), len(extras), len(out_dtypes)
    M = a_list[0].shape[0]

    def body(*refs):
        if norm is not None:
            g_ref = refs[na + nw + ne]
            an_ref = refs[na + nw + ne + 1 + no]

            @pl.when(pl.program_id(1) == 0)
            def _():
                an_ref[...] = _rms_rows(refs[norm[0]][...], g_ref[...]).astype(BF16)

        a_vals = []
        for k, r in enumerate(refs[:na]):
            if norm is not None and k == norm[0]:
                a_vals.append(an_ref[...])
            else:
                a_vals.append(r[...].astype(BF16))
        accs = [jnp.dot(a_vals[pair[i]], refs[na + i][...], preferred_element_type=F32) for i in range(nw)]
        ex = [r[...] for r in refs[na + nw:na + nw + ne]]
        res = epilogue(accs, ex)
        first_out = na + nw + ne + (1 if norm is not None else 0)
        for o_ref, v in zip(refs[first_out:first_out + no], res):
            o_ref[...] = v.astype(o_ref.dtype)

    in_specs = [pl.BlockSpec((tm, a.shape[1]), lambda i, j: (i, 0)) for a in a_list]
    in_specs += list(w_specs)
    for arr, kind in extras:
        if kind == "tile":
            in_specs.append(pl.BlockSpec((tm, tn), lambda i, j: (i, j)))
        else:
            in_specs.append(pl.BlockSpec((arr.shape[0], tn), lambda i, j: (0, j)))
    operands = [*a_list, *w_arrays, *[e[0] for e in extras]]
    scratch = []
    if norm is not None:
        kn = a_list[norm[0]].shape[1]
        in_specs.append(pl.BlockSpec((1, kn), lambda i, j: (0, 0)))
        operands.append(norm[1].reshape(1, kn))
        scratch.append(pltpu.VMEM((tm, kn), BF16))
    return pl.pallas_call(
        body, grid=(M // tm, N // tn), in_specs=in_specs,
        out_specs=[pl.BlockSpec((tm, tn), lambda i, j: (i, j)) for _ in out_dtypes],
        out_shape=[jax.ShapeDtypeStruct((M, N), dt) for dt in out_dtypes],
        scratch_shapes=scratch,
        compiler_params=_cp(("parallel", "arbitrary" if norm is not None else "parallel")), name=name,
    )(*operands)


def _qk_prep(z, cos_t, sin_t, tab_blocks, qn, kns, knw, tm):
    M = z.shape[0]

    def body(zq_ref, zkv_ref, c_ref, s_ref, qn_ref, kns_ref, knw_ref, q_ref, kvb_ref, *row_refs):
        c = c_ref[...]
        s = s_ref[...]
        for h in range(N_HEADS):
            sl = slice(h * HEAD_DIM, (h + 1) * HEAD_DIM)
            q_ref[:, sl] = _norm_rope(zq_ref[:, sl], qn_ref[...], c, s).astype(BF16)
        for kv in range(N_KV):
            vals = [zkv_ref[:, (k * N_KV + kv) * HEAD_DIM:(k * N_KV + kv + 1) * HEAD_DIM] for k in range(6)]
            vals[2] = _norm_rope(vals[2], kns_ref[...], c, s)
            vals[4] = _norm_rope(vals[4], knw_ref[...], c, s)
            for k in range(6):
                row_refs[k][pl.ds(kv, tm, stride=N_KV), :] = vals[k]
            for k in range(2, 6):
                kvb_ref[:, ((k - 2) * N_KV + kv) * HEAD_DIM:((k - 2) * N_KV + kv + 1) * HEAD_DIM] = vals[k].astype(BF16)

    vec = pl.BlockSpec((1, HEAD_DIM), lambda i: (0, 0))
    tab = pl.BlockSpec((tm, HEAD_DIM), lambda i: (i % tab_blocks, 0))
    rows_spec = pl.BlockSpec((N_KV * tm, HEAD_DIM), lambda i: (i, 0))
    rows_shape = jax.ShapeDtypeStruct((N_KV * M, HEAD_DIM), F32)
    return pl.pallas_call(
        body, grid=(M // tm,),
        in_specs=[pl.BlockSpec((tm, Q_W), lambda i: (i, COL_Q // Q_W)),
                  pl.BlockSpec((tm, 6 * KV_W), lambda i: (i, COL_KV // (6 * KV_W))),
                  tab, tab, vec, vec, vec],
        out_specs=[pl.BlockSpec((tm, Q_W), lambda i: (i, 0)), pl.BlockSpec((tm, 4 * KV_W), lambda i: (i, 0))]
        + [rows_spec] * 6,
        out_shape=[jax.ShapeDtypeStruct((M, Q_W), BF16), jax.ShapeDtypeStruct((M, 4 * KV_W), BF16)] + [rows_shape] * 6,
        compiler_params=_cp(("parallel",)), name="qk_prep",
    )(z, z, cos_t, sin_t, qn.reshape(1, -1), kns.reshape(1, -1), knw.reshape(1, -1))


def _compress_bias(pe, w1):
    def body(pe_ref, w1_ref, o_ref):
        for half in range(2):
            acc = jnp.zeros((8, HEAD_DIM), F32)
            for c in range(half * CMP_STRIDE, (half + 1) * CMP_STRIDE):
                acc = acc + _dotf(jnp.broadcast_to(pe_ref[c:c + 1, :], (8, HEAD_DIM)), w1_ref[c])
            o_ref[:, half * HEAD_DIM:(half + 1) * HEAD_DIM] = acc[0:1, :]

    return pl.pallas_call(body, out_shape=jax.ShapeDtypeStruct((1, 2 * HEAD_DIM), F32), name="compress_bias")(pe, w1)


def _compress(pages, ptab, n_seq, pages_per_seq, pe, w1, w2, norm_g, cos_t, sin_t, do_norm, name):
    bias = _compress_bias(pe, w1)
    w1cat = jnp.concatenate([w1[:CMP_STRIDE], w1[CMP_STRIDE:]], axis=2).astype(BF16)
    G = 2 * PAGES_PER_STEP if pages_per_seq % (2 * PAGES_PER_STEP) == 0 else PAGES_PER_STEP
    steps = pages_per_seq // G
    cpp = PAGE_SIZE // CMP_STRIDE
    ch = G * cpp
    n_chunk = pages_per_seq * cpp

    def body(pt_ref, *refs):
        page_refs = refs[:G]
        bias_ref, w1_ref, w2_ref, g_ref, c_ref, s_ref, o_ref, carry_ref, split_ref = refs[G:]
        step = pl.program_id(1)

        @pl.when(step == 0)
        def _():
            carry_ref[...] = jnp.zeros_like(carry_ref)

        for g, pr in enumerate(page_refs):
            for j in range(CMP_SPLIT):
                split_ref[g, j] = pr[0, pl.ds(j, N_KV * PAGE_SIZE // CMP_SPLIT, stride=CMP_SPLIT), :]

        row = lax.broadcasted_iota(jnp.int32, (ch, HEAD_DIM), 0)
        hop = N_KV * CMP_STRIDE // CMP_SPLIT
        for kv in range(N_KV):
            acc = jnp.zeros((ch, 2 * HEAD_DIM), F32)
            for c in range(CMP_STRIDE):
                j = (N_KV * c + kv) % CMP_SPLIT
                start = (N_KV * c + kv) // CMP_SPLIT
                xc = jnp.concatenate(
                    [split_ref[g, j, pl.ds(start, cpp, stride=hop), :] for g in range(G)], axis=0)
                acc = acc + _dotb(xc, w1_ref[c])
            acc = acc + bias_ref[...]
            lo = acc[:, :HEAD_DIM]
            hi = acc[:, HEAD_DIM:]
            lo_prev = jnp.where(row == 0, carry_ref[kv, 0:1, :], pltpu.roll(lo, 1, 0))
            carry_ref[kv, 0:1, :] = lo[ch - 1:ch, :]
            out = _dotb(_gelu(lo_prev + hi), w2_ref[...])
            if do_norm:
                out = _norm_rope(out, g_ref[...], c_ref[...], s_ref[...])
            o_ref[0, kv] = out.astype(BF16)

    def page_map(j, s, g, pt):
        return (pt[s * pages_per_seq + g * G + j], 0, 0)

    full2 = lambda s, g, pt: (0, 0)
    in_specs = [pl.BlockSpec((1, N_KV * PAGE_SIZE, HEAD_DIM), functools.partial(page_map, j)) for j in range(G)]
    in_specs += [pl.BlockSpec((1, 2 * HEAD_DIM), full2),
                 pl.BlockSpec((CMP_STRIDE, HEAD_DIM, 2 * HEAD_DIM), lambda s, g, pt: (0, 0, 0)),
                 pl.BlockSpec((HEAD_DIM, HEAD_DIM), full2),
                 pl.BlockSpec((1, HEAD_DIM), full2),
                 pl.BlockSpec((ch, HEAD_DIM), lambda s, g, pt: (g, 0)),
                 pl.BlockSpec((ch, HEAD_DIM), lambda s, g, pt: (g, 0))]
    gs = pltpu.PrefetchScalarGridSpec(
        num_scalar_prefetch=1, grid=(n_seq, steps), in_specs=in_specs,
        out_specs=pl.BlockSpec((1, N_KV, ch, HEAD_DIM), lambda s, g, pt: (s, 0, g, 0)),
        scratch_shapes=[pltpu.VMEM((N_KV, 8, HEAD_DIM), F32),
                        pltpu.VMEM((G, CMP_SPLIT, N_KV * PAGE_SIZE // CMP_SPLIT, HEAD_DIM), F32)])
    return pl.pallas_call(
        body, grid_spec=gs, out_shape=jax.ShapeDtypeStruct((n_seq, N_KV, n_chunk, HEAD_DIM), BF16),
        compiler_params=_cp(("parallel", "arbitrary")), name=name,
    )(ptab, *([pages] * G), bias, w1cat, w2.astype(BF16), norm_g.reshape(1, -1), cos_t, sin_t)


def _nsa_prompt(q, kcmp, vcmp_t, kvb, vs_t, vw_t, gates_t, ovl_t, expand_t, B, T):
    nqb = T // Q_BLOCK
    n_cmp_rows = T // CMP_STRIDE
    n_sel = T // SEL_BLOCK
    cols = GROUP * Q_BLOCK
    win_tiles = WINDOW // Q_BLOCK
    win_keys = (win_tiles + 1) * Q_BLOCK
    assert T % SEL_KT == 0 and T >= win_keys
    topk = min(SEL_TOPK, n_sel)

    def body(q_ref, kc_ref, vc_ref, ks_ref, vs_ref, kw_ref, vw_ref, gt_ref, ovl_ref, ex_ref, o_ref):
        i = pl.program_id(2)
        q4 = q_ref[0].astype(F32)
        qt = jnp.concatenate([q4[:, h * HEAD_DIM:(h + 1) * HEAD_DIM].T for h in range(GROUP)], axis=1).astype(BF16)
        pos = i * Q_BLOCK + (lax.broadcasted_iota(jnp.int32, (1, cols), 1) & (Q_BLOCK - 1))

        s = jnp.dot(kc_ref[0, 0], qt, preferred_element_type=F32) * ATTN_SCALE
        crow = lax.broadcasted_iota(jnp.int32, (n_cmp_rows, cols), 0)
        cmask = (crow >= 1) & (crow * CMP_STRIDE + (CMP_BLOCK - 1 - CMP_STRIDE) <= pos)
        s = jnp.where(cmask, s, NEG_BIG)
        e = jnp.where(cmask, jnp.exp(s - jnp.max(s, axis=0, keepdims=True)), 0.0)
        p = e / jnp.maximum(jnp.sum(e, axis=0, keepdims=True), 1e-30)
        o_cmp = _dotb(vc_ref[0, 0], p)
        imp = p[:, 0:Q_BLOCK]
        for h in range(1, GROUP):
            imp = imp + p[:, h * Q_BLOCK:(h + 1) * Q_BLOCK]

        score = _dotf(ovl_ref[...], imp)
        j = lax.broadcasted_iota(jnp.int32, (n_sel, Q_BLOCK), 0)
        qblk = (i * Q_BLOCK + lax.broadcasted_iota(jnp.int32, (n_sel, Q_BLOCK), 1)) // SEL_BLOCK
        forced = (j == 0) | ((j <= qblk) & (j > qblk - N_LOCAL))
        score = jnp.where(forced, FORCED_SCORE, jnp.where(j > qblk, NEG_BIG, score))
        rank = jnp.zeros((n_sel, Q_BLOCK), F32)
        for r in range(n_sel):
            sr = score[r:r + 1, :]
            beats = (sr > score) | ((sr == score) & (j > r))
            rank = rank + jnp.where(beats, 1.0, 0.0)
        sel_bias = jnp.where(rank < topk, 0.0, NEG_BIG).astype(BF16)

        krow = lax.broadcasted_iota(jnp.int32, (SEL_KT, cols), 0)

        def sel_tile(u, carry, causal):
            m, l, acc = carry
            k = ks_ref[0, pl.ds(pl.multiple_of(u * SEL_KT, SEL_KT), SEL_KT), :]
            b1 = _dotb(ex_ref[u], sel_bias)
            sc = (jnp.dot(k, qt, preferred_element_type=F32) * ATTN_SCALE
                  + jnp.concatenate([b1] * GROUP, axis=1))
            if causal:
                sc = jnp.where(u * SEL_KT + krow <= pos, sc, NEG_BIG)
            m_new = jnp.maximum(m, jnp.max(sc, axis=0, keepdims=True))
            alpha = jnp.exp(m - m_new)
            ex = jnp.exp(sc - m_new)
            l = alpha * l + jnp.sum(ex, axis=0, keepdims=True)
            acc = alpha * acc + _dotb(vs_ref[0, 0, u], ex)
            return m_new, l, acc

        init = (jnp.full((1, cols), NEG_BIG, F32), jnp.zeros((1, cols), F32), jnp.zeros((HEAD_DIM, cols), F32))
        last = (i * Q_BLOCK + Q_BLOCK + SEL_KT - 1) // SEL_KT - 1
        carry = lax.fori_loop(0, last, functools.partial(sel_tile, causal=False), init)
        _, l_sel, acc_sel = sel_tile(last, carry, causal=True)
        o_sel = acc_sel / jnp.maximum(l_sel, 1e-30)

        t0 = jnp.maximum(i - win_tiles, 0)
        kwin = kw_ref[0, pl.ds(pl.multiple_of(t0 * Q_BLOCK, Q_BLOCK), win_keys), :]
        sw = jnp.dot(kwin, qt, preferred_element_type=F32) * ATTN_SCALE
        kpos = t0 * Q_BLOCK + lax.broadcasted_iota(jnp.int32, (win_keys, cols), 0)
        wmask = (kpos <= pos) & (kpos > pos - WINDOW)
        sw = jnp.where(wmask, sw, NEG_BIG)
        ew = jnp.exp(sw - jnp.max(sw, axis=0, keepdims=True))
        pw = ew / jnp.maximum(jnp.sum(ew, axis=0, keepdims=True), 1e-30)
        o_win = jnp.zeros((HEAD_DIM, cols), F32)
        for t in range(win_tiles + 1):
            o_win = o_win + _dotb(vw_ref[0, 0, t0 + t], pw[t * Q_BLOCK:(t + 1) * Q_BLOCK])

        g = _sigmoid(gt_ref[0, 0])
        for h in range(GROUP):
            c = slice(h * Q_BLOCK, (h + 1) * Q_BLOCK)
            o = (g[h:h + 1, :] * o_cmp[:, c] + g[GROUP + h:GROUP + h + 1, :] * o_sel[:, c]
                 + g[2 * GROUP + h:2 * GROUP + h + 1, :] * o_win[:, c])
            o_ref[0, :, h * HEAD_DIM:(h + 1) * HEAD_DIM] = o.T.astype(BF16)

    def k_spec(off):
        return pl.BlockSpec((1, T, HEAD_DIM), lambda b, kv, i: (b, 0, off + kv))

    vs_spec = pl.BlockSpec((1, 1, T // SEL_KT, HEAD_DIM, SEL_KT), lambda b, kv, i: (b, kv, 0, 0, 0))
    vw_spec = pl.BlockSpec((1, 1, nqb, HEAD_DIM, Q_BLOCK), lambda b, kv, i: (b, kv, 0, 0, 0))
    qo_spec = pl.BlockSpec((1, Q_BLOCK, GROUP * HEAD_DIM), lambda b, kv, i: (b, i, kv))
    return pl.pallas_call(
        body, grid=(B, N_KV, nqb),
        in_specs=[qo_spec,
                  pl.BlockSpec((1, 1, n_cmp_rows, HEAD_DIM), lambda b, kv, i: (b, kv, 0, 0)),
                  pl.BlockSpec((1, 1, HEAD_DIM, n_cmp_rows), lambda b, kv, i: (b, kv, 0, 0)),
                  k_spec(0), vs_spec, k_spec(4), vw_spec,
                  pl.BlockSpec((1, 1, 3 * GROUP, Q_BLOCK), lambda b, kv, i: (b, kv, 0, i)),
                  pl.BlockSpec((n_sel, n_cmp_rows), lambda b, kv, i: (0, 0)),
                  pl.BlockSpec((T // SEL_KT, SEL_KT, n_sel), lambda b, kv, i: (0, 0, 0))],
        out_specs=qo_spec,
        out_shape=jax.ShapeDtypeStruct((B, T, Q_W), BF16),
        compiler_params=_cp(("parallel", "parallel", "arbitrary")), name="nsa_prompt",
    )(q, kcmp, vcmp_t, kvb, vs_t, kvb, vw_t, gates_t, ovl_t, expand_t)


def _nsa_sample(q, kcmp, vcmp, pool_k, pool_v, ptab, kvnew, win_k, win_v, win_off, gates, ovl, expand, DB, past_len):
    G = PAGES_PER_STEP
    n_pages = past_len // PAGE_SIZE
    steps = n_pages // G
    n_cmp_rows = kcmp.shape[2]
    sel_cols = ovl.shape[1]
    n_past_blk = past_len // SEL_BLOCK
    n_sel = -(-(past_len + DEC_SEQ) // SEL_BLOCK)
    w_buf = win_k.shape[1] // N_KV
    rows = GROUP * Q_PAD
    keys = G * PAGE_SIZE
    new_rows = kvnew.shape[1]

    def body(pt_ref, q_ref, kc_ref, vc_ref, ovl_ref, *rest):
        pk = rest[:G]
        pv = rest[G:2 * G]
        (ex_ref, kvn_ref, wk_ref, wv_ref, gt_ref, o_ref,
         sel_scr, m_scr, l_scr, acc_scr, base_scr) = rest[2 * G:]
        step = pl.program_id(1)
        tok = lax.broadcasted_iota(jnp.int32, (rows, 1), 0) & (Q_PAD - 1)
        q8 = q_ref[0]

        def q_rows(kv):
            return jnp.concatenate(
                [q8[:, (kv * GROUP + h) * HEAD_DIM:(kv * GROUP + h + 1) * HEAD_DIM] for h in range(GROUP)], axis=0)

        def head_col(gt, c0):
            return jnp.concatenate(
                [jnp.broadcast_to(gt[:, c0 + h:c0 + h + 1], (Q_PAD, HEAD_DIM)) for h in range(GROUP)], axis=0)

        @pl.when(step == 0)
        def _():
            o_cmps = []
            imps = []
            for kv in range(N_KV):
                s = _dot_nt(q_rows(kv), kc_ref[0, kv]) * ATTN_SCALE
                col = lax.broadcasted_iota(jnp.int32, (rows, n_cmp_rows), 1)
                cmask = (col >= 1) & (col * CMP_STRIDE + (CMP_BLOCK - 1 - CMP_STRIDE) <= past_len + tok)
                p = _masked_softmax(s, cmask)
                o_cmps.append(_dotb(p, vc_ref[0, kv]))
                imp = p[0:Q_PAD]
                for h in range(1, GROUP):
                    imp = imp + p[h * Q_PAD:(h + 1) * Q_PAD]
                imps.append(imp)

            score = _dotf(jnp.concatenate(imps, axis=0), ovl_ref[...])
            j = lax.broadcasted_iota(jnp.int32, (N_KV * Q_PAD, sel_cols), 1)
            trow = lax.broadcasted_iota(jnp.int32, (N_KV * Q_PAD, sel_cols), 0) & (Q_PAD - 1)
            qblk = (past_len + trow) // SEL_BLOCK
            forced = (j == 0) | ((j <= qblk) & (j > qblk - N_LOCAL))
            score = jnp.where(forced, FORCED_SCORE, jnp.where(j > qblk, NEG_BIG, score))
            score = jnp.where(j < n_sel, score, -jnp.inf)
            rank = jnp.zeros((N_KV * Q_PAD, sel_cols), F32)
            for r in range(n_sel):
                sr = score[:, r:r + 1]
                rank = rank + jnp.where((sr > score) | ((sr == score) & (j > r)), 1.0, 0.0)
            sel_all = jnp.where((rank < min(SEL_TOPK, n_sel)) & (j < n_sel), 1.0, 0.0)

            for kv in range(N_KV):
                qh = q_rows(kv)
                o_cmp = o_cmps[kv]
                sel4 = jnp.concatenate([sel_all[kv * Q_PAD:(kv + 1) * Q_PAD]] * GROUP, axis=0)
                sel_scr[kv] = sel4.astype(BF16)

                kb = wk_ref[0, pl.ds(kv, w_buf, stride=N_KV), :].astype(BF16)
                vb = wv_ref[0, pl.ds(kv, w_buf, stride=N_KV), :].astype(BF16)
                kn = kvn_ref[0, :, (4 + kv) * HEAD_DIM:(5 + kv) * HEAD_DIM]
                vn = kvn_ref[0, :, (6 + kv) * HEAD_DIM:(7 + kv) * HEAD_DIM]
                s1 = jnp.where(lax.broadcasted_iota(jnp.int32, (rows, w_buf), 1) - w_buf > tok - WINDOW,
                               _dot_nt(qh, kb) * ATTN_SCALE, NEG_BIG)
                ncol = lax.broadcasted_iota(jnp.int32, (rows, new_rows), 1)
                nmask = (ncol <= tok) & (ncol < DEC_SEQ)
                s2 = jnp.where(nmask, _dot_nt(qh, kn) * ATTN_SCALE, NEG_BIG)
                m = jnp.maximum(jnp.max(s1, axis=-1, keepdims=True), jnp.max(s2, axis=-1, keepdims=True))
                e1 = jnp.where(s1 > 0.5 * NEG_BIG, jnp.exp(s1 - m), 0.0)
                e2 = jnp.where(nmask, jnp.exp(s2 - m), 0.0)
                den = jnp.sum(e1, axis=-1, keepdims=True) + jnp.sum(e2, axis=-1, keepdims=True)
                o_win = (_dotb(e1, vb) + _dotb(e2, vn)) / jnp.maximum(den, 1e-30)

                gt = _sigmoid(gt_ref[0, kv])
                base_scr[kv] = head_col(gt, 0) * o_cmp + head_col(gt, 2 * GROUP) * o_win

                ksn = kvn_ref[0, :, kv * HEAD_DIM:(kv + 1) * HEAD_DIM]
                vsn = kvn_ref[0, :, (2 + kv) * HEAD_DIM:(3 + kv) * HEAD_DIM]
                chosen = sel4[:, n_past_blk:n_past_blk + 1] > 0.5
                smask = nmask & chosen
                s3 = jnp.where(smask, _dot_nt(qh, ksn) * ATTN_SCALE, NEG_BIG)
                m0 = jnp.max(s3, axis=-1, keepdims=True)
                e3 = jnp.where(smask, jnp.exp(s3 - m0), 0.0)
                m_scr[kv] = jnp.broadcast_to(m0, (rows, HEAD_DIM))
                l_scr[kv] = jnp.broadcast_to(jnp.sum(e3, axis=-1, keepdims=True), (rows, HEAD_DIM))
                acc_scr[kv] = _dotb(e3, vsn)

        for kv in range(N_KV):
            qh = q_rows(kv)
            k = jnp.concatenate([r[0, pl.ds(kv, PAGE_SIZE, stride=N_KV), :] for r in pk], axis=0).astype(BF16)
            v = jnp.concatenate([r[0, pl.ds(kv, PAGE_SIZE, stride=N_KV), :] for r in pv], axis=0).astype(BF16)
            mask = _dotb(sel_scr[kv], ex_ref[0]) > 0.5
            sc = jnp.where(mask, _dot_nt(qh, k) * ATTN_SCALE, NEG_BIG)
            m_old = m_scr[kv][:, 0:1]
            l_old = l_scr[kv][:, 0:1]
            m_new = jnp.maximum(m_old, jnp.max(sc, axis=-1, keepdims=True))
            alpha = jnp.exp(m_old - m_new)
            e = jnp.where(mask, jnp.exp(sc - m_new), 0.0)
            l_new = alpha * l_old + jnp.sum(e, axis=-1, keepdims=True)
            acc = alpha * acc_scr[kv] + _dotb(e, v)
            m_scr[kv] = jnp.broadcast_to(m_new, (rows, HEAD_DIM))
            l_scr[kv] = jnp.broadcast_to(l_new, (rows, HEAD_DIM))
            acc_scr[kv] = acc

        @pl.when(step == steps - 1)
        def _():
            for kv in range(N_KV):
                gt = _sigmoid(gt_ref[0, kv])
                o_sel = acc_scr[kv] / jnp.maximum(l_scr[kv], 1e-30)
                o = base_scr[kv] + head_col(gt, GROUP) * o_sel
                for h in range(GROUP):
                    hh = kv * GROUP + h
                    o_ref[0, :, hh * HEAD_DIM:(hh + 1) * HEAD_DIM] = o[h * Q_PAD:(h + 1) * Q_PAD].astype(BF16)

    def page_map(jj, b, g, pt):
        return (pt[b * n_pages + g * G + jj], 0, 0)

    page_specs = [pl.BlockSpec((1, N_KV * PAGE_SIZE, HEAD_DIM), functools.partial(page_map, jj)) for jj in range(G)]
    cmp_spec = pl.BlockSpec((1, N_KV, n_cmp_rows, HEAD_DIM), lambda b, g, pt: (b, 0, 0, 0))
    in_specs = [pl.BlockSpec((1, Q_PAD, Q_W), lambda b, g, pt: (b, 0, 0)), cmp_spec, cmp_spec,
                pl.BlockSpec((n_cmp_rows, sel_cols), lambda b, g, pt: (0, 0))]
    in_specs += page_specs + page_specs
    in_specs += [pl.BlockSpec((1, sel_cols, keys), lambda b, g, pt: (g, 0, 0)),
                 pl.BlockSpec((1, new_rows, 4 * KV_W), lambda b, g, pt: (b, 0, 0)),
                 pl.BlockSpec((1, N_KV * w_buf, HEAD_DIM), lambda b, g, pt: (win_off + b, 0, 0)),
                 pl.BlockSpec((1, N_KV * w_buf, HEAD_DIM), lambda b, g, pt: (win_off + b, 0, 0)),
                 pl.BlockSpec((1, N_KV, Q_PAD, 3 * GROUP), lambda b, g, pt: (b, 0, 0, 0))]
    gs = pltpu.PrefetchScalarGridSpec(
        num_scalar_prefetch=1, grid=(DB, steps), in_specs=in_specs,
        out_specs=pl.BlockSpec((1, Q_PAD, Q_W), lambda b, g, pt: (b, 0, 0)),
        scratch_shapes=[pltpu.VMEM((N_KV, rows, sel_cols), BF16), pltpu.VMEM((N_KV, rows, HEAD_DIM), F32),
                        pltpu.VMEM((N_KV, rows, HEAD_DIM), F32), pltpu.VMEM((N_KV, rows, HEAD_DIM), F32),
                        pltpu.VMEM((N_KV, rows, HEAD_DIM), F32)])
    return pl.pallas_call(
        body, grid_spec=gs, out_shape=jax.ShapeDtypeStruct((DB, Q_PAD, Q_W), BF16),
        compiler_params=_cp(("parallel", "arbitrary")), name="nsa_sample",
    )(ptab, q, kcmp, vcmp, ovl, *([pool_k] * G), *([pool_v] * G), expand, kvnew, win_k, win_v, gates)


def _rg_gates(u, wa_ref, ba, wx_ref, bx, sp):
    ra = []
    rx = []
    for n in range(RG_BLOCKS):
        ub = u[:, n * RG_BW:(n + 1) * RG_BW]
        ra.append(_dotf(ub, wa_ref[n]))
        rx.append(_dotf(ub, wx_ref[n]))
    r = _sigmoid(jnp.concatenate(ra, axis=1) + ba)
    i = _sigmoid(jnp.concatenate(rx, axis=1) + bx)
    log_a = -RG_C * r * sp
    a = jnp.exp(log_a)
    b = jnp.sqrt(1.0 - jnp.exp(2.0 * log_a)) * i * u
    return a, b


def _softplus_neg(lam):
    return jnp.maximum(-lam, 0.0) + jnp.log(1.0 + jnp.exp(-jnp.abs(lam)))


def _rglru_prompt(z, cw, cb, wa, ba, wx, bx, lam, B, T, tt):
    keep = 8

    def body(xr_ref, gr_ref, cw_ref, cb_ref, wa_ref, ba_ref, wx_ref, bx_ref, lam_ref,
             y_ref, hl_ref, a_scr, b_scr, h_scr, tail_scr):
        t = pl.program_id(1)

        @pl.when(t == 0)
        def _():
            h_scr[...] = jnp.zeros_like(h_scr)
            tail_scr[...] = jnp.zeros_like(tail_scr)

        x = xr_ref[0]
        row = lax.broadcasted_iota(jnp.int32, (tt, D_RNN), 0)
        cwv = cw_ref[...]
        u = cb_ref[...] + x * cwv[RG_CONV - 1:RG_CONV]
        for d in range(1, RG_CONV):
            xs = pltpu.roll(x, d, 0)
            for r in range(d):
                xs = jnp.where(row == r, tail_scr[keep - d + r:keep - d + r + 1, :], xs)
            u = u + xs * cwv[RG_CONV - 1 - d:RG_CONV - d]
        tail_scr[...] = x[tt - keep:tt]
        a, b = _rg_gates(u, wa_ref, ba_ref[...], wx_ref, bx_ref[...], _softplus_neg(lam_ref[...]))
        a_scr[...] = a
        b_scr[...] = b

        def step(s, h):
            h = a_scr[pl.ds(s, 1), :] * h + b_scr[pl.ds(s, 1), :]
            b_scr[pl.ds(s, 1), :] = h
            return h

        h = lax.fori_loop(0, tt, step, h_scr[0:1, :], unroll=8)
        h_scr[0:1, :] = h
        hl_ref[0] = h
        y_ref[0] = (b_scr[...] * _gelu(gr_ref[0])).astype(BF16)

    vec = pl.BlockSpec((1, D_RNN), lambda b, t: (0, 0))
    wspec = pl.BlockSpec((RG_BLOCKS, RG_BW, RG_BW), lambda b, t: (0, 0, 0))
    return pl.pallas_call(
        body, grid=(B, T // tt),
        in_specs=[pl.BlockSpec((1, tt, D_RNN), lambda b, t: (b, t, COL_XR // D_RNN)),
                  pl.BlockSpec((1, tt, D_RNN), lambda b, t: (b, t, COL_GR // D_RNN)),
                  pl.BlockSpec((RG_CONV, D_RNN), lambda b, t: (0, 0)), vec, wspec, vec, wspec, vec, vec],
        out_specs=[pl.BlockSpec((1, tt, D_RNN), lambda b, t: (b, t, 0)),
                   pl.BlockSpec((1, 1, D_RNN), lambda b, t: (b, 0, 0))],
        out_shape=[jax.ShapeDtypeStruct((B, T, D_RNN), BF16), jax.ShapeDtypeStruct((B, 1, D_RNN), F32)],
        scratch_shapes=[pltpu.VMEM((tt, D_RNN), F32), pltpu.VMEM((tt, D_RNN), F32),
                        pltpu.VMEM((8, D_RNN), F32), pltpu.VMEM((keep, D_RNN), F32)],
        compiler_params=_cp(("parallel", "arbitrary")), name="rglru_prompt",
    )(z, z, cw, cb.reshape(1, -1), wa, ba.reshape(1, -1), wx, bx.reshape(1, -1), lam.reshape(1, -1))


def _rglru_sample(xr, gr, buf, h0, cw, cb, wa, ba, wx, bx, lam):
    DB = h0.shape[0]

    def body(xr_ref, gr_ref, buf_ref, h0_ref, cw_ref, cb_ref, wa_ref, ba_ref, wx_ref, bx_ref, lam_ref, y_ref, hl_ref):
        xp = [buf_ref[k] for k in range(RG_CONV - 1)] + [xr_ref[t] for t in range(DEC_SEQ)]
        cwv = cw_ref[...]
        sp = _softplus_neg(lam_ref[...])
        h = h0_ref[...]
        for t in range(DEC_SEQ):
            u = cb_ref[...]
            for k in range(RG_CONV):
                u = u + xp[t + k] * cwv[k:k + 1]
            a, b = _rg_gates(u, wa_ref, ba_ref[...], wx_ref, bx_ref[...], sp)
            h = a * h + b
            y_ref[t] = (h * _gelu(gr_ref[t])).astype(BF16)
        hl_ref[...] = h

    return pl.pallas_call(
        body, out_shape=[jax.ShapeDtypeStruct((DEC_SEQ, DB, D_RNN), BF16), jax.ShapeDtypeStruct((DB, D_RNN), F32)],
        compiler_params=pltpu.CompilerParams(vmem_limit_bytes=32 * 1024 * 1024), name="rglru_sample",
    )(xr, gr, buf, h0, cw, cb.reshape(1, -1), wa, ba.reshape(1, -1), wx, bx.reshape(1, -1), lam.reshape(1, -1))


def _s5_discretize(lam_re, lam_im, log_dt, b_re_t, b_im_t):
    def body(lr_ref, li_ref, ldt_ref, br_ref, bi_ref, ar_ref, ai_ref, bbr_ref, bbi_ref):
        lr = lr_ref[...]
        li = li_ref[...]
        dt = jnp.exp(ldt_ref[...])
        mag = jnp.exp(lr * dt)
        ar = mag * jnp.cos(li * dt)
        ai = mag * jnp.sin(li * dt)
        den = lr * lr + li * li
        n_re = ar - 1.0
        f_re = (n_re * lr + ai * li) / den
        f_im = (ai * lr - n_re * li) / den
        ar_ref[...] = ar
        ai_ref[...] = ai
        for c in range(S5_CH):
            bbr_ref[c] = f_re * br_ref[c] - f_im * bi_ref[c]
            bbi_ref[c] = f_re * bi_ref[c] + f_im * br_ref[c]

    gp = jax.ShapeDtypeStruct(lam_re.shape, F32)
    cgp = jax.ShapeDtypeStruct(b_re_t.shape, F32)
    return pl.pallas_call(body, out_shape=[gp, gp, cgp, cgp], name="s5_discretize")(
        lam_re, lam_im, log_dt.reshape(-1, 1), b_re_t, b_im_t)


def _s5_block_weights(bbr, bbi, c_re, c_im):
    eye = jnp.eye(S5_GB, dtype=F32)

    def bblk(bb):
        x = bb.reshape(S5_CH, S5_NGB, S5_GB, S5_STATE)
        x = jnp.einsum("cngp,gh->ngchp", x, eye)
        return x.reshape(S5_NGB, S5_GB * S5_CH, S5_LANES)

    def cblk(c):
        x = c.reshape(S5_NGB, S5_GB, S5_CH, S5_STATE)
        x = jnp.einsum("ngcp,gh->ngphc", x, eye)
        return x.reshape(S5_NGB, S5_LANES, S5_GB * S5_CH)

    return jnp.concatenate([bblk(bbr), bblk(bbi)], axis=2), cblk(c_re), cblk(c_im)


def _s5_prompt(u, bw, cr, ci, ar, ai, d, B, T):
    L = S5_LANES
    lanes = S5_GB * S5_CH
    nseg = S5_NSEG
    sl = T // nseg
    assert sl & (sl - 1) == 0, "segment length must be a power of two"
    chs = min(64, sl)
    ch = chs * nseg
    nchunks = sl // chs

    def body(u_ref, bw_ref, cr_ref, ci_ref, ar_ref, ai_ref, d_ref, zz_ref, sre_ref, sim_ref, h_scr, t_scr):
        a_re = ar_ref[0]
        a_im = ai_ref[0]
        ar8 = jnp.broadcast_to(a_re, (nseg, L))
        ai8 = jnp.broadcast_to(a_im, (nseg, L))

        def load_u(s0):
            for seg in range(nseg):
                t_scr[pl.ds(seg, chs, stride=nseg), :] = u_ref[0, seg, pl.ds(s0, chs), :]
            return t_scr[...]

        def project(c, carry):
            s0 = pl.multiple_of(c * chs, chs)
            h_scr[pl.ds(pl.multiple_of(c * ch, ch), ch), :] = _dotb(load_u(s0), bw_ref[0])
            return carry

        lax.fori_loop(0, nchunks, project, 0)

        def advance(s, carry, store):
            hr, hi = carry
            rows = pl.ds(pl.multiple_of(s * nseg, nseg), nseg)
            bu = h_scr[rows, :]
            nr = ar8 * hr - ai8 * hi + bu[:, :L]
            ni = ar8 * hi + ai8 * hr + bu[:, L:]
            if store:
                h_scr[rows, :] = jnp.concatenate([nr, ni], axis=1)
            return nr, ni

        zero = jnp.zeros((nseg, L), F32)
        er, ei = lax.fori_loop(0, sl, functools.partial(advance, store=False), (zero, zero), unroll=8)

        pr, pi = a_re, a_im
        for _ in range(sl.bit_length() - 1):
            pr, pi = pr * pr - pi * pi, 2.0 * pr * pi
        sr = [jnp.zeros((1, L), F32)]
        si = [jnp.zeros((1, L), F32)]
        for k in range(nseg):
            sr.append(er[k:k + 1] + pr * sr[k] - pi * si[k])
            si.append(ei[k:k + 1] + pr * si[k] + pi * sr[k])
        sre_ref[0, 0] = sr[nseg]
        sim_ref[0, 0] = si[nseg]
        start = (jnp.concatenate(sr[:nseg], axis=0), jnp.concatenate(si[:nseg], axis=0))
        lax.fori_loop(0, sl, functools.partial(advance, store=True), start, unroll=8)

        def emit(c, carry):
            s0 = pl.multiple_of(c * chs, chs)
            hc = h_scr[pl.ds(pl.multiple_of(c * ch, ch), ch), :]
            uc = load_u(s0)
            y = _dotb(hc[:, :L], cr_ref[0]) - _dotb(hc[:, L:], ci_ref[0]) + d_ref[...] * uc
            t_scr[...] = _gelu(y)
            for seg in range(nseg):
                zz_ref[0, seg, pl.ds(s0, chs), :] = t_scr[pl.ds(seg, chs, stride=nseg), :].astype(BF16)
            return carry

        lax.fori_loop(0, nchunks, emit, 0)

    st_spec = pl.BlockSpec((1, 1, 1, L), lambda b, g: (b, g, 0, 0))
    st_shape = jax.ShapeDtypeStruct((B, S5_NGB, 1, L), F32)
    io_spec = pl.BlockSpec((1, nseg, sl, lanes), lambda b, g: (b, 0, 0, g))
    return pl.pallas_call(
        body, grid=(B, S5_NGB),
        in_specs=[io_spec,
                  pl.BlockSpec((1, lanes, 2 * L), lambda b, g: (g, 0, 0)),
                  pl.BlockSpec((1, L, lanes), lambda b, g: (g, 0, 0)),
                  pl.BlockSpec((1, L, lanes), lambda b, g: (g, 0, 0)),
                  pl.BlockSpec((1, 1, L), lambda b, g: (g, 0, 0)),
                  pl.BlockSpec((1, 1, L), lambda b, g: (g, 0, 0)),
                  pl.BlockSpec((1, lanes), lambda b, g: (0, g))],
        out_specs=[io_spec, st_spec, st_spec],
        out_shape=[jax.ShapeDtypeStruct((B, nseg, sl, D_MODEL), BF16), st_shape, st_shape],
        scratch_shapes=[pltpu.VMEM((T, 2 * L), F32), pltpu.VMEM((ch, lanes), F32)],
        compiler_params=_cp(("parallel", "parallel")), name="s5_prompt",
    )(u, bw, cr, ci, ar, ai, d.reshape(1, -1))


def _s5_sample(u, h0r, h0i, bw, cr, ci, ar, ai, d):
    DB = u.shape[1]
    L = S5_LANES
    lanes = S5_GB * S5_CH

    def body(u_ref, hr_ref, hi_ref, bw_ref, cr_ref, ci_ref, ar_ref, ai_ref, d_ref, zz_ref, sre_ref, sim_ref):
        hr = hr_ref[...]
        hi = hi_ref[...]
        a_re = ar_ref[0]
        a_im = ai_ref[0]
        for t in range(DEC_SEQ):
            uv = u_ref[t]
            bu = _dotf(uv, bw_ref[0])
            hr, hi = a_re * hr - a_im * hi + bu[:, :L], a_re * hi + a_im * hr + bu[:, L:]
            y = _dotf(hr, cr_ref[0]) - _dotf(hi, ci_ref[0]) + d_ref[...] * uv
            zz_ref[t] = _gelu(y).astype(BF16)
        sre_ref[...] = hr
        sim_ref[...] = hi

    st_spec = pl.BlockSpec((DB, L), lambda g: (0, g))
    st_shape = jax.ShapeDtypeStruct((DB, S5_GROUPS * S5_STATE), F32)
    return pl.pallas_call(
        body, grid=(S5_NGB,),
        in_specs=[pl.BlockSpec((DEC_SEQ, DB, lanes), lambda g: (0, 0, g)), st_spec, st_spec,
                  pl.BlockSpec((1, lanes, 2 * L), lambda g: (g, 0, 0)),
                  pl.BlockSpec((1, L, lanes), lambda g: (g, 0, 0)),
                  pl.BlockSpec((1, L, lanes), lambda g: (g, 0, 0)),
                  pl.BlockSpec((1, 1, L), lambda g: (g, 0, 0)),
                  pl.BlockSpec((1, 1, L), lambda g: (g, 0, 0)),
                  pl.BlockSpec((1, lanes), lambda g: (0, g))],
        out_specs=[pl.BlockSpec((DEC_SEQ, DB, lanes), lambda g: (0, 0, g)), st_spec, st_spec],
        out_shape=[jax.ShapeDtypeStruct((DEC_SEQ, DB, D_MODEL), BF16), st_shape, st_shape],
        compiler_params=_cp(("parallel",)), name="s5_sample",
    )(u, h0r, h0i, bw, cr, ci, ar, ai, d.reshape(1, -1))


def _ffn_upgate_prompt(x, g, w_up, w_gate, cw, cb, T, tm, tn):
    M, K = x.shape
    (w_up, wu_spec), (w_gate, wg_spec) = _weight_spec(w_up, tn), _weight_spec(w_gate, tn)
    N = w_up.shape[-1]
    tiles_per_seq = T // tm
    pr = 16

    def body(x_ref, xp_ref, g_ref, wu_ref, wg_ref, cw_ref, cb_ref, act_ref, tail_ref, a_scr):
        i = pl.program_id(0)

        @pl.when(pl.program_id(1) == 0)
        def _():
            a_scr[...] = _rms_rows(x_ref[...], g_ref[...]).astype(BF16)

        a = a_scr[...]
        hu = jnp.dot(a, wu_ref[...], preferred_element_type=F32)
        hg = jnp.dot(a, wg_ref[...], preferred_element_type=F32)
        ap = _rms_rows(xp_ref[...], g_ref[...]).astype(BF16)
        prev = jnp.dot(ap, wu_ref[...], preferred_element_type=F32)
        prev = jnp.where(i % tiles_per_seq == 0, 0.0, prev)
        row = lax.broadcasted_iota(jnp.int32, hu.shape, 0)
        h1 = jnp.where(row == 0, prev[pr - 1:pr], pltpu.roll(hu, 1, 0))
        h2 = jnp.where(row == 0, prev[pr - 2:pr - 1], jnp.where(row == 1, prev[pr - 1:pr], pltpu.roll(hu, 2, 0)))
        cwv = cw_ref[...]
        hc = cb_ref[...] + h2 * cwv[0:1] + h1 * cwv[1:2] + hu * cwv[2:3]
        act_ref[...] = (_gelu(hc) * hg).astype(BF16)
        tail_ref[...] = hu[tm - 8:tm]

    return pl.pallas_call(
        body, grid=(M // tm, N // tn),
        in_specs=[pl.BlockSpec((tm, K), lambda i, j: (i, 0)),
                  pl.BlockSpec((pr, K), lambda i, j: (jnp.maximum(i * (tm // pr) - 1, 0), 0)),
                  pl.BlockSpec((1, K), lambda i, j: (0, 0)),
                  wu_spec, wg_spec,
                  pl.BlockSpec((FFN_CONV, tn), lambda i, j: (0, j)),
                  pl.BlockSpec((1, tn), lambda i, j: (0, j))],
        out_specs=[pl.BlockSpec((tm, tn), lambda i, j: (i, j)), pl.BlockSpec((8, tn), lambda i, j: (i, j))],
        out_shape=[jax.ShapeDtypeStruct((M, N), BF16), jax.ShapeDtypeStruct((M // tm * 8, N), F32)],
        scratch_shapes=[pltpu.VMEM((tm, K), BF16)],
        compiler_params=_cp(("parallel", "arbitrary")), name="ffn_upgate_prompt",
    )(x, x, g.reshape(1, K), w_up, w_gate, cw, cb.reshape(1, -1))


def _tile(m, pref):
    return pref if m % pref == 0 else m


def _dense_tail(x1, p, wts, ffn_state, T):
    (norm_ffn, norm_ple, w_up, w_gate, conv_w, conv_b, w_down, w_proj, w_pgate) = wts
    M = x1.shape[0]
    tm = _tile(M, 1024)
    tn = 512
    prompt = ffn_state is None

    if prompt:
        act, tails = _ffn_upgate_prompt(x1, norm_ffn, w_up, w_gate, conv_w, conv_b, T, tm, tn)
        tails = tails.reshape(M // tm, 8, D_FF)
        tiles_per_seq = T // tm
        new_buf = tails[tiles_per_seq - 1::tiles_per_seq, 8 - (FFN_CONV - 1):, :]
    else:
        nseq = M // T
        h1 = jnp.repeat(ffn_state[:, FFN_CONV - 2], T, axis=0)
        h2 = jnp.stack([ffn_state[:, 0], ffn_state[:, 1]] + [ffn_state[:, 1]] * (T - 2), axis=1).reshape(M, D_FF)

        def conv_epilogue(accs, ex):
            hu, hg = accs
            b1, b2, cwv, cbv = ex
            tok = lax.broadcasted_iota(jnp.int32, hu.shape, 0) % T
            s1 = jnp.where(tok >= 1, pltpu.roll(hu, 1, 0), b1)
            s2 = jnp.where(tok >= 2, pltpu.roll(hu, 2, 0), b2)
            hc = cbv + s2 * cwv[0:1] + s1 * cwv[1:2] + hu * cwv[2:3]
            return _gelu(hc) * hg, hu

        act, hu = _mm([x1], [w_up, w_gate], [0, 0],
                      [(h1, "tile"), (h2, "tile"), (conv_w, "rows"), (conv_b.reshape(1, -1), "rows")],
                      conv_epilogue, [BF16, F32], tm, tn, "ffn_upgate_sample", norm=(0, norm_ffn))
        new_buf = hu.reshape(nseq, T, D_FF)[:, T - (FFN_CONV - 1):]

    (x2,) = _mm([act], [w_down], [0], [(x1, "tile")], lambda accs, ex: (ex[0] + accs[0],), [F32],
                _tile(M, 512), tn, "ffn_down")
    (x3,) = _mm([p, x2], [w_proj, w_pgate], [0, 1], [(x2, "tile")],
                lambda accs, ex: (ex[0] + accs[0] * _sigmoid(accs[1]),), [F32], tm, tn, "ple", norm=(1, norm_ple))
    return x3, new_buf


def kernel(x_prompt, x_sample, cache_cmp_k, cache_cmp_v, cache_sel_k, cache_sel_v, cache_win_k, cache_win_v, state_rglru_h, state_rglru_conv, state_s5_re, state_s5_im, state_ffn_conv, page_table, p_prompt, p_sample, norm_mix, norm_ffn, norm_ple, w_in_even, w_out_even, q_norm, k_norm_cmp, k_norm_sel, k_norm_win, cmp_pe_k, cmp_w1_k, cmp_w2_k, cmp_pe_v, cmp_w1_v, cmp_w2_v, rg_conv_w, rg_conv_b, rg_w_a, rg_b_a, rg_w_x, rg_b_x, rg_lam, s5_lam_re, s5_lam_im, s5_log_dt, s5_b_re, s5_b_im, s5_c_re, s5_c_im, s5_d, s5_w_glu_a, s5_w_glu_b, ffn_w_up, ffn_w_gate, ffn_conv_w, ffn_conv_b, ffn_w_down, ple_w_proj, ple_w_gate):
    B, T, _ = x_prompt.shape
    DB = x_sample.shape[0]
    n_pages = page_table.shape[1]
    past_len = n_pages * PAGE_SIZE
    w_buf = cache_win_k.shape[2]
    MP = B * T
    MS = DB * DEC_SEQ
    depth = norm_mix.shape[0]

    def reorder_in(w):
        q, kv, gt, xr, gr = (w[:, :Q_W], w[:, Q_W:Q_W + 6 * KV_W], w[:, Q_W + 6 * KV_W:Q_W + 6 * KV_W + 3 * N_HEADS],
                             w[:, Q_W + 6 * KV_W + 3 * N_HEADS:Q_W + 6 * KV_W + 3 * N_HEADS + D_RNN],
                             w[:, Q_W + 6 * KV_W + 3 * N_HEADS + D_RNN:])
        pad = jnp.zeros((w.shape[0], IN_PAD - COL_GT - 3 * N_HEADS), w.dtype)
        return jnp.concatenate([q, xr, gr, kv, gt, pad], axis=1).astype(BF16)

    stacked = [w.astype(BF16) for w in (ffn_w_up, ffn_w_gate, ffn_w_down, ple_w_proj, ple_w_gate)]

    def layer_dense_weights(li):
        up, gate, down, proj, pgate = [(w, li) for w in stacked]
        return (norm_ffn[li], norm_ple[li], up, gate, ffn_conv_w[li], ffn_conv_b[li], down, proj, pgate)

    xs = {"p": x_prompt.reshape(MP, D_MODEL), "s": x_sample.reshape(MS, D_MODEL)}
    pe_in = {"p": p_prompt.reshape(depth, MP, -1).astype(BF16), "s": p_sample.reshape(depth, MS, -1).astype(BF16)}
    seq_len = {"p": T, "s": DEC_SEQ}
    ev = {"p": [], "s": []}
    od = {"p": [], "s": []}
    ff = {"p": [], "s": []}
    ptab = page_table.reshape(-1).astype(jnp.int32)

    for li in range(depth):
        dense_w = layer_dense_weights(li)
        if li % 2 == 0:
            e = li // 2
            w_in = reorder_in(w_in_even[e])
            w_out = w_out_even[e].astype(BF16)
            w_out_a, w_out_r = w_out[:Q_W], w_out[Q_W:]
            for grp in ("p", "s"):
                x = xs[grp]
                M = x.shape[0]
                L = seq_len[grp]
                nseq = M // L
                (z,) = _mm([x], [w_in], [0], [], lambda accs, ex: (accs[0],), [F32], _tile(M, 1024), 512, "in_proj",
                           norm=(0, norm_mix[li]))
                if grp == "p":
                    cos_t, sin_t = _rope_tables(jnp.arange(T))
                    tmq = _tile(T, 512)
                    tab_blocks = T // tmq
                else:
                    cos_t, sin_t = _rope_tables(jnp.tile(past_len + jnp.arange(DEC_SEQ), DB))
                    tmq = M
                    tab_blocks = 1
                q, kvb, kc, vc, ks, vs, kw, vw = _qk_prep(z, cos_t, sin_t, tab_blocks, q_norm[e], k_norm_sel[e],
                                                          k_norm_win[e], tmq)
                z3 = z.reshape(nseq, L, IN_PAD)

                if grp == "p":
                    n_rows = T // CMP_STRIDE
                    ccos, csin = _rope_tables(jnp.arange(n_rows) * CMP_STRIDE + CMP_STRIDE - 1)
                    ident = jnp.arange(B * (T // PAGE_SIZE), dtype=jnp.int32)
                    kcmp = _compress(kc.reshape(-1, N_KV * PAGE_SIZE, HEAD_DIM), ident, B, T // PAGE_SIZE, cmp_pe_k[e],
                                     cmp_w1_k[e], cmp_w2_k[e], k_norm_cmp[e], ccos, csin, True, "compress_k_prompt")
                    vcmp = _compress(vc.reshape(-1, N_KV * PAGE_SIZE, HEAD_DIM), ident, B, T // PAGE_SIZE, cmp_pe_v[e],
                                     cmp_w1_v[e], cmp_w2_v[e], k_norm_cmp[e], ccos, csin, False, "compress_v_prompt")
                    n_sel = T // SEL_BLOCK
                    nqb = T // Q_BLOCK
                    ovl_t = _overlap_matrix(n_rows, n_sel, n_sel).T
                    ex_t = ((np.arange(T // SEL_KT)[:, None, None] * SEL_KT + np.arange(SEL_KT)[None, :, None])
                            // SEL_BLOCK == np.arange(n_sel)[None, None, :])
                    vs_t = kvb.reshape(B, T // SEL_KT, SEL_KT, 4, N_KV, HEAD_DIM)[:, :, :, 1].transpose(0, 3, 1, 4, 2)
                    vw_t = kvb.reshape(B, nqb, Q_BLOCK, 4, N_KV, HEAD_DIM)[:, :, :, 3].transpose(0, 3, 1, 4, 2)
                    gates_t = z[:, COL_GT:COL_GT + 3 * N_HEADS].reshape(B, T, 3, N_KV, GROUP)
                    gates_t = gates_t.transpose(0, 3, 2, 4, 1).reshape(B, N_KV, 3 * GROUP, T)
                    o_nsa = _nsa_prompt(q.reshape(B, T, Q_W), kcmp, vcmp.swapaxes(2, 3), kvb.reshape(B, T, 4 * KV_W),
                                        vs_t, vw_t, gates_t, ovl_t, jnp.asarray(ex_t, BF16), B, T).reshape(M, Q_W)
                    y_rg, h_last = _rglru_prompt(z.reshape(B, T, IN_PAD), rg_conv_w[e], rg_conv_b[e], rg_w_a[e],
                                                 rg_b_a[e], rg_w_x[e], rg_b_x[e], rg_lam[e], B, T, _tile(T, 512))
                    y_rg = y_rg.reshape(M, D_RNN)
                    h_last = h_last.reshape(B, D_RNN)
                    new_conv = z3[:, T - (RG_CONV - 1):, COL_XR:COL_XR + D_RNN]
                    new_wk = kw.reshape(B, T, N_KV, HEAD_DIM)
                    new_wv = vw.reshape(B, T, N_KV, HEAD_DIM)
                    if T >= w_buf:
                        new_wk, new_wv = new_wk[:, T - w_buf:], new_wv[:, T - w_buf:]
                    else:
                        padw = ((0, 0), (w_buf - T, 0), (0, 0), (0, 0))
                        new_wk, new_wv = jnp.pad(new_wk, padw), jnp.pad(new_wv, padw)
                else:
                    n_rows = past_len // CMP_STRIDE
                    ccos, csin = _rope_tables(jnp.arange(n_rows) * CMP_STRIDE + CMP_STRIDE - 1)
                    ptab_e = ptab + e * cache_cmp_k.shape[1]
                    kcmp = _compress(cache_cmp_k.reshape(-1, N_KV * PAGE_SIZE, HEAD_DIM), ptab_e, DB, n_pages, cmp_pe_k[e],
                                     cmp_w1_k[e], cmp_w2_k[e], k_norm_cmp[e], ccos, csin, True, "compress_k_sample")
                    vcmp = _compress(cache_cmp_v.reshape(-1, N_KV * PAGE_SIZE, HEAD_DIM), ptab_e, DB, n_pages, cmp_pe_v[e],
                                     cmp_w1_v[e], cmp_w2_v[e], k_norm_cmp[e], ccos, csin, False, "compress_v_sample")
                    n_sel = -(-(past_len + DEC_SEQ) // SEL_BLOCK)
                    sel_cols = -(-n_sel // 128) * 128
                    ovl = _overlap_matrix(n_rows, n_sel, sel_cols)
                    keys = PAGES_PER_STEP * PAGE_SIZE
                    steps = n_pages // PAGES_PER_STEP
                    ex = (np.arange(sel_cols)[None, :, None]
                          == (np.arange(steps)[:, None, None] * keys + np.arange(keys)[None, None, :]) // SEL_BLOCK)
                    q8 = jnp.pad(q.reshape(DB, DEC_SEQ, Q_W), ((0, 0), (0, Q_PAD - DEC_SEQ), (0, 0)))
                    kvnew = jnp.pad(kvb.reshape(DB, DEC_SEQ, 4 * KV_W), ((0, 0), (0, 128 - DEC_SEQ), (0, 0)))
                    gt = z3[:, :, COL_GT:COL_GT + 3 * N_HEADS].reshape(DB, DEC_SEQ, 3, N_KV, GROUP)
                    gt = gt.transpose(0, 3, 1, 2, 4).reshape(DB, N_KV, DEC_SEQ, 3 * GROUP)
                    gt8 = jnp.pad(gt, ((0, 0), (0, 0), (0, Q_PAD - DEC_SEQ), (0, 0)))
                    o8 = _nsa_sample(q8, kcmp, vcmp, cache_sel_k.reshape(-1, N_KV * PAGE_SIZE, HEAD_DIM),
                                     cache_sel_v.reshape(-1, N_KV * PAGE_SIZE, HEAD_DIM), ptab_e, kvnew,
                                     cache_win_k.reshape(-1, N_KV * w_buf, HEAD_DIM),
                                     cache_win_v.reshape(-1, N_KV * w_buf, HEAD_DIM),
                                     e * DB, gt8, ovl, jnp.asarray(ex, BF16), DB, past_len)
                    o_nsa = o8[:, :DEC_SEQ].reshape(M, Q_W)
                    xr3 = z3[:, :, COL_XR:COL_XR + D_RNN]
                    gr3 = z3[:, :, COL_GR:COL_GR + D_RNN]
                    y_t, h_last = _rglru_sample(xr3.transpose(1, 0, 2), gr3.transpose(1, 0, 2),
                                                state_rglru_conv[e].transpose(1, 0, 2), state_rglru_h[e],
                                                rg_conv_w[e], rg_conv_b[e], rg_w_a[e], rg_b_a[e], rg_w_x[e],
                                                rg_b_x[e], rg_lam[e])
                    y_rg = y_t.transpose(1, 0, 2).reshape(M, D_RNN)
                    new_conv = jnp.concatenate([state_rglru_conv[e], xr3], axis=1)[:, -(RG_CONV - 1):]
                    kk = jnp.concatenate([cache_win_k[e], kw.reshape(DB, DEC_SEQ, N_KV, HEAD_DIM)], axis=1)
                    vv = jnp.concatenate([cache_win_v[e], vw.reshape(DB, DEC_SEQ, N_KV, HEAD_DIM)], axis=1)
                    new_wk, new_wv = kk[:, -w_buf:], vv[:, -w_buf:]

                rs = lambda a: a.reshape(nseq, L, N_KV, HEAD_DIM)
                ev[grp].append((rs(kc), rs(vc), rs(ks), rs(vs), new_wk, new_wv, h_last, new_conv))
                (x1,) = _mm([o_nsa, y_rg], [w_out_a, w_out_r], [0, 1], [(x, "tile")],
                            lambda accs, exs: (exs[0] + accs[0] + accs[1],), [F32], _tile(M, 1024), 512, "mixer_out")
                xs[grp], nb = _dense_tail(x1, pe_in[grp][li], dense_w, None if grp == "p" else state_ffn_conv[li], L)
                ff[grp].append(nb)
        else:
            o = li // 2
            ar, ai, bbr, bbi = _s5_discretize(s5_lam_re[o], s5_lam_im[o], s5_log_dt[o],
                                              s5_b_re[o].transpose(2, 0, 1), s5_b_im[o].transpose(2, 0, 1))
            bw, cr, ci = _s5_block_weights(bbr, bbi, s5_c_re[o], s5_c_im[o])
            ar = ar.reshape(S5_NGB, 1, S5_LANES)
            ai = ai.reshape(S5_NGB, 1, S5_LANES)
            w_a = s5_w_glu_a[o].astype(BF16)
            w_b = s5_w_glu_b[o].astype(BF16)
            for grp in ("p", "s"):
                x = xs[grp]
                M = x.shape[0]
                L = seq_len[grp]
                glu = lambda accs, exs: (exs[0] + accs[0] * _sigmoid(accs[1]),)
                if grp == "p":
                    sl = T // S5_NSEG
                    hn = _rmsnorm(x, norm_mix[li], F32, _tile(M, 256))
                    zz, sre, sim = _s5_prompt(hn.reshape(B, S5_NSEG, sl, D_MODEL), bw.astype(BF16), cr.astype(BF16),
                                              ci.astype(BF16), ar, ai, s5_d[o], B, T)
                    sre = sre.reshape(B, S5_GROUPS, S5_STATE)
                    sim = sim.reshape(B, S5_GROUPS, S5_STATE)
                    (x1,) = _mm([zz.reshape(M, D_MODEL)], [w_a, w_b], [0, 0], [(x, "tile")], glu, [F32],
                                _tile(M, 1024), 512, "mixer_out")
                else:
                    hn = _rmsnorm(x, norm_mix[li], F32, _tile(M, 256))
                    u = hn.reshape(DB, DEC_SEQ, D_MODEL).transpose(1, 0, 2)
                    zz, sre, sim = _s5_sample(u, state_s5_re[o].reshape(DB, -1), state_s5_im[o].reshape(DB, -1),
                                              bw, cr, ci, ar, ai, s5_d[o])
                    zz = zz.transpose(1, 0, 2).reshape(M, D_MODEL)
                    sre = sre.reshape(DB, S5_GROUPS, S5_STATE)
                    sim = sim.reshape(DB, S5_GROUPS, S5_STATE)
                    (x1,) = _mm([zz], [w_a, w_b], [0, 0], [(x, "tile")], glu, [F32], M, 512, "mixer_out")
                od[grp].append((sre, sim))
                xs[grp], nb = _dense_tail(x1, pe_in[grp][li], dense_w, None if grp == "p" else state_ffn_conv[li], L)
                ff[grp].append(nb)

    def stk(states, i):
        return jnp.stack([s[i] for s in states])

    outs = [xs["p"].reshape(B, T, D_MODEL), xs["s"].reshape(DB, DEC_SEQ, D_MODEL)]
    for i in range(8):
        outs += [stk(ev["p"], i), stk(ev["s"], i)]
    for i in range(2):
        outs += [stk(od["p"], i), stk(od["s"], i)]
    outs += [jnp.stack(ff["p"]), jnp.stack(ff["s"])]
    return tuple(outs)
```

```python
import functools

import numpy as np
import jax
import jax.numpy as jnp
from jax import lax
from jax.experimental import pallas as pl
from jax.experimental.pallas import tpu as pltpu

F32 = jnp.float32
BF16 = jnp.bfloat16

D_MODEL = 2048
DEC_SEQ = 4
PAGE_SIZE = 128
N_HEADS = 8
HEAD_DIM = 128
N_KV = 2
GROUP = N_HEADS // N_KV
ROPE_DIM = HEAD_DIM // 4
ROPE_HALF = ROPE_DIM // 2
ROPE_THETA = 500000.0
CMP_BLOCK = 32
CMP_STRIDE = 16
SEL_BLOCK = 64
SEL_TOPK = 16
N_LOCAL = 2
WINDOW = 512
Q_BLOCK = 128
FORCED_SCORE = 1e9
NEG_BIG = -1e30
ATTN_SCALE = HEAD_DIM ** -0.5
D_RNN = D_MODEL // 2
RG_BLOCKS = 8
RG_BW = D_RNN // RG_BLOCKS
RG_CONV = 4
RG_C = 8.0
S5_CH = 16
S5_GROUPS = D_MODEL // S5_CH
S5_STATE = 64
S5_GB = 8
S5_NGB = S5_GROUPS // S5_GB
S5_LANES = S5_GB * S5_STATE
S5_NSEG = 8
D_FF = ((8 * D_MODEL // 3 + 255) // 256) * 256
FFN_CONV = 3
NORM_EPS = 1e-6
Q_W = N_HEADS * HEAD_DIM
KV_W = N_KV * HEAD_DIM
MIN_PAGES_PER_STEP = 8
CMP_PAGES_PER_STEP = 32
SEL_PAGES_PER_STEP = 16
Q_PAD = 8
CMP_SPLIT = 4
SEL_KT = 512

COL_Q = 0
COL_XR = Q_W
COL_GR = Q_W + D_RNN
COL_KV = Q_W + 2 * D_RNN
COL_GT = COL_KV + 6 * KV_W
IN_PAD = 5120


def _pages_per_step(n_pages, preferred):
    g = preferred
    while g > MIN_PAGES_PER_STEP and n_pages % g:
        g //= 2
    return g


def _cp(sem, vmem_mb=48):
    return pltpu.CompilerParams(dimension_semantics=sem, vmem_limit_bytes=vmem_mb * 1024 * 1024)


def _gelu(x):
    return 0.5 * x * (1.0 + jnp.tanh(0.7978845608028654 * (x + 0.044715 * (x * x * x))))


def _sigmoid(x):
    return 1.0 / (1.0 + jnp.exp(-x))


def _dotb(a, b):
    return jnp.dot(a.astype(BF16), b.astype(BF16), preferred_element_type=F32)


def _dotf(a, b):
    return jnp.dot(a, b, preferred_element_type=F32, precision=lax.Precision.HIGHEST)


def _dot_nt(a, b):
    return lax.dot_general(a, b, (((1,), (1,)), ((), ())), preferred_element_type=F32)


def _masked_softmax(s, mask):
    s = jnp.where(mask, s, NEG_BIG)
    m = jnp.max(s, axis=-1, keepdims=True)
    e = jnp.where(mask, jnp.exp(s - m), 0.0)
    return e / jnp.maximum(jnp.sum(e, axis=-1, keepdims=True), 1e-30)


def _norm_rope(x, g, cos_t, sin_t):
    ms = jnp.mean(x * x, axis=-1, keepdims=True)
    y = x * lax.rsqrt(ms + NORM_EPS) * g
    lane = lax.broadcasted_iota(jnp.int32, y.shape, 1)
    swapped = jnp.where(lane < ROPE_HALF, pltpu.roll(y, HEAD_DIM - ROPE_HALF, 1), pltpu.roll(y, ROPE_HALF, 1))
    return y * cos_t + swapped * sin_t


def _rope_tables(pos):
    inv = ROPE_THETA ** (-jnp.arange(ROPE_HALF, dtype=F32) * 2.0 / ROPE_DIM)
    ang = pos.astype(F32)[:, None] * inv
    cos = jnp.cos(ang)
    sin = jnp.sin(ang)
    n = pos.shape[0]
    ones = jnp.ones((n, HEAD_DIM - ROPE_DIM), F32)
    zeros = jnp.zeros((n, HEAD_DIM - ROPE_DIM), F32)
    return (jnp.concatenate([cos, cos, ones], axis=1), jnp.concatenate([-sin, sin, zeros], axis=1))


def _overlap_matrix(n_rows, n_sel, n_cols):
    n = np.arange(n_rows)[:, None] - 1
    j = np.arange(n_cols)[None]
    c0 = n * CMP_STRIDE
    s0 = j * SEL_BLOCK
    ov = np.minimum(c0 + CMP_BLOCK, s0 + SEL_BLOCK) - np.maximum(c0, s0)
    ov = np.maximum(ov, 0).astype(np.float32) / CMP_BLOCK
    ov = np.where((n >= 0) & (j < n_sel), ov, 0.0)
    return jnp.asarray(ov, F32)


def _rmsnorm(x, g, out_dtype, tm):
    M, D = x.shape

    def body(x_ref, g_ref, o_ref):
        xv = x_ref[...]
        ms = jnp.mean(xv * xv, axis=-1, keepdims=True)
        o_ref[...] = (xv * lax.rsqrt(ms + NORM_EPS) * g_ref[...]).astype(o_ref.dtype)

    return pl.pallas_call(
        body, grid=(M // tm,),
        in_specs=[pl.BlockSpec((tm, D), lambda i: (i, 0)), pl.BlockSpec((1, D), lambda i: (0, 0))],
        out_specs=pl.BlockSpec((tm, D), lambda i: (i, 0)),
        out_shape=jax.ShapeDtypeStruct((M, D), out_dtype),
        compiler_params=_cp(("parallel",)), name="rmsnorm")(x, g.reshape(1, D))


def _rms_rows(xv, g):
    ms = jnp.mean(xv * xv, axis=-1, keepdims=True)
    return xv * lax.rsqrt(ms + NORM_EPS) * g


def _weight_spec(w, tn):
    if isinstance(w, tuple):
        arr, layer = w
        return arr, pl.BlockSpec((None, arr.shape[1], tn), lambda i, j: (layer, 0, j))
    return w, pl.BlockSpec((w.shape[0], tn), lambda i, j: (0, j))


def _mm(a_list, w_list, pair, extras, epilogue, out_dtypes, tm, tn, name, norm=None, vmem_mb=48):
    w_arrays, w_specs = zip(*[_weight_spec(w, tn) for w in w_list])
    N = w_arrays[0].shape[-1]
    na, nw, ne, no = len(a_list), len(w_list), len(extras), len(out_dtypes)
    M = a_list[0].shape[0]

    def body(*refs):
        if norm is not None:
            g_ref = refs[na + nw + ne]
            an_ref = refs[na + nw + ne + 1 + no]

            @pl.when(pl.program_id(1) == 0)
            def _():
                an_ref[...] = _rms_rows(refs[norm[0]][...], g_ref[...]).astype(BF16)

        a_vals = []
        for k, r in enumerate(refs[:na]):
            if norm is not None and k == norm[0]:
                a_vals.append(an_ref[...])
            else:
                a_vals.append(r[...].astype(BF16))
        accs = [jnp.dot(a_vals[pair[i]], refs[na + i][...], preferred_element_type=F32) for i in range(nw)]
        ex = [r[...] for r in refs[na + nw:na + nw + ne]]
        res = epilogue(accs, ex)
        first_out = na + nw + ne + (1 if norm is not None else 0)
        for o_ref, v in zip(refs[first_out:first_out + no], res):
            o_ref[...] = v.astype(o_ref.dtype)

    in_specs = [pl.BlockSpec((tm, a.shape[1]), lambda i, j: (i, 0)) for a in a_list]
    in_specs += list(w_specs)
    for arr, kind in extras:
        if kind == "tile":
            in_specs.append(pl.BlockSpec((tm, tn), lambda i, j: (i, j)))
        else:
            in_specs.append(pl.BlockSpec((arr.shape[0], tn), lambda i, j: (0, j)))
    operands = [*a_list, *w_arrays, *[e[0] for e in extras]]
    scratch = []
    if norm is not None:
        kn = a_list[norm[0]].shape[1]
        in_specs.append(pl.BlockSpec((1, kn), lambda i, j: (0, 0)))
        operands.append(norm[1].reshape(1, kn))
        scratch.append(pltpu.VMEM((tm, kn), BF16))
    return pl.pallas_call(
        body, grid=(M // tm, N // tn), in_specs=in_specs,
        out_specs=[pl.BlockSpec((tm, tn), lambda i, j: (i, j)) for _ in out_dtypes],
        out_shape=[jax.ShapeDtypeStruct((M, N), dt) for dt in out_dtypes],
        scratch_shapes=scratch,
        compiler_params=_cp(("parallel", "arbitrary" if norm is not None else "parallel"), vmem_mb), name=name,
    )(*operands)


def _qk_prep(z, cos_t, sin_t, tab_blocks, qn, kns, knw, tm):
    M = z.shape[0]

    def body(zq_ref, zkv_ref, c_ref, s_ref, qn_ref, kns_ref, knw_ref, q_ref, kvb_ref, *row_refs):
        c = c_ref[...]
        s = s_ref[...]
        for h in range(N_HEADS):
            sl = slice(h * HEAD_DIM, (h + 1) * HEAD_DIM)
            q_ref[:, sl] = _norm_rope(zq_ref[:, sl], qn_ref[...], c, s).astype(BF16)
        for kv in range(N_KV):
            vals = [zkv_ref[:, (k * N_KV + kv) * HEAD_DIM:(k * N_KV + kv + 1) * HEAD_DIM] for k in range(6)]
            vals[2] = _norm_rope(vals[2], kns_ref[...], c, s)
            vals[4] = _norm_rope(vals[4], knw_ref[...], c, s)
            for k in range(6):
                row_refs[k][pl.ds(kv, tm, stride=N_KV), :] = vals[k]
            for k in range(2, 6):
                kvb_ref[:, ((k - 2) * N_KV + kv) * HEAD_DIM:((k - 2) * N_KV + kv + 1) * HEAD_DIM] = vals[k].astype(BF16)

    vec = pl.BlockSpec((1, HEAD_DIM), lambda i: (0, 0))
    tab = pl.BlockSpec((tm, HEAD_DIM), lambda i: (i % tab_blocks, 0))
    rows_spec = pl.BlockSpec((N_KV * tm, HEAD_DIM), lambda i: (i, 0))
    rows_shape = jax.ShapeDtypeStruct((N_KV * M, HEAD_DIM), F32)
    return pl.pallas_call(
        body, grid=(M // tm,),
        in_specs=[pl.BlockSpec((tm, Q_W), lambda i: (i, COL_Q // Q_W)),
                  pl.BlockSpec((tm, 6 * KV_W), lambda i: (i, COL_KV // (6 * KV_W))),
                  tab, tab, vec, vec, vec],
        out_specs=[pl.BlockSpec((tm, Q_W), lambda i: (i, 0)), pl.BlockSpec((tm, 4 * KV_W), lambda i: (i, 0))]
        + [rows_spec] * 6,
        out_shape=[jax.ShapeDtypeStruct((M, Q_W), BF16), jax.ShapeDtypeStruct((M, 4 * KV_W), BF16)] + [rows_shape] * 6,
        compiler_params=_cp(("parallel",)), name="qk_prep",
    )(z, z, cos_t, sin_t, qn.reshape(1, -1), kns.reshape(1, -1), knw.reshape(1, -1))


def _compress_bias(pe, w1):
    def body(pe_ref, w1_ref, o_ref):
        for half in range(2):
            acc = jnp.zeros((8, HEAD_DIM), F32)
            for c in range(half * CMP_STRIDE, (half + 1) * CMP_STRIDE):
                acc = acc + _dotf(jnp.broadcast_to(pe_ref[c:c + 1, :], (8, HEAD_DIM)), w1_ref[c])
            o_ref[:, half * HEAD_DIM:(half + 1) * HEAD_DIM] = acc[0:1, :]

    return pl.pallas_call(body, out_shape=jax.ShapeDtypeStruct((1, 2 * HEAD_DIM), F32), name="compress_bias")(pe, w1)


def _compress(pages, ptab, n_seq, pages_per_seq, pe, w1, w2, norm_g, cos_t, sin_t, do_norm, name):
    bias = _compress_bias(pe, w1)
    w1cat = jnp.concatenate([w1[:CMP_STRIDE], w1[CMP_STRIDE:]], axis=2).astype(BF16)
    G = _pages_per_step(pages_per_seq, CMP_PAGES_PER_STEP)
    steps = pages_per_seq // G
    cpp = PAGE_SIZE // CMP_STRIDE
    ch = G * cpp
    n_chunk = pages_per_seq * cpp

    def body(pt_ref, *refs):
        page_refs = refs[:G]
        bias_ref, w1_ref, w2_ref, g_ref, c_ref, s_ref, o_ref, carry_ref, split_ref = refs[G:]
        step = pl.program_id(1)

        @pl.when(step == 0)
        def _():
            carry_ref[...] = jnp.zeros_like(carry_ref)

        for g, pr in enumerate(page_refs):
            for j in range(CMP_SPLIT):
                split_ref[g, j] = pr[0, pl.ds(j, N_KV * PAGE_SIZE // CMP_SPLIT, stride=CMP_SPLIT), :]

        row = lax.broadcasted_iota(jnp.int32, (ch, HEAD_DIM), 0)
        hop = N_KV * CMP_STRIDE // CMP_SPLIT
        for kv in range(N_KV):
            acc = jnp.zeros((ch, 2 * HEAD_DIM), F32)
            for c in range(CMP_STRIDE):
                j = (N_KV * c + kv) % CMP_SPLIT
                start = (N_KV * c + kv) // CMP_SPLIT
                xc = jnp.concatenate(
                    [split_ref[g, j, pl.ds(start, cpp, stride=hop), :] for g in range(G)], axis=0)
                acc = acc + _dotb(xc, w1_ref[c])
            acc = acc + bias_ref[...]
            lo = acc[:, :HEAD_DIM]
            hi = acc[:, HEAD_DIM:]
            lo_prev = jnp.where(row == 0, carry_ref[kv, 0:1, :], pltpu.roll(lo, 1, 0))
            carry_ref[kv, 0:1, :] = lo[ch - 1:ch, :]
            out = _dotb(_gelu(lo_prev + hi), w2_ref[...])
            if do_norm:
                out = _norm_rope(out, g_ref[...], c_ref[...], s_ref[...])
            o_ref[0, kv] = out.astype(BF16)

    def page_map(j, s, g, pt):
        return (pt[s * pages_per_seq + g * G + j], 0, 0)

    full2 = lambda s, g, pt: (0, 0)
    in_specs = [pl.BlockSpec((1, N_KV * PAGE_SIZE, HEAD_DIM), functools.partial(page_map, j)) for j in range(G)]
    in_specs += [pl.BlockSpec((1, 2 * HEAD_DIM), full2),
                 pl.BlockSpec((CMP_STRIDE, HEAD_DIM, 2 * HEAD_DIM), lambda s, g, pt: (0, 0, 0)),
                 pl.BlockSpec((HEAD_DIM, HEAD_DIM), full2),
                 pl.BlockSpec((1, HEAD_DIM), full2),
                 pl.BlockSpec((ch, HEAD_DIM), lambda s, g, pt: (g, 0)),
                 pl.BlockSpec((ch, HEAD_DIM), lambda s, g, pt: (g, 0))]
    gs = pltpu.PrefetchScalarGridSpec(
        num_scalar_prefetch=1, grid=(n_seq, steps), in_specs=in_specs,
        out_specs=pl.BlockSpec((1, N_KV, ch, HEAD_DIM), lambda s, g, pt: (s, 0, g, 0)),
        scratch_shapes=[pltpu.VMEM((N_KV, 8, HEAD_DIM), F32),
                        pltpu.VMEM((G, CMP_SPLIT, N_KV * PAGE_SIZE // CMP_SPLIT, HEAD_DIM), F32)])
    return pl.pallas_call(
        body, grid_spec=gs, out_shape=jax.ShapeDtypeStruct((n_seq, N_KV, n_chunk, HEAD_DIM), BF16),
        compiler_params=_cp(("parallel", "arbitrary")), name=name,
    )(ptab, *([pages] * G), bias, w1cat, w2.astype(BF16), norm_g.reshape(1, -1), cos_t, sin_t)


def _nsa_prompt(q, kcmp, vcmp_t, kvb, vs_t, vw_t, gates_t, ovl_t, expand_t, B, T):
    nqb = T // Q_BLOCK
    n_cmp_rows = T // CMP_STRIDE
    n_sel = T // SEL_BLOCK
    cols = GROUP * Q_BLOCK
    win_tiles = WINDOW // Q_BLOCK
    win_keys = (win_tiles + 1) * Q_BLOCK
    assert T % SEL_KT == 0 and T >= win_keys
    topk = min(SEL_TOPK, n_sel)

    def body(q_ref, kc_ref, vc_ref, ks_ref, vs_ref, kw_ref, vw_ref, gt_ref, ovl_ref, ex_ref, o_ref):
        i = pl.program_id(2)
        q4 = q_ref[0].astype(F32)
        qt = jnp.concatenate([q4[:, h * HEAD_DIM:(h + 1) * HEAD_DIM].T for h in range(GROUP)], axis=1).astype(BF16)
        pos = i * Q_BLOCK + (lax.broadcasted_iota(jnp.int32, (1, cols), 1) & (Q_BLOCK - 1))

        s = jnp.dot(kc_ref[0, 0], qt, preferred_element_type=F32) * ATTN_SCALE
        crow = lax.broadcasted_iota(jnp.int32, (n_cmp_rows, cols), 0)
        cmask = (crow >= 1) & (crow * CMP_STRIDE + (CMP_BLOCK - 1 - CMP_STRIDE) <= pos)
        s = jnp.where(cmask, s, NEG_BIG)
        e = jnp.where(cmask, jnp.exp(s - jnp.max(s, axis=0, keepdims=True)), 0.0)
        p = e / jnp.maximum(jnp.sum(e, axis=0, keepdims=True), 1e-30)
        o_cmp = _dotb(vc_ref[0, 0], p)
        imp = p[:, 0:Q_BLOCK]
        for h in range(1, GROUP):
            imp = imp + p[:, h * Q_BLOCK:(h + 1) * Q_BLOCK]

        score = _dotf(ovl_ref[...], imp)
        j = lax.broadcasted_iota(jnp.int32, (n_sel, Q_BLOCK), 0)
        qblk = (i * Q_BLOCK + lax.broadcasted_iota(jnp.int32, (n_sel, Q_BLOCK), 1)) // SEL_BLOCK
        forced = (j == 0) | ((j <= qblk) & (j > qblk - N_LOCAL))
        score = jnp.where(forced, FORCED_SCORE, jnp.where(j > qblk, NEG_BIG, score))
        rank = jnp.zeros((n_sel, Q_BLOCK), F32)
        for r in range(n_sel):
            sr = score[r:r + 1, :]
            beats = (sr > score) | ((sr == score) & (j > r))
            rank = rank + jnp.where(beats, 1.0, 0.0)
        sel_bias = jnp.where(rank < topk, 0.0, NEG_BIG).astype(BF16)

        krow = lax.broadcasted_iota(jnp.int32, (SEL_KT, cols), 0)

        def sel_tile(u, carry, causal):
            m, l, acc = carry
            k = ks_ref[0, pl.ds(pl.multiple_of(u * SEL_KT, SEL_KT), SEL_KT), :]
            b1 = _dotb(ex_ref[u], sel_bias)
            sc = (jnp.dot(k, qt, preferred_element_type=F32) * ATTN_SCALE
                  + jnp.concatenate([b1] * GROUP, axis=1))
            if causal:
                sc = jnp.where(u * SEL_KT + krow <= pos, sc, NEG_BIG)
            m_new = jnp.maximum(m, jnp.max(sc, axis=0, keepdims=True))
            alpha = jnp.exp(m - m_new)
            ex = jnp.exp(sc - m_new)
            l = alpha * l + jnp.sum(ex, axis=0, keepdims=True)
            acc = alpha * acc + _dotb(vs_ref[0, 0, u], ex)
            return m_new, l, acc

        init = (jnp.full((1, cols), NEG_BIG, F32), jnp.zeros((1, cols), F32), jnp.zeros((HEAD_DIM, cols), F32))
        last = (i * Q_BLOCK + Q_BLOCK + SEL_KT - 1) // SEL_KT - 1
        carry = lax.fori_loop(0, last, functools.partial(sel_tile, causal=False), init)
        _, l_sel, acc_sel = sel_tile(last, carry, causal=True)
        o_sel = acc_sel / jnp.maximum(l_sel, 1e-30)

        t0 = jnp.maximum(i - win_tiles, 0)
        kwin = kw_ref[0, pl.ds(pl.multiple_of(t0 * Q_BLOCK, Q_BLOCK), win_keys), :]
        sw = jnp.dot(kwin, qt, preferred_element_type=F32) * ATTN_SCALE
        kpos = t0 * Q_BLOCK + lax.broadcasted_iota(jnp.int32, (win_keys, cols), 0)
        wmask = (kpos <= pos) & (kpos > pos - WINDOW)
        sw = jnp.where(wmask, sw, NEG_BIG)
        ew = jnp.exp(sw - jnp.max(sw, axis=0, keepdims=True))
        pw = ew / jnp.maximum(jnp.sum(ew, axis=0, keepdims=True), 1e-30)
        o_win = jnp.zeros((HEAD_DIM, cols), F32)
        for t in range(win_tiles + 1):
            o_win = o_win + _dotb(vw_ref[0, 0, t0 + t], pw[t * Q_BLOCK:(t + 1) * Q_BLOCK])

        g = _sigmoid(gt_ref[0, 0])
        for h in range(GROUP):
            c = slice(h * Q_BLOCK, (h + 1) * Q_BLOCK)
            o = (g[h:h + 1, :] * o_cmp[:, c] + g[GROUP + h:GROUP + h + 1, :] * o_sel[:, c]
                 + g[2 * GROUP + h:2 * GROUP + h + 1, :] * o_win[:, c])
            o_ref[0, :, h * HEAD_DIM:(h + 1) * HEAD_DIM] = o.T.astype(BF16)

    def k_spec(off):
        return pl.BlockSpec((1, T, HEAD_DIM), lambda b, kv, i: (b, 0, off + kv))

    vs_spec = pl.BlockSpec((1, 1, T // SEL_KT, HEAD_DIM, SEL_KT), lambda b, kv, i: (b, kv, 0, 0, 0))
    vw_spec = pl.BlockSpec((1, 1, nqb, HEAD_DIM, Q_BLOCK), lambda b, kv, i: (b, kv, 0, 0, 0))
    qo_spec = pl.BlockSpec((1, Q_BLOCK, GROUP * HEAD_DIM), lambda b, kv, i: (b, i, kv))
    return pl.pallas_call(
        body, grid=(B, N_KV, nqb),
        in_specs=[qo_spec,
                  pl.BlockSpec((1, 1, n_cmp_rows, HEAD_DIM), lambda b, kv, i: (b, kv, 0, 0)),
                  pl.BlockSpec((1, 1, HEAD_DIM, n_cmp_rows), lambda b, kv, i: (b, kv, 0, 0)),
                  k_spec(0), vs_spec, k_spec(4), vw_spec,
                  pl.BlockSpec((1, 1, 3 * GROUP, Q_BLOCK), lambda b, kv, i: (b, kv, 0, i)),
                  pl.BlockSpec((n_sel, n_cmp_rows), lambda b, kv, i: (0, 0)),
                  pl.BlockSpec((T // SEL_KT, SEL_KT, n_sel), lambda b, kv, i: (0, 0, 0))],
        out_specs=qo_spec,
        out_shape=jax.ShapeDtypeStruct((B, T, Q_W), BF16),
        compiler_params=_cp(("parallel", "parallel", "arbitrary")), name="nsa_prompt",
    )(q, kcmp, vcmp_t, kvb, vs_t, kvb, vw_t, gates_t, ovl_t, expand_t)


def _nsa_sample(q, kcmp, vcmp, pool_k, pool_v, ptab, kvnew, win_k, win_v, win_off, gates, ovl, expand, DB, past_len):
    n_pages = past_len // PAGE_SIZE
    G = _pages_per_step(n_pages, SEL_PAGES_PER_STEP)
    steps = n_pages // G
    n_cmp_rows = kcmp.shape[2]
    sel_cols = ovl.shape[1]
    n_past_blk = past_len // SEL_BLOCK
    n_sel = -(-(past_len + DEC_SEQ) // SEL_BLOCK)
    w_buf = win_k.shape[1] // N_KV
    rows = GROUP * Q_PAD
    keys = G * PAGE_SIZE
    new_rows = kvnew.shape[1]

    def body(pt_ref, q_ref, kc_ref, vc_ref, ovl_ref, *rest):
        pk = rest[:G]
        pv = rest[G:2 * G]
        (ex_ref, kvn_ref, wk_ref, wv_ref, gt_ref, o_ref,
         sel_scr, m_scr, l_scr, acc_scr, base_scr) = rest[2 * G:]
        step = pl.program_id(1)
        tok = lax.broadcasted_iota(jnp.int32, (rows, 1), 0) & (Q_PAD - 1)
        q8 = q_ref[0]

        def q_rows(kv):
            return jnp.concatenate(
                [q8[:, (kv * GROUP + h) * HEAD_DIM:(kv * GROUP + h + 1) * HEAD_DIM] for h in range(GROUP)], axis=0)

        def head_col(gt, c0):
            return jnp.concatenate(
                [jnp.broadcast_to(gt[:, c0 + h:c0 + h + 1], (Q_PAD, HEAD_DIM)) for h in range(GROUP)], axis=0)

        @pl.when(step == 0)
        def _():
            o_cmps = []
            imps = []
            for kv in range(N_KV):
                s = _dot_nt(q_rows(kv), kc_ref[0, kv]) * ATTN_SCALE
                col = lax.broadcasted_iota(jnp.int32, (rows, n_cmp_rows), 1)
                cmask = (col >= 1) & (col * CMP_STRIDE + (CMP_BLOCK - 1 - CMP_STRIDE) <= past_len + tok)
                p = _masked_softmax(s, cmask)
                o_cmps.append(_dotb(p, vc_ref[0, kv]))
                imp = p[0:Q_PAD]
                for h in range(1, GROUP):
                    imp = imp + p[h * Q_PAD:(h + 1) * Q_PAD]
                imps.append(imp)

            score = _dotf(jnp.concatenate(imps, axis=0), ovl_ref[...])
            j = lax.broadcasted_iota(jnp.int32, (N_KV * Q_PAD, sel_cols), 1)
            trow = lax.broadcasted_iota(jnp.int32, (N_KV * Q_PAD, sel_cols), 0) & (Q_PAD - 1)
            qblk = (past_len + trow) // SEL_BLOCK
            forced = (j == 0) | ((j <= qblk) & (j > qblk - N_LOCAL))
            score = jnp.where(forced, FORCED_SCORE, jnp.where(j > qblk, NEG_BIG, score))
            score = jnp.where(j < n_sel, score, -jnp.inf)
            rank = jnp.zeros((N_KV * Q_PAD, sel_cols), F32)
            for r in range(n_sel):
                sr = score[:, r:r + 1]
                rank = rank + jnp.where((sr > score) | ((sr == score) & (j > r)), 1.0, 0.0)
            sel_all = jnp.where((rank < min(SEL_TOPK, n_sel)) & (j < n_sel), 1.0, 0.0)

            for kv in range(N_KV):
                qh = q_rows(kv)
                o_cmp = o_cmps[kv]
                sel4 = jnp.concatenate([sel_all[kv * Q_PAD:(kv + 1) * Q_PAD]] * GROUP, axis=0)
                sel_scr[kv] = sel4.astype(BF16)

                kb = wk_ref[0, pl.ds(kv, w_buf, stride=N_KV), :].astype(BF16)
                vb = wv_ref[0, pl.ds(kv, w_buf, stride=N_KV), :].astype(BF16)
                kn = kvn_ref[0, :, (4 + kv) * HEAD_DIM:(5 + kv) * HEAD_DIM]
                vn = kvn_ref[0, :, (6 + kv) * HEAD_DIM:(7 + kv) * HEAD_DIM]
                s1 = jnp.where(lax.broadcasted_iota(jnp.int32, (rows, w_buf), 1) - w_buf > tok - WINDOW,
                               _dot_nt(qh, kb) * ATTN_SCALE, NEG_BIG)
                ncol = lax.broadcasted_iota(jnp.int32, (rows, new_rows), 1)
                nmask = (ncol <= tok) & (ncol < DEC_SEQ)
                s2 = jnp.where(nmask, _dot_nt(qh, kn) * ATTN_SCALE, NEG_BIG)
                m = jnp.maximum(jnp.max(s1, axis=-1, keepdims=True), jnp.max(s2, axis=-1, keepdims=True))
                e1 = jnp.where(s1 > 0.5 * NEG_BIG, jnp.exp(s1 - m), 0.0)
                e2 = jnp.where(nmask, jnp.exp(s2 - m), 0.0)
                den = jnp.sum(e1, axis=-1, keepdims=True) + jnp.sum(e2, axis=-1, keepdims=True)
                o_win = (_dotb(e1, vb) + _dotb(e2, vn)) / jnp.maximum(den, 1e-30)

                gt = _sigmoid(gt_ref[0, kv])
                base_scr[kv] = head_col(gt, 0) * o_cmp + head_col(gt, 2 * GROUP) * o_win

                ksn = kvn_ref[0, :, kv * HEAD_DIM:(kv + 1) * HEAD_DIM]
                vsn = kvn_ref[0, :, (2 + kv) * HEAD_DIM:(3 + kv) * HEAD_DIM]
                chosen = sel4[:, n_past_blk:n_past_blk + 1] > 0.5
                smask = nmask & chosen
                s3 = jnp.where(smask, _dot_nt(qh, ksn) * ATTN_SCALE, NEG_BIG)
                m0 = jnp.max(s3, axis=-1, keepdims=True)
                e3 = jnp.where(smask, jnp.exp(s3 - m0), 0.0)
                m_scr[kv] = jnp.broadcast_to(m0, (rows, HEAD_DIM))
                l_scr[kv] = jnp.broadcast_to(jnp.sum(e3, axis=-1, keepdims=True), (rows, HEAD_DIM))
                acc_scr[kv] = _dotb(e3, vsn)

        for kv in range(N_KV):
            qh = q_rows(kv)
            k = jnp.concatenate([r[0, pl.ds(kv, PAGE_SIZE, stride=N_KV), :] for r in pk], axis=0).astype(BF16)
            v = jnp.concatenate([r[0, pl.ds(kv, PAGE_SIZE, stride=N_KV), :] for r in pv], axis=0).astype(BF16)
            mask = _dotb(sel_scr[kv], ex_ref[0]) > 0.5
            sc = jnp.where(mask, _dot_nt(qh, k) * ATTN_SCALE, NEG_BIG)
            m_old = m_scr[kv][:, 0:1]
            l_old = l_scr[kv][:, 0:1]
            m_new = jnp.maximum(m_old, jnp.max(sc, axis=-1, keepdims=True))
            alpha = jnp.exp(m_old - m_new)
            e = jnp.where(mask, jnp.exp(sc - m_new), 0.0)
            l_new = alpha * l_old + jnp.sum(e, axis=-1, keepdims=True)
            acc = alpha * acc_scr[kv] + _dotb(e, v)
            m_scr[kv] = jnp.broadcast_to(m_new, (rows, HEAD_DIM))
            l_scr[kv] = jnp.broadcast_to(l_new, (rows, HEAD_DIM))
            acc_scr[kv] = acc

        @pl.when(step == steps - 1)
        def _():
            for kv in range(N_KV):
                gt = _sigmoid(gt_ref[0, kv])
                o_sel = acc_scr[kv] / jnp.maximum(l_scr[kv], 1e-30)
                o = base_scr[kv] + head_col(gt, GROUP) * o_sel
                for h in range(GROUP):
                    hh = kv * GROUP + h
                    o_ref[0, :, hh * HEAD_DIM:(hh + 1) * HEAD_DIM] = o[h * Q_PAD:(h + 1) * Q_PAD].astype(BF16)

    def page_map(jj, b, g, pt):
        return (pt[b * n_pages + g * G + jj], 0, 0)

    page_specs = [pl.BlockSpec((1, N_KV * PAGE_SIZE, HEAD_DIM), functools.partial(page_map, jj)) for jj in range(G)]
    cmp_spec = pl.BlockSpec((1, N_KV, n_cmp_rows, HEAD_DIM), lambda b, g, pt: (b, 0, 0, 0))
    in_specs = [pl.BlockSpec((1, Q_PAD, Q_W), lambda b, g, pt: (b, 0, 0)), cmp_spec, cmp_spec,
                pl.BlockSpec((n_cmp_rows, sel_cols), lambda b, g, pt: (0, 0))]
    in_specs += page_specs + page_specs
    in_specs += [pl.BlockSpec((1, sel_cols, keys), lambda b, g, pt: (g, 0, 0)),
                 pl.BlockSpec((1, new_rows, 4 * KV_W), lambda b, g, pt: (b, 0, 0)),
                 pl.BlockSpec((1, N_KV * w_buf, HEAD_DIM), lambda b, g, pt: (win_off + b, 0, 0)),
                 pl.BlockSpec((1, N_KV * w_buf, HEAD_DIM), lambda b, g, pt: (win_off + b, 0, 0)),
                 pl.BlockSpec((1, N_KV, Q_PAD, 3 * GROUP), lambda b, g, pt: (b, 0, 0, 0))]
    gs = pltpu.PrefetchScalarGridSpec(
        num_scalar_prefetch=1, grid=(DB, steps), in_specs=in_specs,
        out_specs=pl.BlockSpec((1, Q_PAD, Q_W), lambda b, g, pt: (b, 0, 0)),
        scratch_shapes=[pltpu.VMEM((N_KV, rows, sel_cols), BF16), pltpu.VMEM((N_KV, rows, HEAD_DIM), F32),
                        pltpu.VMEM((N_KV, rows, HEAD_DIM), F32), pltpu.VMEM((N_KV, rows, HEAD_DIM), F32),
                        pltpu.VMEM((N_KV, rows, HEAD_DIM), F32)])
    return pl.pallas_call(
        body, grid_spec=gs, out_shape=jax.ShapeDtypeStruct((DB, Q_PAD, Q_W), BF16),
        compiler_params=_cp(("parallel", "arbitrary")), name="nsa_sample",
    )(ptab, q, kcmp, vcmp, ovl, *([pool_k] * G), *([pool_v] * G), expand, kvnew, win_k, win_v, gates)


def _rg_gates(u, wa_ref, ba, wx_ref, bx, sp):
    ra = []
    rx = []
    for n in range(RG_BLOCKS):
        ub = u[:, n * RG_BW:(n + 1) * RG_BW]
        ra.append(_dotf(ub, wa_ref[n]))
        rx.append(_dotf(ub, wx_ref[n]))
    r = _sigmoid(jnp.concatenate(ra, axis=1) + ba)
    i = _sigmoid(jnp.concatenate(rx, axis=1) + bx)
    log_a = -RG_C * r * sp
    a = jnp.exp(log_a)
    b = jnp.sqrt(1.0 - jnp.exp(2.0 * log_a)) * i * u
    return a, b


def _softplus_neg(lam):
    return jnp.maximum(-lam, 0.0) + jnp.log(1.0 + jnp.exp(-jnp.abs(lam)))


def _rglru_prompt(z, cw, cb, wa, ba, wx, bx, lam, B, T, tt):
    keep = 8

    def body(xr_ref, gr_ref, cw_ref, cb_ref, wa_ref, ba_ref, wx_ref, bx_ref, lam_ref,
             y_ref, hl_ref, a_scr, b_scr, h_scr, tail_scr):
        t = pl.program_id(1)

        @pl.when(t == 0)
        def _():
            h_scr[...] = jnp.zeros_like(h_scr)
            tail_scr[...] = jnp.zeros_like(tail_scr)

        x = xr_ref[0]
        row = lax.broadcasted_iota(jnp.int32, (tt, D_RNN), 0)
        cwv = cw_ref[...]
        u = cb_ref[...] + x * cwv[RG_CONV - 1:RG_CONV]
        for d in range(1, RG_CONV):
            xs = pltpu.roll(x, d, 0)
            for r in range(d):
                xs = jnp.where(row == r, tail_scr[keep - d + r:keep - d + r + 1, :], xs)
            u = u + xs * cwv[RG_CONV - 1 - d:RG_CONV - d]
        tail_scr[...] = x[tt - keep:tt]
        a, b = _rg_gates(u, wa_ref, ba_ref[...], wx_ref, bx_ref[...], _softplus_neg(lam_ref[...]))
        a_scr[...] = a
        b_scr[...] = b

        def step(s, h):
            h = a_scr[pl.ds(s, 1), :] * h + b_scr[pl.ds(s, 1), :]
            b_scr[pl.ds(s, 1), :] = h
            return h

        h = lax.fori_loop(0, tt, step, h_scr[0:1, :], unroll=8)
        h_scr[0:1, :] = h
        hl_ref[0] = h
        y_ref[0] = (b_scr[...] * _gelu(gr_ref[0])).astype(BF16)

    vec = pl.BlockSpec((1, D_RNN), lambda b, t: (0, 0))
    wspec = pl.BlockSpec((RG_BLOCKS, RG_BW, RG_BW), lambda b, t: (0, 0, 0))
    return pl.pallas_call(
        body, grid=(B, T // tt),
        in_specs=[pl.BlockSpec((1, tt, D_RNN), lambda b, t: (b, t, COL_XR // D_RNN)),
                  pl.BlockSpec((1, tt, D_RNN), lambda b, t: (b, t, COL_GR // D_RNN)),
                  pl.BlockSpec((RG_CONV, D_RNN), lambda b, t: (0, 0)), vec, wspec, vec, wspec, vec, vec],
        out_specs=[pl.BlockSpec((1, tt, D_RNN), lambda b, t: (b, t, 0)),
                   pl.BlockSpec((1, 1, D_RNN), lambda b, t: (b, 0, 0))],
        out_shape=[jax.ShapeDtypeStruct((B, T, D_RNN), BF16), jax.ShapeDtypeStruct((B, 1, D_RNN), F32)],
        scratch_shapes=[pltpu.VMEM((tt, D_RNN), F32), pltpu.VMEM((tt, D_RNN), F32),
                        pltpu.VMEM((8, D_RNN), F32), pltpu.VMEM((keep, D_RNN), F32)],
        compiler_params=_cp(("parallel", "arbitrary")), name="rglru_prompt",
    )(z, z, cw, cb.reshape(1, -1), wa, ba.reshape(1, -1), wx, bx.reshape(1, -1), lam.reshape(1, -1))


def _rglru_sample(xr, gr, buf, h0, cw, cb, wa, ba, wx, bx, lam):
    DB = h0.shape[0]

    def body(xr_ref, gr_ref, buf_ref, h0_ref, cw_ref, cb_ref, wa_ref, ba_ref, wx_ref, bx_ref, lam_ref, y_ref, hl_ref):
        xp = [buf_ref[k] for k in range(RG_CONV - 1)] + [xr_ref[t] for t in range(DEC_SEQ)]
        cwv = cw_ref[...]
        sp = _softplus_neg(lam_ref[...])
        h = h0_ref[...]
        for t in range(DEC_SEQ):
            u = cb_ref[...]
            for k in range(RG_CONV):
                u = u + xp[t + k] * cwv[k:k + 1]
            a, b = _rg_gates(u, wa_ref, ba_ref[...], wx_ref, bx_ref[...], sp)
            h = a * h + b
            y_ref[t] = (h * _gelu(gr_ref[t])).astype(BF16)
        hl_ref[...] = h

    return pl.pallas_call(
        body, out_shape=[jax.ShapeDtypeStruct((DEC_SEQ, DB, D_RNN), BF16), jax.ShapeDtypeStruct((DB, D_RNN), F32)],
        compiler_params=pltpu.CompilerParams(vmem_limit_bytes=32 * 1024 * 1024), name="rglru_sample",
    )(xr, gr, buf, h0, cw, cb.reshape(1, -1), wa, ba.reshape(1, -1), wx, bx.reshape(1, -1), lam.reshape(1, -1))


def _s5_discretize(lam_re, lam_im, log_dt, b_re_t, b_im_t):
    def body(lr_ref, li_ref, ldt_ref, br_ref, bi_ref, ar_ref, ai_ref, bbr_ref, bbi_ref):
        lr = lr_ref[...]
        li = li_ref[...]
        dt = jnp.exp(ldt_ref[...])
        mag = jnp.exp(lr * dt)
        ar = mag * jnp.cos(li * dt)
        ai = mag * jnp.sin(li * dt)
        den = lr * lr + li * li
        n_re = ar - 1.0
        f_re = (n_re * lr + ai * li) / den
        f_im = (ai * lr - n_re * li) / den
        ar_ref[...] = ar
        ai_ref[...] = ai
        for c in range(S5_CH):
            bbr_ref[c] = f_re * br_ref[c] - f_im * bi_ref[c]
            bbi_ref[c] = f_re * bi_ref[c] + f_im * br_ref[c]

    gp = jax.ShapeDtypeStruct(lam_re.shape, F32)
    cgp = jax.ShapeDtypeStruct(b_re_t.shape, F32)
    return pl.pallas_call(body, out_shape=[gp, gp, cgp, cgp], name="s5_discretize")(
        lam_re, lam_im, log_dt.reshape(-1, 1), b_re_t, b_im_t)


def _s5_block_weights(bbr, bbi, c_re, c_im):
    eye = jnp.eye(S5_GB, dtype=F32)

    def bblk(bb):
        x = bb.reshape(S5_CH, S5_NGB, S5_GB, S5_STATE)
        x = jnp.einsum("cngp,gh->ngchp", x, eye)
        return x.reshape(S5_NGB, S5_GB * S5_CH, S5_LANES)

    def cblk(c):
        x = c.reshape(S5_NGB, S5_GB, S5_CH, S5_STATE)
        x = jnp.einsum("ngcp,gh->ngphc", x, eye)
        return x.reshape(S5_NGB, S5_LANES, S5_GB * S5_CH)

    return jnp.concatenate([bblk(bbr), bblk(bbi)], axis=2), cblk(c_re), cblk(c_im)


def _s5_prompt(u, bw, cr, ci, ar, ai, d, B, T):
    L = S5_LANES
    lanes = S5_GB * S5_CH
    nseg = S5_NSEG
    sl = T // nseg
    assert sl & (sl - 1) == 0, "segment length must be a power of two"
    chs = min(64, sl)
    ch = chs * nseg
    nchunks = sl // chs

    def body(u_ref, bw_ref, cr_ref, ci_ref, ar_ref, ai_ref, d_ref, zz_ref, sre_ref, sim_ref, h_scr, t_scr):
        a_re = ar_ref[0]
        a_im = ai_ref[0]
        ar8 = jnp.broadcast_to(a_re, (nseg, L))
        ai8 = jnp.broadcast_to(a_im, (nseg, L))

        def load_u(s0):
            for seg in range(nseg):
                t_scr[pl.ds(seg, chs, stride=nseg), :] = u_ref[0, seg, pl.ds(s0, chs), :]
            return t_scr[...]

        def project(c, carry):
            s0 = pl.multiple_of(c * chs, chs)
            h_scr[pl.ds(pl.multiple_of(c * ch, ch), ch), :] = _dotb(load_u(s0), bw_ref[0])
            return carry

        lax.fori_loop(0, nchunks, project, 0)

        def advance(s, carry, store):
            hr, hi = carry
            rows = pl.ds(pl.multiple_of(s * nseg, nseg), nseg)
            bu = h_scr[rows, :]
            nr = ar8 * hr - ai8 * hi + bu[:, :L]
            ni = ar8 * hi + ai8 * hr + bu[:, L:]
            if store:
                h_scr[rows, :] = jnp.concatenate([nr, ni], axis=1)
            return nr, ni

        zero = jnp.zeros((nseg, L), F32)
        er, ei = lax.fori_loop(0, sl, functools.partial(advance, store=False), (zero, zero), unroll=8)

        pr, pi = a_re, a_im
        for _ in range(sl.bit_length() - 1):
            pr, pi = pr * pr - pi * pi, 2.0 * pr * pi
        sr = [jnp.zeros((1, L), F32)]
        si = [jnp.zeros((1, L), F32)]
        for k in range(nseg):
            sr.append(er[k:k + 1] + pr * sr[k] - pi * si[k])
            si.append(ei[k:k + 1] + pr * si[k] + pi * sr[k])
        sre_ref[0, 0] = sr[nseg]
        sim_ref[0, 0] = si[nseg]
        start = (jnp.concatenate(sr[:nseg], axis=0), jnp.concatenate(si[:nseg], axis=0))
        lax.fori_loop(0, sl, functools.partial(advance, store=True), start, unroll=8)

        def emit(c, carry):
            s0 = pl.multiple_of(c * chs, chs)
            hc = h_scr[pl.ds(pl.multiple_of(c * ch, ch), ch), :]
            uc = load_u(s0)
            y = _dotb(hc[:, :L], cr_ref[0]) - _dotb(hc[:, L:], ci_ref[0]) + d_ref[...] * uc
            t_scr[...] = _gelu(y)
            for seg in range(nseg):
                zz_ref[0, seg, pl.ds(s0, chs), :] = t_scr[pl.ds(seg, chs, stride=nseg), :].astype(BF16)
            return carry

        lax.fori_loop(0, nchunks, emit, 0)

    st_spec = pl.BlockSpec((1, 1, 1, L), lambda b, g: (b, g, 0, 0))
    st_shape = jax.ShapeDtypeStruct((B, S5_NGB, 1, L), F32)
    io_spec = pl.BlockSpec((1, nseg, sl, lanes), lambda b, g: (b, 0, 0, g))
    return pl.pallas_call(
        body, grid=(B, S5_NGB),
        in_specs=[io_spec,
                  pl.BlockSpec((1, lanes, 2 * L), lambda b, g: (g, 0, 0)),
                  pl.BlockSpec((1, L, lanes), lambda b, g: (g, 0, 0)),
                  pl.BlockSpec((1, L, lanes), lambda b, g: (g, 0, 0)),
                  pl.BlockSpec((1, 1, L), lambda b, g: (g, 0, 0)),
                  pl.BlockSpec((1, 1, L), lambda b, g: (g, 0, 0)),
                  pl.BlockSpec((1, lanes), lambda b, g: (0, g))],
        out_specs=[io_spec, st_spec, st_spec],
        out_shape=[jax.ShapeDtypeStruct((B, nseg, sl, D_MODEL), BF16), st_shape, st_shape],
        scratch_shapes=[pltpu.VMEM((T, 2 * L), F32), pltpu.VMEM((ch, lanes), F32)],
        compiler_params=_cp(("parallel", "parallel")), name="s5_prompt",
    )(u, bw, cr, ci, ar, ai, d.reshape(1, -1))


def _s5_sample(u, h0r, h0i, bw, cr, ci, ar, ai, d):
    DB = u.shape[1]
    L = S5_LANES
    lanes = S5_GB * S5_CH

    def body(u_ref, hr_ref, hi_ref, bw_ref, cr_ref, ci_ref, ar_ref, ai_ref, d_ref, zz_ref, sre_ref, sim_ref):
        hr = hr_ref[...]
        hi = hi_ref[...]
        a_re = ar_ref[0]
        a_im = ai_ref[0]
        for t in range(DEC_SEQ):
            uv = u_ref[t]
            bu = _dotf(uv, bw_ref[0])
            hr, hi = a_re * hr - a_im * hi + bu[:, :L], a_re * hi + a_im * hr + bu[:, L:]
            y = _dotf(hr, cr_ref[0]) - _dotf(hi, ci_ref[0]) + d_ref[...] * uv
            zz_ref[t] = _gelu(y).astype(BF16)
        sre_ref[...] = hr
        sim_ref[...] = hi

    st_spec = pl.BlockSpec((DB, L), lambda g: (0, g))
    st_shape = jax.ShapeDtypeStruct((DB, S5_GROUPS * S5_STATE), F32)
    return pl.pallas_call(
        body, grid=(S5_NGB,),
        in_specs=[pl.BlockSpec((DEC_SEQ, DB, lanes), lambda g: (0, 0, g)), st_spec, st_spec,
                  pl.BlockSpec((1, lanes, 2 * L), lambda g: (g, 0, 0)),
                  pl.BlockSpec((1, L, lanes), lambda g: (g, 0, 0)),
                  pl.BlockSpec((1, L, lanes), lambda g: (g, 0, 0)),
                  pl.BlockSpec((1, 1, L), lambda g: (g, 0, 0)),
                  pl.BlockSpec((1, 1, L), lambda g: (g, 0, 0)),
                  pl.BlockSpec((1, lanes), lambda g: (0, g))],
        out_specs=[pl.BlockSpec((DEC_SEQ, DB, lanes), lambda g: (0, 0, g)), st_spec, st_spec],
        out_shape=[jax.ShapeDtypeStruct((DEC_SEQ, DB, D_MODEL), BF16), st_shape, st_shape],
        compiler_params=_cp(("parallel",)), name="s5_sample",
    )(u, h0r, h0i, bw, cr, ci, ar, ai, d.reshape(1, -1))


def _ffn_upgate_prompt(x, g, w_up, w_gate, cw, cb, T, tm, tn):
    M, K = x.shape
    (w_up, wu_spec), (w_gate, wg_spec) = _weight_spec(w_up, tn), _weight_spec(w_gate, tn)
    N = w_up.shape[-1]
    tiles_per_seq = T // tm
    pr = 16

    def body(x_ref, xp_ref, g_ref, wu_ref, wg_ref, cw_ref, cb_ref, act_ref, tail_ref, a_scr):
        i = pl.program_id(0)

        @pl.when(pl.program_id(1) == 0)
        def _():
            a_scr[...] = _rms_rows(x_ref[...], g_ref[...]).astype(BF16)

        a = a_scr[...]
        hu = jnp.dot(a, wu_ref[...], preferred_element_type=F32)
        hg = jnp.dot(a, wg_ref[...], preferred_element_type=F32)
        ap = _rms_rows(xp_ref[...], g_ref[...]).astype(BF16)
        prev = jnp.dot(ap, wu_ref[...], preferred_element_type=F32)
        prev = jnp.where(i % tiles_per_seq == 0, 0.0, prev)
        row = lax.broadcasted_iota(jnp.int32, hu.shape, 0)
        h1 = jnp.where(row == 0, prev[pr - 1:pr], pltpu.roll(hu, 1, 0))
        h2 = jnp.where(row == 0, prev[pr - 2:pr - 1], jnp.where(row == 1, prev[pr - 1:pr], pltpu.roll(hu, 2, 0)))
        cwv = cw_ref[...]
        hc = cb_ref[...] + h2 * cwv[0:1] + h1 * cwv[1:2] + hu * cwv[2:3]
        act_ref[...] = (_gelu(hc) * hg).astype(BF16)
        tail_ref[...] = hu[tm - 8:tm]

    return pl.pallas_call(
        body, grid=(M // tm, N // tn),
        in_specs=[pl.BlockSpec((tm, K), lambda i, j: (i, 0)),
                  pl.BlockSpec((pr, K), lambda i, j: (jnp.maximum(i * (tm // pr) - 1, 0), 0)),
                  pl.BlockSpec((1, K), lambda i, j: (0, 0)),
                  wu_spec, wg_spec,
                  pl.BlockSpec((FFN_CONV, tn), lambda i, j: (0, j)),
                  pl.BlockSpec((1, tn), lambda i, j: (0, j))],
        out_specs=[pl.BlockSpec((tm, tn), lambda i, j: (i, j)), pl.BlockSpec((8, tn), lambda i, j: (i, j))],
        out_shape=[jax.ShapeDtypeStruct((M, N), BF16), jax.ShapeDtypeStruct((M // tm * 8, N), F32)],
        scratch_shapes=[pltpu.VMEM((tm, K), BF16)],
        compiler_params=_cp(("parallel", "arbitrary")), name="ffn_upgate_prompt",
    )(x, x, g.reshape(1, K), w_up, w_gate, cw, cb.reshape(1, -1))


def _tile(m, pref):
    return pref if m % pref == 0 else m


def _dense_tail(x1, p, wts, ffn_state, T):
    (norm_ffn, norm_ple, w_up, w_gate, conv_w, conv_b, w_down, w_proj, w_pgate) = wts
    M = x1.shape[0]
    tm = _tile(M, 1024)
    tn = 512
    prompt = ffn_state is None

    if prompt:
        act, tails = _ffn_upgate_prompt(x1, norm_ffn, w_up, w_gate, conv_w, conv_b, T, tm, tn)
        tails = tails.reshape(M // tm, 8, D_FF)
        tiles_per_seq = T // tm
        new_buf = tails[tiles_per_seq - 1::tiles_per_seq, 8 - (FFN_CONV - 1):, :]
    else:
        nseq = M // T
        h1 = jnp.repeat(ffn_state[:, FFN_CONV - 2], T, axis=0)
        h2 = jnp.stack([ffn_state[:, 0], ffn_state[:, 1]] + [ffn_state[:, 1]] * (T - 2), axis=1).reshape(M, D_FF)

        def conv_epilogue(accs, ex):
            hu, hg = accs
            b1, b2, cwv, cbv = ex
            tok = lax.broadcasted_iota(jnp.int32, hu.shape, 0) % T
            s1 = jnp.where(tok >= 1, pltpu.roll(hu, 1, 0), b1)
            s2 = jnp.where(tok >= 2, pltpu.roll(hu, 2, 0), b2)
            hc = cbv + s2 * cwv[0:1] + s1 * cwv[1:2] + hu * cwv[2:3]
            return _gelu(hc) * hg, hu

        act, hu = _mm([x1], [w_up, w_gate], [0, 0],
                      [(h1, "tile"), (h2, "tile"), (conv_w, "rows"), (conv_b.reshape(1, -1), "rows")],
                      conv_epilogue, [BF16, F32], tm, tn, "ffn_upgate_sample", norm=(0, norm_ffn))
        new_buf = hu.reshape(nseq, T, D_FF)[:, T - (FFN_CONV - 1):]

    (x2,) = _mm([act], [w_down], [0], [(x1, "tile")], lambda accs, ex: (ex[0] + accs[0],), [F32],
                tm, tn, "ffn_down", vmem_mb=56)
    (x3,) = _mm([p, x2], [w_proj, w_pgate], [0, 1], [(x2, "tile")],
                lambda accs, ex: (ex[0] + accs[0] * _sigmoid(accs[1]),), [F32], tm, tn, "ple", norm=(1, norm_ple))
    return x3, new_buf


def kernel(x_prompt, x_sample, cache_cmp_k, cache_cmp_v, cache_sel_k, cache_sel_v, cache_win_k, cache_win_v, state_rglru_h, state_rglru_conv, state_s5_re, state_s5_im, state_ffn_conv, page_table, p_prompt, p_sample, norm_mix, norm_ffn, norm_ple, w_in_even, w_out_even, q_norm, k_norm_cmp, k_norm_sel, k_norm_win, cmp_pe_k, cmp_w1_k, cmp_w2_k, cmp_pe_v, cmp_w1_v, cmp_w2_v, rg_conv_w, rg_conv_b, rg_w_a, rg_b_a, rg_w_x, rg_b_x, rg_lam, s5_lam_re, s5_lam_im, s5_log_dt, s5_b_re, s5_b_im, s5_c_re, s5_c_im, s5_d, s5_w_glu_a, s5_w_glu_b, ffn_w_up, ffn_w_gate, ffn_conv_w, ffn_conv_b, ffn_w_down, ple_w_proj, ple_w_gate):
    B, T, _ = x_prompt.shape
    DB = x_sample.shape[0]
    n_pages = page_table.shape[1]
    past_len = n_pages * PAGE_SIZE
    w_buf = cache_win_k.shape[2]
    MP = B * T
    MS = DB * DEC_SEQ
    depth = norm_mix.shape[0]

    def reorder_in(w):
        q, kv, gt, xr, gr = (w[:, :Q_W], w[:, Q_W:Q_W + 6 * KV_W], w[:, Q_W + 6 * KV_W:Q_W + 6 * KV_W + 3 * N_HEADS],
                             w[:, Q_W + 6 * KV_W + 3 * N_HEADS:Q_W + 6 * KV_W + 3 * N_HEADS + D_RNN],
                             w[:, Q_W + 6 * KV_W + 3 * N_HEADS + D_RNN:])
        pad = jnp.zeros((w.shape[0], IN_PAD - COL_GT - 3 * N_HEADS), w.dtype)
        return jnp.concatenate([q, xr, gr, kv, gt, pad], axis=1).astype(BF16)

    stacked = [w.astype(BF16) for w in (ffn_w_up, ffn_w_gate, ffn_w_down, ple_w_proj, ple_w_gate)]

    def layer_dense_weights(li):
        up, gate, down, proj, pgate = [(w, li) for w in stacked]
        return (norm_ffn[li], norm_ple[li], up, gate, ffn_conv_w[li], ffn_conv_b[li], down, proj, pgate)

    xs = {"p": x_prompt.reshape(MP, D_MODEL), "s": x_sample.reshape(MS, D_MODEL)}
    pe_in = {"p": p_prompt.reshape(depth, MP, -1).astype(BF16), "s": p_sample.reshape(depth, MS, -1).astype(BF16)}
    seq_len = {"p": T, "s": DEC_SEQ}
    ev = {"p": [], "s": []}
    od = {"p": [], "s": []}
    ff = {"p": [], "s": []}
    ptab = page_table.reshape(-1).astype(jnp.int32)

    for li in range(depth):
        dense_w = layer_dense_weights(li)
        if li % 2 == 0:
            e = li // 2
            w_in = reorder_in(w_in_even[e])
            w_out = w_out_even[e].astype(BF16)
            w_out_a, w_out_r = w_out[:Q_W], w_out[Q_W:]
            for grp in ("p", "s"):
                x = xs[grp]
                M = x.shape[0]
                L = seq_len[grp]
                nseq = M // L
                (z,) = _mm([x], [w_in], [0], [], lambda accs, ex: (accs[0],), [F32], _tile(M, 1024), 512, "in_proj",
                           norm=(0, norm_mix[li]))
                if grp == "p":
                    cos_t, sin_t = _rope_tables(jnp.arange(T))
                    tmq = _tile(T, 512)
                    tab_blocks = T // tmq
                else:
                    cos_t, sin_t = _rope_tables(jnp.tile(past_len + jnp.arange(DEC_SEQ), DB))
                    tmq = M
                    tab_blocks = 1
                q, kvb, kc, vc, ks, vs, kw, vw = _qk_prep(z, cos_t, sin_t, tab_blocks, q_norm[e], k_norm_sel[e],
                                                          k_norm_win[e], tmq)
                z3 = z.reshape(nseq, L, IN_PAD)

                if grp == "p":
                    n_rows = T // CMP_STRIDE
                    ccos, csin = _rope_tables(jnp.arange(n_rows) * CMP_STRIDE + CMP_STRIDE - 1)
                    ident = jnp.arange(B * (T // PAGE_SIZE), dtype=jnp.int32)
                    kcmp = _compress(kc.reshape(-1, N_KV * PAGE_SIZE, HEAD_DIM), ident, B, T // PAGE_SIZE, cmp_pe_k[e],
                                     cmp_w1_k[e], cmp_w2_k[e], k_norm_cmp[e], ccos, csin, True, "compress_k_prompt")
                    vcmp = _compress(vc.reshape(-1, N_KV * PAGE_SIZE, HEAD_DIM), ident, B, T // PAGE_SIZE, cmp_pe_v[e],
                                     cmp_w1_v[e], cmp_w2_v[e], k_norm_cmp[e], ccos, csin, False, "compress_v_prompt")
                    n_sel = T // SEL_BLOCK
                    nqb = T // Q_BLOCK
                    ovl_t = _overlap_matrix(n_rows, n_sel, n_sel).T
                    ex_t = ((np.arange(T // SEL_KT)[:, None, None] * SEL_KT + np.arange(SEL_KT)[None, :, None])
                            // SEL_BLOCK == np.arange(n_sel)[None, None, :])
                    vs_t = kvb.reshape(B, T // SEL_KT, SEL_KT, 4, N_KV, HEAD_DIM)[:, :, :, 1].transpose(0, 3, 1, 4, 2)
                    vw_t = kvb.reshape(B, nqb, Q_BLOCK, 4, N_KV, HEAD_DIM)[:, :, :, 3].transpose(0, 3, 1, 4, 2)
                    gates_t = z[:, COL_GT:COL_GT + 3 * N_HEADS].reshape(B, T, 3, N_KV, GROUP)
                    gates_t = gates_t.transpose(0, 3, 2, 4, 1).reshape(B, N_KV, 3 * GROUP, T)
                    o_nsa = _nsa_prompt(q.reshape(B, T, Q_W), kcmp, vcmp.swapaxes(2, 3), kvb.reshape(B, T, 4 * KV_W),
                                        vs_t, vw_t, gates_t, ovl_t, jnp.asarray(ex_t, BF16), B, T).reshape(M, Q_W)
                    y_rg, h_last = _rglru_prompt(z.reshape(B, T, IN_PAD), rg_conv_w[e], rg_conv_b[e], rg_w_a[e],
                                                 rg_b_a[e], rg_w_x[e], rg_b_x[e], rg_lam[e], B, T, _tile(T, 512))
                    y_rg = y_rg.reshape(M, D_RNN)
                    h_last = h_last.reshape(B, D_RNN)
                    new_conv = z3[:, T - (RG_CONV - 1):, COL_XR:COL_XR + D_RNN]
                    new_wk = kw.reshape(B, T, N_KV, HEAD_DIM)
                    new_wv = vw.reshape(B, T, N_KV, HEAD_DIM)
                    if T >= w_buf:
                        new_wk, new_wv = new_wk[:, T - w_buf:], new_wv[:, T - w_buf:]
                    else:
                        padw = ((0, 0), (w_buf - T, 0), (0, 0), (0, 0))
                        new_wk, new_wv = jnp.pad(new_wk, padw), jnp.pad(new_wv, padw)
                else:
                    n_rows = past_len // CMP_STRIDE
                    ccos, csin = _rope_tables(jnp.arange(n_rows) * CMP_STRIDE + CMP_STRIDE - 1)
                    ptab_e = ptab + e * cache_cmp_k.shape[1]
                    kcmp = _compress(cache_cmp_k.reshape(-1, N_KV * PAGE_SIZE, HEAD_DIM), ptab_e, DB, n_pages, cmp_pe_k[e],
                                     cmp_w1_k[e], cmp_w2_k[e], k_norm_cmp[e], ccos, csin, True, "compress_k_sample")
                    vcmp = _compress(cache_cmp_v.reshape(-1, N_KV * PAGE_SIZE, HEAD_DIM), ptab_e, DB, n_pages, cmp_pe_v[e],
                                     cmp_w1_v[e], cmp_w2_v[e], k_norm_cmp[e], ccos, csin, False, "compress_v_sample")
                    n_sel = -(-(past_len + DEC_SEQ) // SEL_BLOCK)
                    sel_cols = -(-n_sel // 128) * 128
                    ovl = _overlap_matrix(n_rows, n_sel, sel_cols)
                    keys = _pages_per_step(n_pages, SEL_PAGES_PER_STEP) * PAGE_SIZE
                    steps = n_pages * PAGE_SIZE // keys
                    ex = (np.arange(sel_cols)[None, :, None]
                          == (np.arange(steps)[:, None, None] * keys + np.arange(keys)[None, None, :]) // SEL_BLOCK)
                    q8 = jnp.pad(q.reshape(DB, DEC_SEQ, Q_W), ((0, 0), (0, Q_PAD - DEC_SEQ), (0, 0)))
                    kvnew = jnp.pad(kvb.reshape(DB, DEC_SEQ, 4 * KV_W), ((0, 0), (0, 128 - DEC_SEQ), (0, 0)))
                    gt = z3[:, :, COL_GT:COL_GT + 3 * N_HEADS].reshape(DB, DEC_SEQ, 3, N_KV, GROUP)
                    gt = gt.transpose(0, 3, 1, 2, 4).reshape(DB, N_KV, DEC_SEQ, 3 * GROUP)
                    gt8 = jnp.pad(gt, ((0, 0), (0, 0), (0, Q_PAD - DEC_SEQ), (0, 0)))
                    o8 = _nsa_sample(q8, kcmp, vcmp, cache_sel_k.reshape(-1, N_KV * PAGE_SIZE, HEAD_DIM),
                                     cache_sel_v.reshape(-1, N_KV * PAGE_SIZE, HEAD_DIM), ptab_e, kvnew,
                                     cache_win_k.reshape(-1, N_KV * w_buf, HEAD_DIM),
                                     cache_win_v.reshape(-1, N_KV * w_buf, HEAD_DIM),
                                     e * DB, gt8, ovl, jnp.asarray(ex, BF16), DB, past_len)
                    o_nsa = o8[:, :DEC_SEQ].reshape(M, Q_W)
                    xr3 = z3[:, :, COL_XR:COL_XR + D_RNN]
                    gr3 = z3[:, :, COL_GR:COL_GR + D_RNN]
                    y_t, h_last = _rglru_sample(xr3.transpose(1, 0, 2), gr3.transpose(1, 0, 2),
                                                state_rglru_conv[e].transpose(1, 0, 2), state_rglru_h[e],
                                                rg_conv_w[e], rg_conv_b[e], rg_w_a[e], rg_b_a[e], rg_w_x[e],
                                                rg_b_x[e], rg_lam[e])
                    y_rg = y_t.transpose(1, 0, 2).reshape(M, D_RNN)
                    new_conv = jnp.concatenate([state_rglru_conv[e], xr3], axis=1)[:, -(RG_CONV - 1):]
                    kk = jnp.concatenate([cache_win_k[e], kw.reshape(DB, DEC_SEQ, N_KV, HEAD_DIM)], axis=1)
                    vv = jnp.concatenate([cache_win_v[e], vw.reshape(DB, DEC_SEQ, N_KV, HEAD_DIM)], axis=1)
                    new_wk, new_wv = kk[:, -w_buf:], vv[:, -w_buf:]

                rs = lambda a: a.reshape(nseq, L, N_KV, HEAD_DIM)
                ev[grp].append((rs(kc), rs(vc), rs(ks), rs(vs), new_wk, new_wv, h_last, new_conv))
                (x1,) = _mm([o_nsa, y_rg], [w_out_a, w_out_r], [0, 1], [(x, "tile")],
                            lambda accs, exs: (exs[0] + accs[0] + accs[1],), [F32], _tile(M, 1024), 512, "mixer_out")
                xs[grp], nb = _dense_tail(x1, pe_in[grp][li], dense_w, None if grp == "p" else state_ffn_conv[li], L)
                ff[grp].append(nb)
        else:
            o = li // 2
            ar, ai, bbr, bbi = _s5_discretize(s5_lam_re[o], s5_lam_im[o], s5_log_dt[o],
                                              s5_b_re[o].transpose(2, 0, 1), s5_b_im[o].transpose(2, 0, 1))
            bw, cr, ci = _s5_block_weights(bbr, bbi, s5_c_re[o], s5_c_im[o])
            ar = ar.reshape(S5_NGB, 1, S5_LANES)
            ai = ai.reshape(S5_NGB, 1, S5_LANES)
            w_a = s5_w_glu_a[o].astype(BF16)
            w_b = s5_w_glu_b[o].astype(BF16)
            for grp in ("p", "s"):
                x = xs[grp]
                M = x.shape[0]
                L = seq_len[grp]
                glu = lambda accs, exs: (exs[0] + accs[0] * _sigmoid(accs[1]),)
                if grp == "p":
                    sl = T // S5_NSEG
                    hn = _rmsnorm(x, norm_mix[li], F32, _tile(M, 256))
                    zz, sre, sim = _s5_prompt(hn.reshape(B, S5_NSEG, sl, D_MODEL), bw.astype(BF16), cr.astype(BF16),
                                              ci.astype(BF16), ar, ai, s5_d[o], B, T)
                    sre = sre.reshape(B, S5_GROUPS, S5_STATE)
                    sim = sim.reshape(B, S5_GROUPS, S5_STATE)
                    (x1,) = _mm([zz.reshape(M, D_MODEL)], [w_a, w_b], [0, 0], [(x, "tile")], glu, [F32],
                                _tile(M, 1024), 512, "mixer_out")
                else:
                    hn = _rmsnorm(x, norm_mix[li], F32, _tile(M, 256))
                    u = hn.reshape(DB, DEC_SEQ, D_MODEL).transpose(1, 0, 2)
                    zz, sre, sim = _s5_sample(u, state_s5_re[o].reshape(DB, -1), state_s5_im[o].reshape(DB, -1),
                                              bw, cr, ci, ar, ai, s5_d[o])
                    zz = zz.transpose(1, 0, 2).reshape(M, D_MODEL)
                    sre = sre.reshape(DB, S5_GROUPS, S5_STATE)
                    sim = sim.reshape(DB, S5_GROUPS, S5_STATE)
                    (x1,) = _mm([zz], [w_a, w_b], [0, 0], [(x, "tile")], glu, [F32], M, 512, "mixer_out")
                od[grp].append((sre, sim))
                xs[grp], nb = _dense_tail(x1, pe_in[grp][li], dense_w, None if grp == "p" else state_ffn_conv[li], L)
                ff[grp].append(nb)

    def stk(states, i):
        return jnp.stack([s[i] for s in states])

    outs = [xs["p"].reshape(B, T, D_MODEL), xs["s"].reshape(DB, DEC_SEQ, D_MODEL)]
    for i in range(8):
        outs += [stk(ev["p"], i), stk(ev["s"], i)]
    for i in range(2):
        outs += [stk(od["p"], i), stk(od["s"], i)]
    outs += [jnp.stack(ff["p"]), jnp.stack(ff["s"])]
    return tuple(outs)
```

```python
import functools

import numpy as np
import jax
import jax.numpy as jnp
from jax import lax
from jax.experimental import pallas as pl
from jax.experimental.pallas import tpu as pltpu

F32 = jnp.float32
BF16 = jnp.bfloat16

D_MODEL = 2048
DEC_SEQ = 4
PAGE_SIZE = 128
N_HEADS = 8
HEAD_DIM = 128
N_KV = 2
GROUP = N_HEADS // N_KV
ROPE_DIM = HEAD_DIM // 4
ROPE_HALF = ROPE_DIM // 2
ROPE_THETA = 500000.0
CMP_BLOCK = 32
CMP_STRIDE = 16
SEL_BLOCK = 64
SEL_TOPK = 16
N_LOCAL = 2
WINDOW = 512
Q_BLOCK = 128
FORCED_SCORE = 1e9
NEG_BIG = -1e30
ATTN_SCALE = HEAD_DIM ** -0.5
D_RNN = D_MODEL // 2
RG_BLOCKS = 8
RG_BW = D_RNN // RG_BLOCKS
RG_CONV = 4
RG_C = 8.0
S5_CH = 16
S5_GROUPS = D_MODEL // S5_CH
S5_STATE = 64
S5_GB = 8
S5_NGB = S5_GROUPS // S5_GB
S5_LANES = S5_GB * S5_STATE
S5_NSEG = 8
D_FF = ((8 * D_MODEL // 3 + 255) // 256) * 256
FFN_CONV = 3
NORM_EPS = 1e-6
Q_W = N_HEADS * HEAD_DIM
KV_W = N_KV * HEAD_DIM
MIN_PAGES_PER_STEP = 8
CMP_PAGES_PER_STEP = 32
SEL_PAGES_PER_STEP = 16
Q_PAD = 8
CMP_SPLIT = 4
SEL_KT = 512

COL_Q = 0
COL_XR = Q_W
COL_GR = Q_W + D_RNN
COL_KV = Q_W + 2 * D_RNN
COL_GT = COL_KV + 6 * KV_W
IN_PAD = 5120


def _pages_per_step(n_pages, preferred):
    g = preferred
    while g > MIN_PAGES_PER_STEP and n_pages % g:
        g //= 2
    return g


def _cp(sem, vmem_mb=48):
    return pltpu.CompilerParams(dimension_semantics=sem, vmem_limit_bytes=vmem_mb * 1024 * 1024)


def _gelu(x):
    return 0.5 * x * (1.0 + jnp.tanh(0.7978845608028654 * (x + 0.044715 * (x * x * x))))


def _sigmoid(x):
    return 1.0 / (1.0 + jnp.exp(-x))


def _dotb(a, b):
    return jnp.dot(a.astype(BF16), b.astype(BF16), preferred_element_type=F32)


def _dotf(a, b):
    return jnp.dot(a, b, preferred_element_type=F32, precision=lax.Precision.HIGHEST)


def _dot_nt(a, b):
    return lax.dot_general(a, b, (((1,), (1,)), ((), ())), preferred_element_type=F32)


def _masked_softmax(s, mask):
    s = jnp.where(mask, s, NEG_BIG)
    m = jnp.max(s, axis=-1, keepdims=True)
    e = jnp.where(mask, jnp.exp(s - m), 0.0)
    return e / jnp.maximum(jnp.sum(e, axis=-1, keepdims=True), 1e-30)


def _norm_rope(x, g, cos_t, sin_t):
    ms = jnp.mean(x * x, axis=-1, keepdims=True)
    y = x * lax.rsqrt(ms + NORM_EPS) * g
    lane = lax.broadcasted_iota(jnp.int32, y.shape, 1)
    swapped = jnp.where(lane < ROPE_HALF, pltpu.roll(y, HEAD_DIM - ROPE_HALF, 1), pltpu.roll(y, ROPE_HALF, 1))
    return y * cos_t + swapped * sin_t


def _rope_tables(pos):
    inv = ROPE_THETA ** (-jnp.arange(ROPE_HALF, dtype=F32) * 2.0 / ROPE_DIM)
    ang = pos.astype(F32)[:, None] * inv
    cos = jnp.cos(ang)
    sin = jnp.sin(ang)
    n = pos.shape[0]
    ones = jnp.ones((n, HEAD_DIM - ROPE_DIM), F32)
    zeros = jnp.zeros((n, HEAD_DIM - ROPE_DIM), F32)
    return (jnp.concatenate([cos, cos, ones], axis=1), jnp.concatenate([-sin, sin, zeros], axis=1))


def _overlap_matrix(n_rows, n_sel, n_cols):
    n = np.arange(n_rows)[:, None] - 1
    j = np.arange(n_cols)[None]
    c0 = n * CMP_STRIDE
    s0 = j * SEL_BLOCK
    ov = np.minimum(c0 + CMP_BLOCK, s0 + SEL_BLOCK) - np.maximum(c0, s0)
    ov = np.maximum(ov, 0).astype(np.float32) / CMP_BLOCK
    ov = np.where((n >= 0) & (j < n_sel), ov, 0.0)
    return jnp.asarray(ov, F32)


def _rmsnorm(x, g, out_dtype, tm):
    M, D = x.shape

    def body(x_ref, g_ref, o_ref):
        xv = x_ref[...]
        ms = jnp.mean(xv * xv, axis=-1, keepdims=True)
        o_ref[...] = (xv * lax.rsqrt(ms + NORM_EPS) * g_ref[...]).astype(o_ref.dtype)

    return pl.pallas_call(
        body, grid=(M // tm,),
        in_specs=[pl.BlockSpec((tm, D), lambda i: (i, 0)), pl.BlockSpec((1, D), lambda i: (0, 0))],
        out_specs=pl.BlockSpec((tm, D), lambda i: (i, 0)),
        out_shape=jax.ShapeDtypeStruct((M, D), out_dtype),
        compiler_params=_cp(("parallel",)), name="rmsnorm")(x, g.reshape(1, D))


def _rms_rows(xv, g):
    ms = jnp.mean(xv * xv, axis=-1, keepdims=True)
    return xv * lax.rsqrt(ms + NORM_EPS) * g


def _weight_spec(w, tn):
    if isinstance(w, tuple):
        arr, layer = w
        return arr, pl.BlockSpec((None, arr.shape[1], tn), lambda i, j: (layer, 0, j))
    return w, pl.BlockSpec((w.shape[0], tn), lambda i, j: (0, j))


def _mm(a_list, w_list, pair, extras, epilogue, out_dtypes, tm, tn, name, norm=None, vmem_mb=48):
    w_arrays, w_specs = zip(*[_weight_spec(w, tn) for w in w_list])
    N = w_arrays[0].shape[-1]
    na, nw, ne, no = len(a_list), len(w_list), len(extras), len(out_dtypes)
    M = a_list[0].shape[0]

    def body(*refs):
        if norm is not None:
            g_ref = refs[na + nw + ne]
            an_ref = refs[na + nw + ne + 1 + no]

            @pl.when(pl.program_id(1) == 0)
            def _():
                an_ref[...] = _rms_rows(refs[norm[0]][...], g_ref[...]).astype(BF16)

        a_vals = []
        for k, r in enumerate(refs[:na]):
            if norm is not None and k == norm[0]:
                a_vals.append(an_ref[...])
            else:
                a_vals.append(r[...].astype(BF16))
        accs = [jnp.dot(a_vals[pair[i]], refs[na + i][...], preferred_element_type=F32) for i in range(nw)]
        ex = [r[...] for r in refs[na + nw:na + nw + ne]]
        res = epilogue(accs, ex)
        first_out = na + nw + ne + (1 if norm is not None else 0)
        for o_ref, v in zip(refs[first_out:first_out + no], res):
            o_ref[...] = v.astype(o_ref.dtype)

    in_specs = [pl.BlockSpec((tm, a.shape[1]), lambda i, j: (i, 0)) for a in a_list]
    in_specs += list(w_specs)
    for arr, kind in extras:
        if kind == "tile":
            in_specs.append(pl.BlockSpec((tm, tn), lambda i, j: (i, j)))
        else:
            in_specs.append(pl.BlockSpec((arr.shape[0], tn), lambda i, j: (0, j)))
    operands = [*a_list, *w_arrays, *[e[0] for e in extras]]
    scratch = []
    if norm is not None:
        kn = a_list[norm[0]].shape[1]
        in_specs.append(pl.BlockSpec((1, kn), lambda i, j: (0, 0)))
        operands.append(norm[1].reshape(1, kn))
        scratch.append(pltpu.VMEM((tm, kn), BF16))
    return pl.pallas_call(
        body, grid=(M // tm, N // tn), in_specs=in_specs,
        out_specs=[pl.BlockSpec((tm, tn), lambda i, j: (i, j)) for _ in out_dtypes],
        out_shape=[jax.ShapeDtypeStruct((M, N), dt) for dt in out_dtypes],
        scratch_shapes=scratch,
        compiler_params=_cp(("parallel", "arbitrary" if norm is not None else "parallel"), vmem_mb), name=name,
    )(*operands)


def _qk_prep(z, cos_t, sin_t, tab_blocks, qn, kns, knw, tm):
    M = z.shape[0]

    def body(zq_ref, zkv_ref, c_ref, s_ref, qn_ref, kns_ref, knw_ref, q_ref, kvb_ref, *row_refs):
        c = c_ref[...]
        s = s_ref[...]
        for h in range(N_HEADS):
            sl = slice(h * HEAD_DIM, (h + 1) * HEAD_DIM)
            q_ref[:, sl] = _norm_rope(zq_ref[:, sl], qn_ref[...], c, s).astype(BF16)
        for kv in range(N_KV):
            vals = [zkv_ref[:, (k * N_KV + kv) * HEAD_DIM:(k * N_KV + kv + 1) * HEAD_DIM] for k in range(6)]
            vals[2] = _norm_rope(vals[2], kns_ref[...], c, s)
            vals[4] = _norm_rope(vals[4], knw_ref[...], c, s)
            for k in range(6):
                row_refs[k][pl.ds(kv, tm, stride=N_KV), :] = vals[k]
            for k in range(2, 6):
                kvb_ref[:, ((k - 2) * N_KV + kv) * HEAD_DIM:((k - 2) * N_KV + kv + 1) * HEAD_DIM] = vals[k].astype(BF16)

    vec = pl.BlockSpec((1, HEAD_DIM), lambda i: (0, 0))
    tab = pl.BlockSpec((tm, HEAD_DIM), lambda i: (i % tab_blocks, 0))
    rows_spec = pl.BlockSpec((N_KV * tm, HEAD_DIM), lambda i: (i, 0))
    rows_shape = jax.ShapeDtypeStruct((N_KV * M, HEAD_DIM), F32)
    return pl.pallas_call(
        body, grid=(M // tm,),
        in_specs=[pl.BlockSpec((tm, Q_W), lambda i: (i, COL_Q // Q_W)),
                  pl.BlockSpec((tm, 6 * KV_W), lambda i: (i, COL_KV // (6 * KV_W))),
                  tab, tab, vec, vec, vec],
        out_specs=[pl.BlockSpec((tm, Q_W), lambda i: (i, 0)), pl.BlockSpec((tm, 4 * KV_W), lambda i: (i, 0))]
        + [rows_spec] * 6,
        out_shape=[jax.ShapeDtypeStruct((M, Q_W), BF16), jax.ShapeDtypeStruct((M, 4 * KV_W), BF16)] + [rows_shape] * 6,
        compiler_params=_cp(("parallel",)), name="qk_prep",
    )(z, z, cos_t, sin_t, qn.reshape(1, -1), kns.reshape(1, -1), knw.reshape(1, -1))


def _compress_bias(pe, w1):
    def body(pe_ref, w1_ref, o_ref):
        for half in range(2):
            acc = jnp.zeros((8, HEAD_DIM), F32)
            for c in range(half * CMP_STRIDE, (half + 1) * CMP_STRIDE):
                acc = acc + _dotf(jnp.broadcast_to(pe_ref[c:c + 1, :], (8, HEAD_DIM)), w1_ref[c])
            o_ref[:, half * HEAD_DIM:(half + 1) * HEAD_DIM] = acc[0:1, :]

    return pl.pallas_call(body, out_shape=jax.ShapeDtypeStruct((1, 2 * HEAD_DIM), F32), name="compress_bias")(pe, w1)


def _compress(pages, ptab, n_seq, pages_per_seq, pe, w1, w2, norm_g, cos_t, sin_t, do_norm, name):
    bias = _compress_bias(pe, w1)
    w1cat = jnp.concatenate([w1[:CMP_STRIDE], w1[CMP_STRIDE:]], axis=2).astype(BF16)
    G = _pages_per_step(pages_per_seq, CMP_PAGES_PER_STEP)
    steps = pages_per_seq // G
    cpp = PAGE_SIZE // CMP_STRIDE
    ch = G * cpp
    n_chunk = pages_per_seq * cpp

    def body(pt_ref, *refs):
        page_refs = refs[:G]
        bias_ref, w1_ref, w2_ref, g_ref, c_ref, s_ref, o_ref, carry_ref, split_ref = refs[G:]
        step = pl.program_id(1)

        @pl.when(step == 0)
        def _():
            carry_ref[...] = jnp.zeros_like(carry_ref)

        for g, pr in enumerate(page_refs):
            for j in range(CMP_SPLIT):
                split_ref[g, j] = pr[0, pl.ds(j, N_KV * PAGE_SIZE // CMP_SPLIT, stride=CMP_SPLIT), :]

        row = lax.broadcasted_iota(jnp.int32, (ch, HEAD_DIM), 0)
        hop = N_KV * CMP_STRIDE // CMP_SPLIT
        acc2 = jnp.zeros((N_KV * ch, 2 * HEAD_DIM), F32)
        for c in range(CMP_STRIDE):
            pieces = []
            for kv in range(N_KV):
                j = (N_KV * c + kv) % CMP_SPLIT
                start = (N_KV * c + kv) // CMP_SPLIT
                pieces += [split_ref[g, j, pl.ds(start, cpp, stride=hop), :] for g in range(G)]
            acc2 = acc2 + _dotb(jnp.concatenate(pieces, axis=0), w1_ref[c])
        acc2 = acc2 + bias_ref[...]
        for kv in range(N_KV):
            lo = acc2[kv * ch:(kv + 1) * ch, :HEAD_DIM]
            hi = acc2[kv * ch:(kv + 1) * ch, HEAD_DIM:]
            lo_prev = jnp.where(row == 0, carry_ref[kv, 0:1, :], pltpu.roll(lo, 1, 0))
            carry_ref[kv, 0:1, :] = lo[ch - 1:ch, :]
            out = _dotb(_gelu(lo_prev + hi), w2_ref[...])
            if do_norm:
                out = _norm_rope(out, g_ref[...], c_ref[...], s_ref[...])
            o_ref[0, kv] = out.astype(BF16)

    def page_map(j, s, g, pt):
        return (pt[s * pages_per_seq + g * G + j], 0, 0)

    full2 = lambda s, g, pt: (0, 0)
    in_specs = [pl.BlockSpec((1, N_KV * PAGE_SIZE, HEAD_DIM), functools.partial(page_map, j)) for j in range(G)]
    in_specs += [pl.BlockSpec((1, 2 * HEAD_DIM), full2),
                 pl.BlockSpec((CMP_STRIDE, HEAD_DIM, 2 * HEAD_DIM), lambda s, g, pt: (0, 0, 0)),
                 pl.BlockSpec((HEAD_DIM, HEAD_DIM), full2),
                 pl.BlockSpec((1, HEAD_DIM), full2),
                 pl.BlockSpec((ch, HEAD_DIM), lambda s, g, pt: (g, 0)),
                 pl.BlockSpec((ch, HEAD_DIM), lambda s, g, pt: (g, 0))]
    gs = pltpu.PrefetchScalarGridSpec(
        num_scalar_prefetch=1, grid=(n_seq, steps), in_specs=in_specs,
        out_specs=pl.BlockSpec((1, N_KV, ch, HEAD_DIM), lambda s, g, pt: (s, 0, g, 0)),
        scratch_shapes=[pltpu.VMEM((N_KV, 8, HEAD_DIM), F32),
                        pltpu.VMEM((G, CMP_SPLIT, N_KV * PAGE_SIZE // CMP_SPLIT, HEAD_DIM), F32)])
    return pl.pallas_call(
        body, grid_spec=gs, out_shape=jax.ShapeDtypeStruct((n_seq, N_KV, n_chunk, HEAD_DIM), BF16),
        compiler_params=_cp(("parallel", "arbitrary")), name=name,
    )(ptab, *([pages] * G), bias, w1cat, w2.astype(BF16), norm_g.reshape(1, -1), cos_t, sin_t)


def _nsa_prompt(q, kcmp, vcmp_t, kvb, vs_t, vw_t, gates_t, ovl_t, expand_t, B, T):
    nqb = T // Q_BLOCK
    n_cmp_rows = T // CMP_STRIDE
    n_sel = T // SEL_BLOCK
    cols = GROUP * Q_BLOCK
    win_tiles = WINDOW // Q_BLOCK
    win_keys = (win_tiles + 1) * Q_BLOCK
    assert T % SEL_KT == 0 and T >= win_keys
    topk = min(SEL_TOPK, n_sel)

    def body(q_ref, kc_ref, vc_ref, ks_ref, vs_ref, kw_ref, vw_ref, gt_ref, ovl_ref, ex_ref, o_ref):
        i = pl.program_id(2)
        q4 = q_ref[0].astype(F32)
        qt = jnp.concatenate([q4[:, h * HEAD_DIM:(h + 1) * HEAD_DIM].T for h in range(GROUP)], axis=1).astype(BF16)
        pos = i * Q_BLOCK + (lax.broadcasted_iota(jnp.int32, (1, cols), 1) & (Q_BLOCK - 1))

        s = jnp.dot(kc_ref[0, 0], qt, preferred_element_type=F32) * ATTN_SCALE
        crow = lax.broadcasted_iota(jnp.int32, (n_cmp_rows, cols), 0)
        cmask = (crow >= 1) & (crow * CMP_STRIDE + (CMP_BLOCK - 1 - CMP_STRIDE) <= pos)
        s = jnp.where(cmask, s, NEG_BIG)
        e = jnp.where(cmask, jnp.exp(s - jnp.max(s, axis=0, keepdims=True)), 0.0)
        p = e * (1.0 / jnp.maximum(jnp.sum(e, axis=0, keepdims=True), 1e-30))
        o_cmp = _dotb(vc_ref[0, 0], p)
        imp = p[:, 0:Q_BLOCK]
        for h in range(1, GROUP):
            imp = imp + p[:, h * Q_BLOCK:(h + 1) * Q_BLOCK]

        score = _dotf(ovl_ref[...], imp)
        j = lax.broadcasted_iota(jnp.int32, (n_sel, Q_BLOCK), 0)
        qblk = (i * Q_BLOCK + lax.broadcasted_iota(jnp.int32, (n_sel, Q_BLOCK), 1)) // SEL_BLOCK
        forced = (j == 0) | ((j <= qblk) & (j > qblk - N_LOCAL))
        score = jnp.where(forced, FORCED_SCORE, jnp.where(j > qblk, NEG_BIG, score))
        rank = jnp.zeros((n_sel, Q_BLOCK), F32)
        for r in range(n_sel):
            sr = score[r:r + 1, :]
            beats = (sr > score) | ((sr == score) & (j > r))
            rank = rank + jnp.where(beats, 1.0, 0.0)
        sel_bias = jnp.where(rank < topk, 0.0, NEG_BIG).astype(BF16)

        krow = lax.broadcasted_iota(jnp.int32, (SEL_KT, cols), 0)

        def sel_tile(u, carry, causal):
            m, l, acc = carry
            k = ks_ref[0, pl.ds(pl.multiple_of(u * SEL_KT, SEL_KT), SEL_KT), :]
            b1 = _dotb(ex_ref[u], sel_bias)
            sc = (jnp.dot(k, qt, preferred_element_type=F32) * ATTN_SCALE
                  + jnp.concatenate([b1] * GROUP, axis=1))
            if causal:
                sc = jnp.where(u * SEL_KT + krow <= pos, sc, NEG_BIG)
            m_new = jnp.maximum(m, jnp.max(sc, axis=0, keepdims=True))
            alpha = jnp.exp(m - m_new)
            ex = jnp.exp(sc - m_new)
            l = alpha * l + jnp.sum(ex, axis=0, keepdims=True)
            acc = alpha * acc + _dotb(vs_ref[0, 0, u], ex)
            return m_new, l, acc

        init = (jnp.full((1, cols), NEG_BIG, F32), jnp.zeros((1, cols), F32), jnp.zeros((HEAD_DIM, cols), F32))
        last = (i * Q_BLOCK + Q_BLOCK + SEL_KT - 1) // SEL_KT - 1
        carry = lax.fori_loop(0, last, functools.partial(sel_tile, causal=False), init)
        _, l_sel, acc_sel = sel_tile(last, carry, causal=True)
        o_sel = acc_sel * (1.0 / jnp.maximum(l_sel, 1e-30))

        t0 = jnp.maximum(i - win_tiles, 0)
        kwin = kw_ref[0, pl.ds(pl.multiple_of(t0 * Q_BLOCK, Q_BLOCK), win_keys), :]
        sw = jnp.dot(kwin, qt, preferred_element_type=F32) * ATTN_SCALE
        kpos = t0 * Q_BLOCK + lax.broadcasted_iota(jnp.int32, (win_keys, cols), 0)
        wmask = (kpos <= pos) & (kpos > pos - WINDOW)
        sw = jnp.where(wmask, sw, NEG_BIG)
        ew = jnp.exp(sw - jnp.max(sw, axis=0, keepdims=True))
        o_win = jnp.zeros((HEAD_DIM, cols), F32)
        for t in range(win_tiles + 1):
            o_win = o_win + _dotb(vw_ref[0, 0, t0 + t], ew[t * Q_BLOCK:(t + 1) * Q_BLOCK])
        o_win = o_win * (1.0 / jnp.maximum(jnp.sum(ew, axis=0, keepdims=True), 1e-30))

        g = _sigmoid(gt_ref[0, 0])
        for h in range(GROUP):
            c = slice(h * Q_BLOCK, (h + 1) * Q_BLOCK)
            o = (g[h:h + 1, :] * o_cmp[:, c] + g[GROUP + h:GROUP + h + 1, :] * o_sel[:, c]
                 + g[2 * GROUP + h:2 * GROUP + h + 1, :] * o_win[:, c])
            o_ref[0, :, h * HEAD_DIM:(h + 1) * HEAD_DIM] = o.T.astype(BF16)

    def k_spec(off):
        return pl.BlockSpec((1, T, HEAD_DIM), lambda b, kv, i: (b, 0, off + kv))

    vs_spec = pl.BlockSpec((1, 1, T // SEL_KT, HEAD_DIM, SEL_KT), lambda b, kv, i: (b, kv, 0, 0, 0))
    vw_spec = pl.BlockSpec((1, 1, nqb, HEAD_DIM, Q_BLOCK), lambda b, kv, i: (b, kv, 0, 0, 0))
    qo_spec = pl.BlockSpec((1, Q_BLOCK, GROUP * HEAD_DIM), lambda b, kv, i: (b, i, kv))
    return pl.pallas_call(
        body, grid=(B, N_KV, nqb),
        in_specs=[qo_spec,
                  pl.BlockSpec((1, 1, n_cmp_rows, HEAD_DIM), lambda b, kv, i: (b, kv, 0, 0)),
                  pl.BlockSpec((1, 1, HEAD_DIM, n_cmp_rows), lambda b, kv, i: (b, kv, 0, 0)),
                  k_spec(0), vs_spec, k_spec(4), vw_spec,
                  pl.BlockSpec((1, 1, 3 * GROUP, Q_BLOCK), lambda b, kv, i: (b, kv, 0, i)),
                  pl.BlockSpec((n_sel, n_cmp_rows), lambda b, kv, i: (0, 0)),
                  pl.BlockSpec((T // SEL_KT, SEL_KT, n_sel), lambda b, kv, i: (0, 0, 0))],
        out_specs=qo_spec,
        out_shape=jax.ShapeDtypeStruct((B, T, Q_W), BF16),
        compiler_params=_cp(("parallel", "parallel", "arbitrary")), name="nsa_prompt",
    )(q, kcmp, vcmp_t, kvb, vs_t, kvb, vw_t, gates_t, ovl_t, expand_t)


def _nsa_sample(q, kcmp, vcmp, pool_k, pool_v, ptab, kvnew, win_k, win_v, win_off, gates, ovl, expand, DB, past_len):
    n_pages = past_len // PAGE_SIZE
    G = _pages_per_step(n_pages, SEL_PAGES_PER_STEP)
    steps = n_pages // G
    n_cmp_rows = kcmp.shape[2]
    sel_cols = ovl.shape[1]
    n_past_blk = past_len // SEL_BLOCK
    n_sel = -(-(past_len + DEC_SEQ) // SEL_BLOCK)
    w_buf = win_k.shape[1] // N_KV
    rows = GROUP * Q_PAD
    keys = G * PAGE_SIZE
    new_rows = kvnew.shape[1]

    def body(pt_ref, q_ref, kc_ref, vc_ref, ovl_ref, *rest):
        pk = rest[:G]
        pv = rest[G:2 * G]
        (ex_ref, kvn_ref, wk_ref, wv_ref, gt_ref, o_ref,
         sel_scr, m_scr, l_scr, acc_scr, base_scr) = rest[2 * G:]
        step = pl.program_id(1)
        tok = lax.broadcasted_iota(jnp.int32, (rows, 1), 0) & (Q_PAD - 1)
        q8 = q_ref[0]

        def q_rows(kv):
            return jnp.concatenate(
                [q8[:, (kv * GROUP + h) * HEAD_DIM:(kv * GROUP + h + 1) * HEAD_DIM] for h in range(GROUP)], axis=0)

        def head_col(gt, c0):
            return jnp.concatenate(
                [jnp.broadcast_to(gt[:, c0 + h:c0 + h + 1], (Q_PAD, HEAD_DIM)) for h in range(GROUP)], axis=0)

        @pl.when(step == 0)
        def _():
            o_cmps = []
            imps = []
            for kv in range(N_KV):
                s = _dot_nt(q_rows(kv), kc_ref[0, kv]) * ATTN_SCALE
                col = lax.broadcasted_iota(jnp.int32, (rows, n_cmp_rows), 1)
                cmask = (col >= 1) & (col * CMP_STRIDE + (CMP_BLOCK - 1 - CMP_STRIDE) <= past_len + tok)
                p = _masked_softmax(s, cmask)
                o_cmps.append(_dotb(p, vc_ref[0, kv]))
                imp = p[0:Q_PAD]
                for h in range(1, GROUP):
                    imp = imp + p[h * Q_PAD:(h + 1) * Q_PAD]
                imps.append(imp)

            score = _dotf(jnp.concatenate(imps, axis=0), ovl_ref[...])
            j = lax.broadcasted_iota(jnp.int32, (N_KV * Q_PAD, sel_cols), 1)
            trow = lax.broadcasted_iota(jnp.int32, (N_KV * Q_PAD, sel_cols), 0) & (Q_PAD - 1)
            qblk = (past_len + trow) // SEL_BLOCK
            forced = (j == 0) | ((j <= qblk) & (j > qblk - N_LOCAL))
            score = jnp.where(forced, FORCED_SCORE, jnp.where(j > qblk, NEG_BIG, score))
            score = jnp.where(j < n_sel, score, -jnp.inf)
            rank = jnp.zeros((N_KV * Q_PAD, sel_cols), F32)
            for r in range(n_sel):
                sr = score[:, r:r + 1]
                rank = rank + jnp.where((sr > score) | ((sr == score) & (j > r)), 1.0, 0.0)
            sel_all = jnp.where((rank < min(SEL_TOPK, n_sel)) & (j < n_sel), 1.0, 0.0)

            for kv in range(N_KV):
                qh = q_rows(kv)
                o_cmp = o_cmps[kv]
                sel4 = jnp.concatenate([sel_all[kv * Q_PAD:(kv + 1) * Q_PAD]] * GROUP, axis=0)
                sel_scr[kv] = sel4.astype(BF16)

                kb = wk_ref[0, pl.ds(kv, w_buf, stride=N_KV), :].astype(BF16)
                vb = wv_ref[0, pl.ds(kv, w_buf, stride=N_KV), :].astype(BF16)
                kn = kvn_ref[0, :, (4 + kv) * HEAD_DIM:(5 + kv) * HEAD_DIM]
                vn = kvn_ref[0, :, (6 + kv) * HEAD_DIM:(7 + kv) * HEAD_DIM]
                s1 = jnp.where(lax.broadcasted_iota(jnp.int32, (rows, w_buf), 1) - w_buf > tok - WINDOW,
                               _dot_nt(qh, kb) * ATTN_SCALE, NEG_BIG)
                ncol = lax.broadcasted_iota(jnp.int32, (rows, new_rows), 1)
                nmask = (ncol <= tok) & (ncol < DEC_SEQ)
                s2 = jnp.where(nmask, _dot_nt(qh, kn) * ATTN_SCALE, NEG_BIG)
                m = jnp.maximum(jnp.max(s1, axis=-1, keepdims=True), jnp.max(s2, axis=-1, keepdims=True))
                e1 = jnp.where(s1 > 0.5 * NEG_BIG, jnp.exp(s1 - m), 0.0)
                e2 = jnp.where(nmask, jnp.exp(s2 - m), 0.0)
                den = jnp.sum(e1, axis=-1, keepdims=True) + jnp.sum(e2, axis=-1, keepdims=True)
                o_win = (_dotb(e1, vb) + _dotb(e2, vn)) / jnp.maximum(den, 1e-30)

                gt = _sigmoid(gt_ref[0, kv])
                base_scr[kv] = head_col(gt, 0) * o_cmp + head_col(gt, 2 * GROUP) * o_win

                ksn = kvn_ref[0, :, kv * HEAD_DIM:(kv + 1) * HEAD_DIM]
                vsn = kvn_ref[0, :, (2 + kv) * HEAD_DIM:(3 + kv) * HEAD_DIM]
                chosen = sel4[:, n_past_blk:n_past_blk + 1] > 0.5
                smask = nmask & chosen
                s3 = jnp.where(smask, _dot_nt(qh, ksn) * ATTN_SCALE, NEG_BIG)
                m0 = jnp.max(s3, axis=-1, keepdims=True)
                e3 = jnp.where(smask, jnp.exp(s3 - m0), 0.0)
                m_scr[kv] = jnp.broadcast_to(m0, (rows, HEAD_DIM))
                l_scr[kv] = jnp.broadcast_to(jnp.sum(e3, axis=-1, keepdims=True), (rows, HEAD_DIM))
                acc_scr[kv] = _dotb(e3, vsn)

        for kv in range(N_KV):
            qh = q_rows(kv)
            k = jnp.concatenate([r[0, pl.ds(kv, PAGE_SIZE, stride=N_KV), :] for r in pk], axis=0).astype(BF16)
            v = jnp.concatenate([r[0, pl.ds(kv, PAGE_SIZE, stride=N_KV), :] for r in pv], axis=0).astype(BF16)
            mask = _dotb(sel_scr[kv], ex_ref[0]) > 0.5
            sc = jnp.where(mask, _dot_nt(qh, k) * ATTN_SCALE, NEG_BIG)
            m_old = m_scr[kv][:, 0:1]
            l_old = l_scr[kv][:, 0:1]
            m_new = jnp.maximum(m_old, jnp.max(sc, axis=-1, keepdims=True))
            alpha = jnp.exp(m_old - m_new)
            e = jnp.where(mask, jnp.exp(sc - m_new), 0.0)
            l_new = alpha * l_old + jnp.sum(e, axis=-1, keepdims=True)
            acc = alpha * acc_scr[kv] + _dotb(e, v)
            m_scr[kv] = jnp.broadcast_to(m_new, (rows, HEAD_DIM))
            l_scr[kv] = jnp.broadcast_to(l_new, (rows, HEAD_DIM))
            acc_scr[kv] = acc

        @pl.when(step == steps - 1)
        def _():
            for kv in range(N_KV):
                gt = _sigmoid(gt_ref[0, kv])
                o_sel = acc_scr[kv] / jnp.maximum(l_scr[kv], 1e-30)
                o = base_scr[kv] + head_col(gt, GROUP) * o_sel
                for h in range(GROUP):
                    hh = kv * GROUP + h
                    o_ref[0, :, hh * HEAD_DIM:(hh + 1) * HEAD_DIM] = o[h * Q_PAD:(h + 1) * Q_PAD].astype(BF16)

    def page_map(jj, b, g, pt):
        return (pt[b * n_pages + g * G + jj], 0, 0)

    page_specs = [pl.BlockSpec((1, N_KV * PAGE_SIZE, HEAD_DIM), functools.partial(page_map, jj)) for jj in range(G)]
    cmp_spec = pl.BlockSpec((1, N_KV, n_cmp_rows, HEAD_DIM), lambda b, g, pt: (b, 0, 0, 0))
    in_specs = [pl.BlockSpec((1, Q_PAD, Q_W), lambda b, g, pt: (b, 0, 0)), cmp_spec, cmp_spec,
                pl.BlockSpec((n_cmp_rows, sel_cols), lambda b, g, pt: (0, 0))]
    in_specs += page_specs + page_specs
    in_specs += [pl.BlockSpec((1, sel_cols, keys), lambda b, g, pt: (g, 0, 0)),
                 pl.BlockSpec((1, new_rows, 4 * KV_W), lambda b, g, pt: (b, 0, 0)),
                 pl.BlockSpec((1, N_KV * w_buf, HEAD_DIM), lambda b, g, pt: (win_off + b, 0, 0)),
                 pl.BlockSpec((1, N_KV * w_buf, HEAD_DIM), lambda b, g, pt: (win_off + b, 0, 0)),
                 pl.BlockSpec((1, N_KV, Q_PAD, 3 * GROUP), lambda b, g, pt: (b, 0, 0, 0))]
    gs = pltpu.PrefetchScalarGridSpec(
        num_scalar_prefetch=1, grid=(DB, steps), in_specs=in_specs,
        out_specs=pl.BlockSpec((1, Q_PAD, Q_W), lambda b, g, pt: (b, 0, 0)),
        scratch_shapes=[pltpu.VMEM((N_KV, rows, sel_cols), BF16), pltpu.VMEM((N_KV, rows, HEAD_DIM), F32),
                        pltpu.VMEM((N_KV, rows, HEAD_DIM), F32), pltpu.VMEM((N_KV, rows, HEAD_DIM), F32),
                        pltpu.VMEM((N_KV, rows, HEAD_DIM), F32)])
    return pl.pallas_call(
        body, grid_spec=gs, out_shape=jax.ShapeDtypeStruct((DB, Q_PAD, Q_W), BF16),
        compiler_params=_cp(("parallel", "arbitrary")), name="nsa_sample",
    )(ptab, q, kcmp, vcmp, ovl, *([pool_k] * G), *([pool_v] * G), expand, kvnew, win_k, win_v, gates)


def _rg_gates(u, wa_ref, ba, wx_ref, bx, sp):
    ra = []
    rx = []
    for n in range(RG_BLOCKS):
        ub = u[:, n * RG_BW:(n + 1) * RG_BW]
        ra.append(_dotf(ub, wa_ref[n]))
        rx.append(_dotf(ub, wx_ref[n]))
    r = _sigmoid(jnp.concatenate(ra, axis=1) + ba)
    i = _sigmoid(jnp.concatenate(rx, axis=1) + bx)
    log_a = -RG_C * r * sp
    a = jnp.exp(log_a)
    b = jnp.sqrt(1.0 - jnp.exp(2.0 * log_a)) * i * u
    return a, b


def _softplus_neg(lam):
    return jnp.maximum(-lam, 0.0) + jnp.log(1.0 + jnp.exp(-jnp.abs(lam)))


def _rglru_prompt(z, cw, cb, wa, ba, wx, bx, lam, B, T, tt):
    keep = 8

    def body(xr_ref, gr_ref, cw_ref, cb_ref, wa_ref, ba_ref, wx_ref, bx_ref, lam_ref,
             y_ref, hl_ref, a_scr, b_scr, h_scr, tail_scr):
        t = pl.program_id(1)

        @pl.when(t == 0)
        def _():
            h_scr[...] = jnp.zeros_like(h_scr)
            tail_scr[...] = jnp.zeros_like(tail_scr)

        x = xr_ref[0]
        row = lax.broadcasted_iota(jnp.int32, (tt, D_RNN), 0)
        cwv = cw_ref[...]
        u = cb_ref[...] + x * cwv[RG_CONV - 1:RG_CONV]
        for d in range(1, RG_CONV):
            xs = pltpu.roll(x, d, 0)
            for r in range(d):
                xs = jnp.where(row == r, tail_scr[keep - d + r:keep - d + r + 1, :], xs)
            u = u + xs * cwv[RG_CONV - 1 - d:RG_CONV - d]
        tail_scr[...] = x[tt - keep:tt]
        a, b = _rg_gates(u, wa_ref, ba_ref[...], wx_ref, bx_ref[...], _softplus_neg(lam_ref[...]))
        a_scr[...] = a
        b_scr[...] = b

        def step(s, h):
            h = a_scr[pl.ds(s, 1), :] * h + b_scr[pl.ds(s, 1), :]
            b_scr[pl.ds(s, 1), :] = h
            return h

        h = lax.fori_loop(0, tt, step, h_scr[0:1, :], unroll=8)
        h_scr[0:1, :] = h
        hl_ref[0] = h
        y_ref[0] = (b_scr[...] * _gelu(gr_ref[0])).astype(BF16)

    vec = pl.BlockSpec((1, D_RNN), lambda b, t: (0, 0))
    wspec = pl.BlockSpec((RG_BLOCKS, RG_BW, RG_BW), lambda b, t: (0, 0, 0))
    return pl.pallas_call(
        body, grid=(B, T // tt),
        in_specs=[pl.BlockSpec((1, tt, D_RNN), lambda b, t: (b, t, COL_XR // D_RNN)),
                  pl.BlockSpec((1, tt, D_RNN), lambda b, t: (b, t, COL_GR // D_RNN)),
                  pl.BlockSpec((RG_CONV, D_RNN), lambda b, t: (0, 0)), vec, wspec, vec, wspec, vec, vec],
        out_specs=[pl.BlockSpec((1, tt, D_RNN), lambda b, t: (b, t, 0)),
                   pl.BlockSpec((1, 1, D_RNN), lambda b, t: (b, 0, 0))],
        out_shape=[jax.ShapeDtypeStruct((B, T, D_RNN), BF16), jax.ShapeDtypeStruct((B, 1, D_RNN), F32)],
        scratch_shapes=[pltpu.VMEM((tt, D_RNN), F32), pltpu.VMEM((tt, D_RNN), F32),
                        pltpu.VMEM((8, D_RNN), F32), pltpu.VMEM((keep, D_RNN), F32)],
        compiler_params=_cp(("parallel", "arbitrary")), name="rglru_prompt",
    )(z, z, cw, cb.reshape(1, -1), wa, ba.reshape(1, -1), wx, bx.reshape(1, -1), lam.reshape(1, -1))


def _rglru_sample(xr, gr, buf, h0, cw, cb, wa, ba, wx, bx, lam):
    DB = h0.shape[0]

    def body(xr_ref, gr_ref, buf_ref, h0_ref, cw_ref, cb_ref, wa_ref, ba_ref, wx_ref, bx_ref, lam_ref, y_ref, hl_ref):
        xp = [buf_ref[k] for k in range(RG_CONV - 1)] + [xr_ref[t] for t in range(DEC_SEQ)]
        cwv = cw_ref[...]
        sp = _softplus_neg(lam_ref[...])
        h = h0_ref[...]
        for t in range(DEC_SEQ):
            u = cb_ref[...]
            for k in range(RG_CONV):
                u = u + xp[t + k] * cwv[k:k + 1]
            a, b = _rg_gates(u, wa_ref, ba_ref[...], wx_ref, bx_ref[...], sp)
            h = a * h + b
            y_ref[t] = (h * _gelu(gr_ref[t])).astype(BF16)
        hl_ref[...] = h

    return pl.pallas_call(
        body, out_shape=[jax.ShapeDtypeStruct((DEC_SEQ, DB, D_RNN), BF16), jax.ShapeDtypeStruct((DB, D_RNN), F32)],
        compiler_params=pltpu.CompilerParams(vmem_limit_bytes=32 * 1024 * 1024), name="rglru_sample",
    )(xr, gr, buf, h0, cw, cb.reshape(1, -1), wa, ba.reshape(1, -1), wx, bx.reshape(1, -1), lam.reshape(1, -1))


def _s5_discretize(lam_re, lam_im, log_dt, b_re_t, b_im_t):
    def body(lr_ref, li_ref, ldt_ref, br_ref, bi_ref, ar_ref, ai_ref, bbr_ref, bbi_ref):
        lr = lr_ref[...]
        li = li_ref[...]
        dt = jnp.exp(ldt_ref[...])
        mag = jnp.exp(lr * dt)
        ar = mag * jnp.cos(li * dt)
        ai = mag * jnp.sin(li * dt)
        den = lr * lr + li * li
        n_re = ar - 1.0
        f_re = (n_re * lr + ai * li) / den
        f_im = (ai * lr - n_re * li) / den
        ar_ref[...] = ar
        ai_ref[...] = ai
        for c in range(S5_CH):
            bbr_ref[c] = f_re * br_ref[c] - f_im * bi_ref[c]
            bbi_ref[c] = f_re * bi_ref[c] + f_im * br_ref[c]

    gp = jax.ShapeDtypeStruct(lam_re.shape, F32)
    cgp = jax.ShapeDtypeStruct(b_re_t.shape, F32)
    return pl.pallas_call(body, out_shape=[gp, gp, cgp, cgp], name="s5_discretize")(
        lam_re, lam_im, log_dt.reshape(-1, 1), b_re_t, b_im_t)


def _s5_block_weights(bbr, bbi, c_re, c_im):
    eye = jnp.eye(S5_GB, dtype=F32)

    def bblk(bb):
        x = bb.reshape(S5_CH, S5_NGB, S5_GB, S5_STATE)
        x = jnp.einsum("cngp,gh->ngchp", x, eye)
        return x.reshape(S5_NGB, S5_GB * S5_CH, S5_LANES)

    def cblk(c):
        x = c.reshape(S5_NGB, S5_GB, S5_CH, S5_STATE)
        x = jnp.einsum("ngcp,gh->ngphc", x, eye)
        return x.reshape(S5_NGB, S5_LANES, S5_GB * S5_CH)

    return jnp.concatenate([bblk(bbr), bblk(bbi)], axis=2), cblk(c_re), cblk(c_im)


def _s5_prompt(u, bw, cr, ci, ar, ai, d, B, T):
    L = S5_LANES
    lanes = S5_GB * S5_CH
    nseg = S5_NSEG
    sl = T // nseg
    assert sl & (sl - 1) == 0, "segment length must be a power of two"
    chs = min(64, sl)
    ch = chs * nseg
    nchunks = sl // chs

    def body(u_ref, bw_ref, cr_ref, ci_ref, ar_ref, ai_ref, d_ref, zz_ref, sre_ref, sim_ref, h_scr, t_scr):
        a_re = ar_ref[0]
        a_im = ai_ref[0]
        ar8 = jnp.broadcast_to(a_re, (nseg, L))
        ai8 = jnp.broadcast_to(a_im, (nseg, L))

        def load_u(s0):
            for seg in range(nseg):
                t_scr[pl.ds(seg, chs, stride=nseg), :] = u_ref[0, seg, pl.ds(s0, chs), :]
            return t_scr[...]

        def project(c, carry):
            s0 = pl.multiple_of(c * chs, chs)
            h_scr[pl.ds(pl.multiple_of(c * ch, ch), ch), :] = _dotb(load_u(s0), bw_ref[0])
            return carry

        lax.fori_loop(0, nchunks, project, 0)

        def advance(s, carry, store):
            hr, hi = carry
            rows = pl.ds(pl.multiple_of(s * nseg, nseg), nseg)
            bu = h_scr[rows, :]
            nr = ar8 * hr - ai8 * hi + bu[:, :L]
            ni = ar8 * hi + ai8 * hr + bu[:, L:]
            if store:
                h_scr[rows, :] = jnp.concatenate([nr, ni], axis=1)
            return nr, ni

        zero = jnp.zeros((nseg, L), F32)
        er, ei = lax.fori_loop(0, sl, functools.partial(advance, store=False), (zero, zero), unroll=8)

        pr, pi = a_re, a_im
        for _ in range(sl.bit_length() - 1):
            pr, pi = pr * pr - pi * pi, 2.0 * pr * pi
        sr = [jnp.zeros((1, L), F32)]
        si = [jnp.zeros((1, L), F32)]
        for k in range(nseg):
            sr.append(er[k:k + 1] + pr * sr[k] - pi * si[k])
            si.append(ei[k:k + 1] + pr * si[k] + pi * sr[k])
        sre_ref[0, 0] = sr[nseg]
        sim_ref[0, 0] = si[nseg]
        start = (jnp.concatenate(sr[:nseg], axis=0), jnp.concatenate(si[:nseg], axis=0))
        lax.fori_loop(0, sl, functools.partial(advance, store=True), start, unroll=8)

        def emit(c, carry):
            s0 = pl.multiple_of(c * chs, chs)
            hc = h_scr[pl.ds(pl.multiple_of(c * ch, ch), ch), :]
            uc = load_u(s0)
            y = _dotb(hc[:, :L], cr_ref[0]) - _dotb(hc[:, L:], ci_ref[0]) + d_ref[...] * uc
            t_scr[...] = _gelu(y)
            for seg in range(nseg):
                zz_ref[0, seg, pl.ds(s0, chs), :] = t_scr[pl.ds(seg, chs, stride=nseg), :].astype(BF16)
            return carry

        lax.fori_loop(0, nchunks, emit, 0)

    st_spec = pl.BlockSpec((1, 1, 1, L), lambda b, g: (b, g, 0, 0))
    st_shape = jax.ShapeDtypeStruct((B, S5_NGB, 1, L), F32)
    io_spec = pl.BlockSpec((1, nseg, sl, lanes), lambda b, g: (b, 0, 0, g))
    return pl.pallas_call(
        body, grid=(B, S5_NGB),
        in_specs=[io_spec,
                  pl.BlockSpec((1, lanes, 2 * L), lambda b, g: (g, 0, 0)),
                  pl.BlockSpec((1, L, lanes), lambda b, g: (g, 0, 0)),
                  pl.BlockSpec((1, L, lanes), lambda b, g: (g, 0, 0)),
                  pl.BlockSpec((1, 1, L), lambda b, g: (g, 0, 0)),
                  pl.BlockSpec((1, 1, L), lambda b, g: (g, 0, 0)),
                  pl.BlockSpec((1, lanes), lambda b, g: (0, g))],
        out_specs=[io_spec, st_spec, st_spec],
        out_shape=[jax.ShapeDtypeStruct((B, nseg, sl, D_MODEL), BF16), st_shape, st_shape],
        scratch_shapes=[pltpu.VMEM((T, 2 * L), F32), pltpu.VMEM((ch, lanes), F32)],
        compiler_params=_cp(("parallel", "parallel")), name="s5_prompt",
    )(u, bw, cr, ci, ar, ai, d.reshape(1, -1))


def _s5_sample(u, h0r, h0i, bw, cr, ci, ar, ai, d):
    DB = u.shape[1]
    L = S5_LANES
    lanes = S5_GB * S5_CH

    def body(u_ref, hr_ref, hi_ref, bw_ref, cr_ref, ci_ref, ar_ref, ai_ref, d_ref, zz_ref, sre_ref, sim_ref):
        hr = hr_ref[...]
        hi = hi_ref[...]
        a_re = ar_ref[0]
        a_im = ai_ref[0]
        for t in range(DEC_SEQ):
            uv = u_ref[t]
            bu = _dotf(uv, bw_ref[0])
            hr, hi = a_re * hr - a_im * hi + bu[:, :L], a_re * hi + a_im * hr + bu[:, L:]
            y = _dotf(hr, cr_ref[0]) - _dotf(hi, ci_ref[0]) + d_ref[...] * uv
            zz_ref[t] = _gelu(y).astype(BF16)
        sre_ref[...] = hr
        sim_ref[...] = hi

    st_spec = pl.BlockSpec((DB, L), lambda g: (0, g))
    st_shape = jax.ShapeDtypeStruct((DB, S5_GROUPS * S5_STATE), F32)
    return pl.pallas_call(
        body, grid=(S5_NGB,),
        in_specs=[pl.BlockSpec((DEC_SEQ, DB, lanes), lambda g: (0, 0, g)), st_spec, st_spec,
                  pl.BlockSpec((1, lanes, 2 * L), lambda g: (g, 0, 0)),
                  pl.BlockSpec((1, L, lanes), lambda g: (g, 0, 0)),
                  pl.BlockSpec((1, L, lanes), lambda g: (g, 0, 0)),
                  pl.BlockSpec((1, 1, L), lambda g: (g, 0, 0)),
                  pl.BlockSpec((1, 1, L), lambda g: (g, 0, 0)),
                  pl.BlockSpec((1, lanes), lambda g: (0, g))],
        out_specs=[pl.BlockSpec((DEC_SEQ, DB, lanes), lambda g: (0, 0, g)), st_spec, st_spec],
        out_shape=[jax.ShapeDtypeStruct((DEC_SEQ, DB, D_MODEL), BF16), st_shape, st_shape],
        compiler_params=_cp(("parallel",)), name="s5_sample",
    )(u, h0r, h0i, bw, cr, ci, ar, ai, d.reshape(1, -1))


def _ffn_upgate_prompt(x, g, w_up, w_gate, cw, cb, T, tm, tn):
    M, K = x.shape
    (w_up, wu_spec), (w_gate, wg_spec) = _weight_spec(w_up, tn), _weight_spec(w_gate, tn)
    N = w_up.shape[-1]
    tiles_per_seq = T // tm
    pr = 16

    def body(x_ref, xp_ref, g_ref, wu_ref, wg_ref, cw_ref, cb_ref, act_ref, tail_ref, a_scr):
        i = pl.program_id(0)

        @pl.when(pl.program_id(1) == 0)
        def _():
            a_scr[...] = _rms_rows(x_ref[...], g_ref[...]).astype(BF16)

        a = a_scr[...]
        hu = jnp.dot(a, wu_ref[...], preferred_element_type=F32)
        hg = jnp.dot(a, wg_ref[...], preferred_element_type=F32)
        ap = _rms_rows(xp_ref[...], g_ref[...]).astype(BF16)
        prev = jnp.dot(ap, wu_ref[...], preferred_element_type=F32)
        prev = jnp.where(i % tiles_per_seq == 0, 0.0, prev)
        row = lax.broadcasted_iota(jnp.int32, hu.shape, 0)
        h1 = jnp.where(row == 0, prev[pr - 1:pr], pltpu.roll(hu, 1, 0))
        h2 = jnp.where(row == 0, prev[pr - 2:pr - 1], jnp.where(row == 1, prev[pr - 1:pr], pltpu.roll(hu, 2, 0)))
        cwv = cw_ref[...]
        hc = cb_ref[...] + h2 * cwv[0:1] + h1 * cwv[1:2] + hu * cwv[2:3]
        act_ref[...] = (_gelu(hc) * hg).astype(BF16)
        tail_ref[...] = hu[tm - 8:tm]

    return pl.pallas_call(
        body, grid=(M // tm, N // tn),
        in_specs=[pl.BlockSpec((tm, K), lambda i, j: (i, 0)),
                  pl.BlockSpec((pr, K), lambda i, j: (jnp.maximum(i * (tm // pr) - 1, 0), 0)),
                  pl.BlockSpec((1, K), lambda i, j: (0, 0)),
                  wu_spec, wg_spec,
                  pl.BlockSpec((FFN_CONV, tn), lambda i, j: (0, j)),
                  pl.BlockSpec((1, tn), lambda i, j: (0, j))],
        out_specs=[pl.BlockSpec((tm, tn), lambda i, j: (i, j)), pl.BlockSpec((8, tn), lambda i, j: (i, j))],
        out_shape=[jax.ShapeDtypeStruct((M, N), BF16), jax.ShapeDtypeStruct((M // tm * 8, N), F32)],
        scratch_shapes=[pltpu.VMEM((tm, K), BF16)],
        compiler_params=_cp(("parallel", "arbitrary")), name="ffn_upgate_prompt",
    )(x, x, g.reshape(1, K), w_up, w_gate, cw, cb.reshape(1, -1))


def _tile(m, pref):
    return pref if m % pref == 0 else m


def _dense_tail(x1, p, wts, ffn_state, T):
    (norm_ffn, norm_ple, w_up, w_gate, conv_w, conv_b, w_down, w_proj, w_pgate) = wts
    M = x1.shape[0]
    tm = _tile(M, 1024)
    tn = 512
    prompt = ffn_state is None

    if prompt:
        act, tails = _ffn_upgate_prompt(x1, norm_ffn, w_up, w_gate, conv_w, conv_b, T, tm, tn)
        tails = tails.reshape(M // tm, 8, D_FF)
        tiles_per_seq = T // tm
        new_buf = tails[tiles_per_seq - 1::tiles_per_seq, 8 - (FFN_CONV - 1):, :]
    else:
        nseq = M // T
        h1 = jnp.repeat(ffn_state[:, FFN_CONV - 2], T, axis=0)
        h2 = jnp.stack([ffn_state[:, 0], ffn_state[:, 1]] + [ffn_state[:, 1]] * (T - 2), axis=1).reshape(M, D_FF)

        def conv_epilogue(accs, ex):
            hu, hg = accs
            b1, b2, cwv, cbv = ex
            tok = lax.broadcasted_iota(jnp.int32, hu.shape, 0) % T
            s1 = jnp.where(tok >= 1, pltpu.roll(hu, 1, 0), b1)
            s2 = jnp.where(tok >= 2, pltpu.roll(hu, 2, 0), b2)
            hc = cbv + s2 * cwv[0:1] + s1 * cwv[1:2] + hu * cwv[2:3]
            return _gelu(hc) * hg, hu

        act, hu = _mm([x1], [w_up, w_gate], [0, 0],
                      [(h1, "tile"), (h2, "tile"), (conv_w, "rows"), (conv_b.reshape(1, -1), "rows")],
                      conv_epilogue, [BF16, F32], tm, tn, "ffn_upgate_sample", norm=(0, norm_ffn))
        new_buf = hu.reshape(nseq, T, D_FF)[:, T - (FFN_CONV - 1):]

    (x2,) = _mm([act], [w_down], [0], [(x1, "tile")], lambda accs, ex: (ex[0] + accs[0],), [F32],
                tm, tn, "ffn_down", vmem_mb=56)
    (x3,) = _mm([p, x2], [w_proj, w_pgate], [0, 1], [(x2, "tile")],
                lambda accs, ex: (ex[0] + accs[0] * _sigmoid(accs[1]),), [F32], tm, tn, "ple", norm=(1, norm_ple))
    return x3, new_buf


def kernel(x_prompt, x_sample, cache_cmp_k, cache_cmp_v, cache_sel_k, cache_sel_v, cache_win_k, cache_win_v, state_rglru_h, state_rglru_conv, state_s5_re, state_s5_im, state_ffn_conv, page_table, p_prompt, p_sample, norm_mix, norm_ffn, norm_ple, w_in_even, w_out_even, q_norm, k_norm_cmp, k_norm_sel, k_norm_win, cmp_pe_k, cmp_w1_k, cmp_w2_k, cmp_pe_v, cmp_w1_v, cmp_w2_v, rg_conv_w, rg_conv_b, rg_w_a, rg_b_a, rg_w_x, rg_b_x, rg_lam, s5_lam_re, s5_lam_im, s5_log_dt, s5_b_re, s5_b_im, s5_c_re, s5_c_im, s5_d, s5_w_glu_a, s5_w_glu_b, ffn_w_up, ffn_w_gate, ffn_conv_w, ffn_conv_b, ffn_w_down, ple_w_proj, ple_w_gate):
    B, T, _ = x_prompt.shape
    DB = x_sample.shape[0]
    n_pages = page_table.shape[1]
    past_len = n_pages * PAGE_SIZE
    w_buf = cache_win_k.shape[2]
    MP = B * T
    MS = DB * DEC_SEQ
    depth = norm_mix.shape[0]

    def reorder_in(w):
        q, kv, gt, xr, gr = (w[:, :Q_W], w[:, Q_W:Q_W + 6 * KV_W], w[:, Q_W + 6 * KV_W:Q_W + 6 * KV_W + 3 * N_HEADS],
                             w[:, Q_W + 6 * KV_W + 3 * N_HEADS:Q_W + 6 * KV_W + 3 * N_HEADS + D_RNN],
                             w[:, Q_W + 6 * KV_W + 3 * N_HEADS + D_RNN:])
        pad = jnp.zeros((w.shape[0], IN_PAD - COL_GT - 3 * N_HEADS), w.dtype)
        return jnp.concatenate([q, xr, gr, kv, gt, pad], axis=1).astype(BF16)

    stacked = [w.astype(BF16) for w in (ffn_w_up, ffn_w_gate, ffn_w_down, ple_w_proj, ple_w_gate)]

    def layer_dense_weights(li):
        up, gate, down, proj, pgate = [(w, li) for w in stacked]
        return (norm_ffn[li], norm_ple[li], up, gate, ffn_conv_w[li], ffn_conv_b[li], down, proj, pgate)

    xs = {"p": x_prompt.reshape(MP, D_MODEL), "s": x_sample.reshape(MS, D_MODEL)}
    pe_in = {"p": p_prompt.reshape(depth, MP, -1).astype(BF16), "s": p_sample.reshape(depth, MS, -1).astype(BF16)}
    seq_len = {"p": T, "s": DEC_SEQ}
    ev = {"p": [], "s": []}
    od = {"p": [], "s": []}
    ff = {"p": [], "s": []}
    ptab = page_table.reshape(-1).astype(jnp.int32)

    for li in range(depth):
        dense_w = layer_dense_weights(li)
        if li % 2 == 0:
            e = li // 2
            w_in = reorder_in(w_in_even[e])
            w_out = w_out_even[e].astype(BF16)
            w_out_a, w_out_r = w_out[:Q_W], w_out[Q_W:]
            for grp in ("p", "s"):
                x = xs[grp]
                M = x.shape[0]
                L = seq_len[grp]
                nseq = M // L
                (z,) = _mm([x], [w_in], [0], [], lambda accs, ex: (accs[0],), [F32], _tile(M, 1024), 512, "in_proj",
                           norm=(0, norm_mix[li]))
                if grp == "p":
                    cos_t, sin_t = _rope_tables(jnp.arange(T))
                    tmq = _tile(T, 512)
                    tab_blocks = T // tmq
                else:
                    cos_t, sin_t = _rope_tables(jnp.tile(past_len + jnp.arange(DEC_SEQ), DB))
                    tmq = M
                    tab_blocks = 1
                q, kvb, kc, vc, ks, vs, kw, vw = _qk_prep(z, cos_t, sin_t, tab_blocks, q_norm[e], k_norm_sel[e],
                                                          k_norm_win[e], tmq)
                z3 = z.reshape(nseq, L, IN_PAD)

                if grp == "p":
                    n_rows = T // CMP_STRIDE
                    ccos, csin = _rope_tables(jnp.arange(n_rows) * CMP_STRIDE + CMP_STRIDE - 1)
                    ident = jnp.arange(B * (T // PAGE_SIZE), dtype=jnp.int32)
                    kcmp = _compress(kc.reshape(-1, N_KV * PAGE_SIZE, HEAD_DIM), ident, B, T // PAGE_SIZE, cmp_pe_k[e],
                                     cmp_w1_k[e], cmp_w2_k[e], k_norm_cmp[e], ccos, csin, True, "compress_k_prompt")
                    vcmp = _compress(vc.reshape(-1, N_KV * PAGE_SIZE, HEAD_DIM), ident, B, T // PAGE_SIZE, cmp_pe_v[e],
                                     cmp_w1_v[e], cmp_w2_v[e], k_norm_cmp[e], ccos, csin, False, "compress_v_prompt")
                    n_sel = T // SEL_BLOCK
                    nqb = T // Q_BLOCK
                    ovl_t = _overlap_matrix(n_rows, n_sel, n_sel).T
                    ex_t = ((np.arange(T // SEL_KT)[:, None, None] * SEL_KT + np.arange(SEL_KT)[None, :, None])
                            // SEL_BLOCK == np.arange(n_sel)[None, None, :])
                    vs_t = kvb.reshape(B, T // SEL_KT, SEL_KT, 4, N_KV, HEAD_DIM)[:, :, :, 1].transpose(0, 3, 1, 4, 2)
                    vw_t = kvb.reshape(B, nqb, Q_BLOCK, 4, N_KV, HEAD_DIM)[:, :, :, 3].transpose(0, 3, 1, 4, 2)
                    gates_t = z[:, COL_GT:COL_GT + 3 * N_HEADS].reshape(B, T, 3, N_KV, GROUP)
                    gates_t = gates_t.transpose(0, 3, 2, 4, 1).reshape(B, N_KV, 3 * GROUP, T)
                    o_nsa = _nsa_prompt(q.reshape(B, T, Q_W), kcmp, vcmp.swapaxes(2, 3), kvb.reshape(B, T, 4 * KV_W),
                                        vs_t, vw_t, gates_t, ovl_t, jnp.asarray(ex_t, BF16), B, T).reshape(M, Q_W)
                    y_rg, h_last = _rglru_prompt(z.reshape(B, T, IN_PAD), rg_conv_w[e], rg_conv_b[e], rg_w_a[e],
                                                 rg_b_a[e], rg_w_x[e], rg_b_x[e], rg_lam[e], B, T, _tile(T, 512))
                    y_rg = y_rg.reshape(M, D_RNN)
                    h_last = h_last.reshape(B, D_RNN)
                    new_conv = z3[:, T - (RG_CONV - 1):, COL_XR:COL_XR + D_RNN]
                    new_wk = kw.reshape(B, T, N_KV, HEAD_DIM)
                    new_wv = vw.reshape(B, T, N_KV, HEAD_DIM)
                    if T >= w_buf:
                        new_wk, new_wv = new_wk[:, T - w_buf:], new_wv[:, T - w_buf:]
                    else:
                        padw = ((0, 0), (w_buf - T, 0), (0, 0), (0, 0))
                        new_wk, new_wv = jnp.pad(new_wk, padw), jnp.pad(new_wv, padw)
                else:
                    n_rows = past_len // CMP_STRIDE
                    ccos, csin = _rope_tables(jnp.arange(n_rows) * CMP_STRIDE + CMP_STRIDE - 1)
                    ptab_e = ptab + e * cache_cmp_k.shape[1]
                    kcmp = _compress(cache_cmp_k.reshape(-1, N_KV * PAGE_SIZE, HEAD_DIM), ptab_e, DB, n_pages, cmp_pe_k[e],
                                     cmp_w1_k[e], cmp_w2_k[e], k_norm_cmp[e], ccos, csin, True, "compress_k_sample")
                    vcmp = _compress(cache_cmp_v.reshape(-1, N_KV * PAGE_SIZE, HEAD_DIM), ptab_e, DB, n_pages, cmp_pe_v[e],
                                     cmp_w1_v[e], cmp_w2_v[e], k_norm_cmp[e], ccos, csin, False, "compress_v_sample")
                    n_sel = -(-(past_len + DEC_SEQ) // SEL_BLOCK)
                    sel_cols = -(-n_sel // 128) * 128
                    ovl = _overlap_matrix(n_rows, n_sel, sel_cols)
                    keys = _pages_per_step(n_pages, SEL_PAGES_PER_STEP) * PAGE_SIZE
                    steps = n_pages * PAGE_SIZE // keys
                    ex = (np.arange(sel_cols)[None, :, None]
                          == (np.arange(steps)[:, None, None] * keys + np.arange(keys)[None, None, :]) // SEL_BLOCK)
                    q8 = jnp.pad(q.reshape(DB, DEC_SEQ, Q_W), ((0, 0), (0, Q_PAD - DEC_SEQ), (0, 0)))
                    kvnew = jnp.pad(kvb.reshape(DB, DEC_SEQ, 4 * KV_W), ((0, 0), (0, 128 - DEC_SEQ), (0, 0)))
                    gt = z3[:, :, COL_GT:COL_GT + 3 * N_HEADS].reshape(DB, DEC_SEQ, 3, N_KV, GROUP)
                    gt = gt.transpose(0, 3, 1, 2, 4).reshape(DB, N_KV, DEC_SEQ, 3 * GROUP)
                    gt8 = jnp.pad(gt, ((0, 0), (0, 0), (0, Q_PAD - DEC_SEQ), (0, 0)))
                    o8 = _nsa_sample(q8, kcmp, vcmp, cache_sel_k.reshape(-1, N_KV * PAGE_SIZE, HEAD_DIM),
                                     cache_sel_v.reshape(-1, N_KV * PAGE_SIZE, HEAD_DIM), ptab_e, kvnew,
                                     cache_win_k.reshape(-1, N_KV * w_buf, HEAD_DIM),
                                     cache_win_v.reshape(-1, N_KV * w_buf, HEAD_DIM),
                                     e * DB, gt8, ovl, jnp.asarray(ex, BF16), DB, past_len)
                    o_nsa = o8[:, :DEC_SEQ].reshape(M, Q_W)
                    xr3 = z3[:, :, COL_XR:COL_XR + D_RNN]
                    gr3 = z3[:, :, COL_GR:COL_GR + D_RNN]
                    y_t, h_last = _rglru_sample(xr3.transpose(1, 0, 2), gr3.transpose(1, 0, 2),
                                                state_rglru_conv[e].transpose(1, 0, 2), state_rglru_h[e],
                                                rg_conv_w[e], rg_conv_b[e], rg_w_a[e], rg_b_a[e], rg_w_x[e],
                                                rg_b_x[e], rg_lam[e])
                    y_rg = y_t.transpose(1, 0, 2).reshape(M, D_RNN)
                    new_conv = jnp.concatenate([state_rglru_conv[e], xr3], axis=1)[:, -(RG_CONV - 1):]
                    kk = jnp.concatenate([cache_win_k[e], kw.reshape(DB, DEC_SEQ, N_KV, HEAD_DIM)], axis=1)
                    vv = jnp.concatenate([cache_win_v[e], vw.reshape(DB, DEC_SEQ, N_KV, HEAD_DIM)], axis=1)
                    new_wk, new_wv = kk[:, -w_buf:], vv[:, -w_buf:]

                rs = lambda a: a.reshape(nseq, L, N_KV, HEAD_DIM)
                ev[grp].append((rs(kc), rs(vc), rs(ks), rs(vs), new_wk, new_wv, h_last, new_conv))
                (x1,) = _mm([o_nsa, y_rg], [w_out_a, w_out_r], [0, 1], [(x, "tile")],
                            lambda accs, exs: (exs[0] + accs[0] + accs[1],), [F32], _tile(M, 1024), 512, "mixer_out")
                xs[grp], nb = _dense_tail(x1, pe_in[grp][li], dense_w, None if grp == "p" else state_ffn_conv[li], L)
                ff[grp].append(nb)
        else:
            o = li // 2
            ar, ai, bbr, bbi = _s5_discretize(s5_lam_re[o], s5_lam_im[o], s5_log_dt[o],
                                              s5_b_re[o].transpose(2, 0, 1), s5_b_im[o].transpose(2, 0, 1))
            bw, cr, ci = _s5_block_weights(bbr, bbi, s5_c_re[o], s5_c_im[o])
            ar = ar.reshape(S5_NGB, 1, S5_LANES)
            ai = ai.reshape(S5_NGB, 1, S5_LANES)
            w_a = s5_w_glu_a[o].astype(BF16)
            w_b = s5_w_glu_b[o].astype(BF16)
            for grp in ("p", "s"):
                x = xs[grp]
                M = x.shape[0]
                L = seq_len[grp]
                glu = lambda accs, exs: (exs[0] + accs[0] * _sigmoid(accs[1]),)
                if grp == "p":
                    sl = T // S5_NSEG
                    hn = _rmsnorm(x, norm_mix[li], F32, _tile(M, 256))
                    zz, sre, sim = _s5_prompt(hn.reshape(B, S5_NSEG, sl, D_MODEL), bw.astype(BF16), cr.astype(BF16),
                                              ci.astype(BF16), ar, ai, s5_d[o], B, T)
                    sre = sre.reshape(B, S5_GROUPS, S5_STATE)
                    sim = sim.reshape(B, S5_GROUPS, S5_STATE)
                    (x1,) = _mm([zz.reshape(M, D_MODEL)], [w_a, w_b], [0, 0], [(x, "tile")], glu, [F32],
                                _tile(M, 1024), 512, "mixer_out")
                else:
                    hn = _rmsnorm(x, norm_mix[li], F32, _tile(M, 256))
                    u = hn.reshape(DB, DEC_SEQ, D_MODEL).transpose(1, 0, 2)
                    zz, sre, sim = _s5_sample(u, state_s5_re[o].reshape(DB, -1), state_s5_im[o].reshape(DB, -1),
                                              bw, cr, ci, ar, ai, s5_d[o])
                    zz = zz.transpose(1, 0, 2).reshape(M, D_MODEL)
                    sre = sre.reshape(DB, S5_GROUPS, S5_STATE)
                    sim = sim.reshape(DB, S5_GROUPS, S5_STATE)
                    (x1,) = _mm([zz], [w_a, w_b], [0, 0], [(x, "tile")], glu, [F32], M, 512, "mixer_out")
                od[grp].append((sre, sim))
                xs[grp], nb = _dense_tail(x1, pe_in[grp][li], dense_w, None if grp == "p" else state_ffn_conv[li], L)
                ff[grp].append(nb)

    def stk(states, i):
        return jnp.stack([s[i] for s in states])

    outs = [xs["p"].reshape(B, T, D_MODEL), xs["s"].reshape(DB, DEC_SEQ, D_MODEL)]
    for i in range(8):
        outs += [stk(ev["p"], i), stk(ev["s"], i)]
    for i in range(2):
        outs += [stk(od["p"], i), stk(od["s"], i)]
    outs += [jnp.stack(ff["p"]), jnp.stack(ff["s"])]
    return tuple(outs)
```

```python
import functools

import numpy as np
import jax
import jax.numpy as jnp
from jax import lax
from jax.experimental import pallas as pl
from jax.experimental.pallas import tpu as pltpu

F32 = jnp.float32
BF16 = jnp.bfloat16

D_MODEL = 2048
DEC_SEQ = 4
PAGE_SIZE = 128
N_HEADS = 8
HEAD_DIM = 128
N_KV = 2
GROUP = N_HEADS // N_KV
ROPE_DIM = HEAD_DIM // 4
ROPE_HALF = ROPE_DIM // 2
ROPE_THETA = 500000.0
CMP_BLOCK = 32
CMP_STRIDE = 16
SEL_BLOCK = 64
SEL_TOPK = 16
N_LOCAL = 2
WINDOW = 512
Q_BLOCK = 128
FORCED_SCORE = 1e9
NEG_BIG = -1e30
ATTN_SCALE = HEAD_DIM ** -0.5
D_RNN = D_MODEL // 2
RG_BLOCKS = 8
RG_BW = D_RNN // RG_BLOCKS
RG_CONV = 4
RG_C = 8.0
S5_CH = 16
S5_GROUPS = D_MODEL // S5_CH
S5_STATE = 64
S5_GB = 8
S5_NGB = S5_GROUPS // S5_GB
S5_LANES = S5_GB * S5_STATE
S5_NSEG = 8
D_FF = ((8 * D_MODEL // 3 + 255) // 256) * 256
FFN_CONV = 3
NORM_EPS = 1e-6
Q_W = N_HEADS * HEAD_DIM
KV_W = N_KV * HEAD_DIM
MIN_PAGES_PER_STEP = 8
CMP_PAGES_PER_STEP = 32
SEL_PAGES_PER_STEP = 16
Q_PAD = 8
CMP_SPLIT = 4
SEL_KT = 512

COL_Q = 0
COL_XR = Q_W
COL_GR = Q_W + D_RNN
COL_KV = Q_W + 2 * D_RNN
COL_GT = COL_KV + 6 * KV_W
IN_PAD = 5120


def _pages_per_step(n_pages, preferred):
    g = preferred
    while g > MIN_PAGES_PER_STEP and n_pages % g:
        g //= 2
    return g


def _cp(sem, vmem_mb=48):
    return pltpu.CompilerParams(dimension_semantics=sem, vmem_limit_bytes=vmem_mb * 1024 * 1024)


def _gelu(x):
    return 0.5 * x * (1.0 + jnp.tanh(0.7978845608028654 * (x + 0.044715 * (x * x * x))))


def _sigmoid(x):
    return 1.0 / (1.0 + jnp.exp(-x))


def _dotb(a, b):
    return jnp.dot(a.astype(BF16), b.astype(BF16), preferred_element_type=F32)


def _dotf(a, b):
    return jnp.dot(a, b, preferred_element_type=F32, precision=lax.Precision.HIGHEST)


def _dot_nt(a, b):
    return lax.dot_general(a, b, (((1,), (1,)), ((), ())), preferred_element_type=F32)


def _masked_softmax(s, mask):
    s = jnp.where(mask, s, NEG_BIG)
    m = jnp.max(s, axis=-1, keepdims=True)
    e = jnp.where(mask, jnp.exp(s - m), 0.0)
    return e / jnp.maximum(jnp.sum(e, axis=-1, keepdims=True), 1e-30)


def _norm_rope(x, g, cos_t, sin_t):
    ms = jnp.mean(x * x, axis=-1, keepdims=True)
    y = x * lax.rsqrt(ms + NORM_EPS) * g
    lane = lax.broadcasted_iota(jnp.int32, y.shape, 1)
    swapped = jnp.where(lane < ROPE_HALF, pltpu.roll(y, HEAD_DIM - ROPE_HALF, 1), pltpu.roll(y, ROPE_HALF, 1))
    return y * cos_t + swapped * sin_t


def _rope_tables(pos):
    inv = ROPE_THETA ** (-jnp.arange(ROPE_HALF, dtype=F32) * 2.0 / ROPE_DIM)
    ang = pos.astype(F32)[:, None] * inv
    cos = jnp.cos(ang)
    sin = jnp.sin(ang)
    n = pos.shape[0]
    ones = jnp.ones((n, HEAD_DIM - ROPE_DIM), F32)
    zeros = jnp.zeros((n, HEAD_DIM - ROPE_DIM), F32)
    return (jnp.concatenate([cos, cos, ones], axis=1), jnp.concatenate([-sin, sin, zeros], axis=1))


def _overlap_matrix(n_rows, n_sel, n_cols):
    n = np.arange(n_rows)[:, None] - 1
    j = np.arange(n_cols)[None]
    c0 = n * CMP_STRIDE
    s0 = j * SEL_BLOCK
    ov = np.minimum(c0 + CMP_BLOCK, s0 + SEL_BLOCK) - np.maximum(c0, s0)
    ov = np.maximum(ov, 0).astype(np.float32) / CMP_BLOCK
    ov = np.where((n >= 0) & (j < n_sel), ov, 0.0)
    return jnp.asarray(ov, F32)


def _rmsnorm(x, g, out_dtype, tm):
    M, D = x.shape

    def body(x_ref, g_ref, o_ref):
        xv = x_ref[...]
        ms = jnp.mean(xv * xv, axis=-1, keepdims=True)
        o_ref[...] = (xv * lax.rsqrt(ms + NORM_EPS) * g_ref[...]).astype(o_ref.dtype)

    return pl.pallas_call(
        body, grid=(M // tm,),
        in_specs=[pl.BlockSpec((tm, D), lambda i: (i, 0)), pl.BlockSpec((1, D), lambda i: (0, 0))],
        out_specs=pl.BlockSpec((tm, D), lambda i: (i, 0)),
        out_shape=jax.ShapeDtypeStruct((M, D), out_dtype),
        compiler_params=_cp(("parallel",)), name="rmsnorm")(x, g.reshape(1, D))


def _rms_rows(xv, g):
    ms = jnp.mean(xv * xv, axis=-1, keepdims=True)
    return xv * lax.rsqrt(ms + NORM_EPS) * g


def _weight_spec(w, tn):
    if isinstance(w, tuple):
        arr, layer = w
        return arr, pl.BlockSpec((None, arr.shape[1], tn), lambda i, j: (layer, 0, j))
    return w, pl.BlockSpec((w.shape[0], tn), lambda i, j: (0, j))


def _mm(a_list, w_list, pair, extras, epilogue, out_dtypes, tm, tn, name, norm=None, vmem_mb=48):
    w_arrays, w_specs = zip(*[_weight_spec(w, tn) for w in w_list])
    N = w_arrays[0].shape[-1]
    na, nw, ne, no = len(a_list), len(w_list), len(extras), len(out_dtypes)
    M = a_list[0].shape[0]

    def body(*refs):
        if norm is not None:
            g_ref = refs[na + nw + ne]
            an_ref = refs[na + nw + ne + 1 + no]

            @pl.when(pl.program_id(1) == 0)
            def _():
                an_ref[...] = _rms_rows(refs[norm[0]][...], g_ref[...]).astype(BF16)

        a_vals = []
        for k, r in enumerate(refs[:na]):
            if norm is not None and k == norm[0]:
                a_vals.append(an_ref[...])
            else:
                a_vals.append(r[...].astype(BF16))
        accs = [jnp.dot(a_vals[pair[i]], refs[na + i][...], preferred_element_type=F32) for i in range(nw)]
        ex = [r[...] for r in refs[na + nw:na + nw + ne]]
        res = epilogue(accs, ex)
        first_out = na + nw + ne + (1 if norm is not None else 0)
        for o_ref, v in zip(refs[first_out:first_out + no], res):
            o_ref[...] = v.astype(o_ref.dtype)

    in_specs = [pl.BlockSpec((tm, a.shape[1]), lambda i, j: (i, 0)) for a in a_list]
    in_specs += list(w_specs)
    for arr, kind in extras:
        if kind == "tile":
            in_specs.append(pl.BlockSpec((tm, tn), lambda i, j: (i, j)))
        else:
            in_specs.append(pl.BlockSpec((arr.shape[0], tn), lambda i, j: (0, j)))
    operands = [*a_list, *w_arrays, *[e[0] for e in extras]]
    scratch = []
    if norm is not None:
        kn = a_list[norm[0]].shape[1]
        in_specs.append(pl.BlockSpec((1, kn), lambda i, j: (0, 0)))
        operands.append(norm[1].reshape(1, kn))
        scratch.append(pltpu.VMEM((tm, kn), BF16))
    return pl.pallas_call(
        body, grid=(M // tm, N // tn), in_specs=in_specs,
        out_specs=[pl.BlockSpec((tm, tn), lambda i, j: (i, j)) for _ in out_dtypes],
        out_shape=[jax.ShapeDtypeStruct((M, N), dt) for dt in out_dtypes],
        scratch_shapes=scratch,
        compiler_params=_cp(("parallel", "arbitrary" if norm is not None else "parallel"), vmem_mb), name=name,
    )(*operands)


def _qk_prep(z, cos_t, sin_t, tab_blocks, qn, kns, knw, tm):
    M = z.shape[0]

    def body(zq_ref, zkv_ref, c_ref, s_ref, qn_ref, kns_ref, knw_ref, q_ref, kvb_ref, *row_refs):
        c = c_ref[...]
        s = s_ref[...]
        for h in range(N_HEADS):
            sl = slice(h * HEAD_DIM, (h + 1) * HEAD_DIM)
            q_ref[:, sl] = _norm_rope(zq_ref[:, sl], qn_ref[...], c, s).astype(BF16)
        for kv in range(N_KV):
            vals = [zkv_ref[:, (k * N_KV + kv) * HEAD_DIM:(k * N_KV + kv + 1) * HEAD_DIM] for k in range(6)]
            vals[2] = _norm_rope(vals[2], kns_ref[...], c, s)
            vals[4] = _norm_rope(vals[4], knw_ref[...], c, s)
            for k in range(6):
                row_refs[k][pl.ds(kv, tm, stride=N_KV), :] = vals[k]
            for k in range(2, 6):
                kvb_ref[:, ((k - 2) * N_KV + kv) * HEAD_DIM:((k - 2) * N_KV + kv + 1) * HEAD_DIM] = vals[k].astype(BF16)

    vec = pl.BlockSpec((1, HEAD_DIM), lambda i: (0, 0))
    tab = pl.BlockSpec((tm, HEAD_DIM), lambda i: (i % tab_blocks, 0))
    rows_spec = pl.BlockSpec((N_KV * tm, HEAD_DIM), lambda i: (i, 0))
    rows_shape = jax.ShapeDtypeStruct((N_KV * M, HEAD_DIM), F32)
    return pl.pallas_call(
        body, grid=(M // tm,),
        in_specs=[pl.BlockSpec((tm, Q_W), lambda i: (i, COL_Q // Q_W)),
                  pl.BlockSpec((tm, 6 * KV_W), lambda i: (i, COL_KV // (6 * KV_W))),
                  tab, tab, vec, vec, vec],
        out_specs=[pl.BlockSpec((tm, Q_W), lambda i: (i, 0)), pl.BlockSpec((tm, 4 * KV_W), lambda i: (i, 0))]
        + [rows_spec] * 6,
        out_shape=[jax.ShapeDtypeStruct((M, Q_W), BF16), jax.ShapeDtypeStruct((M, 4 * KV_W), BF16)] + [rows_shape] * 6,
        compiler_params=_cp(("parallel",)), name="qk_prep",
    )(z, z, cos_t, sin_t, qn.reshape(1, -1), kns.reshape(1, -1), knw.reshape(1, -1))


def _compress_bias(pe, w1):
    def body(pe_ref, w1_ref, o_ref):
        for half in range(2):
            acc = jnp.zeros((8, HEAD_DIM), F32)
            for c in range(half * CMP_STRIDE, (half + 1) * CMP_STRIDE):
                acc = acc + _dotf(jnp.broadcast_to(pe_ref[c:c + 1, :], (8, HEAD_DIM)), w1_ref[c])
            o_ref[:, half * HEAD_DIM:(half + 1) * HEAD_DIM] = acc[0:1, :]

    return pl.pallas_call(body, out_shape=jax.ShapeDtypeStruct((1, 2 * HEAD_DIM), F32), name="compress_bias")(pe, w1)


def _compress(pages, ptab, n_seq, pages_per_seq, pe, w1, w2, norm_g, cos_t, sin_t, do_norm, name):
    bias = _compress_bias(pe, w1)
    w1cat = jnp.concatenate([w1[:CMP_STRIDE], w1[CMP_STRIDE:]], axis=2).astype(BF16)
    G = _pages_per_step(pages_per_seq, CMP_PAGES_PER_STEP)
    steps = pages_per_seq // G
    cpp = PAGE_SIZE // CMP_STRIDE
    ch = G * cpp
    n_chunk = pages_per_seq * cpp

    def body(pt_ref, *refs):
        page_refs = refs[:G]
        bias_ref, w1_ref, w2_ref, g_ref, c_ref, s_ref, o_ref, carry_ref, split_ref = refs[G:]
        step = pl.program_id(1)

        @pl.when(step == 0)
        def _():
            carry_ref[...] = jnp.zeros_like(carry_ref)

        for g, pr in enumerate(page_refs):
            for j in range(CMP_SPLIT):
                split_ref[g, j] = pr[0, pl.ds(j, N_KV * PAGE_SIZE // CMP_SPLIT, stride=CMP_SPLIT), :]

        row = lax.broadcasted_iota(jnp.int32, (ch, HEAD_DIM), 0)
        hop = N_KV * CMP_STRIDE // CMP_SPLIT
        acc2 = jnp.zeros((N_KV * ch, 2 * HEAD_DIM), F32)
        for c in range(CMP_STRIDE):
            pieces = []
            for kv in range(N_KV):
                j = (N_KV * c + kv) % CMP_SPLIT
                start = (N_KV * c + kv) // CMP_SPLIT
                pieces += [split_ref[g, j, pl.ds(start, cpp, stride=hop), :] for g in range(G)]
            acc2 = acc2 + _dotb(jnp.concatenate(pieces, axis=0), w1_ref[c])
        acc2 = acc2 + bias_ref[...]
        for kv in range(N_KV):
            lo = acc2[kv * ch:(kv + 1) * ch, :HEAD_DIM]
            hi = acc2[kv * ch:(kv + 1) * ch, HEAD_DIM:]
            lo_prev = jnp.where(row == 0, carry_ref[kv, 0:1, :], pltpu.roll(lo, 1, 0))
            carry_ref[kv, 0:1, :] = lo[ch - 1:ch, :]
            out = _dotb(_gelu(lo_prev + hi), w2_ref[...])
            if do_norm:
                out = _norm_rope(out, g_ref[...], c_ref[...], s_ref[...])
            o_ref[0, kv] = out.astype(BF16)

    def page_map(j, s, g, pt):
        return (pt[s * pages_per_seq + g * G + j], 0, 0)

    full2 = lambda s, g, pt: (0, 0)
    in_specs = [pl.BlockSpec((1, N_KV * PAGE_SIZE, HEAD_DIM), functools.partial(page_map, j)) for j in range(G)]
    in_specs += [pl.BlockSpec((1, 2 * HEAD_DIM), full2),
                 pl.BlockSpec((CMP_STRIDE, HEAD_DIM, 2 * HEAD_DIM), lambda s, g, pt: (0, 0, 0)),
                 pl.BlockSpec((HEAD_DIM, HEAD_DIM), full2),
                 pl.BlockSpec((1, HEAD_DIM), full2),
                 pl.BlockSpec((ch, HEAD_DIM), lambda s, g, pt: (g, 0)),
                 pl.BlockSpec((ch, HEAD_DIM), lambda s, g, pt: (g, 0))]
    gs = pltpu.PrefetchScalarGridSpec(
        num_scalar_prefetch=1, grid=(n_seq, steps), in_specs=in_specs,
        out_specs=pl.BlockSpec((1, N_KV, ch, HEAD_DIM), lambda s, g, pt: (s, 0, g, 0)),
        scratch_shapes=[pltpu.VMEM((N_KV, 8, HEAD_DIM), F32),
                        pltpu.VMEM((G, CMP_SPLIT, N_KV * PAGE_SIZE // CMP_SPLIT, HEAD_DIM), F32)])
    return pl.pallas_call(
        body, grid_spec=gs, out_shape=jax.ShapeDtypeStruct((n_seq, N_KV, n_chunk, HEAD_DIM), BF16),
        compiler_params=_cp(("parallel", "arbitrary")), name=name,
    )(ptab, *([pages] * G), bias, w1cat, w2.astype(BF16), norm_g.reshape(1, -1), cos_t, sin_t)


def _nsa_prompt(q, kcmp, vcmp_t, kvb, vs_t, vw_t, gates_t, ovl_t, expand_t, B, T):
    nqb = T // Q_BLOCK
    n_cmp_rows = T // CMP_STRIDE
    n_sel = T // SEL_BLOCK
    cols = GROUP * Q_BLOCK
    win_tiles = WINDOW // Q_BLOCK
    win_keys = (win_tiles + 1) * Q_BLOCK
    assert T % SEL_KT == 0 and T >= win_keys
    topk = min(SEL_TOPK, n_sel)

    def body(q_ref, kc_ref, vc_ref, ks_ref, vs_ref, kw_ref, vw_ref, gt_ref, ovl_ref, ex_ref, o_ref):
        i = pl.program_id(2)
        q4 = q_ref[0].astype(F32)
        qt = jnp.concatenate([q4[:, h * HEAD_DIM:(h + 1) * HEAD_DIM].T for h in range(GROUP)], axis=1).astype(BF16)
        pos = i * Q_BLOCK + (lax.broadcasted_iota(jnp.int32, (1, cols), 1) & (Q_BLOCK - 1))

        s = jnp.dot(kc_ref[0, 0], qt, preferred_element_type=F32) * ATTN_SCALE
        crow = lax.broadcasted_iota(jnp.int32, (n_cmp_rows, cols), 0)
        cmask = (crow >= 1) & (crow * CMP_STRIDE + (CMP_BLOCK - 1 - CMP_STRIDE) <= pos)
        s = jnp.where(cmask, s, NEG_BIG)
        e = jnp.where(cmask, jnp.exp(s - jnp.max(s, axis=0, keepdims=True)), 0.0)
        p = e * (1.0 / jnp.maximum(jnp.sum(e, axis=0, keepdims=True), 1e-30))
        o_cmp = _dotb(vc_ref[0, 0], p)
        imp = p[:, 0:Q_BLOCK]
        for h in range(1, GROUP):
            imp = imp + p[:, h * Q_BLOCK:(h + 1) * Q_BLOCK]

        score = _dotf(ovl_ref[...], imp)
        j = lax.broadcasted_iota(jnp.int32, (n_sel, Q_BLOCK), 0)
        qblk = (i * Q_BLOCK + lax.broadcasted_iota(jnp.int32, (n_sel, Q_BLOCK), 1)) // SEL_BLOCK
        forced = (j == 0) | ((j <= qblk) & (j > qblk - N_LOCAL))
        score = jnp.where(forced, FORCED_SCORE, jnp.where(j > qblk, NEG_BIG, score))
        rank = jnp.zeros((n_sel, Q_BLOCK), F32)
        for r in range(n_sel):
            sr = score[r:r + 1, :]
            beats = (sr > score) | ((sr == score) & (j > r))
            rank = rank + jnp.where(beats, 1.0, 0.0)
        sel_bias = jnp.where(rank < topk, 0.0, NEG_BIG).astype(BF16)

        krow = lax.broadcasted_iota(jnp.int32, (SEL_KT, cols), 0)

        def sel_tile(u, carry, causal):
            m, l, acc = carry
            k = ks_ref[0, pl.ds(pl.multiple_of(u * SEL_KT, SEL_KT), SEL_KT), :]
            b1 = _dotb(ex_ref[u], sel_bias)
            sc = (jnp.dot(k, qt, preferred_element_type=F32) * ATTN_SCALE
                  + jnp.concatenate([b1] * GROUP, axis=1))
            if causal:
                sc = jnp.where(u * SEL_KT + krow <= pos, sc, NEG_BIG)
            m_new = jnp.maximum(m, jnp.max(sc, axis=0, keepdims=True))
            alpha = jnp.exp(m - m_new)
            ex = jnp.exp(sc - m_new)
            l = alpha * l + jnp.sum(ex, axis=0, keepdims=True)
            acc = alpha * acc + _dotb(vs_ref[0, 0, u], ex)
            return m_new, l, acc

        init = (jnp.full((1, cols), NEG_BIG, F32), jnp.zeros((1, cols), F32), jnp.zeros((HEAD_DIM, cols), F32))
        last = (i * Q_BLOCK + Q_BLOCK + SEL_KT - 1) // SEL_KT - 1
        carry = lax.fori_loop(0, last, functools.partial(sel_tile, causal=False), init)
        _, l_sel, acc_sel = sel_tile(last, carry, causal=True)
        o_sel = acc_sel * (1.0 / jnp.maximum(l_sel, 1e-30))

        t0 = jnp.maximum(i - win_tiles, 0)
        kwin = kw_ref[0, pl.ds(pl.multiple_of(t0 * Q_BLOCK, Q_BLOCK), win_keys), :]
        sw = jnp.dot(kwin, qt, preferred_element_type=F32) * ATTN_SCALE
        kpos = t0 * Q_BLOCK + lax.broadcasted_iota(jnp.int32, (win_keys, cols), 0)
        wmask = (kpos <= pos) & (kpos > pos - WINDOW)
        sw = jnp.where(wmask, sw, NEG_BIG)
        ew = jnp.exp(sw - jnp.max(sw, axis=0, keepdims=True))
        o_win = jnp.zeros((HEAD_DIM, cols), F32)
        for t in range(win_tiles + 1):
            o_win = o_win + _dotb(vw_ref[0, 0, t0 + t], ew[t * Q_BLOCK:(t + 1) * Q_BLOCK])
        o_win = o_win * (1.0 / jnp.maximum(jnp.sum(ew, axis=0, keepdims=True), 1e-30))

        g = _sigmoid(gt_ref[0, 0])
        for h in range(GROUP):
            c = slice(h * Q_BLOCK, (h + 1) * Q_BLOCK)
            o = (g[h:h + 1, :] * o_cmp[:, c] + g[GROUP + h:GROUP + h + 1, :] * o_sel[:, c]
                 + g[2 * GROUP + h:2 * GROUP + h + 1, :] * o_win[:, c])
            o_ref[0, :, h * HEAD_DIM:(h + 1) * HEAD_DIM] = o.T.astype(BF16)

    def k_spec(off):
        return pl.BlockSpec((1, T, HEAD_DIM), lambda b, kv, i: (b, 0, off + kv))

    vs_spec = pl.BlockSpec((1, 1, T // SEL_KT, HEAD_DIM, SEL_KT), lambda b, kv, i: (b, kv, 0, 0, 0))
    vw_spec = pl.BlockSpec((1, 1, nqb, HEAD_DIM, Q_BLOCK), lambda b, kv, i: (b, kv, 0, 0, 0))
    qo_spec = pl.BlockSpec((1, Q_BLOCK, GROUP * HEAD_DIM), lambda b, kv, i: (b, i, kv))
    return pl.pallas_call(
        body, grid=(B, N_KV, nqb),
        in_specs=[qo_spec,
                  pl.BlockSpec((1, 1, n_cmp_rows, HEAD_DIM), lambda b, kv, i: (b, kv, 0, 0)),
                  pl.BlockSpec((1, 1, HEAD_DIM, n_cmp_rows), lambda b, kv, i: (b, kv, 0, 0)),
                  k_spec(0), vs_spec, k_spec(4), vw_spec,
                  pl.BlockSpec((1, 1, 3 * GROUP, Q_BLOCK), lambda b, kv, i: (b, kv, 0, i)),
                  pl.BlockSpec((n_sel, n_cmp_rows), lambda b, kv, i: (0, 0)),
                  pl.BlockSpec((T // SEL_KT, SEL_KT, n_sel), lambda b, kv, i: (0, 0, 0))],
        out_specs=qo_spec,
        out_shape=jax.ShapeDtypeStruct((B, T, Q_W), BF16),
        compiler_params=_cp(("parallel", "parallel", "arbitrary")), name="nsa_prompt",
    )(q, kcmp, vcmp_t, kvb, vs_t, kvb, vw_t, gates_t, ovl_t, expand_t)


def _nsa_sample(q, kcmp, vcmp, pool_k, pool_v, ptab, kvnew, win_k, win_v, win_off, gates, ovl, expand, DB, past_len):
    n_pages = past_len // PAGE_SIZE
    G = _pages_per_step(n_pages, SEL_PAGES_PER_STEP)
    steps = n_pages // G
    n_cmp_rows = kcmp.shape[2]
    sel_cols = ovl.shape[1]
    n_past_blk = past_len // SEL_BLOCK
    n_sel = -(-(past_len + DEC_SEQ) // SEL_BLOCK)
    w_buf = win_k.shape[1] // N_KV
    rows = GROUP * Q_PAD
    keys = G * PAGE_SIZE
    new_rows = kvnew.shape[1]

    def body(pt_ref, q_ref, kc_ref, vc_ref, ovl_ref, *rest):
        pk = rest[:G]
        pv = rest[G:2 * G]
        (ex_ref, kvn_ref, wk_ref, wv_ref, gt_ref, o_ref,
         sel_scr, m_scr, l_scr, acc_scr, base_scr) = rest[2 * G:]
        step = pl.program_id(1)
        tok = lax.broadcasted_iota(jnp.int32, (rows, 1), 0) & (Q_PAD - 1)
        q8 = q_ref[0]

        def q_rows(kv):
            return jnp.concatenate(
                [q8[:, (kv * GROUP + h) * HEAD_DIM:(kv * GROUP + h + 1) * HEAD_DIM] for h in range(GROUP)], axis=0)

        def head_col(gt, c0):
            return jnp.concatenate(
                [jnp.broadcast_to(gt[:, c0 + h:c0 + h + 1], (Q_PAD, HEAD_DIM)) for h in range(GROUP)], axis=0)

        @pl.when(step == 0)
        def _():
            o_cmps = []
            imps = []
            for kv in range(N_KV):
                s = _dot_nt(q_rows(kv), kc_ref[0, kv]) * ATTN_SCALE
                col = lax.broadcasted_iota(jnp.int32, (rows, n_cmp_rows), 1)
                cmask = (col >= 1) & (col * CMP_STRIDE + (CMP_BLOCK - 1 - CMP_STRIDE) <= past_len + tok)
                p = _masked_softmax(s, cmask)
                o_cmps.append(_dotb(p, vc_ref[0, kv]))
                imp = p[0:Q_PAD]
                for h in range(1, GROUP):
                    imp = imp + p[h * Q_PAD:(h + 1) * Q_PAD]
                imps.append(imp)

            score = _dotf(jnp.concatenate(imps, axis=0), ovl_ref[...])
            j = lax.broadcasted_iota(jnp.int32, (N_KV * Q_PAD, sel_cols), 1)
            trow = lax.broadcasted_iota(jnp.int32, (N_KV * Q_PAD, sel_cols), 0) & (Q_PAD - 1)
            qblk = (past_len + trow) // SEL_BLOCK
            forced = (j == 0) | ((j <= qblk) & (j > qblk - N_LOCAL))
            score = jnp.where(forced, FORCED_SCORE, jnp.where(j > qblk, NEG_BIG, score))
            score = jnp.where(j < n_sel, score, -jnp.inf)
            rank = jnp.zeros((N_KV * Q_PAD, sel_cols), F32)
            for r in range(n_sel):
                sr = score[:, r:r + 1]
                rank = rank + jnp.where((sr > score) | ((sr == score) & (j > r)), 1.0, 0.0)
            sel_all = jnp.where((rank < min(SEL_TOPK, n_sel)) & (j < n_sel), 1.0, 0.0)

            for kv in range(N_KV):
                qh = q_rows(kv)
                o_cmp = o_cmps[kv]
                sel4 = jnp.concatenate([sel_all[kv * Q_PAD:(kv + 1) * Q_PAD]] * GROUP, axis=0)
                sel_scr[kv] = sel4.astype(BF16)

                kb = wk_ref[0, pl.ds(kv, w_buf, stride=N_KV), :].astype(BF16)
                vb = wv_ref[0, pl.ds(kv, w_buf, stride=N_KV), :].astype(BF16)
                kn = kvn_ref[0, :, (4 + kv) * HEAD_DIM:(5 + kv) * HEAD_DIM]
                vn = kvn_ref[0, :, (6 + kv) * HEAD_DIM:(7 + kv) * HEAD_DIM]
                s1 = jnp.where(lax.broadcasted_iota(jnp.int32, (rows, w_buf), 1) - w_buf > tok - WINDOW,
                               _dot_nt(qh, kb) * ATTN_SCALE, NEG_BIG)
                ncol = lax.broadcasted_iota(jnp.int32, (rows, new_rows), 1)
                nmask = (ncol <= tok) & (ncol < DEC_SEQ)
                s2 = jnp.where(nmask, _dot_nt(qh, kn) * ATTN_SCALE, NEG_BIG)
                m = jnp.maximum(jnp.max(s1, axis=-1, keepdims=True), jnp.max(s2, axis=-1, keepdims=True))
                e1 = jnp.where(s1 > 0.5 * NEG_BIG, jnp.exp(s1 - m), 0.0)
                e2 = jnp.where(nmask, jnp.exp(s2 - m), 0.0)
                den = jnp.sum(e1, axis=-1, keepdims=True) + jnp.sum(e2, axis=-1, keepdims=True)
                o_win = (_dotb(e1, vb) + _dotb(e2, vn)) / jnp.maximum(den, 1e-30)

                gt = _sigmoid(gt_ref[0, kv])
                base_scr[kv] = head_col(gt, 0) * o_cmp + head_col(gt, 2 * GROUP) * o_win

                ksn = kvn_ref[0, :, kv * HEAD_DIM:(kv + 1) * HEAD_DIM]
                vsn = kvn_ref[0, :, (2 + kv) * HEAD_DIM:(3 + kv) * HEAD_DIM]
                chosen = sel4[:, n_past_blk:n_past_blk + 1] > 0.5
                smask = nmask & chosen
                s3 = jnp.where(smask, _dot_nt(qh, ksn) * ATTN_SCALE, NEG_BIG)
                m0 = jnp.max(s3, axis=-1, keepdims=True)
                e3 = jnp.where(smask, jnp.exp(s3 - m0), 0.0)
                m_scr[kv] = jnp.broadcast_to(m0, (rows, HEAD_DIM))
                l_scr[kv] = jnp.broadcast_to(jnp.sum(e3, axis=-1, keepdims=True), (rows, HEAD_DIM))
                acc_scr[kv] = _dotb(e3, vsn)

        for kv in range(N_KV):
            qh = q_rows(kv)
            k = jnp.concatenate([r[0, pl.ds(kv, PAGE_SIZE, stride=N_KV), :] for r in pk], axis=0).astype(BF16)
            v = jnp.concatenate([r[0, pl.ds(kv, PAGE_SIZE, stride=N_KV), :] for r in pv], axis=0).astype(BF16)
            mask = _dotb(sel_scr[kv], ex_ref[0]) > 0.5
            sc = jnp.where(mask, _dot_nt(qh, k) * ATTN_SCALE, NEG_BIG)
            m_old = m_scr[kv][:, 0:1]
            l_old = l_scr[kv][:, 0:1]
            m_new = jnp.maximum(m_old, jnp.max(sc, axis=-1, keepdims=True))
            alpha = jnp.exp(m_old - m_new)
            e = jnp.where(mask, jnp.exp(sc - m_new), 0.0)
            l_new = alpha * l_old + jnp.sum(e, axis=-1, keepdims=True)
            acc = alpha * acc_scr[kv] + _dotb(e, v)
            m_scr[kv] = jnp.broadcast_to(m_new, (rows, HEAD_DIM))
            l_scr[kv] = jnp.broadcast_to(l_new, (rows, HEAD_DIM))
            acc_scr[kv] = acc

        @pl.when(step == steps - 1)
        def _():
            for kv in range(N_KV):
                gt = _sigmoid(gt_ref[0, kv])
                o_sel = acc_scr[kv] / jnp.maximum(l_scr[kv], 1e-30)
                o = base_scr[kv] + head_col(gt, GROUP) * o_sel
                for h in range(GROUP):
                    hh = kv * GROUP + h
                    o_ref[0, :, hh * HEAD_DIM:(hh + 1) * HEAD_DIM] = o[h * Q_PAD:(h + 1) * Q_PAD].astype(BF16)

    def page_map(jj, b, g, pt):
        return (pt[b * n_pages + g * G + jj], 0, 0)

    page_specs = [pl.BlockSpec((1, N_KV * PAGE_SIZE, HEAD_DIM), functools.partial(page_map, jj)) for jj in range(G)]
    cmp_spec = pl.BlockSpec((1, N_KV, n_cmp_rows, HEAD_DIM), lambda b, g, pt: (b, 0, 0, 0))
    in_specs = [pl.BlockSpec((1, Q_PAD, Q_W), lambda b, g, pt: (b, 0, 0)), cmp_spec, cmp_spec,
                pl.BlockSpec((n_cmp_rows, sel_cols), lambda b, g, pt: (0, 0))]
    in_specs += page_specs + page_specs
    in_specs += [pl.BlockSpec((1, sel_cols, keys), lambda b, g, pt: (g, 0, 0)),
                 pl.BlockSpec((1, new_rows, 4 * KV_W), lambda b, g, pt: (b, 0, 0)),
                 pl.BlockSpec((1, N_KV * w_buf, HEAD_DIM), lambda b, g, pt: (win_off + b, 0, 0)),
                 pl.BlockSpec((1, N_KV * w_buf, HEAD_DIM), lambda b, g, pt: (win_off + b, 0, 0)),
                 pl.BlockSpec((1, N_KV, Q_PAD, 3 * GROUP), lambda b, g, pt: (b, 0, 0, 0))]
    gs = pltpu.PrefetchScalarGridSpec(
        num_scalar_prefetch=1, grid=(DB, steps), in_specs=in_specs,
        out_specs=pl.BlockSpec((1, Q_PAD, Q_W), lambda b, g, pt: (b, 0, 0)),
        scratch_shapes=[pltpu.VMEM((N_KV, rows, sel_cols), BF16), pltpu.VMEM((N_KV, rows, HEAD_DIM), F32),
                        pltpu.VMEM((N_KV, rows, HEAD_DIM), F32), pltpu.VMEM((N_KV, rows, HEAD_DIM), F32),
                        pltpu.VMEM((N_KV, rows, HEAD_DIM), F32)])
    return pl.pallas_call(
        body, grid_spec=gs, out_shape=jax.ShapeDtypeStruct((DB, Q_PAD, Q_W), BF16),
        compiler_params=_cp(("parallel", "arbitrary")), name="nsa_sample",
    )(ptab, q, kcmp, vcmp, ovl, *([pool_k] * G), *([pool_v] * G), expand, kvnew, win_k, win_v, gates)


def _rg_gates(u, wa_ref, ba, wx_ref, bx, sp):
    ra = []
    rx = []
    for n in range(RG_BLOCKS):
        ub = u[:, n * RG_BW:(n + 1) * RG_BW]
        ra.append(_dotf(ub, wa_ref[n]))
        rx.append(_dotf(ub, wx_ref[n]))
    r = _sigmoid(jnp.concatenate(ra, axis=1) + ba)
    i = _sigmoid(jnp.concatenate(rx, axis=1) + bx)
    log_a = -RG_C * r * sp
    a = jnp.exp(log_a)
    b = jnp.sqrt(1.0 - jnp.exp(2.0 * log_a)) * i * u
    return a, b


def _softplus_neg(lam):
    return jnp.maximum(-lam, 0.0) + jnp.log(1.0 + jnp.exp(-jnp.abs(lam)))


def _rglru_prompt(z, cw, cb, wa, ba, wx, bx, lam, B, T, tt):
    keep = 8

    def body(xr_ref, gr_ref, cw_ref, cb_ref, wa_ref, ba_ref, wx_ref, bx_ref, lam_ref,
             y_ref, hl_ref, a_scr, b_scr, h_scr, tail_scr):
        t = pl.program_id(1)

        @pl.when(t == 0)
        def _():
            h_scr[...] = jnp.zeros_like(h_scr)
            tail_scr[...] = jnp.zeros_like(tail_scr)

        x = xr_ref[0]
        row = lax.broadcasted_iota(jnp.int32, (tt, D_RNN), 0)
        cwv = cw_ref[...]
        u = cb_ref[...] + x * cwv[RG_CONV - 1:RG_CONV]
        for d in range(1, RG_CONV):
            xs = pltpu.roll(x, d, 0)
            for r in range(d):
                xs = jnp.where(row == r, tail_scr[keep - d + r:keep - d + r + 1, :], xs)
            u = u + xs * cwv[RG_CONV - 1 - d:RG_CONV - d]
        tail_scr[...] = x[tt - keep:tt]
        a, b = _rg_gates(u, wa_ref, ba_ref[...], wx_ref, bx_ref[...], _softplus_neg(lam_ref[...]))
        a_scr[...] = a
        b_scr[...] = b

        def step(s, h):
            h = a_scr[pl.ds(s, 1), :] * h + b_scr[pl.ds(s, 1), :]
            b_scr[pl.ds(s, 1), :] = h
            return h

        h = lax.fori_loop(0, tt, step, h_scr[0:1, :], unroll=8)
        h_scr[0:1, :] = h
        hl_ref[0] = h
        y_ref[0] = (b_scr[...] * _gelu(gr_ref[0])).astype(BF16)

    vec = pl.BlockSpec((1, D_RNN), lambda b, t: (0, 0))
    wspec = pl.BlockSpec((RG_BLOCKS, RG_BW, RG_BW), lambda b, t: (0, 0, 0))
    return pl.pallas_call(
        body, grid=(B, T // tt),
        in_specs=[pl.BlockSpec((1, tt, D_RNN), lambda b, t: (b, t, COL_XR // D_RNN)),
                  pl.BlockSpec((1, tt, D_RNN), lambda b, t: (b, t, COL_GR // D_RNN)),
                  pl.BlockSpec((RG_CONV, D_RNN), lambda b, t: (0, 0)), vec, wspec, vec, wspec, vec, vec],
        out_specs=[pl.BlockSpec((1, tt, D_RNN), lambda b, t: (b, t, 0)),
                   pl.BlockSpec((1, 1, D_RNN), lambda b, t: (b, 0, 0))],
        out_shape=[jax.ShapeDtypeStruct((B, T, D_RNN), BF16), jax.ShapeDtypeStruct((B, 1, D_RNN), F32)],
        scratch_shapes=[pltpu.VMEM((tt, D_RNN), F32), pltpu.VMEM((tt, D_RNN), F32),
                        pltpu.VMEM((8, D_RNN), F32), pltpu.VMEM((keep, D_RNN), F32)],
        compiler_params=_cp(("parallel", "arbitrary")), name="rglru_prompt",
    )(z, z, cw, cb.reshape(1, -1), wa, ba.reshape(1, -1), wx, bx.reshape(1, -1), lam.reshape(1, -1))


def _rglru_sample(xr, gr, buf, h0, cw, cb, wa, ba, wx, bx, lam):
    DB = h0.shape[0]

    def body(xr_ref, gr_ref, buf_ref, h0_ref, cw_ref, cb_ref, wa_ref, ba_ref, wx_ref, bx_ref, lam_ref, y_ref, hl_ref):
        xp = [buf_ref[k] for k in range(RG_CONV - 1)] + [xr_ref[t] for t in range(DEC_SEQ)]
        cwv = cw_ref[...]
        sp = _softplus_neg(lam_ref[...])
        h = h0_ref[...]
        for t in range(DEC_SEQ):
            u = cb_ref[...]
            for k in range(RG_CONV):
                u = u + xp[t + k] * cwv[k:k + 1]
            a, b = _rg_gates(u, wa_ref, ba_ref[...], wx_ref, bx_ref[...], sp)
            h = a * h + b
            y_ref[t] = (h * _gelu(gr_ref[t])).astype(BF16)
        hl_ref[...] = h

    return pl.pallas_call(
        body, out_shape=[jax.ShapeDtypeStruct((DEC_SEQ, DB, D_RNN), BF16), jax.ShapeDtypeStruct((DB, D_RNN), F32)],
        compiler_params=pltpu.CompilerParams(vmem_limit_bytes=32 * 1024 * 1024), name="rglru_sample",
    )(xr, gr, buf, h0, cw, cb.reshape(1, -1), wa, ba.reshape(1, -1), wx, bx.reshape(1, -1), lam.reshape(1, -1))


def _s5_discretize(lam_re, lam_im, log_dt, b_re_t, b_im_t):
    def body(lr_ref, li_ref, ldt_ref, br_ref, bi_ref, ar_ref, ai_ref, bbr_ref, bbi_ref):
        lr = lr_ref[...]
        li = li_ref[...]
        dt = jnp.exp(ldt_ref[...])
        mag = jnp.exp(lr * dt)
        ar = mag * jnp.cos(li * dt)
        ai = mag * jnp.sin(li * dt)
        den = lr * lr + li * li
        n_re = ar - 1.0
        f_re = (n_re * lr + ai * li) / den
        f_im = (ai * lr - n_re * li) / den
        ar_ref[...] = ar
        ai_ref[...] = ai
        for c in range(S5_CH):
            bbr_ref[c] = f_re * br_ref[c] - f_im * bi_ref[c]
            bbi_ref[c] = f_re * bi_ref[c] + f_im * br_ref[c]

    gp = jax.ShapeDtypeStruct(lam_re.shape, F32)
    cgp = jax.ShapeDtypeStruct(b_re_t.shape, F32)
    return pl.pallas_call(body, out_shape=[gp, gp, cgp, cgp], name="s5_discretize")(
        lam_re, lam_im, log_dt.reshape(-1, 1), b_re_t, b_im_t)


def _s5_block_weights(bbr, bbi, c_re, c_im):
    eye = jnp.eye(S5_GB, dtype=F32)

    def bblk(bb):
        x = bb.reshape(S5_CH, S5_NGB, S5_GB, S5_STATE)
        x = jnp.einsum("cngp,gh->ngchp", x, eye)
        return x.reshape(S5_NGB, S5_GB * S5_CH, S5_LANES)

    def cblk(c):
        x = c.reshape(S5_NGB, S5_GB, S5_CH, S5_STATE)
        x = jnp.einsum("ngcp,gh->ngphc", x, eye)
        return x.reshape(S5_NGB, S5_LANES, S5_GB * S5_CH)

    return jnp.concatenate([bblk(bbr), bblk(bbi)], axis=2), cblk(c_re), cblk(c_im)


def _s5_prompt(u, bw, cr, ci, ar, ai, d, B, T):
    L = S5_LANES
    lanes = S5_GB * S5_CH
    nseg = S5_NSEG
    sl = T // nseg
    assert sl & (sl - 1) == 0, "segment length must be a power of two"
    chs = min(64, sl)
    ch = chs * nseg
    nchunks = sl // chs

    def body(u_ref, bw_ref, cr_ref, ci_ref, ar_ref, ai_ref, d_ref, zz_ref, sre_ref, sim_ref, h_scr, t_scr):
        a_re = ar_ref[0]
        a_im = ai_ref[0]
        ar8 = jnp.broadcast_to(a_re, (nseg, L))
        ai8 = jnp.broadcast_to(a_im, (nseg, L))

        def load_u(s0):
            for seg in range(nseg):
                t_scr[pl.ds(seg, chs, stride=nseg), :] = u_ref[0, seg, pl.ds(s0, chs), :]
            return t_scr[...]

        def project(c, carry):
            s0 = pl.multiple_of(c * chs, chs)
            h_scr[pl.ds(pl.multiple_of(c * ch, ch), ch), :] = _dotb(load_u(s0), bw_ref[0])
            return carry

        def advance(s, carry, store):
            hr, hi = carry
            rows = pl.ds(pl.multiple_of(s * nseg, nseg), nseg)
            bu = h_scr[rows, :]
            nr = ar8 * hr - ai8 * hi + bu[:, :L]
            ni = ar8 * hi + ai8 * hr + bu[:, L:]
            if store:
                h_scr[rows, :] = jnp.concatenate([nr, ni], axis=1)
            return nr, ni

        def scan_chunk(c, carry, store):
            for k in range(chs):
                carry = advance(c * chs + k, carry, store)
            return carry

        def first_pass(c, carry):
            carry = scan_chunk(c, carry, False)
            project(jnp.minimum(c + 1, nchunks - 1), 0)
            return carry

        project(jnp.int32(0), 0)
        zero = jnp.zeros((nseg, L), F32)
        er, ei = lax.fori_loop(0, nchunks, first_pass, (zero, zero))

        pr, pi = a_re, a_im
        for _ in range(sl.bit_length() - 1):
            pr, pi = pr * pr - pi * pi, 2.0 * pr * pi
        sr = [jnp.zeros((1, L), F32)]
        si = [jnp.zeros((1, L), F32)]
        for k in range(nseg):
            sr.append(er[k:k + 1] + pr * sr[k] - pi * si[k])
            si.append(ei[k:k + 1] + pr * si[k] + pi * sr[k])
        sre_ref[0, 0] = sr[nseg]
        sim_ref[0, 0] = si[nseg]
        start = (jnp.concatenate(sr[:nseg], axis=0), jnp.concatenate(si[:nseg], axis=0))

        def emit(c, carry):
            s0 = pl.multiple_of(c * chs, chs)
            hc = h_scr[pl.ds(pl.multiple_of(c * ch, ch), ch), :]
            uc = load_u(s0)
            y = _dotb(hc[:, :L], cr_ref[0]) - _dotb(hc[:, L:], ci_ref[0]) + d_ref[...] * uc
            t_scr[...] = _gelu(y)
            for seg in range(nseg):
                zz_ref[0, seg, pl.ds(s0, chs), :] = t_scr[pl.ds(seg, chs, stride=nseg), :].astype(BF16)
            return carry

        def second_pass(c, carry):
            emit(jnp.maximum(c - 1, 0), 0)
            return scan_chunk(c, carry, True)

        lax.fori_loop(0, nchunks, second_pass, start)
        emit(jnp.int32(nchunks - 1), 0)

    st_spec = pl.BlockSpec((1, 1, 1, L), lambda b, g: (b, g, 0, 0))
    st_shape = jax.ShapeDtypeStruct((B, S5_NGB, 1, L), F32)
    io_spec = pl.BlockSpec((1, nseg, sl, lanes), lambda b, g: (b, 0, 0, g))
    return pl.pallas_call(
        body, grid=(B, S5_NGB),
        in_specs=[io_spec,
                  pl.BlockSpec((1, lanes, 2 * L), lambda b, g: (g, 0, 0)),
                  pl.BlockSpec((1, L, lanes), lambda b, g: (g, 0, 0)),
                  pl.BlockSpec((1, L, lanes), lambda b, g: (g, 0, 0)),
                  pl.BlockSpec((1, 1, L), lambda b, g: (g, 0, 0)),
                  pl.BlockSpec((1, 1, L), lambda b, g: (g, 0, 0)),
                  pl.BlockSpec((1, lanes), lambda b, g: (0, g))],
        out_specs=[io_spec, st_spec, st_spec],
        out_shape=[jax.ShapeDtypeStruct((B, nseg, sl, D_MODEL), BF16), st_shape, st_shape],
        scratch_shapes=[pltpu.VMEM((T, 2 * L), F32), pltpu.VMEM((ch, lanes), F32)],
        compiler_params=_cp(("parallel", "parallel")), name="s5_prompt",
    )(u, bw, cr, ci, ar, ai, d.reshape(1, -1))


def _s5_sample(u, h0r, h0i, bw, cr, ci, ar, ai, d):
    DB = u.shape[1]
    L = S5_LANES
    lanes = S5_GB * S5_CH

    def body(u_ref, hr_ref, hi_ref, bw_ref, cr_ref, ci_ref, ar_ref, ai_ref, d_ref, zz_ref, sre_ref, sim_ref):
        hr = hr_ref[...]
        hi = hi_ref[...]
        a_re = ar_ref[0]
        a_im = ai_ref[0]
        for t in range(DEC_SEQ):
            uv = u_ref[t]
            bu = _dotf(uv, bw_ref[0])
            hr, hi = a_re * hr - a_im * hi + bu[:, :L], a_re * hi + a_im * hr + bu[:, L:]
            y = _dotf(hr, cr_ref[0]) - _dotf(hi, ci_ref[0]) + d_ref[...] * uv
            zz_ref[t] = _gelu(y).astype(BF16)
        sre_ref[...] = hr
        sim_ref[...] = hi

    st_spec = pl.BlockSpec((DB, L), lambda g: (0, g))
    st_shape = jax.ShapeDtypeStruct((DB, S5_GROUPS * S5_STATE), F32)
    return pl.pallas_call(
        body, grid=(S5_NGB,),
        in_specs=[pl.BlockSpec((DEC_SEQ, DB, lanes), lambda g: (0, 0, g)), st_spec, st_spec,
                  pl.BlockSpec((1, lanes, 2 * L), lambda g: (g, 0, 0)),
                  pl.BlockSpec((1, L, lanes), lambda g: (g, 0, 0)),
                  pl.BlockSpec((1, L, lanes), lambda g: (g, 0, 0)),
                  pl.BlockSpec((1, 1, L), lambda g: (g, 0, 0)),
                  pl.BlockSpec((1, 1, L), lambda g: (g, 0, 0)),
                  pl.BlockSpec((1, lanes), lambda g: (0, g))],
        out_specs=[pl.BlockSpec((DEC_SEQ, DB, lanes), lambda g: (0, 0, g)), st_spec, st_spec],
        out_shape=[jax.ShapeDtypeStruct((DEC_SEQ, DB, D_MODEL), BF16), st_shape, st_shape],
        compiler_params=_cp(("parallel",)), name="s5_sample",
    )(u, h0r, h0i, bw, cr, ci, ar, ai, d.reshape(1, -1))


def _ffn_upgate_prompt(x, g, w_up, w_gate, cw, cb, T, tm, tn):
    M, K = x.shape
    (w_up, wu_spec), (w_gate, wg_spec) = _weight_spec(w_up, tn), _weight_spec(w_gate, tn)
    N = w_up.shape[-1]
    tiles_per_seq = T // tm
    pr = 16

    def body(x_ref, xp_ref, g_ref, wu_ref, wg_ref, cw_ref, cb_ref, act_ref, tail_ref, a_scr):
        i = pl.program_id(0)

        @pl.when(pl.program_id(1) == 0)
        def _():
            a_scr[...] = _rms_rows(x_ref[...], g_ref[...]).astype(BF16)

        a = a_scr[...]
        hu = jnp.dot(a, wu_ref[...], preferred_element_type=F32)
        hg = jnp.dot(a, wg_ref[...], preferred_element_type=F32)
        ap = _rms_rows(xp_ref[...], g_ref[...]).astype(BF16)
        prev = jnp.dot(ap, wu_ref[...], preferred_element_type=F32)
        prev = jnp.where(i % tiles_per_seq == 0, 0.0, prev)
        row = lax.broadcasted_iota(jnp.int32, hu.shape, 0)
        h1 = jnp.where(row == 0, prev[pr - 1:pr], pltpu.roll(hu, 1, 0))
        h2 = jnp.where(row == 0, prev[pr - 2:pr - 1], jnp.where(row == 1, prev[pr - 1:pr], pltpu.roll(hu, 2, 0)))
        cwv = cw_ref[...]
        hc = cb_ref[...] + h2 * cwv[0:1] + h1 * cwv[1:2] + hu * cwv[2:3]
        act_ref[...] = (_gelu(hc) * hg).astype(BF16)
        tail_ref[...] = hu[tm - 8:tm]

    return pl.pallas_call(
        body, grid=(M // tm, N // tn),
        in_specs=[pl.BlockSpec((tm, K), lambda i, j: (i, 0)),
                  pl.BlockSpec((pr, K), lambda i, j: (jnp.maximum(i * (tm // pr) - 1, 0), 0)),
                  pl.BlockSpec((1, K), lambda i, j: (0, 0)),
                  wu_spec, wg_spec,
                  pl.BlockSpec((FFN_CONV, tn), lambda i, j: (0, j)),
                  pl.BlockSpec((1, tn), lambda i, j: (0, j))],
        out_specs=[pl.BlockSpec((tm, tn), lambda i, j: (i, j)), pl.BlockSpec((8, tn), lambda i, j: (i, j))],
        out_shape=[jax.ShapeDtypeStruct((M, N), BF16), jax.ShapeDtypeStruct((M // tm * 8, N), F32)],
        scratch_shapes=[pltpu.VMEM((tm, K), BF16)],
        compiler_params=_cp(("parallel", "arbitrary")), name="ffn_upgate_prompt",
    )(x, x, g.reshape(1, K), w_up, w_gate, cw, cb.reshape(1, -1))


def _tile(m, pref):
    return pref if m % pref == 0 else m


def _dense_tail(x1, p, wts, ffn_state, T):
    (norm_ffn, norm_ple, w_up, w_gate, conv_w, conv_b, w_down, w_proj, w_pgate) = wts
    M = x1.shape[0]
    tm = _tile(M, 1024)
    tn = 512
    prompt = ffn_state is None

    if prompt:
        act, tails = _ffn_upgate_prompt(x1, norm_ffn, w_up, w_gate, conv_w, conv_b, T, tm, tn)
        tails = tails.reshape(M // tm, 8, D_FF)
        tiles_per_seq = T // tm
        new_buf = tails[tiles_per_seq - 1::tiles_per_seq, 8 - (FFN_CONV - 1):, :]
    else:
        nseq = M // T
        h1 = jnp.repeat(ffn_state[:, FFN_CONV - 2], T, axis=0)
        h2 = jnp.stack([ffn_state[:, 0], ffn_state[:, 1]] + [ffn_state[:, 1]] * (T - 2), axis=1).reshape(M, D_FF)

        def conv_epilogue(accs, ex):
            hu, hg = accs
            b1, b2, cwv, cbv = ex
            tok = lax.broadcasted_iota(jnp.int32, hu.shape, 0) % T
            s1 = jnp.where(tok >= 1, pltpu.roll(hu, 1, 0), b1)
            s2 = jnp.where(tok >= 2, pltpu.roll(hu, 2, 0), b2)
            hc = cbv + s2 * cwv[0:1] + s1 * cwv[1:2] + hu * cwv[2:3]
            return _gelu(hc) * hg, hu

        act, hu = _mm([x1], [w_up, w_gate], [0, 0],
                      [(h1, "tile"), (h2, "tile"), (conv_w, "rows"), (conv_b.reshape(1, -1), "rows")],
                      conv_epilogue, [BF16, F32], tm, tn, "ffn_upgate_sample", norm=(0, norm_ffn))
        new_buf = hu.reshape(nseq, T, D_FF)[:, T - (FFN_CONV - 1):]

    (x2,) = _mm([act], [w_down], [0], [(x1, "tile")], lambda accs, ex: (ex[0] + accs[0],), [F32],
                tm, tn, "ffn_down", vmem_mb=56)
    (x3,) = _mm([p, x2], [w_proj, w_pgate], [0, 1], [(x2, "tile")],
                lambda accs, ex: (ex[0] + accs[0] * _sigmoid(accs[1]),), [F32], tm, tn, "ple", norm=(1, norm_ple))
    return x3, new_buf


def kernel(x_prompt, x_sample, cache_cmp_k, cache_cmp_v, cache_sel_k, cache_sel_v, cache_win_k, cache_win_v, state_rglru_h, state_rglru_conv, state_s5_re, state_s5_im, state_ffn_conv, page_table, p_prompt, p_sample, norm_mix, norm_ffn, norm_ple, w_in_even, w_out_even, q_norm, k_norm_cmp, k_norm_sel, k_norm_win, cmp_pe_k, cmp_w1_k, cmp_w2_k, cmp_pe_v, cmp_w1_v, cmp_w2_v, rg_conv_w, rg_conv_b, rg_w_a, rg_b_a, rg_w_x, rg_b_x, rg_lam, s5_lam_re, s5_lam_im, s5_log_dt, s5_b_re, s5_b_im, s5_c_re, s5_c_im, s5_d, s5_w_glu_a, s5_w_glu_b, ffn_w_up, ffn_w_gate, ffn_conv_w, ffn_conv_b, ffn_w_down, ple_w_proj, ple_w_gate):
    B, T, _ = x_prompt.shape
    DB = x_sample.shape[0]
    n_pages = page_table.shape[1]
    past_len = n_pages * PAGE_SIZE
    w_buf = cache_win_k.shape[2]
    MP = B * T
    MS = DB * DEC_SEQ
    depth = norm_mix.shape[0]

    def reorder_in(w):
        q, kv, gt, xr, gr = (w[:, :Q_W], w[:, Q_W:Q_W + 6 * KV_W], w[:, Q_W + 6 * KV_W:Q_W + 6 * KV_W + 3 * N_HEADS],
                             w[:, Q_W + 6 * KV_W + 3 * N_HEADS:Q_W + 6 * KV_W + 3 * N_HEADS + D_RNN],
                             w[:, Q_W + 6 * KV_W + 3 * N_HEADS + D_RNN:])
        pad = jnp.zeros((w.shape[0], IN_PAD - COL_GT - 3 * N_HEADS), w.dtype)
        return jnp.concatenate([q, xr, gr, kv, gt, pad], axis=1).astype(BF16)

    stacked = [w.astype(BF16) for w in (ffn_w_up, ffn_w_gate, ffn_w_down, ple_w_proj, ple_w_gate)]

    def layer_dense_weights(li):
        up, gate, down, proj, pgate = [(w, li) for w in stacked]
        return (norm_ffn[li], norm_ple[li], up, gate, ffn_conv_w[li], ffn_conv_b[li], down, proj, pgate)

    xs = {"p": x_prompt.reshape(MP, D_MODEL), "s": x_sample.reshape(MS, D_MODEL)}
    pe_in = {"p": p_prompt.reshape(depth, MP, -1).astype(BF16), "s": p_sample.reshape(depth, MS, -1).astype(BF16)}
    seq_len = {"p": T, "s": DEC_SEQ}
    ev = {"p": [], "s": []}
    od = {"p": [], "s": []}
    ff = {"p": [], "s": []}
    ptab = page_table.reshape(-1).astype(jnp.int32)

    for li in range(depth):
        dense_w = layer_dense_weights(li)
        if li % 2 == 0:
            e = li // 2
            w_in = reorder_in(w_in_even[e])
            w_out = w_out_even[e].astype(BF16)
            w_out_a, w_out_r = w_out[:Q_W], w_out[Q_W:]
            for grp in ("p", "s"):
                x = xs[grp]
                M = x.shape[0]
                L = seq_len[grp]
                nseq = M // L
                (z,) = _mm([x], [w_in], [0], [], lambda accs, ex: (accs[0],), [F32], _tile(M, 1024), 512, "in_proj",
                           norm=(0, norm_mix[li]))
                if grp == "p":
                    cos_t, sin_t = _rope_tables(jnp.arange(T))
                    tmq = _tile(T, 512)
                    tab_blocks = T // tmq
                else:
                    cos_t, sin_t = _rope_tables(jnp.tile(past_len + jnp.arange(DEC_SEQ), DB))
                    tmq = M
                    tab_blocks = 1
                q, kvb, kc, vc, ks, vs, kw, vw = _qk_prep(z, cos_t, sin_t, tab_blocks, q_norm[e], k_norm_sel[e],
                                                          k_norm_win[e], tmq)
                z3 = z.reshape(nseq, L, IN_PAD)

                if grp == "p":
                    n_rows = T // CMP_STRIDE
                    ccos, csin = _rope_tables(jnp.arange(n_rows) * CMP_STRIDE + CMP_STRIDE - 1)
                    ident = jnp.arange(B * (T // PAGE_SIZE), dtype=jnp.int32)
                    kcmp = _compress(kc.reshape(-1, N_KV * PAGE_SIZE, HEAD_DIM), ident, B, T // PAGE_SIZE, cmp_pe_k[e],
                                     cmp_w1_k[e], cmp_w2_k[e], k_norm_cmp[e], ccos, csin, True, "compress_k_prompt")
                    vcmp = _compress(vc.reshape(-1, N_KV * PAGE_SIZE, HEAD_DIM), ident, B, T // PAGE_SIZE, cmp_pe_v[e],
                                     cmp_w1_v[e], cmp_w2_v[e], k_norm_cmp[e], ccos, csin, False, "compress_v_prompt")
                    n_sel = T // SEL_BLOCK
                    nqb = T // Q_BLOCK
                    ovl_t = _overlap_matrix(n_rows, n_sel, n_sel).T
                    ex_t = ((np.arange(T // SEL_KT)[:, None, None] * SEL_KT + np.arange(SEL_KT)[None, :, None])
                            // SEL_BLOCK == np.arange(n_sel)[None, None, :])
                    vs_t = kvb.reshape(B, T // SEL_KT, SEL_KT, 4, N_KV, HEAD_DIM)[:, :, :, 1].transpose(0, 3, 1, 4, 2)
                    vw_t = kvb.reshape(B, nqb, Q_BLOCK, 4, N_KV, HEAD_DIM)[:, :, :, 3].transpose(0, 3, 1, 4, 2)
                    gates_t = z[:, COL_GT:COL_GT + 3 * N_HEADS].reshape(B, T, 3, N_KV, GROUP)
                    gates_t = gates_t.transpose(0, 3, 2, 4, 1).reshape(B, N_KV, 3 * GROUP, T)
                    o_nsa = _nsa_prompt(q.reshape(B, T, Q_W), kcmp, vcmp.swapaxes(2, 3), kvb.reshape(B, T, 4 * KV_W),
                                        vs_t, vw_t, gates_t, ovl_t, jnp.asarray(ex_t, BF16), B, T).reshape(M, Q_W)
                    y_rg, h_last = _rglru_prompt(z.reshape(B, T, IN_PAD), rg_conv_w[e], rg_conv_b[e], rg_w_a[e],
                                                 rg_b_a[e], rg_w_x[e], rg_b_x[e], rg_lam[e], B, T, _tile(T, 512))
                    y_rg = y_rg.reshape(M, D_RNN)
                    h_last = h_last.reshape(B, D_RNN)
                    new_conv = z3[:, T - (RG_CONV - 1):, COL_XR:COL_XR + D_RNN]
                    new_wk = kw.reshape(B, T, N_KV, HEAD_DIM)
                    new_wv = vw.reshape(B, T, N_KV, HEAD_DIM)
                    if T >= w_buf:
                        new_wk, new_wv = new_wk[:, T - w_buf:], new_wv[:, T - w_buf:]
                    else:
                        padw = ((0, 0), (w_buf - T, 0), (0, 0), (0, 0))
                        new_wk, new_wv = jnp.pad(new_wk, padw), jnp.pad(new_wv, padw)
                else:
                    n_rows = past_len // CMP_STRIDE
                    ccos, csin = _rope_tables(jnp.arange(n_rows) * CMP_STRIDE + CMP_STRIDE - 1)
                    ptab_e = ptab + e * cache_cmp_k.shape[1]
                    kcmp = _compress(cache_cmp_k.reshape(-1, N_KV * PAGE_SIZE, HEAD_DIM), ptab_e, DB, n_pages, cmp_pe_k[e],
                                     cmp_w1_k[e], cmp_w2_k[e], k_norm_cmp[e], ccos, csin, True, "compress_k_sample")
                    vcmp = _compress(cache_cmp_v.reshape(-1, N_KV * PAGE_SIZE, HEAD_DIM), ptab_e, DB, n_pages, cmp_pe_v[e],
                                     cmp_w1_v[e], cmp_w2_v[e], k_norm_cmp[e], ccos, csin, False, "compress_v_sample")
                    n_sel = -(-(past_len + DEC_SEQ) // SEL_BLOCK)
                    sel_cols = -(-n_sel // 128) * 128
                    ovl = _overlap_matrix(n_rows, n_sel, sel_cols)
                    keys = _pages_per_step(n_pages, SEL_PAGES_PER_STEP) * PAGE_SIZE
                    steps = n_pages * PAGE_SIZE // keys
                    ex = (np.arange(sel_cols)[None, :, None]
                          == (np.arange(steps)[:, None, None] * keys + np.arange(keys)[None, None, :]) // SEL_BLOCK)
                    q8 = jnp.pad(q.reshape(DB, DEC_SEQ, Q_W), ((0, 0), (0, Q_PAD - DEC_SEQ), (0, 0)))
                    kvnew = jnp.pad(kvb.reshape(DB, DEC_SEQ, 4 * KV_W), ((0, 0), (0, 128 - DEC_SEQ), (0, 0)))
                    gt = z3[:, :, COL_GT:COL_GT + 3 * N_HEADS].reshape(DB, DEC_SEQ, 3, N_KV, GROUP)
                    gt = gt.transpose(0, 3, 1, 2, 4).reshape(DB, N_KV, DEC_SEQ, 3 * GROUP)
                    gt8 = jnp.pad(gt, ((0, 0), (0, 0), (0, Q_PAD - DEC_SEQ), (0, 0)))
                    o8 = _nsa_sample(q8, kcmp, vcmp, cache_sel_k.reshape(-1, N_KV * PAGE_SIZE, HEAD_DIM),
                                     cache_sel_v.reshape(-1, N_KV * PAGE_SIZE, HEAD_DIM), ptab_e, kvnew,
                                     cache_win_k.reshape(-1, N_KV * w_buf, HEAD_DIM),
                                     cache_win_v.reshape(-1, N_KV * w_buf, HEAD_DIM),
                                     e * DB, gt8, ovl, jnp.asarray(ex, BF16), DB, past_len)
                    o_nsa = o8[:, :DEC_SEQ].reshape(M, Q_W)
                    xr3 = z3[:, :, COL_XR:COL_XR + D_RNN]
                    gr3 = z3[:, :, COL_GR:COL_GR + D_RNN]
                    y_t, h_last = _rglru_sample(xr3.transpose(1, 0, 2), gr3.transpose(1, 0, 2),
                                                state_rglru_conv[e].transpose(1, 0, 2), state_rglru_h[e],
                                                rg_conv_w[e], rg_conv_b[e], rg_w_a[e], rg_b_a[e], rg_w_x[e],
                                                rg_b_x[e], rg_lam[e])
                    y_rg = y_t.transpose(1, 0, 2).reshape(M, D_RNN)
                    new_conv = jnp.concatenate([state_rglru_conv[e], xr3], axis=1)[:, -(RG_CONV - 1):]
                    kk = jnp.concatenate([cache_win_k[e], kw.reshape(DB, DEC_SEQ, N_KV, HEAD_DIM)], axis=1)
                    vv = jnp.concatenate([cache_win_v[e], vw.reshape(DB, DEC_SEQ, N_KV, HEAD_DIM)], axis=1)
                    new_wk, new_wv = kk[:, -w_buf:], vv[:, -w_buf:]

                rs = lambda a: a.reshape(nseq, L, N_KV, HEAD_DIM)
                ev[grp].append((rs(kc), rs(vc), rs(ks), rs(vs), new_wk, new_wv, h_last, new_conv))
                (x1,) = _mm([o_nsa, y_rg], [w_out_a, w_out_r], [0, 1], [(x, "tile")],
                            lambda accs, exs: (exs[0] + accs[0] + accs[1],), [F32], _tile(M, 1024), 512, "mixer_out")
                xs[grp], nb = _dense_tail(x1, pe_in[grp][li], dense_w, None if grp == "p" else state_ffn_conv[li], L)
                ff[grp].append(nb)
        else:
            o = li // 2
            ar, ai, bbr, bbi = _s5_discretize(s5_lam_re[o], s5_lam_im[o], s5_log_dt[o],
                                              s5_b_re[o].transpose(2, 0, 1), s5_b_im[o].transpose(2, 0, 1))
            bw, cr, ci = _s5_block_weights(bbr, bbi, s5_c_re[o], s5_c_im[o])
            ar = ar.reshape(S5_NGB, 1, S5_LANES)
            ai = ai.reshape(S5_NGB, 1, S5_LANES)
            w_a = s5_w_glu_a[o].astype(BF16)
            w_b = s5_w_glu_b[o].astype(BF16)
            for grp in ("p", "s"):
                x = xs[grp]
                M = x.shape[0]
                L = seq_len[grp]
                glu = lambda accs, exs: (exs[0] + accs[0] * _sigmoid(accs[1]),)
                if grp == "p":
                    sl = T // S5_NSEG
                    hn = _rmsnorm(x, norm_mix[li], F32, _tile(M, 256))
                    zz, sre, sim = _s5_prompt(hn.reshape(B, S5_NSEG, sl, D_MODEL), bw.astype(BF16), cr.astype(BF16),
                                              ci.astype(BF16), ar, ai, s5_d[o], B, T)
                    sre = sre.reshape(B, S5_GROUPS, S5_STATE)
                    sim = sim.reshape(B, S5_GROUPS, S5_STATE)
                    (x1,) = _mm([zz.reshape(M, D_MODEL)], [w_a, w_b], [0, 0], [(x, "tile")], glu, [F32],
                                _tile(M, 1024), 512, "mixer_out")
                else:
                    hn = _rmsnorm(x, norm_mix[li], F32, _tile(M, 256))
                    u = hn.reshape(DB, DEC_SEQ, D_MODEL).transpose(1, 0, 2)
                    zz, sre, sim = _s5_sample(u, state_s5_re[o].reshape(DB, -1), state_s5_im[o].reshape(DB, -1),
                                              bw, cr, ci, ar, ai, s5_d[o])
                    zz = zz.transpose(1, 0, 2).reshape(M, D_MODEL)
                    sre = sre.reshape(DB, S5_GROUPS, S5_STATE)
                    sim = sim.reshape(DB, S5_GROUPS, S5_STATE)
                    (x1,) = _mm([zz], [w_a, w_b], [0, 0], [(x, "tile")], glu, [F32], M, 512, "mixer_out")
                od[grp].append((sre, sim))
                xs[grp], nb = _dense_tail(x1, pe_in[grp][li], dense_w, None if grp == "p" else state_ffn_conv[li], L)
                ff[grp].append(nb)

    def stk(states, i):
        return jnp.stack([s[i] for s in states])

    outs = [xs["p"].reshape(B, T, D_MODEL), xs["s"].reshape(DB, DEC_SEQ, D_MODEL)]
    for i in range(8):
        outs += [stk(ev["p"], i), stk(ev["s"], i)]
    for i in range(2):
        outs += [stk(od["p"], i), stk(od["s"], i)]
    outs += [jnp.stack(ff["p"]), jnp.stack(ff["s"])]
    return tuple(outs)
```

```python
import functools

import numpy as np
import jax
import jax.numpy as jnp
from jax import lax
from jax.experimental import pallas as pl
from jax.experimental.pallas import tpu as pltpu

F32 = jnp.float32
BF16 = jnp.bfloat16

D_MODEL = 2048
DEC_SEQ = 4
PAGE_SIZE = 128
N_HEADS = 8
HEAD_DIM = 128
N_KV = 2
GROUP = N_HEADS // N_KV
ROPE_DIM = HEAD_DIM // 4
ROPE_HALF = ROPE_DIM // 2
ROPE_THETA = 500000.0
CMP_BLOCK = 32
CMP_STRIDE = 16
SEL_BLOCK = 64
SEL_TOPK = 16
N_LOCAL = 2
WINDOW = 512
Q_BLOCK = 128
FORCED_SCORE = 1e9
NEG_BIG = -1e30
ATTN_SCALE = HEAD_DIM ** -0.5
D_RNN = D_MODEL // 2
RG_BLOCKS = 8
RG_BW = D_RNN // RG_BLOCKS
RG_CONV = 4
RG_C = 8.0
S5_CH = 16
S5_GROUPS = D_MODEL // S5_CH
S5_STATE = 64
S5_GB = 8
S5_NGB = S5_GROUPS // S5_GB
S5_LANES = S5_GB * S5_STATE
S5_NSEG = 8
D_FF = ((8 * D_MODEL // 3 + 255) // 256) * 256
FFN_CONV = 3
NORM_EPS = 1e-6
Q_W = N_HEADS * HEAD_DIM
KV_W = N_KV * HEAD_DIM
MIN_PAGES_PER_STEP = 8
CMP_PAGES_PER_STEP = 32
SEL_PAGES_PER_STEP = 16
Q_PAD = 8
CMP_SPLIT = 4
SEL_KT = 512

COL_Q = 0
COL_XR = Q_W
COL_GR = Q_W + D_RNN
COL_KV = Q_W + 2 * D_RNN
COL_GT = COL_KV + 6 * KV_W
IN_PAD = 5120


def _pages_per_step(n_pages, preferred):
    g = preferred
    while g > MIN_PAGES_PER_STEP and n_pages % g:
        g //= 2
    return g


def _cp(sem, vmem_mb=48):
    return pltpu.CompilerParams(dimension_semantics=sem, vmem_limit_bytes=vmem_mb * 1024 * 1024)


def _gelu(x):
    return 0.5 * x * (1.0 + jnp.tanh(0.7978845608028654 * (x + 0.044715 * (x * x * x))))


def _sigmoid(x):
    return 1.0 / (1.0 + jnp.exp(-x))


def _dotb(a, b):
    return jnp.dot(a.astype(BF16), b.astype(BF16), preferred_element_type=F32)


def _dotf(a, b):
    return jnp.dot(a, b, preferred_element_type=F32, precision=lax.Precision.HIGHEST)


def _dot_nt(a, b):
    return lax.dot_general(a, b, (((1,), (1,)), ((), ())), preferred_element_type=F32)


def _masked_softmax(s, mask):
    s = jnp.where(mask, s, NEG_BIG)
    m = jnp.max(s, axis=-1, keepdims=True)
    e = jnp.where(mask, jnp.exp(s - m), 0.0)
    return e / jnp.maximum(jnp.sum(e, axis=-1, keepdims=True), 1e-30)


def _norm_rope(x, g, cos_t, sin_t):
    ms = jnp.mean(x * x, axis=-1, keepdims=True)
    y = x * lax.rsqrt(ms + NORM_EPS) * g
    lane = lax.broadcasted_iota(jnp.int32, y.shape, 1)
    swapped = jnp.where(lane < ROPE_HALF, pltpu.roll(y, HEAD_DIM - ROPE_HALF, 1), pltpu.roll(y, ROPE_HALF, 1))
    return y * cos_t + swapped * sin_t


def _rope_tables(pos):
    inv = ROPE_THETA ** (-jnp.arange(ROPE_HALF, dtype=F32) * 2.0 / ROPE_DIM)
    ang = pos.astype(F32)[:, None] * inv
    cos = jnp.cos(ang)
    sin = jnp.sin(ang)
    n = pos.shape[0]
    ones = jnp.ones((n, HEAD_DIM - ROPE_DIM), F32)
    zeros = jnp.zeros((n, HEAD_DIM - ROPE_DIM), F32)
    return (jnp.concatenate([cos, cos, ones], axis=1), jnp.concatenate([-sin, sin, zeros], axis=1))


def _overlap_matrix(n_rows, n_sel, n_cols):
    n = np.arange(n_rows)[:, None] - 1
    j = np.arange(n_cols)[None]
    c0 = n * CMP_STRIDE
    s0 = j * SEL_BLOCK
    ov = np.minimum(c0 + CMP_BLOCK, s0 + SEL_BLOCK) - np.maximum(c0, s0)
    ov = np.maximum(ov, 0).astype(np.float32) / CMP_BLOCK
    ov = np.where((n >= 0) & (j < n_sel), ov, 0.0)
    return jnp.asarray(ov, F32)


def _rmsnorm(x, g, out_dtype, tm):
    M, D = x.shape

    def body(x_ref, g_ref, o_ref):
        xv = x_ref[...]
        ms = jnp.mean(xv * xv, axis=-1, keepdims=True)
        o_ref[...] = (xv * lax.rsqrt(ms + NORM_EPS) * g_ref[...]).astype(o_ref.dtype)

    return pl.pallas_call(
        body, grid=(M // tm,),
        in_specs=[pl.BlockSpec((tm, D), lambda i: (i, 0)), pl.BlockSpec((1, D), lambda i: (0, 0))],
        out_specs=pl.BlockSpec((tm, D), lambda i: (i, 0)),
        out_shape=jax.ShapeDtypeStruct((M, D), out_dtype),
        compiler_params=_cp(("parallel",)), name="rmsnorm")(x, g.reshape(1, D))


def _rms_rows(xv, g):
    ms = jnp.mean(xv * xv, axis=-1, keepdims=True)
    return xv * lax.rsqrt(ms + NORM_EPS) * g


def _weight_spec(w, tn):
    if isinstance(w, tuple):
        arr, layer = w
        return arr, pl.BlockSpec((None, arr.shape[1], tn), lambda i, j: (layer, 0, j))
    return w, pl.BlockSpec((w.shape[0], tn), lambda i, j: (0, j))


def _mm(a_list, w_list, pair, extras, epilogue, out_dtypes, tm, tn, name, norm=None, vmem_mb=48):
    w_arrays, w_specs = zip(*[_weight_spec(w, tn) for w in w_list])
    N = w_arrays[0].shape[-1]
    na, nw, ne, no = len(a_list), len(w_list), len(extras), len(out_dtypes)
    M = a_list[0].shape[0]

    def body(*refs):
        if norm is not None:
            g_ref = refs[na + nw + ne]
            an_ref = refs[na + nw + ne + 1 + no]

            @pl.when(pl.program_id(1) == 0)
            def _():
                an_ref[...] = _rms_rows(refs[norm[0]][...], g_ref[...]).astype(BF16)

        a_vals = []
        for k, r in enumerate(refs[:na]):
            if norm is not None and k == norm[0]:
                a_vals.append(an_ref[...])
            else:
                a_vals.append(r[...].astype(BF16))
        accs = [jnp.dot(a_vals[pair[i]], refs[na + i][...], preferred_element_type=F32) for i in range(nw)]
        ex = [r[...] for r in refs[na + nw:na + nw + ne]]
        res = epilogue(accs, ex)
        first_out = na + nw + ne + (1 if norm is not None else 0)
        for o_ref, v in zip(refs[first_out:first_out + no], res):
            o_ref[...] = v.astype(o_ref.dtype)

    in_specs = [pl.BlockSpec((tm, a.shape[1]), lambda i, j: (i, 0)) for a in a_list]
    in_specs += list(w_specs)
    for arr, kind in extras:
        if kind == "tile":
            in_specs.append(pl.BlockSpec((tm, tn), lambda i, j: (i, j)))
        else:
            in_specs.append(pl.BlockSpec((arr.shape[0], tn), lambda i, j: (0, j)))
    operands = [*a_list, *w_arrays, *[e[0] for e in extras]]
    scratch = []
    if norm is not None:
        kn = a_list[norm[0]].shape[1]
        in_specs.append(pl.BlockSpec((1, kn), lambda i, j: (0, 0)))
        operands.append(norm[1].reshape(1, kn))
        scratch.append(pltpu.VMEM((tm, kn), BF16))
    return pl.pallas_call(
        body, grid=(M // tm, N // tn), in_specs=in_specs,
        out_specs=[pl.BlockSpec((tm, tn), lambda i, j: (i, j)) for _ in out_dtypes],
        out_shape=[jax.ShapeDtypeStruct((M, N), dt) for dt in out_dtypes],
        scratch_shapes=scratch,
        compiler_params=_cp(("parallel", "arbitrary" if norm is not None else "parallel"), vmem_mb), name=name,
    )(*operands)


def _qk_prep(z, cos_t, sin_t, tab_blocks, qn, kns, knw, tm):
    M = z.shape[0]

    def body(zq_ref, zkv_ref, c_ref, s_ref, qn_ref, kns_ref, knw_ref, q_ref, kvb_ref, *row_refs):
        c = c_ref[...]
        s = s_ref[...]
        for h in range(N_HEADS):
            sl = slice(h * HEAD_DIM, (h + 1) * HEAD_DIM)
            q_ref[:, sl] = _norm_rope(zq_ref[:, sl], qn_ref[...], c, s).astype(BF16)
        for kv in range(N_KV):
            vals = [zkv_ref[:, (k * N_KV + kv) * HEAD_DIM:(k * N_KV + kv + 1) * HEAD_DIM] for k in range(6)]
            vals[2] = _norm_rope(vals[2], kns_ref[...], c, s)
            vals[4] = _norm_rope(vals[4], knw_ref[...], c, s)
            for k in range(6):
                row_refs[k][pl.ds(kv, tm, stride=N_KV), :] = vals[k]
            for k in range(2, 6):
                kvb_ref[:, ((k - 2) * N_KV + kv) * HEAD_DIM:((k - 2) * N_KV + kv + 1) * HEAD_DIM] = vals[k].astype(BF16)

    vec = pl.BlockSpec((1, HEAD_DIM), lambda i: (0, 0))
    tab = pl.BlockSpec((tm, HEAD_DIM), lambda i: (i % tab_blocks, 0))
    rows_spec = pl.BlockSpec((N_KV * tm, HEAD_DIM), lambda i: (i, 0))
    rows_shape = jax.ShapeDtypeStruct((N_KV * M, HEAD_DIM), F32)
    return pl.pallas_call(
        body, grid=(M // tm,),
        in_specs=[pl.BlockSpec((tm, Q_W), lambda i: (i, COL_Q // Q_W)),
                  pl.BlockSpec((tm, 6 * KV_W), lambda i: (i, COL_KV // (6 * KV_W))),
                  tab, tab, vec, vec, vec],
        out_specs=[pl.BlockSpec((tm, Q_W), lambda i: (i, 0)), pl.BlockSpec((tm, 4 * KV_W), lambda i: (i, 0))]
        + [rows_spec] * 6,
        out_shape=[jax.ShapeDtypeStruct((M, Q_W), BF16), jax.ShapeDtypeStruct((M, 4 * KV_W), BF16)] + [rows_shape] * 6,
        compiler_params=_cp(("parallel",)), name="qk_prep",
    )(z, z, cos_t, sin_t, qn.reshape(1, -1), kns.reshape(1, -1), knw.reshape(1, -1))


def _compress_bias(pe, w1):
    def body(pe_ref, w1_ref, o_ref):
        for half in range(2):
            acc = jnp.zeros((8, HEAD_DIM), F32)
            for c in range(half * CMP_STRIDE, (half + 1) * CMP_STRIDE):
                acc = acc + _dotf(jnp.broadcast_to(pe_ref[c:c + 1, :], (8, HEAD_DIM)), w1_ref[c])
            o_ref[:, half * HEAD_DIM:(half + 1) * HEAD_DIM] = acc[0:1, :]

    return pl.pallas_call(body, out_shape=jax.ShapeDtypeStruct((1, 2 * HEAD_DIM), F32), name="compress_bias")(pe, w1)


def _compress(pages, ptab, n_seq, pages_per_seq, pe, w1, w2, norm_g, cos_t, sin_t, do_norm, name):
    bias = _compress_bias(pe, w1)
    w1cat = jnp.concatenate([w1[:CMP_STRIDE], w1[CMP_STRIDE:]], axis=2).astype(BF16)
    G = _pages_per_step(pages_per_seq, CMP_PAGES_PER_STEP)
    steps = pages_per_seq // G
    cpp = PAGE_SIZE // CMP_STRIDE
    ch = G * cpp
    n_chunk = pages_per_seq * cpp

    def body(pt_ref, *refs):
        page_refs = refs[:G]
        bias_ref, w1_ref, w2_ref, g_ref, c_ref, s_ref, o_ref, carry_ref, split_ref = refs[G:]
        step = pl.program_id(1)

        @pl.when(step == 0)
        def _():
            carry_ref[...] = jnp.zeros_like(carry_ref)

        for g, pr in enumerate(page_refs):
            for j in range(CMP_SPLIT):
                split_ref[g, j] = pr[0, pl.ds(j, N_KV * PAGE_SIZE // CMP_SPLIT, stride=CMP_SPLIT), :]

        row = lax.broadcasted_iota(jnp.int32, (ch, HEAD_DIM), 0)
        hop = N_KV * CMP_STRIDE // CMP_SPLIT
        acc2 = jnp.zeros((N_KV * ch, 2 * HEAD_DIM), F32)
        for c in range(CMP_STRIDE):
            pieces = []
            for kv in range(N_KV):
                j = (N_KV * c + kv) % CMP_SPLIT
                start = (N_KV * c + kv) // CMP_SPLIT
                pieces += [split_ref[g, j, pl.ds(start, cpp, stride=hop), :] for g in range(G)]
            acc2 = acc2 + _dotb(jnp.concatenate(pieces, axis=0), w1_ref[c])
        acc2 = acc2 + bias_ref[...]
        for kv in range(N_KV):
            lo = acc2[kv * ch:(kv + 1) * ch, :HEAD_DIM]
            hi = acc2[kv * ch:(kv + 1) * ch, HEAD_DIM:]
            lo_prev = jnp.where(row == 0, carry_ref[kv, 0:1, :], pltpu.roll(lo, 1, 0))
            carry_ref[kv, 0:1, :] = lo[ch - 1:ch, :]
            out = _dotb(_gelu(lo_prev + hi), w2_ref[...])
            if do_norm:
                out = _norm_rope(out, g_ref[...], c_ref[...], s_ref[...])
            o_ref[0, kv] = out.astype(BF16)

    def page_map(j, s, g, pt):
        return (pt[s * pages_per_seq + g * G + j], 0, 0)

    full2 = lambda s, g, pt: (0, 0)
    in_specs = [pl.BlockSpec((1, N_KV * PAGE_SIZE, HEAD_DIM), functools.partial(page_map, j)) for j in range(G)]
    in_specs += [pl.BlockSpec((1, 2 * HEAD_DIM), full2),
                 pl.BlockSpec((CMP_STRIDE, HEAD_DIM, 2 * HEAD_DIM), lambda s, g, pt: (0, 0, 0)),
                 pl.BlockSpec((HEAD_DIM, HEAD_DIM), full2),
                 pl.BlockSpec((1, HEAD_DIM), full2),
                 pl.BlockSpec((ch, HEAD_DIM), lambda s, g, pt: (g, 0)),
                 pl.BlockSpec((ch, HEAD_DIM), lambda s, g, pt: (g, 0))]
    gs = pltpu.PrefetchScalarGridSpec(
        num_scalar_prefetch=1, grid=(n_seq, steps), in_specs=in_specs,
        out_specs=pl.BlockSpec((1, N_KV, ch, HEAD_DIM), lambda s, g, pt: (s, 0, g, 0)),
        scratch_shapes=[pltpu.VMEM((N_KV, 8, HEAD_DIM), F32),
                        pltpu.VMEM((G, CMP_SPLIT, N_KV * PAGE_SIZE // CMP_SPLIT, HEAD_DIM), F32)])
    return pl.pallas_call(
        body, grid_spec=gs, out_shape=jax.ShapeDtypeStruct((n_seq, N_KV, n_chunk, HEAD_DIM), BF16),
        compiler_params=_cp(("parallel", "arbitrary")), name=name,
    )(ptab, *([pages] * G), bias, w1cat, w2.astype(BF16), norm_g.reshape(1, -1), cos_t, sin_t)


def _nsa_prompt(q, kcmp, vcmp_t, kvb, vs_t, vw_t, gates_t, ovl_t, expand_t, B, T):
    nqb = T // Q_BLOCK
    n_cmp_rows = T // CMP_STRIDE
    n_sel = T // SEL_BLOCK
    cols = GROUP * Q_BLOCK
    win_tiles = WINDOW // Q_BLOCK
    win_keys = (win_tiles + 1) * Q_BLOCK
    assert T % SEL_KT == 0 and T >= win_keys
    topk = min(SEL_TOPK, n_sel)

    def body(q_ref, kc_ref, vc_ref, ks_ref, vs_ref, kw_ref, vw_ref, gt_ref, ovl_ref, ex_ref, o_ref):
        i = pl.program_id(2)
        q4 = q_ref[0].astype(F32)
        qt = jnp.concatenate([q4[:, h * HEAD_DIM:(h + 1) * HEAD_DIM].T for h in range(GROUP)], axis=1).astype(BF16)
        pos = i * Q_BLOCK + (lax.broadcasted_iota(jnp.int32, (1, cols), 1) & (Q_BLOCK - 1))

        s = jnp.dot(kc_ref[0, 0], qt, preferred_element_type=F32) * ATTN_SCALE
        crow = lax.broadcasted_iota(jnp.int32, (n_cmp_rows, cols), 0)
        cmask = (crow >= 1) & (crow * CMP_STRIDE + (CMP_BLOCK - 1 - CMP_STRIDE) <= pos)
        s = jnp.where(cmask, s, NEG_BIG)
        e = jnp.where(cmask, jnp.exp(s - jnp.max(s, axis=0, keepdims=True)), 0.0)
        p = e * (1.0 / jnp.maximum(jnp.sum(e, axis=0, keepdims=True), 1e-30))
        o_cmp = _dotb(vc_ref[0, 0], p)
        imp = p[:, 0:Q_BLOCK]
        for h in range(1, GROUP):
            imp = imp + p[:, h * Q_BLOCK:(h + 1) * Q_BLOCK]

        score = _dotf(ovl_ref[...], imp)
        j = lax.broadcasted_iota(jnp.int32, (n_sel, Q_BLOCK), 0)
        qblk = (i * Q_BLOCK + lax.broadcasted_iota(jnp.int32, (n_sel, Q_BLOCK), 1)) // SEL_BLOCK
        forced = (j == 0) | ((j <= qblk) & (j > qblk - N_LOCAL))
        score = jnp.where(forced, FORCED_SCORE, jnp.where(j > qblk, NEG_BIG, score))
        rank = jnp.zeros((n_sel, Q_BLOCK), F32)
        for r in range(n_sel):
            sr = score[r:r + 1, :]
            beats = (sr > score) | ((sr == score) & (j > r))
            rank = rank + jnp.where(beats, 1.0, 0.0)
        sel_bias = jnp.where(rank < topk, 0.0, NEG_BIG).astype(BF16)

        krow = lax.broadcasted_iota(jnp.int32, (SEL_KT, cols), 0)

        def sel_tile(u, carry, causal):
            m, l, acc = carry
            k = ks_ref[0, pl.ds(pl.multiple_of(u * SEL_KT, SEL_KT), SEL_KT), :]
            b1 = _dotb(ex_ref[u], sel_bias)
            sc = (jnp.dot(k, qt, preferred_element_type=F32) * ATTN_SCALE
                  + jnp.concatenate([b1] * GROUP, axis=1))
            if causal:
                sc = jnp.where(u * SEL_KT + krow <= pos, sc, NEG_BIG)
            m_new = jnp.maximum(m, jnp.max(sc, axis=0, keepdims=True))
            alpha = jnp.exp(m - m_new)
            ex = jnp.exp(sc - m_new)
            l = alpha * l + jnp.sum(ex, axis=0, keepdims=True)
            acc = alpha * acc + _dotb(vs_ref[0, 0, u], ex)
            return m_new, l, acc

        init = (jnp.full((1, cols), NEG_BIG, F32), jnp.zeros((1, cols), F32), jnp.zeros((HEAD_DIM, cols), F32))
        last = (i * Q_BLOCK + Q_BLOCK + SEL_KT - 1) // SEL_KT - 1
        carry = lax.fori_loop(0, last, functools.partial(sel_tile, causal=False), init)
        _, l_sel, acc_sel = sel_tile(last, carry, causal=True)
        o_sel = acc_sel * (1.0 / jnp.maximum(l_sel, 1e-30))

        t0 = jnp.maximum(i - win_tiles, 0)
        kwin = kw_ref[0, pl.ds(pl.multiple_of(t0 * Q_BLOCK, Q_BLOCK), win_keys), :]
        sw = jnp.dot(kwin, qt, preferred_element_type=F32) * ATTN_SCALE
        kpos = t0 * Q_BLOCK + lax.broadcasted_iota(jnp.int32, (win_keys, cols), 0)
        wmask = (kpos <= pos) & (kpos > pos - WINDOW)
        sw = jnp.where(wmask, sw, NEG_BIG)
        ew = jnp.exp(sw - jnp.max(sw, axis=0, keepdims=True))
        o_win = jnp.zeros((HEAD_DIM, cols), F32)
        for t in range(win_tiles + 1):
            o_win = o_win + _dotb(vw_ref[0, 0, t0 + t], ew[t * Q_BLOCK:(t + 1) * Q_BLOCK])
        o_win = o_win * (1.0 / jnp.maximum(jnp.sum(ew, axis=0, keepdims=True), 1e-30))

        g = _sigmoid(gt_ref[0, 0])
        for h in range(GROUP):
            c = slice(h * Q_BLOCK, (h + 1) * Q_BLOCK)
            o = (g[h:h + 1, :] * o_cmp[:, c] + g[GROUP + h:GROUP + h + 1, :] * o_sel[:, c]
                 + g[2 * GROUP + h:2 * GROUP + h + 1, :] * o_win[:, c])
            o_ref[0, :, h * HEAD_DIM:(h + 1) * HEAD_DIM] = o.T.astype(BF16)

    def k_spec(off):
        return pl.BlockSpec((1, T, HEAD_DIM), lambda b, kv, i: (b, 0, off + kv))

    vs_spec = pl.BlockSpec((1, 1, T // SEL_KT, HEAD_DIM, SEL_KT), lambda b, kv, i: (b, kv, 0, 0, 0))
    vw_spec = pl.BlockSpec((1, 1, nqb, HEAD_DIM, Q_BLOCK), lambda b, kv, i: (b, kv, 0, 0, 0))
    qo_spec = pl.BlockSpec((1, Q_BLOCK, GROUP * HEAD_DIM), lambda b, kv, i: (b, i, kv))
    return pl.pallas_call(
        body, grid=(B, N_KV, nqb),
        in_specs=[qo_spec,
                  pl.BlockSpec((1, 1, n_cmp_rows, HEAD_DIM), lambda b, kv, i: (b, kv, 0, 0)),
                  pl.BlockSpec((1, 1, HEAD_DIM, n_cmp_rows), lambda b, kv, i: (b, kv, 0, 0)),
                  k_spec(0), vs_spec, k_spec(4), vw_spec,
                  pl.BlockSpec((1, 1, 3 * GROUP, Q_BLOCK), lambda b, kv, i: (b, kv, 0, i)),
                  pl.BlockSpec((n_sel, n_cmp_rows), lambda b, kv, i: (0, 0)),
                  pl.BlockSpec((T // SEL_KT, SEL_KT, n_sel), lambda b, kv, i: (0, 0, 0))],
        out_specs=qo_spec,
        out_shape=jax.ShapeDtypeStruct((B, T, Q_W), BF16),
        compiler_params=_cp(("parallel", "parallel", "arbitrary")), name="nsa_prompt",
    )(q, kcmp, vcmp_t, kvb, vs_t, kvb, vw_t, gates_t, ovl_t, expand_t)


def _nsa_sample(q, kcmp, vcmp, pool_k, pool_v, ptab, kvnew, win_k, win_v, win_off, gates, ovl, expand, DB, past_len):
    n_pages = past_len // PAGE_SIZE
    G = _pages_per_step(n_pages, SEL_PAGES_PER_STEP)
    steps = n_pages // G
    n_cmp_rows = kcmp.shape[2]
    sel_cols = ovl.shape[1]
    n_past_blk = past_len // SEL_BLOCK
    n_sel = -(-(past_len + DEC_SEQ) // SEL_BLOCK)
    w_buf = win_k.shape[1] // N_KV
    rows = GROUP * Q_PAD
    keys = G * PAGE_SIZE
    new_rows = kvnew.shape[1]

    def body(pt_ref, q_ref, kc_ref, vc_ref, ovl_ref, *rest):
        pk = rest[:G]
        pv = rest[G:2 * G]
        (ex_ref, kvn_ref, wk_ref, wv_ref, gt_ref, o_ref,
         sel_scr, m_scr, l_scr, acc_scr, base_scr) = rest[2 * G:]
        step = pl.program_id(1)
        tok = lax.broadcasted_iota(jnp.int32, (rows, 1), 0) & (Q_PAD - 1)
        q8 = q_ref[0]

        def q_rows(kv):
            return jnp.concatenate(
                [q8[:, (kv * GROUP + h) * HEAD_DIM:(kv * GROUP + h + 1) * HEAD_DIM] for h in range(GROUP)], axis=0)

        def head_col(gt, c0):
            return jnp.concatenate(
                [jnp.broadcast_to(gt[:, c0 + h:c0 + h + 1], (Q_PAD, HEAD_DIM)) for h in range(GROUP)], axis=0)

        @pl.when(step == 0)
        def _():
            o_cmps = []
            imps = []
            for kv in range(N_KV):
                s = _dot_nt(q_rows(kv), kc_ref[0, kv]) * ATTN_SCALE
                col = lax.broadcasted_iota(jnp.int32, (rows, n_cmp_rows), 1)
                cmask = (col >= 1) & (col * CMP_STRIDE + (CMP_BLOCK - 1 - CMP_STRIDE) <= past_len + tok)
                p = _masked_softmax(s, cmask)
                o_cmps.append(_dotb(p, vc_ref[0, kv]))
                imp = p[0:Q_PAD]
                for h in range(1, GROUP):
                    imp = imp + p[h * Q_PAD:(h + 1) * Q_PAD]
                imps.append(imp)

            score = _dotf(jnp.concatenate(imps, axis=0), ovl_ref[...])
            j = lax.broadcasted_iota(jnp.int32, (N_KV * Q_PAD, sel_cols), 1)
            trow = lax.broadcasted_iota(jnp.int32, (N_KV * Q_PAD, sel_cols), 0) & (Q_PAD - 1)
            qblk = (past_len + trow) // SEL_BLOCK
            forced = (j == 0) | ((j <= qblk) & (j > qblk - N_LOCAL))
            score = jnp.where(forced, FORCED_SCORE, jnp.where(j > qblk, NEG_BIG, score))
            score = jnp.where(j < n_sel, score, -jnp.inf)
            rank = jnp.zeros((N_KV * Q_PAD, sel_cols), F32)
            for r in range(n_sel):
                sr = score[:, r:r + 1]
                rank = rank + jnp.where((sr > score) | ((sr == score) & (j > r)), 1.0, 0.0)
            sel_all = jnp.where((rank < min(SEL_TOPK, n_sel)) & (j < n_sel), 1.0, 0.0)

            for kv in range(N_KV):
                qh = q_rows(kv)
                o_cmp = o_cmps[kv]
                sel4 = jnp.concatenate([sel_all[kv * Q_PAD:(kv + 1) * Q_PAD]] * GROUP, axis=0)
                sel_scr[kv] = sel4.astype(BF16)

                kb = wk_ref[0, pl.ds(kv, w_buf, stride=N_KV), :].astype(BF16)
                vb = wv_ref[0, pl.ds(kv, w_buf, stride=N_KV), :].astype(BF16)
                kn = kvn_ref[0, :, (4 + kv) * HEAD_DIM:(5 + kv) * HEAD_DIM]
                vn = kvn_ref[0, :, (6 + kv) * HEAD_DIM:(7 + kv) * HEAD_DIM]
                s1 = jnp.where(lax.broadcasted_iota(jnp.int32, (rows, w_buf), 1) - w_buf > tok - WINDOW,
                               _dot_nt(qh, kb) * ATTN_SCALE, NEG_BIG)
                ncol = lax.broadcasted_iota(jnp.int32, (rows, new_rows), 1)
                nmask = (ncol <= tok) & (ncol < DEC_SEQ)
                s2 = jnp.where(nmask, _dot_nt(qh, kn) * ATTN_SCALE, NEG_BIG)
                m = jnp.maximum(jnp.max(s1, axis=-1, keepdims=True), jnp.max(s2, axis=-1, keepdims=True))
                e1 = jnp.where(s1 > 0.5 * NEG_BIG, jnp.exp(s1 - m), 0.0)
                e2 = jnp.where(nmask, jnp.exp(s2 - m), 0.0)
                den = jnp.sum(e1, axis=-1, keepdims=True) + jnp.sum(e2, axis=-1, keepdims=True)
                o_win = (_dotb(e1, vb) + _dotb(e2, vn)) / jnp.maximum(den, 1e-30)

                gt = _sigmoid(gt_ref[0, kv])
                base_scr[kv] = head_col(gt, 0) * o_cmp + head_col(gt, 2 * GROUP) * o_win

                ksn = kvn_ref[0, :, kv * HEAD_DIM:(kv + 1) * HEAD_DIM]
                vsn = kvn_ref[0, :, (2 + kv) * HEAD_DIM:(3 + kv) * HEAD_DIM]
                chosen = sel4[:, n_past_blk:n_past_blk + 1] > 0.5
                smask = nmask & chosen
                s3 = jnp.where(smask, _dot_nt(qh, ksn) * ATTN_SCALE, NEG_BIG)
                m0 = jnp.max(s3, axis=-1, keepdims=True)
                e3 = jnp.where(smask, jnp.exp(s3 - m0), 0.0)
                m_scr[kv] = jnp.broadcast_to(m0, (rows, HEAD_DIM))
                l_scr[kv] = jnp.broadcast_to(jnp.sum(e3, axis=-1, keepdims=True), (rows, HEAD_DIM))
                acc_scr[kv] = _dotb(e3, vsn)

        for kv in range(N_KV):
            qh = q_rows(kv)
            k = jnp.concatenate([r[0, pl.ds(kv, PAGE_SIZE, stride=N_KV), :] for r in pk], axis=0).astype(BF16)
            v = jnp.concatenate([r[0, pl.ds(kv, PAGE_SIZE, stride=N_KV), :] for r in pv], axis=0).astype(BF16)
            mask = _dotb(sel_scr[kv], ex_ref[0]) > 0.5
            sc = jnp.where(mask, _dot_nt(qh, k) * ATTN_SCALE, NEG_BIG)
            m_old = m_scr[kv][:, 0:1]
            l_old = l_scr[kv][:, 0:1]
            m_new = jnp.maximum(m_old, jnp.max(sc, axis=-1, keepdims=True))
            alpha = jnp.exp(m_old - m_new)
            e = jnp.where(mask, jnp.exp(sc - m_new), 0.0)
            l_new = alpha * l_old + jnp.sum(e, axis=-1, keepdims=True)
            acc = alpha * acc_scr[kv] + _dotb(e, v)
            m_scr[kv] = jnp.broadcast_to(m_new, (rows, HEAD_DIM))
            l_scr[kv] = jnp.broadcast_to(l_new, (rows, HEAD_DIM))
            acc_scr[kv] = acc

        @pl.when(step == steps - 1)
        def _():
            for kv in range(N_KV):
                gt = _sigmoid(gt_ref[0, kv])
                o_sel = acc_scr[kv] / jnp.maximum(l_scr[kv], 1e-30)
                o = base_scr[kv] + head_col(gt, GROUP) * o_sel
                for h in range(GROUP):
                    hh = kv * GROUP + h
                    o_ref[0, :, hh * HEAD_DIM:(hh + 1) * HEAD_DIM] = o[h * Q_PAD:(h + 1) * Q_PAD].astype(BF16)

    def page_map(jj, b, g, pt):
        return (pt[b * n_pages + g * G + jj], 0, 0)

    page_specs = [pl.BlockSpec((1, N_KV * PAGE_SIZE, HEAD_DIM), functools.partial(page_map, jj)) for jj in range(G)]
    cmp_spec = pl.BlockSpec((1, N_KV, n_cmp_rows, HEAD_DIM), lambda b, g, pt: (b, 0, 0, 0))
    in_specs = [pl.BlockSpec((1, Q_PAD, Q_W), lambda b, g, pt: (b, 0, 0)), cmp_spec, cmp_spec,
                pl.BlockSpec((n_cmp_rows, sel_cols), lambda b, g, pt: (0, 0))]
    in_specs += page_specs + page_specs
    in_specs += [pl.BlockSpec((1, sel_cols, keys), lambda b, g, pt: (g, 0, 0)),
                 pl.BlockSpec((1, new_rows, 4 * KV_W), lambda b, g, pt: (b, 0, 0)),
                 pl.BlockSpec((1, N_KV * w_buf, HEAD_DIM), lambda b, g, pt: (win_off + b, 0, 0)),
                 pl.BlockSpec((1, N_KV * w_buf, HEAD_DIM), lambda b, g, pt: (win_off + b, 0, 0)),
                 pl.BlockSpec((1, N_KV, Q_PAD, 3 * GROUP), lambda b, g, pt: (b, 0, 0, 0))]
    gs = pltpu.PrefetchScalarGridSpec(
        num_scalar_prefetch=1, grid=(DB, steps), in_specs=in_specs,
        out_specs=pl.BlockSpec((1, Q_PAD, Q_W), lambda b, g, pt: (b, 0, 0)),
        scratch_shapes=[pltpu.VMEM((N_KV, rows, sel_cols), BF16), pltpu.VMEM((N_KV, rows, HEAD_DIM), F32),
                        pltpu.VMEM((N_KV, rows, HEAD_DIM), F32), pltpu.VMEM((N_KV, rows, HEAD_DIM), F32),
                        pltpu.VMEM((N_KV, rows, HEAD_DIM), F32)])
    return pl.pallas_call(
        body, grid_spec=gs, out_shape=jax.ShapeDtypeStruct((DB, Q_PAD, Q_W), BF16),
        compiler_params=_cp(("parallel", "arbitrary")), name="nsa_sample",
    )(ptab, q, kcmp, vcmp, ovl, *([pool_k] * G), *([pool_v] * G), expand, kvnew, win_k, win_v, gates)


def _rg_gates(u, wa_ref, ba, wx_ref, bx, sp):
    ra = []
    rx = []
    for n in range(RG_BLOCKS):
        ub = u[:, n * RG_BW:(n + 1) * RG_BW]
        ra.append(_dotf(ub, wa_ref[n]))
        rx.append(_dotf(ub, wx_ref[n]))
    r = _sigmoid(jnp.concatenate(ra, axis=1) + ba)
    i = _sigmoid(jnp.concatenate(rx, axis=1) + bx)
    log_a = -RG_C * r * sp
    a = jnp.exp(log_a)
    b = jnp.sqrt(1.0 - jnp.exp(2.0 * log_a)) * i * u
    return a, b


def _softplus_neg(lam):
    return jnp.maximum(-lam, 0.0) + jnp.log(1.0 + jnp.exp(-jnp.abs(lam)))


def _rglru_prompt(z, cw, cb, wa, ba, wx, bx, lam, B, T, tt):
    keep = 8

    def body(xr_ref, gr_ref, cw_ref, cb_ref, wa_ref, ba_ref, wx_ref, bx_ref, lam_ref,
             y_ref, hl_ref, a_scr, b_scr, h_scr, tail_scr):
        t = pl.program_id(0)

        @pl.when(t == 0)
        def _():
            h_scr[...] = jnp.zeros_like(h_scr)
            tail_scr[...] = jnp.zeros_like(tail_scr)

        row = lax.broadcasted_iota(jnp.int32, (tt, D_RNN), 0)
        cwv = cw_ref[...]
        sp = _softplus_neg(lam_ref[...])
        for bi in range(B):
            x = xr_ref[bi]
            u = cb_ref[...] + x * cwv[RG_CONV - 1:RG_CONV]
            for d in range(1, RG_CONV):
                xs = pltpu.roll(x, d, 0)
                for r in range(d):
                    xs = jnp.where(row == r, tail_scr[bi, keep - d + r:keep - d + r + 1, :], xs)
                u = u + xs * cwv[RG_CONV - 1 - d:RG_CONV - d]
            tail_scr[bi] = x[tt - keep:tt]
            a, b = _rg_gates(u, wa_ref, ba_ref[...], wx_ref, bx_ref[...], sp)
            a_scr[bi] = a
            b_scr[bi] = b

        def step(s, hs):
            out = []
            for bi in range(B):
                h = a_scr[bi, pl.ds(s, 1), :] * hs[bi] + b_scr[bi, pl.ds(s, 1), :]
                b_scr[bi, pl.ds(s, 1), :] = h
                out.append(h)
            return tuple(out)

        hs = lax.fori_loop(0, tt, step, tuple(h_scr[bi, 0:1, :] for bi in range(B)), unroll=8)
        for bi in range(B):
            h_scr[bi, 0:1, :] = hs[bi]
            hl_ref[bi] = hs[bi]
            y_ref[bi] = (b_scr[bi] * _gelu(gr_ref[bi])).astype(BF16)

    vec = pl.BlockSpec((1, D_RNN), lambda t: (0, 0))
    wspec = pl.BlockSpec((RG_BLOCKS, RG_BW, RG_BW), lambda t: (0, 0, 0))
    return pl.pallas_call(
        body, grid=(T // tt,),
        in_specs=[pl.BlockSpec((B, tt, D_RNN), lambda t: (0, t, COL_XR // D_RNN)),
                  pl.BlockSpec((B, tt, D_RNN), lambda t: (0, t, COL_GR // D_RNN)),
                  pl.BlockSpec((RG_CONV, D_RNN), lambda t: (0, 0)), vec, wspec, vec, wspec, vec, vec],
        out_specs=[pl.BlockSpec((B, tt, D_RNN), lambda t: (0, t, 0)),
                   pl.BlockSpec((B, 1, D_RNN), lambda t: (0, 0, 0))],
        out_shape=[jax.ShapeDtypeStruct((B, T, D_RNN), BF16), jax.ShapeDtypeStruct((B, 1, D_RNN), F32)],
        scratch_shapes=[pltpu.VMEM((B, tt, D_RNN), F32), pltpu.VMEM((B, tt, D_RNN), F32),
                        pltpu.VMEM((B, 8, D_RNN), F32), pltpu.VMEM((B, keep, D_RNN), F32)],
        compiler_params=_cp(("arbitrary",)), name="rglru_prompt",
    )(z, z, cw, cb.reshape(1, -1), wa, ba.reshape(1, -1), wx, bx.reshape(1, -1), lam.reshape(1, -1))


def _rglru_sample(xr, gr, buf, h0, cw, cb, wa, ba, wx, bx, lam):
    DB = h0.shape[0]

    def body(xr_ref, gr_ref, buf_ref, h0_ref, cw_ref, cb_ref, wa_ref, ba_ref, wx_ref, bx_ref, lam_ref, y_ref, hl_ref):
        xp = [buf_ref[k] for k in range(RG_CONV - 1)] + [xr_ref[t] for t in range(DEC_SEQ)]
        cwv = cw_ref[...]
        sp = _softplus_neg(lam_ref[...])
        h = h0_ref[...]
        for t in range(DEC_SEQ):
            u = cb_ref[...]
            for k in range(RG_CONV):
                u = u + xp[t + k] * cwv[k:k + 1]
            a, b = _rg_gates(u, wa_ref, ba_ref[...], wx_ref, bx_ref[...], sp)
            h = a * h + b
            y_ref[t] = (h * _gelu(gr_ref[t])).astype(BF16)
        hl_ref[...] = h

    return pl.pallas_call(
        body, out_shape=[jax.ShapeDtypeStruct((DEC_SEQ, DB, D_RNN), BF16), jax.ShapeDtypeStruct((DB, D_RNN), F32)],
        compiler_params=pltpu.CompilerParams(vmem_limit_bytes=32 * 1024 * 1024), name="rglru_sample",
    )(xr, gr, buf, h0, cw, cb.reshape(1, -1), wa, ba.reshape(1, -1), wx, bx.reshape(1, -1), lam.reshape(1, -1))


def _s5_discretize(lam_re, lam_im, log_dt, b_re_t, b_im_t):
    def body(lr_ref, li_ref, ldt_ref, br_ref, bi_ref, ar_ref, ai_ref, bbr_ref, bbi_ref):
        lr = lr_ref[...]
        li = li_ref[...]
        dt = jnp.exp(ldt_ref[...])
        mag = jnp.exp(lr * dt)
        ar = mag * jnp.cos(li * dt)
        ai = mag * jnp.sin(li * dt)
        den = lr * lr + li * li
        n_re = ar - 1.0
        f_re = (n_re * lr + ai * li) / den
        f_im = (ai * lr - n_re * li) / den
        ar_ref[...] = ar
        ai_ref[...] = ai
        for c in range(S5_CH):
            bbr_ref[c] = f_re * br_ref[c] - f_im * bi_ref[c]
            bbi_ref[c] = f_re * bi_ref[c] + f_im * br_ref[c]

    gp = jax.ShapeDtypeStruct(lam_re.shape, F32)
    cgp = jax.ShapeDtypeStruct(b_re_t.shape, F32)
    return pl.pallas_call(body, out_shape=[gp, gp, cgp, cgp], name="s5_discretize")(
        lam_re, lam_im, log_dt.reshape(-1, 1), b_re_t, b_im_t)


def _s5_block_weights(bbr, bbi, c_re, c_im):
    eye = jnp.eye(S5_GB, dtype=F32)

    def bblk(bb):
        x = bb.reshape(S5_CH, S5_NGB, S5_GB, S5_STATE)
        x = jnp.einsum("cngp,gh->ngchp", x, eye)
        return x.reshape(S5_NGB, S5_GB * S5_CH, S5_LANES)

    def cblk(c):
        x = c.reshape(S5_NGB, S5_GB, S5_CH, S5_STATE)
        x = jnp.einsum("ngcp,gh->ngphc", x, eye)
        return x.reshape(S5_NGB, S5_LANES, S5_GB * S5_CH)

    return jnp.concatenate([bblk(bbr), bblk(bbi)], axis=2), cblk(c_re), cblk(c_im)


def _s5_prompt(u, bw, cr, ci, ar, ai, d, B, T):
    L = S5_LANES
    lanes = S5_GB * S5_CH
    nseg = S5_NSEG
    sl = T // nseg
    assert sl & (sl - 1) == 0, "segment length must be a power of two"
    chs = min(64, sl)
    ch = chs * nseg
    nchunks = sl // chs

    def body(u_ref, bw_ref, cr_ref, ci_ref, ar_ref, ai_ref, d_ref, zz_ref, sre_ref, sim_ref, h_scr, t_scr):
        a_re = ar_ref[0]
        a_im = ai_ref[0]
        ar8 = jnp.broadcast_to(a_re, (nseg, L))
        ai8 = jnp.broadcast_to(a_im, (nseg, L))

        def load_u(s0):
            for seg in range(nseg):
                t_scr[pl.ds(seg, chs, stride=nseg), :] = u_ref[0, seg, pl.ds(s0, chs), :]
            return t_scr[...]

        def project(c, carry):
            s0 = pl.multiple_of(c * chs, chs)
            h_scr[pl.ds(pl.multiple_of(c * ch, ch), ch), :] = _dotb(load_u(s0), bw_ref[0])
            return carry

        lax.fori_loop(0, nchunks, project, 0)

        def advance(s, carry, store):
            hr, hi = carry
            rows = pl.ds(pl.multiple_of(s * nseg, nseg), nseg)
            bu = h_scr[rows, :]
            nr = ar8 * hr - ai8 * hi + bu[:, :L]
            ni = ar8 * hi + ai8 * hr + bu[:, L:]
            if store:
                h_scr[rows, :] = jnp.concatenate([nr, ni], axis=1)
            return nr, ni

        zero = jnp.zeros((nseg, L), F32)
        er, ei = lax.fori_loop(0, sl, functools.partial(advance, store=False), (zero, zero), unroll=8)

        pr, pi = a_re, a_im
        for _ in range(sl.bit_length() - 1):
            pr, pi = pr * pr - pi * pi, 2.0 * pr * pi
        sr = [jnp.zeros((1, L), F32)]
        si = [jnp.zeros((1, L), F32)]
        for k in range(nseg):
            sr.append(er[k:k + 1] + pr * sr[k] - pi * si[k])
            si.append(ei[k:k + 1] + pr * si[k] + pi * sr[k])
        sre_ref[0, 0] = sr[nseg]
        sim_ref[0, 0] = si[nseg]
        start = (jnp.concatenate(sr[:nseg], axis=0), jnp.concatenate(si[:nseg], axis=0))
        lax.fori_loop(0, sl, functools.partial(advance, store=True), start, unroll=8)

        def emit(c, carry):
            s0 = pl.multiple_of(c * chs, chs)
            hc = h_scr[pl.ds(pl.multiple_of(c * ch, ch), ch), :]
            uc = load_u(s0)
            y = _dotb(hc[:, :L], cr_ref[0]) - _dotb(hc[:, L:], ci_ref[0]) + d_ref[...] * uc
            t_scr[...] = _gelu(y)
            for seg in range(nseg):
                zz_ref[0, seg, pl.ds(s0, chs), :] = t_scr[pl.ds(seg, chs, stride=nseg), :].astype(BF16)
            return carry

        lax.fori_loop(0, nchunks, emit, 0)

    st_spec = pl.BlockSpec((1, 1, 1, L), lambda b, g: (b, g, 0, 0))
    st_shape = jax.ShapeDtypeStruct((B, S5_NGB, 1, L), F32)
    io_spec = pl.BlockSpec((1, nseg, sl, lanes), lambda b, g: (b, 0, 0, g))
    return pl.pallas_call(
        body, grid=(B, S5_NGB),
        in_specs=[io_spec,
                  pl.BlockSpec((1, lanes, 2 * L), lambda b, g: (g, 0, 0)),
                  pl.BlockSpec((1, L, lanes), lambda b, g: (g, 0, 0)),
                  pl.BlockSpec((1, L, lanes), lambda b, g: (g, 0, 0)),
                  pl.BlockSpec((1, 1, L), lambda b, g: (g, 0, 0)),
                  pl.BlockSpec((1, 1, L), lambda b, g: (g, 0, 0)),
                  pl.BlockSpec((1, lanes), lambda b, g: (0, g))],
        out_specs=[io_spec, st_spec, st_spec],
        out_shape=[jax.ShapeDtypeStruct((B, nseg, sl, D_MODEL), BF16), st_shape, st_shape],
        scratch_shapes=[pltpu.VMEM((T, 2 * L), F32), pltpu.VMEM((ch, lanes), F32)],
        compiler_params=_cp(("parallel", "parallel")), name="s5_prompt",
    )(u, bw, cr, ci, ar, ai, d.reshape(1, -1))


def _s5_sample(u, h0r, h0i, bw, cr, ci, ar, ai, d):
    DB = u.shape[1]
    L = S5_LANES
    lanes = S5_GB * S5_CH

    def body(u_ref, hr_ref, hi_ref, bw_ref, cr_ref, ci_ref, ar_ref, ai_ref, d_ref, zz_ref, sre_ref, sim_ref):
        hr = hr_ref[...]
        hi = hi_ref[...]
        a_re = ar_ref[0]
        a_im = ai_ref[0]
        for t in range(DEC_SEQ):
            uv = u_ref[t]
            bu = _dotf(uv, bw_ref[0])
            hr, hi = a_re * hr - a_im * hi + bu[:, :L], a_re * hi + a_im * hr + bu[:, L:]
            y = _dotf(hr, cr_ref[0]) - _dotf(hi, ci_ref[0]) + d_ref[...] * uv
            zz_ref[t] = _gelu(y).astype(BF16)
        sre_ref[...] = hr
        sim_ref[...] = hi

    st_spec = pl.BlockSpec((DB, L), lambda g: (0, g))
    st_shape = jax.ShapeDtypeStruct((DB, S5_GROUPS * S5_STATE), F32)
    return pl.pallas_call(
        body, grid=(S5_NGB,),
        in_specs=[pl.BlockSpec((DEC_SEQ, DB, lanes), lambda g: (0, 0, g)), st_spec, st_spec,
                  pl.BlockSpec((1, lanes, 2 * L), lambda g: (g, 0, 0)),
                  pl.BlockSpec((1, L, lanes), lambda g: (g, 0, 0)),
                  pl.BlockSpec((1, L, lanes), lambda g: (g, 0, 0)),
                  pl.BlockSpec((1, 1, L), lambda g: (g, 0, 0)),
                  pl.BlockSpec((1, 1, L), lambda g: (g, 0, 0)),
                  pl.BlockSpec((1, lanes), lambda g: (0, g))],
        out_specs=[pl.BlockSpec((DEC_SEQ, DB, lanes), lambda g: (0, 0, g)), st_spec, st_spec],
        out_shape=[jax.ShapeDtypeStruct((DEC_SEQ, DB, D_MODEL), BF16), st_shape, st_shape],
        compiler_params=_cp(("parallel",)), name="s5_sample",
    )(u, h0r, h0i, bw, cr, ci, ar, ai, d.reshape(1, -1))


def _ffn_upgate_prompt(x, g, w_up, w_gate, cw, cb, T, tm, tn):
    M, K = x.shape
    (w_up, wu_spec), (w_gate, wg_spec) = _weight_spec(w_up, tn), _weight_spec(w_gate, tn)
    N = w_up.shape[-1]
    tiles_per_seq = T // tm
    pr = 16

    def body(x_ref, xp_ref, g_ref, wu_ref, wg_ref, cw_ref, cb_ref, act_ref, tail_ref, a_scr):
        i = pl.program_id(0)

        @pl.when(pl.program_id(1) == 0)
        def _():
            a_scr[...] = _rms_rows(x_ref[...], g_ref[...]).astype(BF16)

        a = a_scr[...]
        hu = jnp.dot(a, wu_ref[...], preferred_element_type=F32)
        hg = jnp.dot(a, wg_ref[...], preferred_element_type=F32)
        ap = _rms_rows(xp_ref[...], g_ref[...]).astype(BF16)
        prev = jnp.dot(ap, wu_ref[...], preferred_element_type=F32)
        prev = jnp.where(i % tiles_per_seq == 0, 0.0, prev)
        row = lax.broadcasted_iota(jnp.int32, hu.shape, 0)
        h1 = jnp.where(row == 0, prev[pr - 1:pr], pltpu.roll(hu, 1, 0))
        h2 = jnp.where(row == 0, prev[pr - 2:pr - 1], jnp.where(row == 1, prev[pr - 1:pr], pltpu.roll(hu, 2, 0)))
        cwv = cw_ref[...]
        hc = cb_ref[...] + h2 * cwv[0:1] + h1 * cwv[1:2] + hu * cwv[2:3]
        act_ref[...] = (_gelu(hc) * hg).astype(BF16)
        tail_ref[...] = hu[tm - 8:tm]

    return pl.pallas_call(
        body, grid=(M // tm, N // tn),
        in_specs=[pl.BlockSpec((tm, K), lambda i, j: (i, 0)),
                  pl.BlockSpec((pr, K), lambda i, j: (jnp.maximum(i * (tm // pr) - 1, 0), 0)),
                  pl.BlockSpec((1, K), lambda i, j: (0, 0)),
                  wu_spec, wg_spec,
                  pl.BlockSpec((FFN_CONV, tn), lambda i, j: (0, j)),
                  pl.BlockSpec((1, tn), lambda i, j: (0, j))],
        out_specs=[pl.BlockSpec((tm, tn), lambda i, j: (i, j)), pl.BlockSpec((8, tn), lambda i, j: (i, j))],
        out_shape=[jax.ShapeDtypeStruct((M, N), BF16), jax.ShapeDtypeStruct((M // tm * 8, N), F32)],
        scratch_shapes=[pltpu.VMEM((tm, K), BF16)],
        compiler_params=_cp(("parallel", "arbitrary")), name="ffn_upgate_prompt",
    )(x, x, g.reshape(1, K), w_up, w_gate, cw, cb.reshape(1, -1))


def _tile(m, pref):
    return pref if m % pref == 0 else m


def _dense_tail(x1, p, wts, ffn_state, T):
    (norm_ffn, norm_ple, w_up, w_gate, conv_w, conv_b, w_down, w_proj, w_pgate) = wts
    M = x1.shape[0]
    tm = _tile(M, 1024)
    tn = 512
    prompt = ffn_state is None

    if prompt:
        act, tails = _ffn_upgate_prompt(x1, norm_ffn, w_up, w_gate, conv_w, conv_b, T, tm, tn)
        tails = tails.reshape(M // tm, 8, D_FF)
        tiles_per_seq = T // tm
        new_buf = tails[tiles_per_seq - 1::tiles_per_seq, 8 - (FFN_CONV - 1):, :]
    else:
        nseq = M // T
        h1 = jnp.repeat(ffn_state[:, FFN_CONV - 2], T, axis=0)
        h2 = jnp.stack([ffn_state[:, 0], ffn_state[:, 1]] + [ffn_state[:, 1]] * (T - 2), axis=1).reshape(M, D_FF)

        def conv_epilogue(accs, ex):
            hu, hg = accs
            b1, b2, cwv, cbv = ex
            tok = lax.broadcasted_iota(jnp.int32, hu.shape, 0) % T
            s1 = jnp.where(tok >= 1, pltpu.roll(hu, 1, 0), b1)
            s2 = jnp.where(tok >= 2, pltpu.roll(hu, 2, 0), b2)
            hc = cbv + s2 * cwv[0:1] + s1 * cwv[1:2] + hu * cwv[2:3]
            return _gelu(hc) * hg, hu

        act, hu = _mm([x1], [w_up, w_gate], [0, 0],
                      [(h1, "tile"), (h2, "tile"), (conv_w, "rows"), (conv_b.reshape(1, -1), "rows")],
                      conv_epilogue, [BF16, F32], tm, tn, "ffn_upgate_sample", norm=(0, norm_ffn))
        new_buf = hu.reshape(nseq, T, D_FF)[:, T - (FFN_CONV - 1):]

    (x2,) = _mm([act], [w_down], [0], [(x1, "tile")], lambda accs, ex: (ex[0] + accs[0],), [F32],
                tm, tn, "ffn_down", vmem_mb=56)
    (x3,) = _mm([p, x2], [w_proj, w_pgate], [0, 1], [(x2, "tile")],
                lambda accs, ex: (ex[0] + accs[0] * _sigmoid(accs[1]),), [F32], tm, tn, "ple", norm=(1, norm_ple))
    return x3, new_buf


def kernel(x_prompt, x_sample, cache_cmp_k, cache_cmp_v, cache_sel_k, cache_sel_v, cache_win_k, cache_win_v, state_rglru_h, state_rglru_conv, state_s5_re, state_s5_im, state_ffn_conv, page_table, p_prompt, p_sample, norm_mix, norm_ffn, norm_ple, w_in_even, w_out_even, q_norm, k_norm_cmp, k_norm_sel, k_norm_win, cmp_pe_k, cmp_w1_k, cmp_w2_k, cmp_pe_v, cmp_w1_v, cmp_w2_v, rg_conv_w, rg_conv_b, rg_w_a, rg_b_a, rg_w_x, rg_b_x, rg_lam, s5_lam_re, s5_lam_im, s5_log_dt, s5_b_re, s5_b_im, s5_c_re, s5_c_im, s5_d, s5_w_glu_a, s5_w_glu_b, ffn_w_up, ffn_w_gate, ffn_conv_w, ffn_conv_b, ffn_w_down, ple_w_proj, ple_w_gate):
    B, T, _ = x_prompt.shape
    DB = x_sample.shape[0]
    n_pages = page_table.shape[1]
    past_len = n_pages * PAGE_SIZE
    w_buf = cache_win_k.shape[2]
    MP = B * T
    MS = DB * DEC_SEQ
    depth = norm_mix.shape[0]

    def reorder_in(w):
        q, kv, gt, xr, gr = (w[:, :Q_W], w[:, Q_W:Q_W + 6 * KV_W], w[:, Q_W + 6 * KV_W:Q_W + 6 * KV_W + 3 * N_HEADS],
                             w[:, Q_W + 6 * KV_W + 3 * N_HEADS:Q_W + 6 * KV_W + 3 * N_HEADS + D_RNN],
                             w[:, Q_W + 6 * KV_W + 3 * N_HEADS + D_RNN:])
        pad = jnp.zeros((w.shape[0], IN_PAD - COL_GT - 3 * N_HEADS), w.dtype)
        return jnp.concatenate([q, xr, gr, kv, gt, pad], axis=1).astype(BF16)

    stacked = [w.astype(BF16) for w in (ffn_w_up, ffn_w_gate, ffn_w_down, ple_w_proj, ple_w_gate)]

    def layer_dense_weights(li):
        up, gate, down, proj, pgate = [(w, li) for w in stacked]
        return (norm_ffn[li], norm_ple[li], up, gate, ffn_conv_w[li], ffn_conv_b[li], down, proj, pgate)

    xs = {"p": x_prompt.reshape(MP, D_MODEL), "s": x_sample.reshape(MS, D_MODEL)}
    pe_in = {"p": p_prompt.reshape(depth, MP, -1).astype(BF16), "s": p_sample.reshape(depth, MS, -1).astype(BF16)}
    seq_len = {"p": T, "s": DEC_SEQ}
    ev = {"p": [], "s": []}
    od = {"p": [], "s": []}
    ff = {"p": [], "s": []}
    ptab = page_table.reshape(-1).astype(jnp.int32)

    for li in range(depth):
        dense_w = layer_dense_weights(li)
        if li % 2 == 0:
            e = li // 2
            w_in = reorder_in(w_in_even[e])
            w_out = w_out_even[e].astype(BF16)
            w_out_a, w_out_r = w_out[:Q_W], w_out[Q_W:]
            for grp in ("p", "s"):
                x = xs[grp]
                M = x.shape[0]
                L = seq_len[grp]
                nseq = M // L
                (z,) = _mm([x], [w_in], [0], [], lambda accs, ex: (accs[0],), [F32], _tile(M, 1024), 512, "in_proj",
                           norm=(0, norm_mix[li]))
                if grp == "p":
                    cos_t, sin_t = _rope_tables(jnp.arange(T))
                    tmq = _tile(T, 512)
                    tab_blocks = T // tmq
                else:
                    cos_t, sin_t = _rope_tables(jnp.tile(past_len + jnp.arange(DEC_SEQ), DB))
                    tmq = M
                    tab_blocks = 1
                q, kvb, kc, vc, ks, vs, kw, vw = _qk_prep(z, cos_t, sin_t, tab_blocks, q_norm[e], k_norm_sel[e],
                                                          k_norm_win[e], tmq)
                z3 = z.reshape(nseq, L, IN_PAD)

                if grp == "p":
                    n_rows = T // CMP_STRIDE
                    ccos, csin = _rope_tables(jnp.arange(n_rows) * CMP_STRIDE + CMP_STRIDE - 1)
                    ident = jnp.arange(B * (T // PAGE_SIZE), dtype=jnp.int32)
                    kcmp = _compress(kc.reshape(-1, N_KV * PAGE_SIZE, HEAD_DIM), ident, B, T // PAGE_SIZE, cmp_pe_k[e],
                                     cmp_w1_k[e], cmp_w2_k[e], k_norm_cmp[e], ccos, csin, True, "compress_k_prompt")
                    vcmp = _compress(vc.reshape(-1, N_KV * PAGE_SIZE, HEAD_DIM), ident, B, T // PAGE_SIZE, cmp_pe_v[e],
                                     cmp_w1_v[e], cmp_w2_v[e], k_norm_cmp[e], ccos, csin, False, "compress_v_prompt")
                    n_sel = T // SEL_BLOCK
                    nqb = T // Q_BLOCK
                    ovl_t = _overlap_matrix(n_rows, n_sel, n_sel).T
                    ex_t = ((np.arange(T // SEL_KT)[:, None, None] * SEL_KT + np.arange(SEL_KT)[None, :, None])
                            // SEL_BLOCK == np.arange(n_sel)[None, None, :])
                    vs_t = kvb.reshape(B, T // SEL_KT, SEL_KT, 4, N_KV, HEAD_DIM)[:, :, :, 1].transpose(0, 3, 1, 4, 2)
                    vw_t = kvb.reshape(B, nqb, Q_BLOCK, 4, N_KV, HEAD_DIM)[:, :, :, 3].transpose(0, 3, 1, 4, 2)
                    gates_t = z[:, COL_GT:COL_GT + 3 * N_HEADS].reshape(B, T, 3, N_KV, GROUP)
                    gates_t = gates_t.transpose(0, 3, 2, 4, 1).reshape(B, N_KV, 3 * GROUP, T)
                    o_nsa = _nsa_prompt(q.reshape(B, T, Q_W), kcmp, vcmp.swapaxes(2, 3), kvb.reshape(B, T, 4 * KV_W),
                                        vs_t, vw_t, gates_t, ovl_t, jnp.asarray(ex_t, BF16), B, T).reshape(M, Q_W)
                    y_rg, h_last = _rglru_prompt(z.reshape(B, T, IN_PAD), rg_conv_w[e], rg_conv_b[e], rg_w_a[e],
                                                 rg_b_a[e], rg_w_x[e], rg_b_x[e], rg_lam[e], B, T, _tile(T, 512))
                    y_rg = y_rg.reshape(M, D_RNN)
                    h_last = h_last.reshape(B, D_RNN)
                    new_conv = z3[:, T - (RG_CONV - 1):, COL_XR:COL_XR + D_RNN]
                    new_wk = kw.reshape(B, T, N_KV, HEAD_DIM)
                    new_wv = vw.reshape(B, T, N_KV, HEAD_DIM)
                    if T >= w_buf:
                        new_wk, new_wv = new_wk[:, T - w_buf:], new_wv[:, T - w_buf:]
                    else:
                        padw = ((0, 0), (w_buf - T, 0), (0, 0), (0, 0))
                        new_wk, new_wv = jnp.pad(new_wk, padw), jnp.pad(new_wv, padw)
                else:
                    n_rows = past_len // CMP_STRIDE
                    ccos, csin = _rope_tables(jnp.arange(n_rows) * CMP_STRIDE + CMP_STRIDE - 1)
                    ptab_e = ptab + e * cache_cmp_k.shape[1]
                    kcmp = _compress(cache_cmp_k.reshape(-1, N_KV * PAGE_SIZE, HEAD_DIM), ptab_e, DB, n_pages, cmp_pe_k[e],
                                     cmp_w1_k[e], cmp_w2_k[e], k_norm_cmp[e], ccos, csin, True, "compress_k_sample")
                    vcmp = _compress(cache_cmp_v.reshape(-1, N_KV * PAGE_SIZE, HEAD_DIM), ptab_e, DB, n_pages, cmp_pe_v[e],
                                     cmp_w1_v[e], cmp_w2_v[e], k_norm_cmp[e], ccos, csin, False, "compress_v_sample")
                    n_sel = -(-(past_len + DEC_SEQ) // SEL_BLOCK)
                    sel_cols = -(-n_sel // 128) * 128
                    ovl = _overlap_matrix(n_rows, n_sel, sel_cols)
                    keys = _pages_per_step(n_pages, SEL_PAGES_PER_STEP) * PAGE_SIZE
                    steps = n_pages * PAGE_SIZE // keys
                    ex = (np.arange(sel_cols)[None, :, None]
                          == (np.arange(steps)[:, None, None] * keys + np.arange(keys)[None, None, :]) // SEL_BLOCK)
                    q8 = jnp.pad(q.reshape(DB, DEC_SEQ, Q_W), ((0, 0), (0, Q_PAD - DEC_SEQ), (0, 0)))
                    kvnew = jnp.pad(kvb.reshape(DB, DEC_SEQ, 4 * KV_W), ((0, 0), (0, 128 - DEC_SEQ), (0, 0)))
                    gt = z3[:, :, COL_GT:COL_GT + 3 * N_HEADS].reshape(DB, DEC_SEQ, 3, N_KV, GROUP)
                    gt = gt.transpose(0, 3, 1, 2, 4).reshape(DB, N_KV, DEC_SEQ, 3 * GROUP)
                    gt8 = jnp.pad(gt, ((0, 0), (0, 0), (0, Q_PAD - DEC_SEQ), (0, 0)))
                    o8 = _nsa_sample(q8, kcmp, vcmp, cache_sel_k.reshape(-1, N_KV * PAGE_SIZE, HEAD_DIM),
                                     cache_sel_v.reshape(-1, N_KV * PAGE_SIZE, HEAD_DIM), ptab_e, kvnew,
                                     cache_win_k.reshape(-1, N_KV * w_buf, HEAD_DIM),
                                     cache_win_v.reshape(-1, N_KV * w_buf, HEAD_DIM),
                                     e * DB, gt8, ovl, jnp.asarray(ex, BF16), DB, past_len)
                    o_nsa = o8[:, :DEC_SEQ].reshape(M, Q_W)
                    xr3 = z3[:, :, COL_XR:COL_XR + D_RNN]
                    gr3 = z3[:, :, COL_GR:COL_GR + D_RNN]
                    y_t, h_last = _rglru_sample(xr3.transpose(1, 0, 2), gr3.transpose(1, 0, 2),
                                                state_rglru_conv[e].transpose(1, 0, 2), state_rglru_h[e],
                                                rg_conv_w[e], rg_conv_b[e], rg_w_a[e], rg_b_a[e], rg_w_x[e],
                                                rg_b_x[e], rg_lam[e])
                    y_rg = y_t.transpose(1, 0, 2).reshape(M, D_RNN)
                    new_conv = jnp.concatenate([state_rglru_conv[e], xr3], axis=1)[:, -(RG_CONV - 1):]
                    kk = jnp.concatenate([cache_win_k[e], kw.reshape(DB, DEC_SEQ, N_KV, HEAD_DIM)], axis=1)
                    vv = jnp.concatenate([cache_win_v[e], vw.reshape(DB, DEC_SEQ, N_KV, HEAD_DIM)], axis=1)
                    new_wk, new_wv = kk[:, -w_buf:], vv[:, -w_buf:]

                rs = lambda a: a.reshape(nseq, L, N_KV, HEAD_DIM)
                ev[grp].append((rs(kc), rs(vc), rs(ks), rs(vs), new_wk, new_wv, h_last, new_conv))
                (x1,) = _mm([o_nsa, y_rg], [w_out_a, w_out_r], [0, 1], [(x, "tile")],
                            lambda accs, exs: (exs[0] + accs[0] + accs[1],), [F32], _tile(M, 1024), 512, "mixer_out")
                xs[grp], nb = _dense_tail(x1, pe_in[grp][li], dense_w, None if grp == "p" else state_ffn_conv[li], L)
                ff[grp].append(nb)
        else:
            o = li // 2
            ar, ai, bbr, bbi = _s5_discretize(s5_lam_re[o], s5_lam_im[o], s5_log_dt[o],
                                              s5_b_re[o].transpose(2, 0, 1), s5_b_im[o].transpose(2, 0, 1))
            bw, cr, ci = _s5_block_weights(bbr, bbi, s5_c_re[o], s5_c_im[o])
            ar = ar.reshape(S5_NGB, 1, S5_LANES)
            ai = ai.reshape(S5_NGB, 1, S5_LANES)
            w_a = s5_w_glu_a[o].astype(BF16)
            w_b = s5_w_glu_b[o].astype(BF16)
            for grp in ("p", "s"):
                x = xs[grp]
                M = x.shape[0]
                L = seq_len[grp]
                glu = lambda accs, exs: (exs[0] + accs[0] * _sigmoid(accs[1]),)
                if grp == "p":
                    sl = T // S5_NSEG
                    hn = _rmsnorm(x, norm_mix[li], F32, _tile(M, 256))
                    zz, sre, sim = _s5_prompt(hn.reshape(B, S5_NSEG, sl, D_MODEL), bw.astype(BF16), cr.astype(BF16),
                                              ci.astype(BF16), ar, ai, s5_d[o], B, T)
                    sre = sre.reshape(B, S5_GROUPS, S5_STATE)
                    sim = sim.reshape(B, S5_GROUPS, S5_STATE)
                    (x1,) = _mm([zz.reshape(M, D_MODEL)], [w_a, w_b], [0, 0], [(x, "tile")], glu, [F32],
                                _tile(M, 1024), 512, "mixer_out")
                else:
                    hn = _rmsnorm(x, norm_mix[li], F32, _tile(M, 256))
                    u = hn.reshape(DB, DEC_SEQ, D_MODEL).transpose(1, 0, 2)
                    zz, sre, sim = _s5_sample(u, state_s5_re[o].reshape(DB, -1), state_s5_im[o].reshape(DB, -1),
                                              bw, cr, ci, ar, ai, s5_d[o])
                    zz = zz.transpose(1, 0, 2).reshape(M, D_MODEL)
                    sre = sre.reshape(DB, S5_GROUPS, S5_STATE)
                    sim = sim.reshape(DB, S5_GROUPS, S5_STATE)
                    (x1,) = _mm([zz], [w_a, w_b], [0, 0], [(x, "tile")], glu, [F32], M, 512, "mixer_out")
                od[grp].append((sre, sim))
                xs[grp], nb = _dense_tail(x1, pe_in[grp][li], dense_w, None if grp == "p" else state_ffn_conv[li], L)
                ff[grp].append(nb)

    def stk(states, i):
        return jnp.stack([s[i] for s in states])

    outs = [xs["p"].reshape(B, T, D_MODEL), xs["s"].reshape(DB, DEC_SEQ, D_MODEL)]
    for i in range(8):
        outs += [stk(ev["p"], i), stk(ev["s"], i)]
    for i in range(2):
        outs += [stk(od["p"], i), stk(od["s"], i)]
    outs += [jnp.stack(ff["p"]), jnp.stack(ff["s"])]
    return tuple(outs)
```
